```python
import math
import jax, jax.numpy as jnp
from jax import lax
import numpy as np

D_MODEL = 1024
BATCH = 8
SEQ = 2048
DEPTH = 2

N_MIXERS = 2
HEAD_DIM = 64
GRID_W = 64
N_MEM = 256
MEM_HEADS = 4
A_Q_HEADS = 12
A_KV_HEADS = 4
A_GROUP = A_Q_HEADS // A_KV_HEADS
Q_BLOCK = 128
AXIAL_THETA = 10000.0
B_HEADS = 8
DILATED_PATTERNS = ((128, 1), (512, 4), (2048, 16))
N_PATTERNS = len(DILATED_PATTERNS)
PARTIAL_ROT_DIMS = HEAD_DIM // 4
PARTIAL_THETA = 500000.0
N_EXPERTS = 16
N_EXPERT_GROUPS = 4
EXPERTS_PER_GROUP = N_EXPERTS // N_EXPERT_GROUPS
TOP_K = 2
D_EXPERT = 512
ALPHA = (2 * DEPTH) ** 0.25
BETA = (8 * DEPTH) ** -0.25
NORM_EPS = 1e-6
NEG_BIG = -1e30
N_A_LAYERS = (DEPTH + 1) // 2
N_B_LAYERS = DEPTH // 2
A_IN = (A_Q_HEADS + 2 * A_KV_HEADS + MEM_HEADS) * HEAD_DIM
A_OUT = (A_Q_HEADS + MEM_HEADS) * HEAD_DIM
B_IN = (3 * N_PATTERNS * B_HEADS + MEM_HEADS) * HEAD_DIM
B_OUT = (B_HEADS + MEM_HEADS) * HEAD_DIM

kernel_name = 'hybrid_gqa_dilated_moe_encoder'

F32 = jnp.float32


def layer_norm(x, g, b):
    xf = x.astype(F32)
    mu = jnp.mean(xf, -1, keepdims=True)
    var = jnp.mean(jnp.square(xf - mu), -1, keepdims=True)
    return ((xf - mu) * lax.rsqrt(var + NORM_EPS) * g.astype(F32) + b.astype(F32)).astype(x.dtype)


def rms_norm(x, g):
    xf = x.astype(F32)
    return (xf * lax.rsqrt(jnp.mean(xf * xf, -1, keepdims=True) + NORM_EPS) * g.astype(F32)).astype(x.dtype)


def rope_freqs(n, theta):
    return 1.0 / (theta ** (jnp.arange(0, n, 2, dtype=F32) / n))


def rotate(x, ang):
    cos = jnp.cos(ang)[:, None, :]
    sin = jnp.sin(ang)[:, None, :]
    xf = x.astype(F32)
    x1, x2 = jnp.split(xf, 2, axis=-1)
    return jnp.concatenate([x1 * cos - x2 * sin, x2 * cos + x1 * sin], -1).astype(x.dtype)


def axial_rope(x, seq_len):
    rows = seq_len // GRID_W
    row = jnp.repeat(jnp.arange(rows), GRID_W).astype(F32)
    col = jnp.tile(jnp.arange(GRID_W), rows).astype(F32)
    half = HEAD_DIM // 2
    inv = rope_freqs(half, AXIAL_THETA)
    return jnp.concatenate([rotate(x[..., :half], row[:, None] * inv),
                            rotate(x[..., half:], col[:, None] * inv)], -1)


def partial_rope(x, seq_len):
    pos = jnp.arange(seq_len, dtype=F32)
    ang = pos[:, None] * rope_freqs(PARTIAL_ROT_DIMS, PARTIAL_THETA)
    return jnp.concatenate([rotate(x[..., :PARTIAL_ROT_DIMS], ang), x[..., PARTIAL_ROT_DIMS:]], -1)


def gqa_block_attention(q, k, v):
    b, s = q.shape[:2]
    nq = s // Q_BLOCK
    scale = HEAD_DIM ** -0.5
    qb = q.reshape(b, nq, Q_BLOCK, A_KV_HEADS, A_GROUP, HEAD_DIM).transpose(1, 0, 2, 3, 4, 5)

    def block(qblk):
        sc = jnp.einsum('bqkgd,bskd->bkgqs', qblk, k, preferred_element_type=F32) * scale
        p = jax.nn.softmax(sc, axis=-1).astype(v.dtype)
        return jnp.einsum('bkgqs,bskd->bqkgd', p, v)

    o = lax.map(block, qb)
    return o.transpose(1, 0, 2, 3, 4, 5).reshape(b, s, A_Q_HEADS * HEAD_DIM)


def dilated_band_attention(q, k, v, dilation, n_side):
    b, s, h, hd = q.shape
    L = s // dilation
    P = n_side
    nb = -(-L // P)
    Lp = nb * P
    r = b * dilation

    def to_residue(t):
        return t.reshape(b, L, dilation, h, hd).transpose(0, 2, 1, 3, 4).reshape(r, L, h, hd)

    qr, kr, vr = to_residue(q), to_residue(k), to_residue(v)
    qr = jnp.pad(qr, ((0, 0), (0, Lp - L), (0, 0), (0, 0)))
    pad_kv = ((0, 0), (P, Lp - L + P), (0, 0), (0, 0))
    kr = jnp.pad(kr, pad_kv)
    vr = jnp.pad(vr, pad_kv)

    def windows(t):
        tb = t.reshape(r, nb + 2, P, h, hd)
        return jnp.concatenate([tb[:, :-2], tb[:, 1:-1], tb[:, 2:]], axis=2)

    kw, vw = windows(kr), windows(vr)
    qb = qr.reshape(r, nb, P, h, hd)
    qpos = jnp.arange(nb)[:, None] * P + jnp.arange(P)[None, :]
    kpos = (jnp.arange(nb)[:, None] - 1) * P + jnp.arange(3 * P)[None, :]
    dist = qpos[:, :, None] - kpos[:, None, :]
    valid = (jnp.abs(dist) <= n_side) & (kpos[:, None, :] >= 0) & (kpos[:, None, :] < L)
    sc = jnp.einsum('rnqhd,rnkhd->rnhqk', qb, kw, preferred_element_type=F32) * (HEAD_DIM ** -0.5)
    sc = jnp.where(valid[None, :, None], sc, NEG_BIG)
    lse = jax.nn.logsumexp(sc, axis=-1)
    p = jnp.exp(sc - lse[..., None]).astype(v.dtype)
    o = jnp.einsum('rnhqk,rnkhd->rnqhd', p, vw).reshape(r, Lp, h, hd)[:, :L]
    o = o.reshape(b, dilation, L, h, hd).transpose(0, 2, 1, 3, 4).reshape(b, s, h, hd)
    lse = lse.transpose(0, 1, 3, 2).reshape(r, Lp, h)[:, :L]
    lse = lse.reshape(b, dilation, L, h).transpose(0, 2, 1, 3).reshape(b, s, h)
    return o, lse


def dilated_mixture(q, k, v):
    b, s = q.shape[:2]
    outs, lses = [], []
    for g, (window, dilation) in enumerate(DILATED_PATTERNS):
        o, l = dilated_band_attention(q[:, :, g], k[:, :, g], v[:, :, g], dilation, (window // 2) // dilation)
        outs.append(o)
        lses.append(l)
    wts = jax.nn.softmax(jnp.stack(lses, 0), axis=0)
    o = jnp.einsum('gbsh,gbshd->bshd', wts, jnp.stack(outs, 0).astype(F32))
    return o.astype(q.dtype).reshape(b, s, B_HEADS * HEAD_DIM)


def memory_attention(q, mk, mv):
    b, s = q.shape[:2]
    sc = jnp.einsum('bshd,bmhd->bhsm', q, mk, preferred_element_type=F32) * (HEAD_DIM ** -0.5)
    p = jax.nn.softmax(sc, axis=-1).astype(mv.dtype)
    return jnp.einsum('bhsm,bmhd->bshd', p, mv).reshape(b, s, MEM_HEADS * HEAD_DIM)


def grouped_moe(x, router_w, router_b, w_gate, w_up, w_down):
    b, s, d = x.shape
    xt = x.reshape(b * s, d)
    aff = jax.nn.sigmoid(jnp.dot(xt, router_w, preferred_element_type=F32))
    sel = aff + router_b.astype(F32)
    grp_score = lax.top_k(sel.reshape(-1, N_EXPERT_GROUPS, EXPERTS_PER_GROUP), TOP_K)[0].sum(-1)
    best = jnp.argmax(grp_score, axis=-1)
    in_grp = (jnp.arange(N_EXPERTS) // EXPERTS_PER_GROUP)[None, :] == best[:, None]
    _, idx = lax.top_k(jnp.where(in_grp, sel, -jnp.inf), TOP_K)
    w = jnp.take_along_axis(aff, idx, axis=-1)
    w = w / jnp.sum(w, -1, keepdims=True)
    gates = jnp.sum(jax.nn.one_hot(idx, N_EXPERTS, dtype=F32) * w[..., None], axis=1)

    def expert_step(acc, p):
        wg, wu, wd, g = p
        hdn = jax.nn.silu(xt @ wg) * (xt @ wu)
        return acc + g[:, None] * (hdn @ wd).astype(F32), None

    acc, _ = lax.scan(expert_step, jnp.zeros((b * s, d), F32), (w_gate, w_up, w_down, gates.T))
    return acc.astype(x.dtype).reshape(b, s, d)


def setup_inputs(seed: int = 0) -> dict:
    key = jax.random.key(seed)
    ks = jax.random.split(key, 20)
    hd = HEAD_DIM
    nrm = jax.random.normal
    x = nrm(ks[0], (BATCH, SEQ, D_MODEL), F32)
    mem = nrm(ks[1], (BATCH, N_MEM, D_MODEL), F32)
    mkv_scale = jnp.concatenate([jnp.ones((MEM_HEADS * hd,), F32), jnp.full((MEM_HEADS * hd,), BETA, F32)])
    w_mem_kv = nrm(ks[2], (D_MODEL, 2 * MEM_HEADS * hd), F32) * D_MODEL ** -0.5 * mkv_scale
    router_w = nrm(ks[3], (D_MODEL, N_EXPERTS), F32) * D_MODEL ** -0.5
    router_b = nrm(ks[4], (N_EXPERTS,), F32) * 0.01
    a_scale = jnp.concatenate([jnp.ones(((A_Q_HEADS + A_KV_HEADS) * hd,), F32),
                               jnp.full((A_KV_HEADS * hd,), BETA, F32),
                               jnp.ones((MEM_HEADS * hd,), F32)])
    a_w_in = nrm(ks[5], (N_A_LAYERS, D_MODEL, A_IN), F32) * D_MODEL ** -0.5 * a_scale
    a_w_out = nrm(ks[6], (N_A_LAYERS, A_OUT, D_MODEL), F32) * A_OUT ** -0.5 * BETA
    a_q_norm = 1.0 + 0.02 * nrm(ks[7], (N_A_LAYERS, hd), F32)
    a_k_norm = 1.0 + 0.02 * nrm(ks[8], (N_A_LAYERS, hd), F32)
    b_scale = jnp.concatenate([jnp.ones((2 * N_PATTERNS * B_HEADS * hd,), F32),
                               jnp.full((N_PATTERNS * B_HEADS * hd,), BETA, F32),
                               jnp.ones((MEM_HEADS * hd,), F32)])
    b_w_in = nrm(ks[9], (N_B_LAYERS, D_MODEL, B_IN), F32) * D_MODEL ** -0.5 * b_scale
    b_w_out = nrm(ks[10], (N_B_LAYERS, B_OUT, D_MODEL), F32) * B_OUT ** -0.5 * BETA
    ln1_g = 1.0 + 0.02 * nrm(ks[11], (DEPTH, D_MODEL), F32)
    ln1_b = 0.02 * nrm(ks[12], (DEPTH, D_MODEL), F32)
    ln2_g = 1.0 + 0.02 * nrm(ks[13], (DEPTH, D_MODEL), F32)
    ln2_b = 0.02 * nrm(ks[14], (DEPTH, D_MODEL), F32)
    w_gate = nrm(ks[15], (DEPTH, N_EXPERTS, D_MODEL, D_EXPERT), F32) * D_MODEL ** -0.5
    w_up = nrm(ks[16], (DEPTH, N_EXPERTS, D_MODEL, D_EXPERT), F32) * D_MODEL ** -0.5
    w_down = nrm(ks[17], (DEPTH, N_EXPERTS, D_EXPERT, D_MODEL), F32) * D_EXPERT ** -0.5 * BETA
    return {'x': x, 'mem': mem, 'w_mem_kv': w_mem_kv, 'router_w': router_w, 'router_b': router_b,
            'a_w_in': a_w_in, 'a_w_out': a_w_out, 'a_q_norm': a_q_norm, 'a_k_norm': a_k_norm,
            'b_w_in': b_w_in, 'b_w_out': b_w_out,
            'ln1_g': ln1_g, 'ln1_b': ln1_b, 'ln2_g': ln2_g, 'ln2_b': ln2_b,
            'w_gate': w_gate, 'w_up': w_up, 'w_down': w_down}


def reference(x, mem, w_mem_kv, router_w, router_b, a_w_in, a_w_out, a_q_norm, a_k_norm,
              b_w_in, b_w_out, ln1_g, ln1_b, ln2_g, ln2_b, w_gate, w_up, w_down):
    b, s, _ = x.shape
    hd = HEAD_DIM
    mkv = jnp.einsum('bmd,de->bme', mem, w_mem_kv)
    mem_k, mem_v = jnp.split(mkv, 2, axis=-1)
    mem_k = mem_k.reshape(b, N_MEM, MEM_HEADS, hd)
    mem_v = mem_v.reshape(b, N_MEM, MEM_HEADS, hd)
    a_split = [A_Q_HEADS * hd, (A_Q_HEADS + A_KV_HEADS) * hd, (A_Q_HEADS + 2 * A_KV_HEADS) * hd]
    nb_cols = N_PATTERNS * B_HEADS * hd
    b_split = [nb_cols, 2 * nb_cols, 3 * nb_cols]
    for i in range(DEPTH):
        j = i // N_MIXERS
        if i % N_MIXERS == 0:
            proj = x @ a_w_in[j]
            qa, ka, va, qm = jnp.split(proj, a_split, axis=-1)
            qa = axial_rope(rms_norm(qa.reshape(b, s, A_Q_HEADS, hd), a_q_norm[j]), s)
            ka = axial_rope(rms_norm(ka.reshape(b, s, A_KV_HEADS, hd), a_k_norm[j]), s)
            va = va.reshape(b, s, A_KV_HEADS, hd)
            mix = gqa_block_attention(qa, ka, va)
            w_out = a_w_out[j]
        else:
            proj = x @ b_w_in[j]
            qb, kb, vb, qm = jnp.split(proj, b_split, axis=-1)
            qb = partial_rope(qb.reshape(b, s, N_PATTERNS * B_HEADS, hd), s).reshape(b, s, N_PATTERNS, B_HEADS, hd)
            kb = partial_rope(kb.reshape(b, s, N_PATTERNS * B_HEADS, hd), s).reshape(b, s, N_PATTERNS, B_HEADS, hd)
            vb = vb.reshape(b, s, N_PATTERNS, B_HEADS, hd)
            mix = dilated_mixture(qb, kb, vb)
            w_out = b_w_out[j]
        memo = memory_attention(qm.reshape(b, s, MEM_HEADS, hd), mem_k, mem_v)
        attn = jnp.concatenate([mix, memo], axis=-1) @ w_out
        x = layer_norm(ALPHA * x + attn, ln1_g[i], ln1_b[i])
        ffn = grouped_moe(x, router_w, router_b, w_gate[i], w_up[i], w_down[i])
        x = layer_norm(ALPHA * x + ffn, ln2_g[i], ln2_b[i])
    return x
```

```python
import functools

import numpy as np
import jax
import jax.numpy as jnp
from jax import lax
from jax.experimental import pallas as pl
from jax.experimental.pallas import tpu as pltpu

F32 = jnp.float32
BF16 = jnp.bfloat16

D_MODEL = 1024
BATCH = 8
SEQ = 2048
N_TOK = BATCH * SEQ
HEAD_DIM = 64
GRID_W = 64
N_MEM = 256
MEM_HEADS = 4
A_Q_HEADS = 12
A_KV_HEADS = 4
AXIAL_THETA = 10000.0
B_HEADS = 8
DILATIONS = (1, 4, 16)
BAND_RADIUS = 64
PARTIAL_ROT_DIMS = HEAD_DIM // 4
PARTIAL_THETA = 500000.0
N_EXPERTS = 16
N_EXPERT_GROUPS = 4
EXPERTS_PER_GROUP = 4
D_EXPERT = 512
DEPTH = 2
ALPHA = (2 * DEPTH) ** 0.25
NORM_EPS = 1e-6
NEG_BIG = -1e30
SCALE = HEAD_DIM ** -0.5

LANES = 128
MEM_W = MEM_HEADS * HEAD_DIM

PAIRS = ((0, 1), (0, 2), (0, 3), (1, 2), (1, 3), (2, 3))
N_CLASSES = N_EXPERT_GROUPS * len(PAIRS)
CLASS_ROWS = 32
MOE_TM = 256
MOE_ROWS = N_TOK + N_CLASSES * MOE_TM
MOE_TILES = MOE_ROWS // MOE_TM
XA_W = D_MODEL + LANES

VMEM_LIMIT = 56 * 1024 * 1024


def _params(sem, vmem=VMEM_LIMIT):
    return pltpu.CompilerParams(dimension_semantics=sem, vmem_limit_bytes=vmem)


def _dot(a, b):
    return jnp.dot(a, b, preferred_element_type=F32)


def _dot_nt(a, b):
    return lax.dot_general(a, b, (((1,), (1,)), ((), ())), preferred_element_type=F32)


def _split_bf16(x):
    hi = x.astype(BF16)
    lo = (x - hi.astype(F32)).astype(BF16)
    return hi, lo


def _layer_norm(z, g, b):
    mu = jnp.mean(z, axis=-1, keepdims=True)
    zc = z - mu
    var = jnp.mean(zc * zc, axis=-1, keepdims=True)
    return zc * lax.rsqrt(var + NORM_EPS) * g + b


def _rope(z, c, sa, sb, shift):
    return z * c + pltpu.roll(z, LANES - shift, 1) * sa + pltpu.roll(z, shift, 1) * sb


def _proj_kernel(x_ref, w_ref, o_ref, *, scale):
    y = _dot(x_ref[...].astype(BF16), w_ref[...])
    o_ref[...] = (y * scale).astype(o_ref.dtype)


def _proj(x, w, col_block, width, scale, tm=512):
    n, k = x.shape
    return pl.pallas_call(
        functools.partial(_proj_kernel, scale=scale),
        grid=(n // tm,),
        in_specs=[pl.BlockSpec((tm, k), lambda i: (i, 0)),
                  pl.BlockSpec((k, width), lambda i: (0, col_block))],
        out_specs=pl.BlockSpec((tm, width), lambda i: (i, 0)),
        out_shape=jax.ShapeDtypeStruct((n, width), BF16),
        compiler_params=_params(("parallel",)),
        name="proj",
    )(x, w)


def _a_in_kernel(x_ref, w_ref, m_ref, qg_ref, kg_ref, c_ref, sa_ref, sb_ref, q_ref, k_ref, v_ref, qm_ref):
    y = _dot(x_ref[...].astype(BF16), w_ref[...])
    m = m_ref[...]
    c, sa, sb = c_ref[...], sa_ref[...], sb_ref[...]

    def norm_rope(z, gain):
        hi, lo = _split_bf16(z * z)
        ms = (_dot(hi, m) + _dot(lo, m)) * (1.0 / HEAD_DIM)
        z = z * lax.rsqrt(ms + NORM_EPS) * gain
        return _rope(z, c, sa, sb, 16)

    nq = A_Q_HEADS * HEAD_DIM // LANES
    nk = A_KV_HEADS * HEAD_DIM // LANES
    for ch in range(nq):
        q_ref[:, ch * LANES:(ch + 1) * LANES] = norm_rope(y[:, ch * LANES:(ch + 1) * LANES], qg_ref[...]).astype(BF16)
    off = nq * LANES
    for ch in range(nk):
        z = y[:, off + ch * LANES: off + (ch + 1) * LANES]
        k_ref[:, ch * LANES:(ch + 1) * LANES] = norm_rope(z, kg_ref[...]).astype(BF16)
    off += nk * LANES
    v_ref[...] = y[:, off:off + nk * LANES].astype(BF16)
    off += nk * LANES
    qm_ref[...] = (y[:, off:off + MEM_W] * SCALE).astype(BF16)


def _a_in_proj(x, w, m128, qg, kg, tabs, tm=512):
    c, sa, sb = tabs
    nblk = SEQ // tm
    a_in = w.shape[1]
    tab_spec = pl.BlockSpec((tm, LANES), lambda i: (i % nblk, 0))
    row = lambda width: pl.BlockSpec((tm, width), lambda i: (i, 0))
    const = lambda shape: pl.BlockSpec(shape, lambda i: (0, 0))
    qw, kw = A_Q_HEADS * HEAD_DIM, A_KV_HEADS * HEAD_DIM
    return pl.pallas_call(
        _a_in_kernel,
        grid=(N_TOK // tm,),
        in_specs=[row(D_MODEL), const((D_MODEL, a_in)), const((LANES, LANES)), const((1, LANES)), const((1, LANES)),
                  tab_spec, tab_spec, tab_spec],
        out_specs=[row(qw), row(kw), row(kw), row(MEM_W)],
        out_shape=[jax.ShapeDtypeStruct((N_TOK, qw), BF16), jax.ShapeDtypeStruct((N_TOK, kw), BF16),
                   jax.ShapeDtypeStruct((N_TOK, kw), BF16), jax.ShapeDtypeStruct((N_TOK, MEM_W), BF16)],
        compiler_params=_params(("parallel",)),
        name="a_in_proj",
    )(x, w, m128, qg, kg, c, sa, sb)


def _b_in_kernel(x_ref, wq_ref, wk_ref, wv_ref, c_ref, sa_ref, sb_ref, q_ref, k_ref, v_ref):
    xb = x_ref[0].astype(BF16)
    c, sa, sb = c_ref[0], sa_ref[0], sb_ref[0]
    width = B_HEADS * HEAD_DIM
    yq = _dot(xb, wq_ref[...]) * SCALE
    yk = _dot(xb, wk_ref[...])
    for ch in range(width // LANES):
        sl = slice(ch * LANES, (ch + 1) * LANES)
        q_ref[0, 0, :, sl] = _rope(yq[:, sl], c, sa, sb, PARTIAL_ROT_DIMS // 2).astype(BF16)
        k_ref[0, 0, :, sl] = _rope(yk[:, sl], c, sa, sb, PARTIAL_ROT_DIMS // 2).astype(BF16)
    v_ref[0, 0] = _dot(xb, wv_ref[...]).astype(BF16)


def _b_in_proj(x, w, g, d, tabs):
    length = SEQ // d
    tl = min(length, 512)
    width = B_HEADS * HEAD_DIM
    xv = x.reshape(BATCH, length, d * D_MODEL)
    c, sa, sb = tabs
    tab_spec = pl.BlockSpec((1, tl, LANES), lambda b, r, i: (r, i, 0))
    w_spec = lambda blk: pl.BlockSpec((D_MODEL, width), lambda b, r, i: (0, blk))
    o_spec = pl.BlockSpec((1, 1, tl, width), lambda b, r, i: (b, r, i, 0))
    o_shape = jax.ShapeDtypeStruct((BATCH, d, length, width), BF16)
    q, k, v = pl.pallas_call(
        _b_in_kernel,
        grid=(BATCH, d, length // tl),
        in_specs=[pl.BlockSpec((1, tl, D_MODEL), lambda b, r, i: (b, i, r)),
                  w_spec(g), w_spec(3 + g), w_spec(6 + g), tab_spec, tab_spec, tab_spec],
        out_specs=[o_spec, o_spec, o_spec],
        out_shape=[o_shape, o_shape, o_shape],
        compiler_params=_params(("parallel", "parallel", "parallel")),
        name=f"b_in_proj_d{d}",
    )(xv, w, w, w, c, sa, sb)
    return q, k, v


def _softmax_pv(s, v):
    m = jnp.max(s, axis=1, keepdims=True)
    p = jnp.exp(s - m)
    l = jnp.sum(p, axis=1, keepdims=True)
    return _dot(p.astype(BF16), v), m, l


def _gqa_kernel(q_ref, k_ref, v_ref, o_ref):
    tq = q_ref.shape[0]
    group = A_Q_HEADS // A_KV_HEADS
    for j in range(2):
        kj = k_ref[:, j * HEAD_DIM:(j + 1) * HEAD_DIM]
        vj = v_ref[:, j * HEAD_DIM:(j + 1) * HEAD_DIM]
        heads = [group * j + g for g in range(group)]
        qs = jnp.concatenate([q_ref[:, h * HEAD_DIM:(h + 1) * HEAD_DIM] for h in heads], axis=0)
        o, _, l = _softmax_pv(_dot_nt(qs, kj), vj)
        o = o / l
        for g, h in enumerate(heads):
            o_ref[:, h * HEAD_DIM:(h + 1) * HEAD_DIM] = o[g * tq:(g + 1) * tq].astype(BF16)


def _gqa_attention(q, k, v, tq=128):
    nq = SEQ // tq
    qw = q.shape[1] // 2
    kw = k.shape[1] // 2
    return pl.pallas_call(
        _gqa_kernel,
        grid=(BATCH, 2, nq),
        in_specs=[pl.BlockSpec((tq, qw), lambda b, p, i: (b * nq + i, p)),
                  pl.BlockSpec((SEQ, kw), lambda b, p, i: (b, p)),
                  pl.BlockSpec((SEQ, kw), lambda b, p, i: (b, p))],
        out_specs=pl.BlockSpec((tq, qw), lambda b, p, i: (b * nq + i, p)),
        out_shape=jax.ShapeDtypeStruct(q.shape, BF16),
        compiler_params=_params(("parallel", "parallel", "parallel")),
        name="gqa_attention",
    )(q, k, v)


def _mem_attn_kernel(q_ref, k_ref, v_ref, o_ref):
    for h in range(MEM_HEADS):
        sl = slice(h * HEAD_DIM, (h + 1) * HEAD_DIM)
        o, _, l = _softmax_pv(_dot_nt(q_ref[:, sl], k_ref[:, sl]), v_ref[:, sl])
        o_ref[:, sl] = (o / l).astype(BF16)


def _mem_attention(qm, mkv, tq=512):
    nq = SEQ // tq
    return pl.pallas_call(
        _mem_attn_kernel,
        grid=(BATCH, nq),
        in_specs=[pl.BlockSpec((tq, MEM_W), lambda b, i: (b * nq + i, 0)),
                  pl.BlockSpec((N_MEM, MEM_W), lambda b, i: (b, 0)),
                  pl.BlockSpec((N_MEM, MEM_W), lambda b, i: (b, 1))],
        out_specs=pl.BlockSpec((tq, MEM_W), lambda b, i: (b * nq + i, 0)),
        out_shape=jax.ShapeDtypeStruct((N_TOK, MEM_W), BF16),
        compiler_params=_params(("parallel", "parallel")),
        name="mem_attention",
    )(qm, mkv, mkv)


def _band_kernel(q_ref, k_ref, v_ref, o_ref, lse_ref, *, length, tq, win):
    i = pl.program_id(1)
    start = jnp.clip(i * tq - BAND_RADIUS, 0, length - win)
    start = pl.multiple_of(start, BAND_RADIUS)
    qpos = i * tq + lax.broadcasted_iota(jnp.int32, (tq, win), 0)
    kpos = start + lax.broadcasted_iota(jnp.int32, (tq, win), 1)
    valid = jnp.abs(qpos - kpos) <= BAND_RADIUS
    lane = lax.broadcasted_iota(jnp.int32, (tq, LANES), 1)
    lse_all = jnp.zeros((tq, LANES), F32)
    for h in range(B_HEADS):
        sl = slice(h * HEAD_DIM, (h + 1) * HEAD_DIM)
        kh = k_ref[0, pl.ds(start, win), sl]
        vh = v_ref[0, pl.ds(start, win), sl]
        s = jnp.where(valid, _dot_nt(q_ref[0, :, sl], kh), NEG_BIG)
        o, m, l = _softmax_pv(s, vh)
        o_ref[0, :, sl] = (o / l).astype(BF16)
        lse_all = jnp.where(lane == h, m + jnp.log(l), lse_all)
    lse_ref[0] = lse_all


def _band_attention(q, k, v, d):
    length = SEQ // d
    tq = min(length, 128)
    win = min(length, tq + 2 * BAND_RADIUS)
    width = B_HEADS * HEAD_DIM
    nseq = BATCH * d
    q, k, v = (t.reshape(nseq, length, width) for t in (q, k, v))
    seq_spec = pl.BlockSpec((1, length, width), lambda s, i: (s, 0, 0))
    o, lse = pl.pallas_call(
        functools.partial(_band_kernel, length=length, tq=tq, win=win),
        grid=(nseq, length // tq),
        in_specs=[pl.BlockSpec((1, tq, width), lambda s, i: (s, i, 0)), seq_spec, seq_spec],
        out_specs=[pl.BlockSpec((1, tq, width), lambda s, i: (s, i, 0)),
                   pl.BlockSpec((1, tq, LANES), lambda s, i: (s, i, 0))],
        out_shape=[jax.ShapeDtypeStruct((nseq, length, width), BF16),
                   jax.ShapeDtypeStruct((nseq, length, LANES), F32)],
        compiler_params=_params(("parallel", "parallel")),
        name=f"band_attention_d{d}",
    )(q, k, v)
    return o, lse


def _merge_kernel(o0_ref, o1_ref, o2_ref, l0_ref, l1_ref, l2_ref, e_ref, o_ref):
    l0, l1, l2 = l0_ref[0], l1_ref[0, 0], l2_ref[0, 0]
    m = jnp.maximum(jnp.maximum(l0, l1), l2)
    e0, e1, e2 = jnp.exp(l0 - m), jnp.exp(l1 - m), jnp.exp(l2 - m)
    inv = 1.0 / (e0 + e1 + e2)
    expand = e_ref[...]

    def widen(w):
        hi, lo = _split_bf16(w)
        return _dot(hi, expand) + _dot(lo, expand)

    o = (widen(e0 * inv) * o0_ref[0].astype(F32) + widen(e1 * inv) * o1_ref[0, 0].astype(F32)
         + widen(e2 * inv) * o2_ref[0, 0].astype(F32))
    o_ref[0] = o.astype(BF16)


def _merge_groups(outs, lses, expand):
    width = B_HEADS * HEAD_DIM
    dmax = DILATIONS[-1]
    lmin = SEQ // dmax
    rep = dmax // DILATIONS[1]

    def views(t, w):
        t0 = t[0].reshape(BATCH, lmin, dmax * w)
        t1 = t[1].reshape(BATCH, DILATIONS[1], lmin, rep * w)
        t2 = t[2].reshape(BATCH, dmax, lmin, w)
        return t0, t1, t2

    def specs(w):
        return [pl.BlockSpec((1, lmin, w), lambda b, r: (b, 0, r)),
                pl.BlockSpec((1, 1, lmin, w), lambda b, r: (b, r % DILATIONS[1], 0, r // DILATIONS[1])),
                pl.BlockSpec((1, 1, lmin, w), lambda b, r: (b, r, 0, 0))]

    out = pl.pallas_call(
        _merge_kernel,
        grid=(BATCH, dmax),
        in_specs=specs(width) + specs(LANES) + [pl.BlockSpec((LANES, width), lambda b, r: (0, 0))],
        out_specs=pl.BlockSpec((1, lmin, width), lambda b, r: (b, 0, r)),
        out_shape=jax.ShapeDtypeStruct((BATCH, lmin, dmax * width), BF16),
        compiler_params=_params(("parallel", "parallel")),
        name="merge_groups",
    )(*views(outs, width), *views(lses, LANES), expand)
    return out.reshape(N_TOK, width)


def _out_ln_kernel(mix_ref, memo_ref, w1_ref, w2_ref, x_ref, g_ref, b_ref, o_ref):
    attn = _dot(mix_ref[...], w1_ref[...]) + _dot(memo_ref[...], w2_ref[...])
    o_ref[...] = _layer_norm(ALPHA * x_ref[...] + attn, g_ref[...], b_ref[...])


def _out_proj_ln(mix, memo, w1, w2, x, g, b, tm=512):
    kmix = mix.shape[1]
    row = lambda width: pl.BlockSpec((tm, width), lambda i: (i, 0))
    const = lambda shape: pl.BlockSpec(shape, lambda i: (0, 0))
    return pl.pallas_call(
        _out_ln_kernel,
        grid=(N_TOK // tm,),
        in_specs=[row(kmix), row(MEM_W), const((kmix, D_MODEL)), const((MEM_W, D_MODEL)), row(D_MODEL),
                  const((1, D_MODEL)), const((1, D_MODEL))],
        out_specs=row(D_MODEL),
        out_shape=jax.ShapeDtypeStruct((N_TOK, D_MODEL), F32),
        compiler_params=_params(("parallel",)),
        name="out_proj_ln",
    )(mix, memo, w1, w2, x, g, b)


def _router_kernel(x_ref, wh_ref, wl_ref, rb_ref, tri_ref, xa_ref, cls_ref, rank_ref, cnt_ref, carry_ref):
    step = pl.program_id(0)

    @pl.when(step == 0)
    def _():
        carry_ref[...] = jnp.zeros_like(carry_ref)

    x = x_ref[...]
    tm = x.shape[0]
    xh, xl = _split_bf16(x)
    logits = _dot_nt(wh_ref[...], xh) + _dot_nt(wh_ref[...], xl) + _dot_nt(wl_ref[...], xh)
    aff = 1.0 / (1.0 + jnp.exp(-logits))
    sel = aff + rb_ref[...]
    s = [sel[e:e + 1, :] for e in range(N_EXPERTS)]
    a = [aff[e:e + 1, :] for e in range(N_EXPERTS)]

    def top2_sum(v):
        hi01, lo01 = jnp.maximum(v[0], v[1]), jnp.minimum(v[0], v[1])
        hi23, lo23 = jnp.maximum(v[2], v[3]), jnp.minimum(v[2], v[3])
        return jnp.maximum(hi01, hi23) + jnp.maximum(jnp.minimum(hi01, hi23), jnp.maximum(lo01, lo23))

    gscore = [top2_sum(s[EXPERTS_PER_GROUP * g:EXPERTS_PER_GROUP * (g + 1)]) for g in range(N_EXPERT_GROUPS)]
    best = jnp.zeros((1, tm), jnp.int32)
    best_score = gscore[0]
    for g in range(1, N_EXPERT_GROUPS):
        better = gscore[g] > best_score
        best = jnp.where(better, g, best)
        best_score = jnp.where(better, gscore[g], best_score)

    def pick(rows, j):
        out = rows[j]
        for g in range(1, N_EXPERT_GROUPS):
            out = jnp.where(best == g, rows[EXPERTS_PER_GROUP * g + j], out)
        return out

    t = [pick(s, j) for j in range(EXPERTS_PER_GROUP)]
    w = [pick(a, j) for j in range(EXPERTS_PER_GROUP)]

    def first_max(v):
        mx = jnp.maximum(jnp.maximum(v[0], v[1]), jnp.maximum(v[2], v[3]))
        idx = jnp.full((1, tm), EXPERTS_PER_GROUP - 1, jnp.int32)
        for j in range(EXPERTS_PER_GROUP - 2, -1, -1):
            idx = jnp.where(v[j] == mx, j, idx)
        return idx

    i1 = first_max(t)
    i2 = first_max([jnp.where(i1 == j, -jnp.inf, t[j]) for j in range(EXPERTS_PER_GROUP)])
    lo, hi = jnp.minimum(i1, i2), jnp.maximum(i1, i2)

    def take(rows, idx):
        out = rows[0]
        for j in range(1, EXPERTS_PER_GROUP):
            out = jnp.where(idx == j, rows[j], out)
        return out

    w_lo, w_hi = take(w, lo), take(w, hi)
    den = w_lo + w_hi
    pair = jnp.where(lo == 0, hi - 1, jnp.where(lo == 1, hi + 1, len(PAIRS) - 1))
    cls = best * len(PAIRS) + pair

    onehot = (lax.broadcasted_iota(jnp.int32, (CLASS_ROWS, tm), 0) == cls).astype(F32)
    before = _dot(onehot.astype(BF16), tri_ref[...]) + carry_ref[:, 0:1]
    rank = jnp.sum(onehot * before, axis=0, keepdims=True)
    carry_ref[...] = carry_ref[...] + jnp.sum(onehot, axis=1, keepdims=True)

    cls_ref[0] = cls
    rank_ref[0] = rank.astype(jnp.int32)
    cnt_ref[...] = carry_ref[...]
    gates = jnp.concatenate([w_lo / den, w_hi / den, jnp.zeros((LANES - 2, tm), F32)], axis=0)
    xa_ref[:, :D_MODEL] = x
    xa_ref[:, D_MODEL:] = gates.T


def _router(x1, wh, wl, rb, tri, tm=512):
    nblk = N_TOK // tm
    const = lambda shape: pl.BlockSpec(shape, lambda i: (0, 0))
    row3 = pl.BlockSpec((1, 1, tm), lambda i: (i, 0, 0))
    xa, cls, rank, cnt = pl.pallas_call(
        _router_kernel,
        grid=(nblk,),
        in_specs=[pl.BlockSpec((tm, D_MODEL), lambda i: (i, 0)), const((N_EXPERTS, D_MODEL)),
                  const((N_EXPERTS, D_MODEL)), const((N_EXPERTS, 1)), const((tm, tm))],
        out_specs=[pl.BlockSpec((tm, XA_W), lambda i: (i, 0)), row3, row3, const((CLASS_ROWS, LANES))],
        out_shape=[jax.ShapeDtypeStruct((N_TOK, XA_W), F32), jax.ShapeDtypeStruct((nblk, 1, tm), jnp.int32),
                   jax.ShapeDtypeStruct((nblk, 1, tm), jnp.int32), jax.ShapeDtypeStruct((CLASS_ROWS, LANES), F32)],
        scratch_shapes=[pltpu.VMEM((CLASS_ROWS, LANES), F32)],
        compiler_params=_params(("arbitrary",)),
        name="router",
    )(x1, wh, wl, rb, tri)
    return xa, cls.reshape(N_TOK), rank.reshape(N_TOK), cnt[:N_CLASSES, 0].astype(jnp.int32)


def _dispatch_kernel(pos_ref, xa_ref, init_ref, xs_ref, sem, *, chunk):
    del init_ref
    base = pl.program_id(0) * chunk

    def issue(r, carry):
        t = base + r
        pltpu.make_async_copy(xa_ref.at[pl.ds(t, 1)], xs_ref.at[pl.ds(pos_ref[t], 1)], sem).start()
        return carry

    def retire(r, carry):
        pltpu.make_async_copy(xa_ref.at[pl.ds(0, 1)], xs_ref.at[pl.ds(0, 1)], sem).wait()
        return carry

    lax.fori_loop(0, chunk, issue, 0)
    lax.fori_loop(0, chunk, retire, 0)


def _dispatch(pos, xa, chunk=1024):
    init = jnp.zeros((MOE_ROWS, XA_W), F32)
    return pl.pallas_call(
        functools.partial(_dispatch_kernel, chunk=chunk),
        grid_spec=pltpu.PrefetchScalarGridSpec(
            num_scalar_prefetch=1,
            grid=(N_TOK // chunk,),
            in_specs=[pl.BlockSpec(memory_space=pl.ANY), pl.BlockSpec(memory_space=pl.ANY)],
            out_specs=pl.BlockSpec(memory_space=pl.ANY),
            scratch_shapes=[pltpu.SemaphoreType.DMA(())],
        ),
        out_shape=jax.ShapeDtypeStruct((MOE_ROWS, XA_W), F32),
        input_output_aliases={2: 0},
        compiler_params=_params(("arbitrary",)),
        name="moe_dispatch",
    )(pos, xa, init)


def _expert_kernel(e1_ref, e2_ref, used_ref, xs_ref, wg1_ref, wu1_ref, wd1_ref, wg2_ref, wu2_ref, wd2_ref, ys_ref):
    i = pl.program_id(0)

    @pl.when(used_ref[i] != 0)
    def _():
        x = xs_ref[:, :D_MODEL].astype(BF16)
        gates = xs_ref[:, D_MODEL:]

        def expert(wg_ref, wu_ref, wd_ref):
            gate = _dot(x, wg_ref[0])
            h = gate * (1.0 / (1.0 + jnp.exp(-gate))) * _dot(x, wu_ref[0])
            return _dot(h.astype(BF16), wd_ref[0])

        ys_ref[...] = (gates[:, 0:1] * expert(wg1_ref, wu1_ref, wd1_ref)
                       + gates[:, 1:2] * expert(wg2_ref, wu2_ref, wd2_ref))

    @pl.when(used_ref[i] == 0)
    def _():
        ys_ref[...] = jnp.zeros_like(ys_ref)


def _experts(tile_e1, tile_e2, tile_used, xs, wg, wu, wd):
    up = lambda sel: pl.BlockSpec((1, D_MODEL, D_EXPERT), lambda i, e1, e2, u: ((e1, e2)[sel][i], 0, 0))
    down = lambda sel: pl.BlockSpec((1, D_EXPERT, D_MODEL), lambda i, e1, e2, u: ((e1, e2)[sel][i], 0, 0))
    return pl.pallas_call(
        _expert_kernel,
        grid_spec=pltpu.PrefetchScalarGridSpec(
            num_scalar_prefetch=3,
            grid=(MOE_TILES,),
            in_specs=[pl.BlockSpec((MOE_TM, XA_W), lambda i, e1, e2, u: (i, 0)),
                      up(0), up(0), down(0), up(1), up(1), down(1)],
            out_specs=pl.BlockSpec((MOE_TM, D_MODEL), lambda i, e1, e2, u: (i, 0)),
        ),
        out_shape=jax.ShapeDtypeStruct((MOE_ROWS, D_MODEL), F32),
        compiler_params=_params(("arbitrary",)),
        name="moe_experts",
    )(tile_e1, tile_e2, tile_used, xs, wg, wu, wd, wg, wu, wd)


def _combine_kernel(pos_ref, ys_ref, x_ref, g_ref, b_ref, o_ref, buf, sem, *, tc):
    base = pl.program_id(0) * tc

    def issue(r, carry):
        pltpu.make_async_copy(ys_ref.at[pl.ds(pos_ref[base + r], 1)], buf.at[pl.ds(r, 1)], sem).start()
        return carry

    def retire(r, carry):
        pltpu.make_async_copy(ys_ref.at[pl.ds(0, 1)], buf.at[pl.ds(0, 1)], sem).wait()
        return carry

    lax.fori_loop(0, tc, issue, 0)
    lax.fori_loop(0, tc, retire, 0)
    o_ref[...] = _layer_norm(ALPHA * x_ref[...] + buf[...], g_ref[...], b_ref[...])


def _combine_ln(pos, ys, xa, g, b, tc=256):
    return pl.pallas_call(
        functools.partial(_combine_kernel, tc=tc),
        grid_spec=pltpu.PrefetchScalarGridSpec(
            num_scalar_prefetch=1,
            grid=(N_TOK // tc,),
            in_specs=[pl.BlockSpec(memory_space=pl.ANY),
                      pl.BlockSpec((tc, D_MODEL), lambda i, p: (i, 0)),
                      pl.BlockSpec((1, D_MODEL), lambda i, p: (0, 0)),
                      pl.BlockSpec((1, D_MODEL), lambda i, p: (0, 0))],
            out_specs=pl.BlockSpec((tc, D_MODEL), lambda i, p: (i, 0)),
            scratch_shapes=[pltpu.VMEM((tc, D_MODEL), F32), pltpu.SemaphoreType.DMA(())],
        ),
        out_shape=jax.ShapeDtypeStruct((N_TOK, D_MODEL), F32),
        compiler_params=_params(("arbitrary",)),
        name="moe_combine_ln",
    )(pos, ys, xa, g, b)


_CLASS_E1 = np.array([EXPERTS_PER_GROUP * g + p[0] for g in range(N_EXPERT_GROUPS) for p in PAIRS], np.int32)
_CLASS_E2 = np.array([EXPERTS_PER_GROUP * g + p[1] for g in range(N_EXPERT_GROUPS) for p in PAIRS], np.int32)


def _moe_ln(x1, router, wg, wu, wd, g, b):
    wh, wl, rb, tri = router
    xa, cls, rank, counts = _router(x1, wh, wl, rb, tri)
    padded = (counts + MOE_TM - 1) // MOE_TM * MOE_TM
    ends = jnp.cumsum(padded)
    starts = ends - padded
    pos = starts[cls] + rank
    tile_start = jnp.arange(MOE_TILES, dtype=jnp.int32) * MOE_TM
    tile_used = (tile_start < ends[-1]).astype(jnp.int32)
    last_cls = jnp.max(jnp.where(counts > 0, jnp.arange(N_CLASSES), 0))
    tile_cls = jnp.minimum(jnp.sum(tile_start[:, None] >= ends[None, :], axis=1), last_cls)
    tile_e1 = jnp.asarray(_CLASS_E1)[tile_cls]
    tile_e2 = jnp.asarray(_CLASS_E2)[tile_cls]
    xs = _dispatch(pos, xa)
    ys = _experts(tile_e1, tile_e2, tile_used, xs, wg, wu, wd)
    return _combine_ln(pos, ys, xa, g, b)


def _tile_heads(t64):
    return jnp.concatenate([t64, t64], axis=-1)


def _axial_tables():
    rows = SEQ // GRID_W
    row = jnp.repeat(jnp.arange(rows), GRID_W).astype(F32)
    col = jnp.tile(jnp.arange(GRID_W), rows).astype(F32)
    half = HEAD_DIM // 2
    inv = 1.0 / (AXIAL_THETA ** (jnp.arange(0, half, 2, dtype=F32) / half))
    ar, ac = row[:, None] * inv, col[:, None] * inv
    z = jnp.zeros_like(ar)
    c = jnp.concatenate([jnp.cos(ar), jnp.cos(ar), jnp.cos(ac), jnp.cos(ac)], -1)
    sa = jnp.concatenate([-jnp.sin(ar), z, -jnp.sin(ac), z], -1)
    sb = jnp.concatenate([z, jnp.sin(ar), z, jnp.sin(ac)], -1)
    return tuple(_tile_heads(t) for t in (c, sa, sb))


def _partial_tables():
    n = PARTIAL_ROT_DIMS
    pos = jnp.arange(SEQ, dtype=F32)
    ang = pos[:, None] * (1.0 / (PARTIAL_THETA ** (jnp.arange(0, n, 2, dtype=F32) / n)))
    z = jnp.zeros_like(ang)
    rest = HEAD_DIM - n
    c = jnp.concatenate([jnp.cos(ang), jnp.cos(ang), jnp.ones((SEQ, rest), F32)], -1)
    sa = jnp.concatenate([-jnp.sin(ang), z, jnp.zeros((SEQ, rest), F32)], -1)
    sb = jnp.concatenate([z, jnp.sin(ang), jnp.zeros((SEQ, rest), F32)], -1)
    return tuple(_tile_heads(t) for t in (c, sa, sb))


def _residue_order(tab, d):
    return tab.reshape(SEQ // d, d, LANES).transpose(1, 0, 2)


def kernel(x, mem, w_mem_kv, router_w, router_b, a_w_in, a_w_out, a_q_norm, a_k_norm, b_w_in, b_w_out,
           ln1_g, ln1_b, ln2_g, ln2_b, w_gate, w_up, w_down):
    x = x.reshape(N_TOK, D_MODEL)
    row = lambda v: v.reshape(1, -1)

    lane = np.arange(LANES)
    m128 = jnp.asarray((lane[:, None] // HEAD_DIM == lane[None, :] // HEAD_DIM), BF16)
    col = np.arange(B_HEADS * HEAD_DIM)
    expand = jnp.asarray(lane[:, None] == col[None, :] // HEAD_DIM, BF16)
    tri = jnp.asarray(np.triu(np.ones((512, 512), np.float32), 1), BF16)
    rw_hi = router_w.T.astype(BF16)
    rw_lo = (router_w.T - rw_hi.astype(F32)).astype(BF16)
    router = (rw_hi, rw_lo, router_b.reshape(N_EXPERTS, 1), tri)

    mkv = _proj(mem.reshape(BATCH * N_MEM, D_MODEL), w_mem_kv.astype(BF16), 0, 2 * MEM_W, 1.0)

    qg = row(jnp.tile(a_q_norm[0], 2) * SCALE)
    kg = row(jnp.tile(a_k_norm[0], 2))
    q, k, v, qm = _a_in_proj(x, a_w_in[0].astype(BF16), m128, qg, kg, _axial_tables())
    mix = _gqa_attention(q, k, v)
    memo = _mem_attention(qm, mkv)
    w_out = a_w_out[0].astype(BF16)
    kmix = A_Q_HEADS * HEAD_DIM
    x = _out_proj_ln(mix, memo, w_out[:kmix], w_out[kmix:], x, row(ln1_g[0]), row(ln1_b[0]))
    x = _moe_ln(x, router, w_gate[0].astype(BF16), w_up[0].astype(BF16), w_down[0].astype(BF16),
                row(ln2_g[0]), row(ln2_b[0]))

    w_in = b_w_in[0].astype(BF16)
    ptabs = _partial_tables()
    outs, lses = [], []
    for g, d in enumerate(DILATIONS):
        tabs = tuple(_residue_order(t, d) for t in ptabs)
        qb, kb, vb = _b_in_proj(x, w_in, g, d, tabs)
        o, lse = _band_attention(qb, kb, vb, d)
        outs.append(o)
        lses.append(lse)
    mix = _merge_groups(outs, lses, expand)
    qm = _proj(x, w_in, (3 * len(DILATIONS) * B_HEADS * HEAD_DIM) // MEM_W, MEM_W, SCALE)
    memo = _mem_attention(qm, mkv)
    w_out = b_w_out[0].astype(BF16)
    kmix = B_HEADS * HEAD_DIM
    x = _out_proj_ln(mix, memo, w_out[:kmix], w_out[kmix:], x, row(ln1_g[1]), row(ln1_b[1]))
    x = _moe_ln(x, router, w_gate[1].astype(BF16), w_up[1].astype(BF16), w_down[1].astype(BF16),
                row(ln2_g[1]), row(ln2_b[1]))
    return x.reshape(BATCH, SEQ, D_MODEL)
```

```python
import functools

import numpy as np
import jax
import jax.numpy as jnp
from jax import lax
from jax.experimental import pallas as pl
from jax.experimental.pallas import tpu as pltpu

F32 = jnp.float32
BF16 = jnp.bfloat16

D_MODEL = 1024
BATCH = 8
SEQ = 2048
N_TOK = BATCH * SEQ
HEAD_DIM = 64
GRID_W = 64
N_MEM = 256
MEM_HEADS = 4
A_Q_HEADS = 12
A_KV_HEADS = 4
AXIAL_THETA = 10000.0
B_HEADS = 8
DILATIONS = (1, 4, 16)
BAND_RADIUS = 64
PARTIAL_ROT_DIMS = HEAD_DIM // 4
PARTIAL_THETA = 500000.0
N_EXPERTS = 16
N_EXPERT_GROUPS = 4
EXPERTS_PER_GROUP = 4
D_EXPERT = 512
DEPTH = 2
ALPHA = (2 * DEPTH) ** 0.25
NORM_EPS = 1e-6
NEG_BIG = -1e30
SCALE = HEAD_DIM ** -0.5

LANES = 128
MEM_W = MEM_HEADS * HEAD_DIM

PAIRS = ((0, 1), (0, 2), (0, 3), (1, 2), (1, 3), (2, 3))
N_CLASSES = N_EXPERT_GROUPS * len(PAIRS)
CLASS_ROWS = 32
MOE_TM = 256
MOE_ROWS = N_TOK + N_CLASSES * MOE_TM
MOE_TILES = MOE_ROWS // MOE_TM
XA_W = D_MODEL + LANES

VMEM_LIMIT = 56 * 1024 * 1024


def _params(sem, vmem=VMEM_LIMIT):
    return pltpu.CompilerParams(dimension_semantics=sem, vmem_limit_bytes=vmem)


def _dot(a, b):
    return jnp.dot(a, b, preferred_element_type=F32)


def _dot_nt(a, b):
    return lax.dot_general(a, b, (((1,), (1,)), ((), ())), preferred_element_type=F32)


def _split_bf16(x):
    hi = x.astype(BF16)
    lo = (x - hi.astype(F32)).astype(BF16)
    return hi, lo


def _layer_norm(z, g, b):
    mu = jnp.mean(z, axis=-1, keepdims=True)
    zc = z - mu
    var = jnp.mean(zc * zc, axis=-1, keepdims=True)
    return zc * lax.rsqrt(var + NORM_EPS) * g + b


def _rope(z, c, sa, sb, shift):
    return z * c + pltpu.roll(z, LANES - shift, 1) * sa + pltpu.roll(z, shift, 1) * sb


def _proj_kernel(x_ref, w_ref, o_ref, *, scale):
    y = _dot(x_ref[...].astype(BF16), w_ref[...])
    o_ref[...] = (y * scale).astype(o_ref.dtype)


def _proj(x, w, col_block, width, scale, tm=512):
    n, k = x.shape
    return pl.pallas_call(
        functools.partial(_proj_kernel, scale=scale),
        grid=(n // tm,),
        in_specs=[pl.BlockSpec((tm, k), lambda i: (i, 0)),
                  pl.BlockSpec((k, width), lambda i: (0, col_block))],
        out_specs=pl.BlockSpec((tm, width), lambda i: (i, 0)),
        out_shape=jax.ShapeDtypeStruct((n, width), BF16),
        compiler_params=_params(("parallel",)),
        name="proj",
    )(x, w)


def _a_in_kernel(x_ref, w_ref, m_ref, qg_ref, kg_ref, c_ref, sa_ref, sb_ref, q_ref, k_ref, v_ref, qm_ref):
    y = _dot(x_ref[...].astype(BF16), w_ref[...])
    m = m_ref[...]
    c, sa, sb = c_ref[...], sa_ref[...], sb_ref[...]

    def norm_rope(z, gain):
        hi, lo = _split_bf16(z * z)
        ms = (_dot(hi, m) + _dot(lo, m)) * (1.0 / HEAD_DIM)
        z = z * lax.rsqrt(ms + NORM_EPS) * gain
        return _rope(z, c, sa, sb, 16)

    nq = A_Q_HEADS * HEAD_DIM // LANES
    nk = A_KV_HEADS * HEAD_DIM // LANES
    for ch in range(nq):
        q_ref[:, ch * LANES:(ch + 1) * LANES] = norm_rope(y[:, ch * LANES:(ch + 1) * LANES], qg_ref[...]).astype(BF16)
    off = nq * LANES
    for ch in range(nk):
        z = y[:, off + ch * LANES: off + (ch + 1) * LANES]
        k_ref[:, ch * LANES:(ch + 1) * LANES] = norm_rope(z, kg_ref[...]).astype(BF16)
    off += nk * LANES
    v_ref[...] = y[:, off:off + nk * LANES].astype(BF16)
    off += nk * LANES
    qm_ref[...] = (y[:, off:off + MEM_W] * SCALE).astype(BF16)


def _a_in_proj(x, w, m128, qg, kg, tabs, tm=512):
    c, sa, sb = tabs
    nblk = SEQ // tm
    a_in = w.shape[1]
    tab_spec = pl.BlockSpec((tm, LANES), lambda i: (i % nblk, 0))
    row = lambda width: pl.BlockSpec((tm, width), lambda i: (i, 0))
    const = lambda shape: pl.BlockSpec(shape, lambda i: (0, 0))
    qw, kw = A_Q_HEADS * HEAD_DIM, A_KV_HEADS * HEAD_DIM
    return pl.pallas_call(
        _a_in_kernel,
        grid=(N_TOK // tm,),
        in_specs=[row(D_MODEL), const((D_MODEL, a_in)), const((LANES, LANES)), const((1, LANES)), const((1, LANES)),
                  tab_spec, tab_spec, tab_spec],
        out_specs=[row(qw), row(kw), row(kw), row(MEM_W)],
        out_shape=[jax.ShapeDtypeStruct((N_TOK, qw), BF16), jax.ShapeDtypeStruct((N_TOK, kw), BF16),
                   jax.ShapeDtypeStruct((N_TOK, kw), BF16), jax.ShapeDtypeStruct((N_TOK, MEM_W), BF16)],
        compiler_params=_params(("parallel",)),
        name="a_in_proj",
    )(x, w, m128, qg, kg, c, sa, sb)


def _b_in_kernel(x_ref, wq_ref, wk_ref, wv_ref, c_ref, sa_ref, sb_ref, q_ref, k_ref, v_ref):
    xb = x_ref[0].astype(BF16)
    c, sa, sb = c_ref[0], sa_ref[0], sb_ref[0]
    width = B_HEADS * HEAD_DIM
    yq = _dot(xb, wq_ref[...]) * SCALE
    yk = _dot(xb, wk_ref[...])
    for ch in range(width // LANES):
        sl = slice(ch * LANES, (ch + 1) * LANES)
        q_ref[0, 0, :, sl] = _rope(yq[:, sl], c, sa, sb, PARTIAL_ROT_DIMS // 2).astype(BF16)
        k_ref[0, 0, :, sl] = _rope(yk[:, sl], c, sa, sb, PARTIAL_ROT_DIMS // 2).astype(BF16)
    v_ref[0, 0] = _dot(xb, wv_ref[...]).astype(BF16)


def _b_in_proj(x, w, g, d, tabs):
    length = SEQ // d
    tl = min(length, 512)
    width = B_HEADS * HEAD_DIM
    xv = x.reshape(BATCH, length, d * D_MODEL)
    c, sa, sb = tabs
    tab_spec = pl.BlockSpec((1, tl, LANES), lambda b, r, i: (r, i, 0))
    w_spec = lambda blk: pl.BlockSpec((D_MODEL, width), lambda b, r, i: (0, blk))
    o_spec = pl.BlockSpec((1, 1, tl, width), lambda b, r, i: (b, r, i, 0))
    o_shape = jax.ShapeDtypeStruct((BATCH, d, length, width), BF16)
    q, k, v = pl.pallas_call(
        _b_in_kernel,
        grid=(BATCH, d, length // tl),
        in_specs=[pl.BlockSpec((1, tl, D_MODEL), lambda b, r, i: (b, i, r)),
                  w_spec(g), w_spec(3 + g), w_spec(6 + g), tab_spec, tab_spec, tab_spec],
        out_specs=[o_spec, o_spec, o_spec],
        out_shape=[o_shape, o_shape, o_shape],
        compiler_params=_params(("parallel", "parallel", "parallel")),
        name=f"b_in_proj_d{d}",
    )(xv, w, w, w, c, sa, sb)
    return q, k, v


def _softmax_pv(s, v):
    m = jnp.max(s, axis=1, keepdims=True)
    p = jnp.exp(s - m)
    l = jnp.sum(p, axis=1, keepdims=True)
    return _dot(p.astype(BF16), v), m, l


def _gqa_kernel(q_ref, k_ref, v_ref, o_ref):
    tq = q_ref.shape[0]
    group = A_Q_HEADS // A_KV_HEADS
    for j in range(2):
        kj = k_ref[:, j * HEAD_DIM:(j + 1) * HEAD_DIM]
        vj = v_ref[:, j * HEAD_DIM:(j + 1) * HEAD_DIM]
        heads = [group * j + g for g in range(group)]
        qs = jnp.concatenate([q_ref[:, h * HEAD_DIM:(h + 1) * HEAD_DIM] for h in heads], axis=0)
        o, _, l = _softmax_pv(_dot_nt(qs, kj), vj)
        o = o / l
        for g, h in enumerate(heads):
            o_ref[:, h * HEAD_DIM:(h + 1) * HEAD_DIM] = o[g * tq:(g + 1) * tq].astype(BF16)


def _gqa_attention(q, k, v, tq=128):
    nq = SEQ // tq
    qw = q.shape[1] // 2
    kw = k.shape[1] // 2
    return pl.pallas_call(
        _gqa_kernel,
        grid=(BATCH, 2, nq),
        in_specs=[pl.BlockSpec((tq, qw), lambda b, p, i: (b * nq + i, p)),
                  pl.BlockSpec((SEQ, kw), lambda b, p, i: (b, p)),
                  pl.BlockSpec((SEQ, kw), lambda b, p, i: (b, p))],
        out_specs=pl.BlockSpec((tq, qw), lambda b, p, i: (b * nq + i, p)),
        out_shape=jax.ShapeDtypeStruct(q.shape, BF16),
        compiler_params=_params(("parallel", "parallel", "parallel")),
        name="gqa_attention",
    )(q, k, v)


def _mem_attn_kernel(q_ref, k_ref, v_ref, o_ref):
    for h in range(MEM_HEADS):
        sl = slice(h * HEAD_DIM, (h + 1) * HEAD_DIM)
        o, _, l = _softmax_pv(_dot_nt(q_ref[:, sl], k_ref[:, sl]), v_ref[:, sl])
        o_ref[:, sl] = (o / l).astype(BF16)


def _mem_attention(qm, mkv, tq=512):
    nq = SEQ // tq
    return pl.pallas_call(
        _mem_attn_kernel,
        grid=(BATCH, nq),
        in_specs=[pl.BlockSpec((tq, MEM_W), lambda b, i: (b * nq + i, 0)),
                  pl.BlockSpec((N_MEM, MEM_W), lambda b, i: (b, 0)),
                  pl.BlockSpec((N_MEM, MEM_W), lambda b, i: (b, 1))],
        out_specs=pl.BlockSpec((tq, MEM_W), lambda b, i: (b * nq + i, 0)),
        out_shape=jax.ShapeDtypeStruct((N_TOK, MEM_W), BF16),
        compiler_params=_params(("parallel", "parallel")),
        name="mem_attention",
    )(qm, mkv, mkv)


def _band_kernel(q_ref, k_ref, v_ref, o_ref, lse_ref, *, length, tq, win):
    i = pl.program_id(1)
    start = jnp.clip(i * tq - BAND_RADIUS, 0, length - win)
    start = pl.multiple_of(start, BAND_RADIUS)
    qpos = i * tq + lax.broadcasted_iota(jnp.int32, (tq, win), 0)
    kpos = start + lax.broadcasted_iota(jnp.int32, (tq, win), 1)
    valid = jnp.abs(qpos - kpos) <= BAND_RADIUS
    lane = lax.broadcasted_iota(jnp.int32, (tq, LANES), 1)
    lse_all = jnp.zeros((tq, LANES), F32)
    for h in range(B_HEADS):
        sl = slice(h * HEAD_DIM, (h + 1) * HEAD_DIM)
        kh = k_ref[0, pl.ds(start, win), sl]
        vh = v_ref[0, pl.ds(start, win), sl]
        s = jnp.where(valid, _dot_nt(q_ref[0, :, sl], kh), NEG_BIG)
        o, m, l = _softmax_pv(s, vh)
        o_ref[0, :, sl] = (o / l).astype(BF16)
        lse_all = jnp.where(lane == h, m + jnp.log(l), lse_all)
    lse_ref[0] = lse_all


def _band_attention(q, k, v, d):
    length = SEQ // d
    tq = min(length, 128)
    win = min(length, tq + 2 * BAND_RADIUS)
    width = B_HEADS * HEAD_DIM
    nseq = BATCH * d
    q, k, v = (t.reshape(nseq, length, width) for t in (q, k, v))
    seq_spec = pl.BlockSpec((1, length, width), lambda s, i: (s, 0, 0))
    o, lse = pl.pallas_call(
        functools.partial(_band_kernel, length=length, tq=tq, win=win),
        grid=(nseq, length // tq),
        in_specs=[pl.BlockSpec((1, tq, width), lambda s, i: (s, i, 0)), seq_spec, seq_spec],
        out_specs=[pl.BlockSpec((1, tq, width), lambda s, i: (s, i, 0)),
                   pl.BlockSpec((1, tq, LANES), lambda s, i: (s, i, 0))],
        out_shape=[jax.ShapeDtypeStruct((nseq, length, width), BF16),
                   jax.ShapeDtypeStruct((nseq, length, LANES), F32)],
        compiler_params=_params(("parallel", "parallel")),
        name=f"band_attention_d{d}",
    )(q, k, v)
    return o, lse


def _merge_kernel(o0_ref, o1_ref, o2_ref, l0_ref, l1_ref, l2_ref, e_ref, o_ref):
    l0, l1, l2 = l0_ref[0], l1_ref[0, 0], l2_ref[0, 0]
    m = jnp.maximum(jnp.maximum(l0, l1), l2)
    e0, e1, e2 = jnp.exp(l0 - m), jnp.exp(l1 - m), jnp.exp(l2 - m)
    inv = 1.0 / (e0 + e1 + e2)
    expand = e_ref[...]

    def widen(w):
        hi, lo = _split_bf16(w)
        return _dot(hi, expand) + _dot(lo, expand)

    o = (widen(e0 * inv) * o0_ref[0].astype(F32) + widen(e1 * inv) * o1_ref[0, 0].astype(F32)
         + widen(e2 * inv) * o2_ref[0, 0].astype(F32))
    o_ref[0] = o.astype(BF16)


def _merge_groups(outs, lses, expand):
    width = B_HEADS * HEAD_DIM
    dmax = DILATIONS[-1]
    lmin = SEQ // dmax
    rep = dmax // DILATIONS[1]

    def views(t, w):
        t0 = t[0].reshape(BATCH, lmin, dmax * w)
        t1 = t[1].reshape(BATCH, DILATIONS[1], lmin, rep * w)
        t2 = t[2].reshape(BATCH, dmax, lmin, w)
        return t0, t1, t2

    def specs(w):
        return [pl.BlockSpec((1, lmin, w), lambda b, r: (b, 0, r)),
                pl.BlockSpec((1, 1, lmin, w), lambda b, r: (b, r % DILATIONS[1], 0, r // DILATIONS[1])),
                pl.BlockSpec((1, 1, lmin, w), lambda b, r: (b, r, 0, 0))]

    out = pl.pallas_call(
        _merge_kernel,
        grid=(BATCH, dmax),
        in_specs=specs(width) + specs(LANES) + [pl.BlockSpec((LANES, width), lambda b, r: (0, 0))],
        out_specs=pl.BlockSpec((1, lmin, width), lambda b, r: (b, 0, r)),
        out_shape=jax.ShapeDtypeStruct((BATCH, lmin, dmax * width), BF16),
        compiler_params=_params(("parallel", "parallel")),
        name="merge_groups",
    )(*views(outs, width), *views(lses, LANES), expand)
    return out.reshape(N_TOK, width)


def _out_ln_kernel(mix_ref, memo_ref, w1_ref, w2_ref, x_ref, g_ref, b_ref, o_ref):
    attn = _dot(mix_ref[...], w1_ref[...]) + _dot(memo_ref[...], w2_ref[...])
    o_ref[...] = _layer_norm(ALPHA * x_ref[...] + attn, g_ref[...], b_ref[...])


def _out_proj_ln(mix, memo, w1, w2, x, g, b, tm=512):
    kmix = mix.shape[1]
    row = lambda width: pl.BlockSpec((tm, width), lambda i: (i, 0))
    const = lambda shape: pl.BlockSpec(shape, lambda i: (0, 0))
    return pl.pallas_call(
        _out_ln_kernel,
        grid=(N_TOK // tm,),
        in_specs=[row(kmix), row(MEM_W), const((kmix, D_MODEL)), const((MEM_W, D_MODEL)), row(D_MODEL),
                  const((1, D_MODEL)), const((1, D_MODEL))],
        out_specs=row(D_MODEL),
        out_shape=jax.ShapeDtypeStruct((N_TOK, D_MODEL), F32),
        compiler_params=_params(("parallel",)),
        name="out_proj_ln",
    )(mix, memo, w1, w2, x, g, b)


def _router_kernel(x_ref, wh_ref, wl_ref, rb_ref, tri_ref, xa_ref, cls_ref, rank_ref, cnt_ref, carry_ref):
    step = pl.program_id(0)

    @pl.when(step == 0)
    def _():
        carry_ref[...] = jnp.zeros_like(carry_ref)

    x = x_ref[...]
    tm = x.shape[0]
    xh, xl = _split_bf16(x)
    logits = _dot_nt(wh_ref[...], xh) + _dot_nt(wh_ref[...], xl) + _dot_nt(wl_ref[...], xh)
    aff = 1.0 / (1.0 + jnp.exp(-logits))
    sel = aff + rb_ref[...]
    s = [sel[e:e + 1, :] for e in range(N_EXPERTS)]
    a = [aff[e:e + 1, :] for e in range(N_EXPERTS)]

    def top2_sum(v):
        hi01, lo01 = jnp.maximum(v[0], v[1]), jnp.minimum(v[0], v[1])
        hi23, lo23 = jnp.maximum(v[2], v[3]), jnp.minimum(v[2], v[3])
        return jnp.maximum(hi01, hi23) + jnp.maximum(jnp.minimum(hi01, hi23), jnp.maximum(lo01, lo23))

    gscore = [top2_sum(s[EXPERTS_PER_GROUP * g:EXPERTS_PER_GROUP * (g + 1)]) for g in range(N_EXPERT_GROUPS)]
    best = jnp.zeros((1, tm), jnp.int32)
    best_score = gscore[0]
    for g in range(1, N_EXPERT_GROUPS):
        better = gscore[g] > best_score
        best = jnp.where(better, g, best)
        best_score = jnp.where(better, gscore[g], best_score)

    def pick(rows, j):
        out = rows[j]
        for g in range(1, N_EXPERT_GROUPS):
            out = jnp.where(best == g, rows[EXPERTS_PER_GROUP * g + j], out)
        return out

    t = [pick(s, j) for j in range(EXPERTS_PER_GROUP)]
    w = [pick(a, j) for j in range(EXPERTS_PER_GROUP)]

    def first_max(v):
        mx = jnp.maximum(jnp.maximum(v[0], v[1]), jnp.maximum(v[2], v[3]))
        idx = jnp.full((1, tm), EXPERTS_PER_GROUP - 1, jnp.int32)
        for j in range(EXPERTS_PER_GROUP - 2, -1, -1):
            idx = jnp.where(v[j] == mx, j, idx)
        return idx

    i1 = first_max(t)
    i2 = first_max([jnp.where(i1 == j, -jnp.inf, t[j]) for j in range(EXPERTS_PER_GROUP)])
    lo, hi = jnp.minimum(i1, i2), jnp.maximum(i1, i2)

    def take(rows, idx):
        out = rows[0]
        for j in range(1, EXPERTS_PER_GROUP):
            out = jnp.where(idx == j, rows[j], out)
        return out

    w_lo, w_hi = take(w, lo), take(w, hi)
    den = w_lo + w_hi
    pair = jnp.where(lo == 0, hi - 1, jnp.where(lo == 1, hi + 1, len(PAIRS) - 1))
    cls = best * len(PAIRS) + pair

    onehot = (lax.broadcasted_iota(jnp.int32, (CLASS_ROWS, tm), 0) == cls).astype(F32)
    before = _dot(onehot.astype(BF16), tri_ref[...]) + carry_ref[:, 0:1]
    rank = jnp.sum(onehot * before, axis=0, keepdims=True)
    carry_ref[...] = carry_ref[...] + jnp.sum(onehot, axis=1, keepdims=True)

    cls_ref[0] = cls
    rank_ref[0] = rank.astype(jnp.int32)
    cnt_ref[...] = carry_ref[...]
    gates = jnp.concatenate([w_lo / den, w_hi / den, jnp.zeros((LANES - 2, tm), F32)], axis=0)
    xa_ref[:, :D_MODEL] = x
    xa_ref[:, D_MODEL:] = gates.T


def _router(x1, wh, wl, rb, tri, tm=512):
    nblk = N_TOK // tm
    const = lambda shape: pl.BlockSpec(shape, lambda i: (0, 0))
    row3 = pl.BlockSpec((1, 1, tm), lambda i: (i, 0, 0))
    xa, cls, rank, cnt = pl.pallas_call(
        _router_kernel,
        grid=(nblk,),
        in_specs=[pl.BlockSpec((tm, D_MODEL), lambda i: (i, 0)), const((N_EXPERTS, D_MODEL)),
                  const((N_EXPERTS, D_MODEL)), const((N_EXPERTS, 1)), const((tm, tm))],
        out_specs=[pl.BlockSpec((tm, XA_W), lambda i: (i, 0)), row3, row3, const((CLASS_ROWS, LANES))],
        out_shape=[jax.ShapeDtypeStruct((N_TOK, XA_W), F32), jax.ShapeDtypeStruct((nblk, 1, tm), jnp.int32),
                   jax.ShapeDtypeStruct((nblk, 1, tm), jnp.int32), jax.ShapeDtypeStruct((CLASS_ROWS, LANES), F32)],
        scratch_shapes=[pltpu.VMEM((CLASS_ROWS, LANES), F32)],
        compiler_params=_params(("arbitrary",)),
        name="router",
    )(x1, wh, wl, rb, tri)
    return xa, cls.reshape(N_TOK), rank.reshape(N_TOK), cnt[:N_CLASSES, 0].astype(jnp.int32)


def _dispatch_kernel(pos_ref, xa_ref, init_ref, xs_ref, sem, *, chunk):
    del init_ref
    base = pl.program_id(0) * chunk

    def issue(r, carry):
        pltpu.make_async_copy(xa_ref.at[pl.ds(r, 1)], xs_ref.at[pl.ds(pos_ref[base + r], 1)], sem).start()
        return carry

    def retire(r, carry):
        pltpu.make_async_copy(xa_ref.at[pl.ds(0, 1)], xs_ref.at[pl.ds(0, 1)], sem).wait()
        return carry

    lax.fori_loop(0, chunk, issue, 0, unroll=8)
    lax.fori_loop(0, chunk, retire, 0, unroll=8)


def _dispatch(pos, xa, chunk=512):
    init = jnp.zeros((MOE_ROWS, XA_W), F32)
    return pl.pallas_call(
        functools.partial(_dispatch_kernel, chunk=chunk),
        grid_spec=pltpu.PrefetchScalarGridSpec(
            num_scalar_prefetch=1,
            grid=(N_TOK // chunk,),
            in_specs=[pl.BlockSpec((chunk, XA_W), lambda i, p: (i, 0)), pl.BlockSpec(memory_space=pl.ANY)],
            out_specs=pl.BlockSpec(memory_space=pl.ANY),
            scratch_shapes=[pltpu.SemaphoreType.DMA(())],
        ),
        out_shape=jax.ShapeDtypeStruct((MOE_ROWS, XA_W), F32),
        input_output_aliases={2: 0},
        compiler_params=_params(("arbitrary",)),
        name="moe_dispatch",
    )(pos, xa, init)


def _expert_kernel(e1_ref, e2_ref, used_ref, xs_ref, wg1_ref, wu1_ref, wd1_ref, wg2_ref, wu2_ref, wd2_ref, ys_ref):
    i = pl.program_id(0)

    @pl.when(used_ref[i] != 0)
    def _():
        x = xs_ref[:, :D_MODEL].astype(BF16)
        gates = xs_ref[:, D_MODEL:]

        def expert(wg_ref, wu_ref, wd_ref):
            gate = _dot(x, wg_ref[0])
            h = gate * (1.0 / (1.0 + jnp.exp(-gate))) * _dot(x, wu_ref[0])
            return _dot(h.astype(BF16), wd_ref[0])

        ys_ref[...] = (gates[:, 0:1] * expert(wg1_ref, wu1_ref, wd1_ref)
                       + gates[:, 1:2] * expert(wg2_ref, wu2_ref, wd2_ref))

    @pl.when(used_ref[i] == 0)
    def _():
        ys_ref[...] = jnp.zeros_like(ys_ref)


def _experts(tile_e1, tile_e2, tile_used, xs, wg, wu, wd):
    up = lambda sel: pl.BlockSpec((1, D_MODEL, D_EXPERT), lambda i, e1, e2, u: ((e1, e2)[sel][i], 0, 0))
    down = lambda sel: pl.BlockSpec((1, D_EXPERT, D_MODEL), lambda i, e1, e2, u: ((e1, e2)[sel][i], 0, 0))
    return pl.pallas_call(
        _expert_kernel,
        grid_spec=pltpu.PrefetchScalarGridSpec(
            num_scalar_prefetch=3,
            grid=(MOE_TILES,),
            in_specs=[pl.BlockSpec((MOE_TM, XA_W), lambda i, e1, e2, u: (i, 0)),
                      up(0), up(0), down(0), up(1), up(1), down(1)],
            out_specs=pl.BlockSpec((MOE_TM, D_MODEL), lambda i, e1, e2, u: (i, 0)),
        ),
        out_shape=jax.ShapeDtypeStruct((MOE_ROWS, D_MODEL), F32),
        compiler_params=_params(("arbitrary",)),
        name="moe_experts",
    )(tile_e1, tile_e2, tile_used, xs, wg, wu, wd, wg, wu, wd)


def _combine_kernel(pos_ref, ys_ref, x_ref, g_ref, b_ref, o_ref, buf, sem, *, tc):
    i = pl.program_id(0)
    slot = i % 2

    def gather(step, into):
        def issue(r, carry):
            row = pos_ref[step * tc + r]
            pltpu.make_async_copy(ys_ref.at[pl.ds(row, 1)], buf.at[into, pl.ds(r, 1)], sem.at[into]).start()
            return carry

        lax.fori_loop(0, tc, issue, 0, unroll=8)

    @pl.when(i == 0)
    def _():
        gather(0, 0)

    @pl.when(i + 1 < pl.num_programs(0))
    def _():
        gather(i + 1, 1 - slot)

    def retire(r, carry):
        pltpu.make_async_copy(ys_ref.at[pl.ds(0, 1)], buf.at[slot, pl.ds(0, 1)], sem.at[slot]).wait()
        return carry

    lax.fori_loop(0, tc, retire, 0, unroll=8)
    o_ref[...] = _layer_norm(ALPHA * x_ref[...] + buf[slot], g_ref[...], b_ref[...])


def _combine_ln(pos, ys, xa, g, b, tc=256):
    return pl.pallas_call(
        functools.partial(_combine_kernel, tc=tc),
        grid_spec=pltpu.PrefetchScalarGridSpec(
            num_scalar_prefetch=1,
            grid=(N_TOK // tc,),
            in_specs=[pl.BlockSpec(memory_space=pl.ANY),
                      pl.BlockSpec((tc, D_MODEL), lambda i, p: (i, 0)),
                      pl.BlockSpec((1, D_MODEL), lambda i, p: (0, 0)),
                      pl.BlockSpec((1, D_MODEL), lambda i, p: (0, 0))],
            out_specs=pl.BlockSpec((tc, D_MODEL), lambda i, p: (i, 0)),
            scratch_shapes=[pltpu.VMEM((2, tc, D_MODEL), F32), pltpu.SemaphoreType.DMA((2,))],
        ),
        out_shape=jax.ShapeDtypeStruct((N_TOK, D_MODEL), F32),
        compiler_params=_params(("arbitrary",)),
        name="moe_combine_ln",
    )(pos, ys, xa, g, b)


_CLASS_E1 = np.array([EXPERTS_PER_GROUP * g + p[0] for g in range(N_EXPERT_GROUPS) for p in PAIRS], np.int32)
_CLASS_E2 = np.array([EXPERTS_PER_GROUP * g + p[1] for g in range(N_EXPERT_GROUPS) for p in PAIRS], np.int32)


def _moe_ln(x1, router, wg, wu, wd, g, b):
    wh, wl, rb, tri = router
    xa, cls, rank, counts = _router(x1, wh, wl, rb, tri)
    padded = (counts + MOE_TM - 1) // MOE_TM * MOE_TM
    ends = jnp.cumsum(padded)
    starts = ends - padded
    pos = starts[cls] + rank
    tile_start = jnp.arange(MOE_TILES, dtype=jnp.int32) * MOE_TM
    tile_used = (tile_start < ends[-1]).astype(jnp.int32)
    last_cls = jnp.max(jnp.where(counts > 0, jnp.arange(N_CLASSES), 0))
    tile_cls = jnp.minimum(jnp.sum(tile_start[:, None] >= ends[None, :], axis=1), last_cls)
    tile_e1 = jnp.asarray(_CLASS_E1)[tile_cls]
    tile_e2 = jnp.asarray(_CLASS_E2)[tile_cls]
    xs = _dispatch(pos, xa)
    ys = _experts(tile_e1, tile_e2, tile_used, xs, wg, wu, wd)
    return _combine_ln(pos, ys, xa, g, b)


def _tile_heads(t64):
    return jnp.concatenate([t64, t64], axis=-1)


def _axial_tables():
    rows = SEQ // GRID_W
    row = jnp.repeat(jnp.arange(rows), GRID_W).astype(F32)
    col = jnp.tile(jnp.arange(GRID_W), rows).astype(F32)
    half = HEAD_DIM // 2
    inv = 1.0 / (AXIAL_THETA ** (jnp.arange(0, half, 2, dtype=F32) / half))
    ar, ac = row[:, None] * inv, col[:, None] * inv
    z = jnp.zeros_like(ar)
    c = jnp.concatenate([jnp.cos(ar), jnp.cos(ar), jnp.cos(ac), jnp.cos(ac)], -1)
    sa = jnp.concatenate([-jnp.sin(ar), z, -jnp.sin(ac), z], -1)
    sb = jnp.concatenate([z, jnp.sin(ar), z, jnp.sin(ac)], -1)
    return tuple(_tile_heads(t) for t in (c, sa, sb))


def _partial_tables():
    n = PARTIAL_ROT_DIMS
    pos = jnp.arange(SEQ, dtype=F32)
    ang = pos[:, None] * (1.0 / (PARTIAL_THETA ** (jnp.arange(0, n, 2, dtype=F32) / n)))
    z = jnp.zeros_like(ang)
    rest = HEAD_DIM - n
    c = jnp.concatenate([jnp.cos(ang), jnp.cos(ang), jnp.ones((SEQ, rest), F32)], -1)
    sa = jnp.concatenate([-jnp.sin(ang), z, jnp.zeros((SEQ, rest), F32)], -1)
    sb = jnp.concatenate([z, jnp.sin(ang), jnp.zeros((SEQ, rest), F32)], -1)
    return tuple(_tile_heads(t) for t in (c, sa, sb))


def _residue_order(tab, d):
    return tab.reshape(SEQ // d, d, LANES).transpose(1, 0, 2)


def kernel(x, mem, w_mem_kv, router_w, router_b, a_w_in, a_w_out, a_q_norm, a_k_norm, b_w_in, b_w_out,
           ln1_g, ln1_b, ln2_g, ln2_b, w_gate, w_up, w_down):
    x = x.reshape(N_TOK, D_MODEL)
    row = lambda v: v.reshape(1, -1)

    lane = np.arange(LANES)
    m128 = jnp.asarray((lane[:, None] // HEAD_DIM == lane[None, :] // HEAD_DIM), BF16)
    col = np.arange(B_HEADS * HEAD_DIM)
    expand = jnp.asarray(lane[:, None] == col[None, :] // HEAD_DIM, BF16)
    tri = jnp.asarray(np.triu(np.ones((512, 512), np.float32), 1), BF16)
    rw_hi = router_w.T.astype(BF16)
    rw_lo = (router_w.T - rw_hi.astype(F32)).astype(BF16)
    router = (rw_hi, rw_lo, router_b.reshape(N_EXPERTS, 1), tri)

    mkv = _proj(mem.reshape(BATCH * N_MEM, D_MODEL), w_mem_kv.astype(BF16), 0, 2 * MEM_W, 1.0)

    qg = row(jnp.tile(a_q_norm[0], 2) * SCALE)
    kg = row(jnp.tile(a_k_norm[0], 2))
    q, k, v, qm = _a_in_proj(x, a_w_in[0].astype(BF16), m128, qg, kg, _axial_tables())
    mix = _gqa_attention(q, k, v)
    memo = _mem_attention(qm, mkv)
    w_out = a_w_out[0].astype(BF16)
    kmix = A_Q_HEADS * HEAD_DIM
    x = _out_proj_ln(mix, memo, w_out[:kmix], w_out[kmix:], x, row(ln1_g[0]), row(ln1_b[0]))
    x = _moe_ln(x, router, w_gate[0].astype(BF16), w_up[0].astype(BF16), w_down[0].astype(BF16),
                row(ln2_g[0]), row(ln2_b[0]))

    w_in = b_w_in[0].astype(BF16)
    ptabs = _partial_tables()
    outs, lses = [], []
    for g, d in enumerate(DILATIONS):
        tabs = tuple(_residue_order(t, d) for t in ptabs)
        qb, kb, vb = _b_in_proj(x, w_in, g, d, tabs)
        o, lse = _band_attention(qb, kb, vb, d)
        outs.append(o)
        lses.append(lse)
    mix = _merge_groups(outs, lses, expand)
    qm = _proj(x, w_in, (3 * len(DILATIONS) * B_HEADS * HEAD_DIM) // MEM_W, MEM_W, SCALE)
    memo = _mem_attention(qm, mkv)
    w_out = b_w_out[0].astype(BF16)
    kmix = B_HEADS * HEAD_DIM
    x = _out_proj_ln(mix, memo, w_out[:kmix], w_out[kmix:], x, row(ln1_g[1]), row(ln1_b[1]))
    x = _moe_ln(x, router, w_gate[1].astype(BF16), w_up[1].astype(BF16), w_down[1].astype(BF16),
                row(ln2_g[1]), row(ln2_b[1]))
    return x.reshape(BATCH, SEQ, D_MODEL)
```

```python
import functools
import math

import numpy as np
import jax
import jax.numpy as jnp
from jax import lax
from jax.experimental import pallas as pl
from jax.experimental.pallas import tpu as pltpu

F32 = jnp.float32
BF16 = jnp.bfloat16

D_MODEL = 1024
BATCH = 8
SEQ = 2048
N_TOK = BATCH * SEQ
HEAD_DIM = 64
GRID_W = 64
N_MEM = 256
MEM_HEADS = 4
A_Q_HEADS = 12
A_KV_HEADS = 4
AXIAL_THETA = 10000.0
B_HEADS = 8
B_WIDTH = B_HEADS * HEAD_DIM
DILATIONS = (1, 4, 16)
BAND_RADIUS = 64
PARTIAL_ROT_DIMS = HEAD_DIM // 4
PARTIAL_THETA = 500000.0
N_EXPERTS = 16
N_EXPERT_GROUPS = 4
EXPERTS_PER_GROUP = 4
D_EXPERT = 512
DEPTH = 2
ALPHA = (2 * DEPTH) ** 0.25
NORM_EPS = 1e-6
NEG_BIG = -1e30
SCALE = HEAD_DIM ** -0.5
LOG2_E = math.log2(math.e)
_Q_HEAD_ORDER = tuple(6 * p + 3 * half + t for p in range(2) for t in range(3) for half in range(2))

LANES = 128
MEM_W = MEM_HEADS * HEAD_DIM
TOK_TILE = 512
TILES_PER_SEQ = SEQ // TOK_TILE
BAND_TQ = 128
BAND_TILES = 4

PAIRS = ((0, 1), (0, 2), (0, 3), (1, 2), (1, 3), (2, 3))
N_CLASSES = N_EXPERT_GROUPS * len(PAIRS)
CLASS_ROWS = 32
MOE_TM = 256
MOE_ROWS = N_TOK + N_CLASSES * MOE_TM
MOE_TILES = MOE_ROWS // MOE_TM
XA_W = D_MODEL + LANES

VMEM_LIMIT = 56 * 1024 * 1024


def _params(sem, vmem=VMEM_LIMIT):
    return pltpu.CompilerParams(dimension_semantics=sem, vmem_limit_bytes=vmem)


def _dot(a, b):
    return jnp.dot(a, b, preferred_element_type=F32)


def _dot_nt(a, b):
    return lax.dot_general(a, b, (((1,), (1,)), ((), ())), preferred_element_type=F32)


def _split_bf16(x):
    hi = x.astype(BF16)
    lo = (x - hi.astype(F32)).astype(BF16)
    return hi, lo


def _layer_norm(z, g, b):
    mu = jnp.mean(z, axis=-1, keepdims=True)
    zc = z - mu
    var = jnp.mean(zc * zc, axis=-1, keepdims=True)
    return zc * lax.rsqrt(var + NORM_EPS) * g + b


def _rope(z, c, sa, sb, shift):
    return z * c + pltpu.roll(z, LANES - shift, 1) * sa + pltpu.roll(z, shift, 1) * sb


def _first_head(shape):
    return lax.broadcasted_iota(jnp.int32, shape, 1) < HEAD_DIM


def _pair_attention(q2, k2, v3, first, valid=None, base2=False):
    res = []
    for pick in (first, jnp.logical_not(first)):
        s = _dot_nt(jnp.where(pick, q2, jnp.zeros_like(q2)), k2)
        if valid is not None:
            s = jnp.where(valid, s, NEG_BIG)
        m = jnp.max(s, axis=1, keepdims=True)
        p = jnp.exp2(s - m) if base2 else jnp.exp(s - m)
        res.append((_dot(p.astype(BF16), v3), m))
    return res


def _pair_output(res, first):
    (ol_a, _), (ol_b, _) = res
    return jnp.where(first, ol_a[:, :LANES], ol_b[:, :LANES]) / jnp.where(first, ol_a[:, LANES:], ol_b[:, LANES:])


def _with_ones(v2):
    return jnp.concatenate([v2, jnp.ones_like(v2)], axis=1)


def _proj_kernel(x_ref, w_ref, o_ref):
    o_ref[...] = _dot(x_ref[...].astype(BF16), w_ref[...]).astype(o_ref.dtype)


def _proj(x, w, tm=TOK_TILE):
    n, k = x.shape
    width = w.shape[1]
    return pl.pallas_call(
        _proj_kernel,
        grid=(n // tm,),
        in_specs=[pl.BlockSpec((tm, k), lambda i: (i, 0)), pl.BlockSpec((k, width), lambda i: (0, 0))],
        out_specs=pl.BlockSpec((tm, width), lambda i: (i, 0)),
        out_shape=jax.ShapeDtypeStruct((n, width), BF16),
        compiler_params=_params(("parallel",)),
        name="proj",
    )(x, w)


def _a_in_kernel(x_ref, w_ref, m_ref, qg_ref, kg_ref, c_ref, sa_ref, sb_ref, q_ref, k_ref, v_ref, qm_ref):
    y = _dot(x_ref[...].astype(BF16), w_ref[...])
    m = m_ref[...]
    c, sa, sb = c_ref[...], sa_ref[...], sb_ref[...]

    def norm_rope(z, gain):
        hi, lo = _split_bf16(z * z)
        ms = (_dot(hi, m) + _dot(lo, m)) * (1.0 / HEAD_DIM)
        z = z * lax.rsqrt(ms + NORM_EPS) * gain
        return _rope(z, c, sa, sb, 16)

    nq = A_Q_HEADS * HEAD_DIM // LANES
    nk = A_KV_HEADS * HEAD_DIM // LANES
    for ch in range(nq):
        q_ref[:, ch * LANES:(ch + 1) * LANES] = norm_rope(y[:, ch * LANES:(ch + 1) * LANES], qg_ref[...]).astype(BF16)
    off = nq * LANES
    for ch in range(nk):
        z = y[:, off + ch * LANES: off + (ch + 1) * LANES]
        k_ref[:, ch * LANES:(ch + 1) * LANES] = norm_rope(z, kg_ref[...]).astype(BF16)
    off += nk * LANES
    ones = jnp.ones((y.shape[0], LANES), BF16)
    for ch in range(nk):
        v_ref[:, 2 * ch * LANES:(2 * ch + 1) * LANES] = y[:, off + ch * LANES: off + (ch + 1) * LANES].astype(BF16)
        v_ref[:, (2 * ch + 1) * LANES:(2 * ch + 2) * LANES] = ones
    off += nk * LANES
    qm_ref[...] = (y[:, off:off + MEM_W] * SCALE).astype(BF16)


def _a_in_proj(x, w, m128, qg, kg, tabs, tm=TOK_TILE):
    c, sa, sb = tabs
    nblk = SEQ // tm
    a_in = w.shape[1]
    tab_spec = pl.BlockSpec((tm, LANES), lambda i: (i % nblk, 0))
    row = lambda width: pl.BlockSpec((tm, width), lambda i: (i, 0))
    const = lambda shape: pl.BlockSpec(shape, lambda i: (0, 0))
    qw, kw = A_Q_HEADS * HEAD_DIM, A_KV_HEADS * HEAD_DIM
    return pl.pallas_call(
        _a_in_kernel,
        grid=(N_TOK // tm,),
        in_specs=[row(D_MODEL), const((D_MODEL, a_in)), const((LANES, LANES)), const((1, LANES)), const((1, LANES)),
                  tab_spec, tab_spec, tab_spec],
        out_specs=[row(qw), row(kw), row(2 * kw), row(MEM_W)],
        out_shape=[jax.ShapeDtypeStruct((N_TOK, qw), BF16), jax.ShapeDtypeStruct((N_TOK, kw), BF16),
                   jax.ShapeDtypeStruct((N_TOK, 2 * kw), BF16), jax.ShapeDtypeStruct((N_TOK, MEM_W), BF16)],
        compiler_params=_params(("parallel",)),
        name="a_in_proj",
    )(x, w, m128, qg, kg, c, sa, sb)


def _b_in_kernel(*refs):
    ng = len(DILATIONS)
    nx = D_MODEL // LANES
    x_refs, w_ref, refs = refs[:nx], refs[nx], refs[nx + 1:]
    tabs, outs, qm_ref = refs[:3 * ng], refs[3 * ng:6 * ng], refs[6 * ng]
    xb1 = None
    for g, d in enumerate(DILATIONS):
        n = TOK_TILE // d
        if d == 1:
            cols = [r[...] for r in x_refs]
        else:
            cols = [jnp.concatenate([r[pl.ds(k, n, stride=d), :] for k in range(d)], axis=0) for r in x_refs]
        xb = jnp.concatenate(cols, axis=1).astype(BF16)
        xb1 = xb if d == 1 else xb1
        c, sa, sb = (t[...].reshape(TOK_TILE, LANES) for t in tabs[3 * g:3 * g + 3])
        q_ref, k_ref, v_ref = outs[3 * g:3 * g + 3]
        yq = _dot(xb, w_ref[:, g * B_WIDTH:(g + 1) * B_WIDTH]) * SCALE
        yk = _dot(xb, w_ref[:, (ng + g) * B_WIDTH:(ng + g + 1) * B_WIDTH])
        yv = _dot(xb, w_ref[:, (2 * ng + g) * B_WIDTH:(2 * ng + g + 1) * B_WIDTH])
        for ch in range(B_WIDTH // LANES):
            sl = slice(ch * LANES, (ch + 1) * LANES)
            q_ref[0, :, :, sl] = _rope(yq[:, sl], c, sa, sb, PARTIAL_ROT_DIMS // 2).reshape(d, n, LANES).astype(BF16)
            k_ref[0, :, :, sl] = _rope(yk[:, sl], c, sa, sb, PARTIAL_ROT_DIMS // 2).reshape(d, n, LANES).astype(BF16)
        v_ref[0] = yv.reshape(d, n, B_WIDTH).astype(BF16)
    qm = _dot(xb1, w_ref[:, 3 * ng * B_WIDTH:3 * ng * B_WIDTH + MEM_W])
    qm_ref[...] = (qm * SCALE).astype(BF16)


def _b_in_proj(x, w, tabs):
    nx = D_MODEL // LANES
    in_specs = [pl.BlockSpec((TOK_TILE, LANES), functools.partial(lambda i, c: (i, c), c=c)) for c in range(nx)]
    in_specs.append(pl.BlockSpec(w.shape, lambda i: (0, 0)))
    out_specs, out_shape = [], []
    for d in DILATIONS:
        n = TOK_TILE // d
        in_specs += [pl.BlockSpec((d, n, LANES), lambda i: (0, i % TILES_PER_SEQ, 0))] * 3
        out_specs += [pl.BlockSpec((1, d, n, B_WIDTH), lambda i: (i // TILES_PER_SEQ, 0, i % TILES_PER_SEQ, 0))] * 3
        out_shape += [jax.ShapeDtypeStruct((BATCH, d, SEQ // d, B_WIDTH), BF16)] * 3
    out_specs.append(pl.BlockSpec((TOK_TILE, MEM_W), lambda i: (i, 0)))
    out_shape.append(jax.ShapeDtypeStruct((N_TOK, MEM_W), BF16))
    res = pl.pallas_call(
        _b_in_kernel,
        grid=(N_TOK // TOK_TILE,),
        in_specs=in_specs,
        out_specs=out_specs,
        out_shape=out_shape,
        compiler_params=_params(("parallel",)),
        name="b_in_proj",
    )(*([x] * nx), w, *[t for group in tabs for t in group])
    return [res[3 * g:3 * g + 3] for g in range(len(DILATIONS))], res[-1]


def _gqa_kernel(q_ref, k_ref, v_ref, o_ref):
    tq = q_ref.shape[0]
    ntile = q_ref.shape[1] // LANES
    q = jnp.concatenate([q_ref[:, t * LANES:(t + 1) * LANES] for t in range(ntile)], axis=0)
    first = _first_head(q.shape)
    o = _pair_output(_pair_attention(q, k_ref[...], v_ref[...], first, base2=True), first)
    for t in range(ntile):
        o_ref[:, t * LANES:(t + 1) * LANES] = o[t * tq:(t + 1) * tq].astype(BF16)


def _gqa_attention(q, k, v, tq=128):
    nq = SEQ // tq
    qw = q.shape[1] // 2
    kw = k.shape[1] // 2
    return pl.pallas_call(
        _gqa_kernel,
        grid=(BATCH, 2, nq),
        in_specs=[pl.BlockSpec((tq, qw), lambda b, p, i: (b * nq + i, p)),
                  pl.BlockSpec((SEQ, kw), lambda b, p, i: (b, p)),
                  pl.BlockSpec((SEQ, 2 * kw), lambda b, p, i: (b, p))],
        out_specs=pl.BlockSpec((tq, qw), lambda b, p, i: (b * nq + i, p)),
        out_shape=jax.ShapeDtypeStruct(q.shape, BF16),
        compiler_params=_params(("parallel", "parallel", "parallel")),
        name="gqa_attention",
    )(q, k, v)


def _band_kernel(q_ref, k_ref, v_ref, o_ref, lse_ref, *, length, seg, win):
    tq = BAND_TQ
    first = _first_head((tq, LANES))
    lane = lax.broadcasted_iota(jnp.int32, (tq, LANES), 1)
    row = lax.broadcasted_iota(jnp.int32, (tq, win), 0)
    col = lax.broadcasted_iota(jnp.int32, (tq, win), 1)
    for u in range(BAND_TILES):
        tile = pl.program_id(1) * BAND_TILES + u
        start = pl.multiple_of(jnp.clip(tile * tq - BAND_RADIUS, 0, length - win), BAND_RADIUS)
        qpos, kpos = tile * tq + row, start + col
        valid = jnp.abs(qpos - kpos) <= BAND_RADIUS
        if seg < length:
            valid = jnp.logical_and(valid, qpos // seg == kpos // seg)
        rows = slice(u * tq, (u + 1) * tq)
        lse = jnp.zeros((tq, LANES), F32)
        for j in range(B_WIDTH // LANES):
            sl = slice(j * LANES, (j + 1) * LANES)
            res = _pair_attention(q_ref[0, rows, sl], k_ref[0, pl.ds(start, win), sl],
                                  _with_ones(v_ref[0, pl.ds(start, win), sl]), first, valid)
            o_ref[0, rows, sl] = _pair_output(res, first).astype(BF16)
            for half, (ol, m) in enumerate(res):
                lse = jnp.where(lane == 2 * j + half, m + jnp.log(ol[:, LANES:]), lse)
        lse_ref[0, rows, :] = lse


def _band_attention(q, k, v, d):
    seg = SEQ // d
    step_rows = BAND_TILES * BAND_TQ
    length = max(seg, step_rows)
    win = BAND_TQ + 2 * BAND_RADIUS
    nblk = N_TOK // length
    q, k, v = (t.reshape(nblk, length, B_WIDTH) for t in (q, k, v))
    seq_spec = pl.BlockSpec((1, length, B_WIDTH), lambda s, i: (s, 0, 0))
    o, lse = pl.pallas_call(
        functools.partial(_band_kernel, length=length, seg=seg, win=win),
        grid=(nblk, length // step_rows),
        in_specs=[pl.BlockSpec((1, step_rows, B_WIDTH), lambda s, i: (s, i, 0)), seq_spec, seq_spec],
        out_specs=[pl.BlockSpec((1, step_rows, B_WIDTH), lambda s, i: (s, i, 0)),
                   pl.BlockSpec((1, step_rows, LANES), lambda s, i: (s, i, 0))],
        out_shape=[jax.ShapeDtypeStruct((nblk, length, B_WIDTH), BF16),
                   jax.ShapeDtypeStruct((nblk, length, LANES), F32)],
        compiler_params=_params(("parallel", "parallel")),
        name=f"band_attention_d{d}",
    )(q, k, v)
    return o.reshape(BATCH, d, seg, B_WIDTH), lse.reshape(BATCH, d, seg, LANES)


def _route(x, wh_ref, wl_ref, rb_ref, tri_ref, xa_ref, cls_ref, rank_ref, cnt_ref, carry_ref):
    tm = x.shape[0]
    xh, xl = _split_bf16(x)
    logits = _dot_nt(wh_ref[...], xh) + _dot_nt(wh_ref[...], xl) + _dot_nt(wl_ref[...], xh)
    aff = 1.0 / (1.0 + jnp.exp(-logits))
    sel = aff + rb_ref[...]
    s = [sel[e:e + 1, :] for e in range(N_EXPERTS)]
    a = [aff[e:e + 1, :] for e in range(N_EXPERTS)]

    def top2_sum(v):
        hi01, lo01 = jnp.maximum(v[0], v[1]), jnp.minimum(v[0], v[1])
        hi23, lo23 = jnp.maximum(v[2], v[3]), jnp.minimum(v[2], v[3])
        return jnp.maximum(hi01, hi23) + jnp.maximum(jnp.minimum(hi01, hi23), jnp.maximum(lo01, lo23))

    gscore = [top2_sum(s[EXPERTS_PER_GROUP * g:EXPERTS_PER_GROUP * (g + 1)]) for g in range(N_EXPERT_GROUPS)]
    best = jnp.zeros((1, tm), jnp.int32)
    best_score = gscore[0]
    for g in range(1, N_EXPERT_GROUPS):
        better = gscore[g] > best_score
        best = jnp.where(better, g, best)
        best_score = jnp.where(better, gscore[g], best_score)

    def pick(rows, j):
        out = rows[j]
        for g in range(1, N_EXPERT_GROUPS):
            out = jnp.where(best == g, rows[EXPERTS_PER_GROUP * g + j], out)
        return out

    t = [pick(s, j) for j in range(EXPERTS_PER_GROUP)]
    w = [pick(a, j) for j in range(EXPERTS_PER_GROUP)]

    def first_max(v):
        mx = jnp.maximum(jnp.maximum(v[0], v[1]), jnp.maximum(v[2], v[3]))
        idx = jnp.full((1, tm), EXPERTS_PER_GROUP - 1, jnp.int32)
        for j in range(EXPERTS_PER_GROUP - 2, -1, -1):
            idx = jnp.where(v[j] == mx, j, idx)
        return idx

    i1 = first_max(t)
    i2 = first_max([jnp.where(i1 == j, -jnp.inf, t[j]) for j in range(EXPERTS_PER_GROUP)])
    lo, hi = jnp.minimum(i1, i2), jnp.maximum(i1, i2)

    def take(rows, idx):
        out = rows[0]
        for j in range(1, EXPERTS_PER_GROUP):
            out = jnp.where(idx == j, rows[j], out)
        return out

    w_lo, w_hi = take(w, lo), take(w, hi)
    den = w_lo + w_hi
    pair = jnp.where(lo == 0, hi - 1, jnp.where(lo == 1, hi + 1, len(PAIRS) - 1))
    cls = best * len(PAIRS) + pair

    onehot = (lax.broadcasted_iota(jnp.int32, (CLASS_ROWS, tm), 0) == cls).astype(F32)
    before = _dot(onehot.astype(BF16), tri_ref[...]) + carry_ref[:, 0:1]
    rank = jnp.sum(onehot * before, axis=0, keepdims=True)
    carry_ref[...] = carry_ref[...] + jnp.sum(onehot, axis=1, keepdims=True)

    cls_ref[0] = cls
    rank_ref[0] = rank.astype(jnp.int32)
    cnt_ref[...] = carry_ref[...]
    gates = jnp.concatenate([w_lo / den, w_hi / den, jnp.zeros((LANES - 2, tm), F32)], axis=0)
    xa_ref[:, :D_MODEL] = x
    xa_ref[:, D_MODEL:] = gates.T


def _post_attn_kernel(*refs, merge):
    if merge:
        o_refs, l_refs, e_ref, refs = refs[:3], refs[3:6], refs[6], refs[7:]
    else:
        mix_ref, refs = refs[0], refs[1:]
    (qm_ref, mk_ref, mv_ref, w1_ref, w2_ref, x_ref, g_ref, b_ref, wh_ref, wl_ref, rb_ref, tri_ref,
     xa_ref, cls_ref, rank_ref, cnt_ref, carry_ref) = refs[:17]

    @pl.when(pl.program_id(0) == 0)
    def _():
        carry_ref[...] = jnp.zeros_like(carry_ref)

    if merge:
        ot_ref, lt_ref = refs[17], refs[18]
        nch = B_WIDTH // LANES
        for g, d in enumerate(DILATIONS):
            n = TOK_TILE // d
            for r in range(d):
                rows = pl.ds(r, n, stride=d) if d > 1 else slice(None)
                og = o_refs[g][0, r].astype(F32)
                for ch in range(nch):
                    ot_ref[g, ch, rows, :] = og[:, ch * LANES:(ch + 1) * LANES]
                lt_ref[g, rows, :] = l_refs[g][0, r]
        lses = [lt_ref[g] for g in range(len(DILATIONS))]
        m = functools.reduce(jnp.maximum, lses)
        es = [jnp.exp(l - m) for l in lses]
        inv = 1.0 / functools.reduce(jnp.add, es)
        expand = e_ref[...]

        def widen(w):
            hi, lo = _split_bf16(w)
            return _dot(hi, expand) + _dot(lo, expand)

        ws = [widen(e * inv) for e in es]
        mix = jnp.concatenate(
            [functools.reduce(jnp.add, [w[:, ch * LANES:(ch + 1) * LANES] * ot_ref[g, ch] for g, w in enumerate(ws)])
             for ch in range(nch)], axis=1).astype(BF16)
    else:
        mix = mix_ref[...]

    first = _first_head((TOK_TILE, LANES))
    memo = []
    for j in range(MEM_W // LANES):
        sl = slice(j * LANES, (j + 1) * LANES)
        memo.append(_pair_output(_pair_attention(qm_ref[:, sl], mk_ref[:, sl], _with_ones(mv_ref[:, sl]), first), first))
    memo = jnp.concatenate(memo, axis=1).astype(BF16)

    attn = _dot(mix, w1_ref[...]) + _dot(memo, w2_ref[...])
    x1 = _layer_norm(ALPHA * x_ref[...] + attn, g_ref[...], b_ref[...])
    _route(x1, wh_ref, wl_ref, rb_ref, tri_ref, xa_ref, cls_ref, rank_ref, cnt_ref, carry_ref)


def _post_attn(mix, qm, mkv, w1, w2, x, g, b, router, expand=None):
    merge = expand is not None
    wh, wl, rb, tri = router
    nblk = N_TOK // TOK_TILE
    row = lambda width: pl.BlockSpec((TOK_TILE, width), lambda i: (i, 0))
    const = lambda shape: pl.BlockSpec(shape, lambda i: (0,) * len(shape))
    row3 = pl.BlockSpec((1, 1, TOK_TILE), lambda i: (i, 0, 0))
    scratch = [pltpu.VMEM((CLASS_ROWS, LANES), F32)]
    if merge:
        outs, lses = mix
        resid = lambda d, width: pl.BlockSpec((1, d, TOK_TILE // d, width),
                                              lambda i: (i // TILES_PER_SEQ, 0, i % TILES_PER_SEQ, 0))
        lead_specs = ([resid(d, B_WIDTH) for d in DILATIONS] + [resid(d, LANES) for d in DILATIONS]
                      + [const(expand.shape)])
        lead = [*outs, *lses, expand]
        scratch += [pltpu.VMEM((len(DILATIONS), B_WIDTH // LANES, TOK_TILE, LANES), F32),
                    pltpu.VMEM((len(DILATIONS), TOK_TILE, LANES), F32)]
    else:
        lead_specs, lead = [row(mix.shape[1])], [mix]
    xa, cls, rank, cnt = pl.pallas_call(
        functools.partial(_post_attn_kernel, merge=merge),
        grid=(nblk,),
        in_specs=lead_specs + [row(MEM_W),
                               pl.BlockSpec((N_MEM, MEM_W), lambda i: (i // TILES_PER_SEQ, 0)),
                               pl.BlockSpec((N_MEM, MEM_W), lambda i: (i // TILES_PER_SEQ, 1)),
                               const(w1.shape), const(w2.shape), row(D_MODEL), const((1, D_MODEL)), const((1, D_MODEL)),
                               const(wh.shape), const(wl.shape), const(rb.shape), const(tri.shape)],
        out_specs=[row(XA_W), row3, row3, const((CLASS_ROWS, LANES))],
        out_shape=[jax.ShapeDtypeStruct((N_TOK, XA_W), F32), jax.ShapeDtypeStruct((nblk, 1, TOK_TILE), jnp.int32),
                   jax.ShapeDtypeStruct((nblk, 1, TOK_TILE), jnp.int32), jax.ShapeDtypeStruct((CLASS_ROWS, LANES), F32)],
        scratch_shapes=scratch,
        compiler_params=_params(("arbitrary",)),
        name="post_attn_merge" if merge else "post_attn",
    )(*lead, qm, mkv, mkv, w1, w2, x, g, b, wh, wl, rb, tri)
    return xa, cls.reshape(N_TOK), rank.reshape(N_TOK), cnt[:N_CLASSES, 0].astype(jnp.int32)


def _dispatch_kernel(pos_ref, xa_ref, init_ref, xs_ref, sem, *, chunk):
    del init_ref
    base = pl.program_id(0) * chunk

    def issue(r, carry):
        pltpu.make_async_copy(xa_ref.at[pl.ds(r, 1)], xs_ref.at[pl.ds(pos_ref[base + r], 1)], sem).start()
        return carry

    def retire(r, carry):
        pltpu.make_async_copy(xa_ref.at[pl.ds(0, 1)], xs_ref.at[pl.ds(0, 1)], sem).wait()
        return carry

    lax.fori_loop(0, chunk, issue, 0, unroll=8)
    lax.fori_loop(0, chunk, retire, 0, unroll=8)


def _dispatch(pos, xa, chunk=512):
    init = jnp.zeros((MOE_ROWS, XA_W), F32)
    return pl.pallas_call(
        functools.partial(_dispatch_kernel, chunk=chunk),
        grid_spec=pltpu.PrefetchScalarGridSpec(
            num_scalar_prefetch=1,
            grid=(N_TOK // chunk,),
            in_specs=[pl.BlockSpec((chunk, XA_W), lambda i, p: (i, 0)), pl.BlockSpec(memory_space=pl.ANY)],
            out_specs=pl.BlockSpec(memory_space=pl.ANY),
            scratch_shapes=[pltpu.SemaphoreType.DMA(())],
        ),
        out_shape=jax.ShapeDtypeStruct((MOE_ROWS, XA_W), F32),
        input_output_aliases={2: 0},
        compiler_params=_params(("arbitrary",)),
        name="moe_dispatch",
    )(pos, xa, init)


def _expert_kernel(e1_ref, e2_ref, used_ref, fresh_ref, xs_ref, wg1_ref, wu1_ref, wd1_ref, wg2_ref, wu2_ref, wd2_ref,
                   ys_ref, wg_s, wu_s, wd_s):
    i = pl.program_id(0)

    @pl.when(fresh_ref[i] != 0)
    def _():
        for slot, (wg_ref, wu_ref, wd_ref) in enumerate(((wg1_ref, wu1_ref, wd1_ref), (wg2_ref, wu2_ref, wd2_ref))):
            wg_s[slot] = wg_ref[0].astype(BF16)
            wu_s[slot] = wu_ref[0].astype(BF16)
            wd_s[slot] = wd_ref[0].astype(BF16)

    @pl.when(used_ref[i] != 0)
    def _():
        x = xs_ref[:, :D_MODEL].astype(BF16)
        gates = xs_ref[:, D_MODEL:]

        def expert(slot):
            gate = _dot(x, wg_s[slot])
            h = gate * (1.0 / (1.0 + jnp.exp(-gate))) * _dot(x, wu_s[slot])
            return _dot(h.astype(BF16), wd_s[slot])

        ys_ref[...] = gates[:, 0:1] * expert(0) + gates[:, 1:2] * expert(1)

    @pl.when(used_ref[i] == 0)
    def _():
        ys_ref[...] = jnp.zeros_like(ys_ref)


def _experts(tile_e1, tile_e2, tile_used, tile_fresh, xs, wg, wu, wd):
    up = lambda sel: pl.BlockSpec((1, D_MODEL, D_EXPERT), lambda i, e1, e2, u, f: ((e1, e2)[sel][i], 0, 0))
    down = lambda sel: pl.BlockSpec((1, D_EXPERT, D_MODEL), lambda i, e1, e2, u, f: ((e1, e2)[sel][i], 0, 0))
    return pl.pallas_call(
        _expert_kernel,
        grid_spec=pltpu.PrefetchScalarGridSpec(
            num_scalar_prefetch=4,
            grid=(MOE_TILES,),
            in_specs=[pl.BlockSpec((MOE_TM, XA_W), lambda i, e1, e2, u, f: (i, 0)),
                      up(0), up(0), down(0), up(1), up(1), down(1)],
            out_specs=pl.BlockSpec((MOE_TM, D_MODEL), lambda i, e1, e2, u, f: (i, 0)),
            scratch_shapes=[pltpu.VMEM((2, D_MODEL, D_EXPERT), BF16), pltpu.VMEM((2, D_MODEL, D_EXPERT), BF16),
                            pltpu.VMEM((2, D_EXPERT, D_MODEL), BF16)],
        ),
        out_shape=jax.ShapeDtypeStruct((MOE_ROWS, D_MODEL), F32),
        compiler_params=_params(("arbitrary",)),
        name="moe_experts",
    )(tile_e1, tile_e2, tile_used, tile_fresh, xs, wg, wu, wd, wg, wu, wd)


def _combine_kernel(pos_ref, ys_ref, x_ref, g_ref, b_ref, o_ref, buf, sem, *, tc):
    i = pl.program_id(0)
    slot = i % 2

    def gather(step, into):
        def issue(r, carry):
            row = pos_ref[step * tc + r]
            pltpu.make_async_copy(ys_ref.at[pl.ds(row, 1)], buf.at[into, pl.ds(r, 1)], sem.at[into]).start()
            return carry

        lax.fori_loop(0, tc, issue, 0, unroll=8)

    @pl.when(i == 0)
    def _():
        gather(0, 0)

    @pl.when(i + 1 < pl.num_programs(0))
    def _():
        gather(i + 1, 1 - slot)

    def retire(r, carry):
        pltpu.make_async_copy(ys_ref.at[pl.ds(0, 1)], buf.at[slot, pl.ds(0, 1)], sem.at[slot]).wait()
        return carry

    lax.fori_loop(0, tc, retire, 0, unroll=8)
    o_ref[...] = _layer_norm(ALPHA * x_ref[...] + buf[slot], g_ref[...], b_ref[...])


def _combine_ln(pos, ys, xa, g, b, tc=256):
    return pl.pallas_call(
        functools.partial(_combine_kernel, tc=tc),
        grid_spec=pltpu.PrefetchScalarGridSpec(
            num_scalar_prefetch=1,
            grid=(N_TOK // tc,),
            in_specs=[pl.BlockSpec(memory_space=pl.ANY),
                      pl.BlockSpec((tc, D_MODEL), lambda i, p: (i, 0)),
                      pl.BlockSpec((1, D_MODEL), lambda i, p: (0, 0)),
                      pl.BlockSpec((1, D_MODEL), lambda i, p: (0, 0))],
            out_specs=pl.BlockSpec((tc, D_MODEL), lambda i, p: (i, 0)),
            scratch_shapes=[pltpu.VMEM((2, tc, D_MODEL), F32), pltpu.SemaphoreType.DMA((2,))],
        ),
        out_shape=jax.ShapeDtypeStruct((N_TOK, D_MODEL), F32),
        compiler_params=_params(("arbitrary",)),
        name="moe_combine_ln",
    )(pos, ys, xa, g, b)


_CLASS_E1 = np.array([EXPERTS_PER_GROUP * g + p[0] for g in range(N_EXPERT_GROUPS) for p in PAIRS], np.int32)
_CLASS_E2 = np.array([EXPERTS_PER_GROUP * g + p[1] for g in range(N_EXPERT_GROUPS) for p in PAIRS], np.int32)


def _moe_ln(routed, wg, wu, wd, g, b):
    xa, cls, rank, counts = routed
    padded = (counts + MOE_TM - 1) // MOE_TM * MOE_TM
    ends = jnp.cumsum(padded)
    starts = ends - padded
    pos = starts[cls] + rank
    tile_start = jnp.arange(MOE_TILES, dtype=jnp.int32) * MOE_TM
    tile_used = (tile_start < ends[-1]).astype(jnp.int32)
    last_cls = jnp.max(jnp.where(counts > 0, jnp.arange(N_CLASSES), 0))
    tile_cls = jnp.minimum(jnp.sum(tile_start[:, None] >= ends[None, :], axis=1), last_cls)
    prev_cls = jnp.concatenate([jnp.full((1,), -1, tile_cls.dtype), tile_cls[:-1]])
    tile_fresh = tile_used * (tile_cls != prev_cls).astype(jnp.int32)
    tile_e1 = jnp.asarray(_CLASS_E1)[tile_cls]
    tile_e2 = jnp.asarray(_CLASS_E2)[tile_cls]
    xs = _dispatch(pos, xa)
    ys = _experts(tile_e1, tile_e2, tile_used, tile_fresh, xs, wg, wu, wd)
    return _combine_ln(pos, ys, xa, g, b)


def _tile_heads(t64):
    return jnp.concatenate([t64, t64], axis=-1)


def _axial_tables():
    rows = SEQ // GRID_W
    row = jnp.repeat(jnp.arange(rows), GRID_W).astype(F32)
    col = jnp.tile(jnp.arange(GRID_W), rows).astype(F32)
    half = HEAD_DIM // 2
    inv = 1.0 / (AXIAL_THETA ** (jnp.arange(0, half, 2, dtype=F32) / half))
    ar, ac = row[:, None] * inv, col[:, None] * inv
    z = jnp.zeros_like(ar)
    c = jnp.concatenate([jnp.cos(ar), jnp.cos(ar), jnp.cos(ac), jnp.cos(ac)], -1)
    sa = jnp.concatenate([-jnp.sin(ar), z, -jnp.sin(ac), z], -1)
    sb = jnp.concatenate([z, jnp.sin(ar), z, jnp.sin(ac)], -1)
    return tuple(_tile_heads(t) for t in (c, sa, sb))


def _partial_tables():
    n = PARTIAL_ROT_DIMS
    pos = jnp.arange(SEQ, dtype=F32)
    ang = pos[:, None] * (1.0 / (PARTIAL_THETA ** (jnp.arange(0, n, 2, dtype=F32) / n)))
    z = jnp.zeros_like(ang)
    rest = HEAD_DIM - n
    c = jnp.concatenate([jnp.cos(ang), jnp.cos(ang), jnp.ones((SEQ, rest), F32)], -1)
    sa = jnp.concatenate([-jnp.sin(ang), z, jnp.zeros((SEQ, rest), F32)], -1)
    sb = jnp.concatenate([z, jnp.sin(ang), jnp.zeros((SEQ, rest), F32)], -1)
    return tuple(_tile_heads(t) for t in (c, sa, sb))


def _residue_order(tab, d):
    return tab.reshape(SEQ // d, d, LANES).transpose(1, 0, 2)


def kernel(x, mem, w_mem_kv, router_w, router_b, a_w_in, a_w_out, a_q_norm, a_k_norm, b_w_in, b_w_out,
           ln1_g, ln1_b, ln2_g, ln2_b, w_gate, w_up, w_down):
    x = x.reshape(N_TOK, D_MODEL)
    row = lambda v: v.reshape(1, -1)

    lane = np.arange(LANES)
    m128 = jnp.asarray((lane[:, None] // HEAD_DIM == lane[None, :] // HEAD_DIM), BF16)
    expand = jnp.asarray(lane[:, None] == np.arange(B_WIDTH)[None, :] // HEAD_DIM, BF16)
    tri = jnp.asarray(np.triu(np.ones((TOK_TILE, TOK_TILE), np.float32), 1), BF16)
    rw_hi = router_w.T.astype(BF16)
    rw_lo = (router_w.T - rw_hi.astype(F32)).astype(BF16)
    router = (rw_hi, rw_lo, router_b.reshape(N_EXPERTS, 1), tri)

    mkv = _proj(mem.reshape(BATCH * N_MEM, D_MODEL), w_mem_kv.astype(BF16))

    kmix = A_Q_HEADS * HEAD_DIM
    qcols = (np.asarray(_Q_HEAD_ORDER)[:, None] * HEAD_DIM + np.arange(HEAD_DIM)[None, :]).reshape(-1)
    w_in = jnp.concatenate([a_w_in[0][:, qcols], a_w_in[0][:, kmix:]], axis=1).astype(BF16)
    qg = row(jnp.tile(a_q_norm[0], 2) * (SCALE * LOG2_E))
    kg = row(jnp.tile(a_k_norm[0], 2))
    q, k, v, qm = _a_in_proj(x, w_in, m128, qg, kg, _axial_tables())
    mix = _gqa_attention(q, k, v)
    w_out = a_w_out[0].astype(BF16)
    routed = _post_attn(mix, qm, mkv, w_out[qcols], w_out[kmix:], x, row(ln1_g[0]), row(ln1_b[0]), router)
    x = _moe_ln(routed, w_gate[0], w_up[0], w_down[0], row(ln2_g[0]), row(ln2_b[0]))

    ptabs = _partial_tables()
    tabs = [tuple(_residue_order(t, d) for t in ptabs) for d in DILATIONS]
    qkv, qm = _b_in_proj(x, b_w_in[0].astype(BF16), tabs)
    outs, lses = zip(*[_band_attention(*qkv[g], d) for g, d in enumerate(DILATIONS)])
    w_out = b_w_out[0].astype(BF16)
    routed = _post_attn((outs, lses), qm, mkv, w_out[:B_WIDTH], w_out[B_WIDTH:], x, row(ln1_g[1]), row(ln1_b[1]),
                        router, expand)
    x = _moe_ln(routed, w_gate[1], w_up[1], w_down[1], row(ln2_g[1]), row(ln2_b[1]))
    return x.reshape(BATCH, SEQ, D_MODEL)
```

```python
import functools
import math

import numpy as np
import jax
import jax.numpy as jnp
from jax import lax
from jax.experimental import pallas as pl
from jax.experimental.pallas import tpu as pltpu

F32 = jnp.float32
BF16 = jnp.bfloat16

D_MODEL = 1024
BATCH = 8
SEQ = 2048
N_TOK = BATCH * SEQ
HEAD_DIM = 64
GRID_W = 64
N_MEM = 256
MEM_HEADS = 4
A_Q_HEADS = 12
A_KV_HEADS = 4
AXIAL_THETA = 10000.0
B_HEADS = 8
B_WIDTH = B_HEADS * HEAD_DIM
DILATIONS = (1, 4, 16)
BAND_RADIUS = 64
PARTIAL_ROT_DIMS = HEAD_DIM // 4
PARTIAL_THETA = 500000.0
N_EXPERTS = 16
N_EXPERT_GROUPS = 4
EXPERTS_PER_GROUP = 4
D_EXPERT = 512
DEPTH = 2
ALPHA = (2 * DEPTH) ** 0.25
NORM_EPS = 1e-6
NEG_BIG = -1e30
SCALE = HEAD_DIM ** -0.5
LOG2_E = math.log2(math.e)
_Q_HEAD_ORDER = tuple(6 * p + 3 * half + t for p in range(2) for t in range(3) for half in range(2))

LANES = 128
MEM_W = MEM_HEADS * HEAD_DIM
TOK_TILE = 512
TILES_PER_SEQ = SEQ // TOK_TILE
GQA_KC = 256
BAND_TQ = 128
BAND_TILES = 4

PAIRS = ((0, 1), (0, 2), (0, 3), (1, 2), (1, 3), (2, 3))
N_CLASSES = N_EXPERT_GROUPS * len(PAIRS)
CLASS_ROWS = 32
MOE_TM = 256
MOE_ROWS = N_TOK + N_CLASSES * MOE_TM
MOE_TILES = MOE_ROWS // MOE_TM
XA_W = D_MODEL + LANES
ROW_DMA_UNROLL = 8

VMEM_LIMIT = 56 * 1024 * 1024


def _params(sem, vmem=VMEM_LIMIT):
    return pltpu.CompilerParams(dimension_semantics=sem, vmem_limit_bytes=vmem)


def _dot(a, b):
    return jnp.dot(a, b, preferred_element_type=F32)


def _dot_nt(a, b):
    return lax.dot_general(a, b, (((1,), (1,)), ((), ())), preferred_element_type=F32)


def _split_bf16(x):
    hi = x.astype(BF16)
    lo = (x - hi.astype(F32)).astype(BF16)
    return hi, lo


def _layer_norm(z, g, b):
    mu = jnp.mean(z, axis=-1, keepdims=True)
    zc = z - mu
    var = jnp.mean(zc * zc, axis=-1, keepdims=True)
    return zc * lax.rsqrt(var + NORM_EPS) * g + b


def _rope(z, c, sa, sb, shift):
    return z * c + pltpu.roll(z, LANES - shift, 1) * sa + pltpu.roll(z, shift, 1) * sb


def _first_head(shape):
    return lax.broadcasted_iota(jnp.int32, shape, 1) < HEAD_DIM


def _pair_attention(q2, k2, v3, first, valid=None, base2=False):
    res = []
    for pick in (first, jnp.logical_not(first)):
        s = _dot_nt(jnp.where(pick, q2, jnp.zeros_like(q2)), k2)
        if valid is not None:
            s = jnp.where(valid, s, NEG_BIG)
        m = jnp.max(s, axis=1, keepdims=True)
        p = jnp.exp2(s - m) if base2 else jnp.exp(s - m)
        res.append((_dot(p.astype(BF16), v3), m))
    return res


def _pair_output(res, first):
    (ol_a, _), (ol_b, _) = res
    return jnp.where(first, ol_a[:, :LANES], ol_b[:, :LANES]) / jnp.where(first, ol_a[:, LANES:], ol_b[:, LANES:])


def _with_ones(v2):
    return jnp.concatenate([v2, jnp.ones_like(v2)], axis=1)


def _proj_kernel(x_ref, w_ref, o_ref):
    o_ref[...] = _dot(x_ref[...].astype(BF16), w_ref[...]).astype(o_ref.dtype)


def _proj(x, w, tm=TOK_TILE):
    n, k = x.shape
    width = w.shape[1]
    return pl.pallas_call(
        _proj_kernel,
        grid=(n // tm,),
        in_specs=[pl.BlockSpec((tm, k), lambda i: (i, 0)), pl.BlockSpec((k, width), lambda i: (0, 0))],
        out_specs=pl.BlockSpec((tm, width), lambda i: (i, 0)),
        out_shape=jax.ShapeDtypeStruct((n, width), BF16),
        compiler_params=_params(("parallel",)),
        name="proj",
    )(x, w)


def _a_in_kernel(x_ref, w_ref, m_ref, qg_ref, kg_ref, c_ref, sa_ref, sb_ref, q_ref, k_ref, v_ref, qm_ref):
    y = _dot(x_ref[...].astype(BF16), w_ref[...])
    m = m_ref[...]
    c, sa, sb = c_ref[...], sa_ref[...], sb_ref[...]

    def norm_rope(z, gain):
        hi, lo = _split_bf16(z * z)
        ms = (_dot(hi, m) + _dot(lo, m)) * (1.0 / HEAD_DIM)
        z = z * lax.rsqrt(ms + NORM_EPS) * gain
        return _rope(z, c, sa, sb, 16)

    nq = A_Q_HEADS * HEAD_DIM // LANES
    nk = A_KV_HEADS * HEAD_DIM // LANES
    for ch in range(nq):
        q_ref[:, ch * LANES:(ch + 1) * LANES] = norm_rope(y[:, ch * LANES:(ch + 1) * LANES], qg_ref[...]).astype(BF16)
    off = nq * LANES
    for ch in range(nk):
        z = y[:, off + ch * LANES: off + (ch + 1) * LANES]
        k_ref[:, ch * LANES:(ch + 1) * LANES] = norm_rope(z, kg_ref[...]).astype(BF16)
    off += nk * LANES
    ones = jnp.ones((y.shape[0], LANES), BF16)
    for ch in range(nk):
        v_ref[:, 2 * ch * LANES:(2 * ch + 1) * LANES] = y[:, off + ch * LANES: off + (ch + 1) * LANES].astype(BF16)
        v_ref[:, (2 * ch + 1) * LANES:(2 * ch + 2) * LANES] = ones
    off += nk * LANES
    qm_ref[...] = (y[:, off:off + MEM_W] * SCALE).astype(BF16)


def _a_in_proj(x, w, m128, qg, kg, tabs, tm=TOK_TILE):
    c, sa, sb = tabs
    nblk = SEQ // tm
    a_in = w.shape[1]
    tab_spec = pl.BlockSpec((tm, LANES), lambda i: (i % nblk, 0))
    row = lambda width: pl.BlockSpec((tm, width), lambda i: (i, 0))
    const = lambda shape: pl.BlockSpec(shape, lambda i: (0, 0))
    qw, kw = A_Q_HEADS * HEAD_DIM, A_KV_HEADS * HEAD_DIM
    return pl.pallas_call(
        _a_in_kernel,
        grid=(N_TOK // tm,),
        in_specs=[row(D_MODEL), const((D_MODEL, a_in)), const((LANES, LANES)), const((1, LANES)), const((1, LANES)),
                  tab_spec, tab_spec, tab_spec],
        out_specs=[row(qw), row(kw), row(2 * kw), row(MEM_W)],
        out_shape=[jax.ShapeDtypeStruct((N_TOK, qw), BF16), jax.ShapeDtypeStruct((N_TOK, kw), BF16),
                   jax.ShapeDtypeStruct((N_TOK, 2 * kw), BF16), jax.ShapeDtypeStruct((N_TOK, MEM_W), BF16)],
        compiler_params=_params(("parallel",)),
        name="a_in_proj",
    )(x, w, m128, qg, kg, c, sa, sb)


def _b_in_kernel(*refs):
    ng = len(DILATIONS)
    nx = D_MODEL // LANES
    x_refs, w_ref, refs = refs[:nx], refs[nx], refs[nx + 1:]
    tabs, outs, qm_ref = refs[:3 * ng], refs[3 * ng:6 * ng], refs[6 * ng]
    xb1 = None
    for g, d in enumerate(DILATIONS):
        n = TOK_TILE // d
        if d == 1:
            cols = [r[...] for r in x_refs]
        else:
            cols = [jnp.concatenate([r[pl.ds(k, n, stride=d), :] for k in range(d)], axis=0) for r in x_refs]
        xb = jnp.concatenate(cols, axis=1).astype(BF16)
        xb1 = xb if d == 1 else xb1
        c, sa, sb = (t[...].reshape(TOK_TILE, LANES) for t in tabs[3 * g:3 * g + 3])
        q_ref, k_ref, v_ref = outs[3 * g:3 * g + 3]
        yq = _dot(xb, w_ref[:, g * B_WIDTH:(g + 1) * B_WIDTH]) * SCALE
        yk = _dot(xb, w_ref[:, (ng + g) * B_WIDTH:(ng + g + 1) * B_WIDTH])
        yv = _dot(xb, w_ref[:, (2 * ng + g) * B_WIDTH:(2 * ng + g + 1) * B_WIDTH])
        for ch in range(B_WIDTH // LANES):
            sl = slice(ch * LANES, (ch + 1) * LANES)
            q_ref[0, :, :, sl] = _rope(yq[:, sl], c, sa, sb, PARTIAL_ROT_DIMS // 2).reshape(d, n, LANES).astype(BF16)
            k_ref[0, :, :, sl] = _rope(yk[:, sl], c, sa, sb, PARTIAL_ROT_DIMS // 2).reshape(d, n, LANES).astype(BF16)
        v_ref[0] = yv.reshape(d, n, B_WIDTH).astype(BF16)
    qm = _dot(xb1, w_ref[:, 3 * ng * B_WIDTH:3 * ng * B_WIDTH + MEM_W])
    qm_ref[...] = (qm * SCALE).astype(BF16)


def _b_in_proj(x, w, tabs):
    nx = D_MODEL // LANES
    in_specs = [pl.BlockSpec((TOK_TILE, LANES), functools.partial(lambda i, c: (i, c), c=c)) for c in range(nx)]
    in_specs.append(pl.BlockSpec(w.shape, lambda i: (0, 0)))
    out_specs, out_shape = [], []
    for d in DILATIONS:
        n = TOK_TILE // d
        in_specs += [pl.BlockSpec((d, n, LANES), lambda i: (0, i % TILES_PER_SEQ, 0))] * 3
        out_specs += [pl.BlockSpec((1, d, n, B_WIDTH), lambda i: (i // TILES_PER_SEQ, 0, i % TILES_PER_SEQ, 0))] * 3
        out_shape += [jax.ShapeDtypeStruct((BATCH, d, SEQ // d, B_WIDTH), BF16)] * 3
    out_specs.append(pl.BlockSpec((TOK_TILE, MEM_W), lambda i: (i, 0)))
    out_shape.append(jax.ShapeDtypeStruct((N_TOK, MEM_W), BF16))
    res = pl.pallas_call(
        _b_in_kernel,
        grid=(N_TOK // TOK_TILE,),
        in_specs=in_specs,
        out_specs=out_specs,
        out_shape=out_shape,
        compiler_params=_params(("parallel",)),
        name="b_in_proj",
    )(*([x] * nx), w, *[t for group in tabs for t in group])
    return [res[3 * g:3 * g + 3] for g in range(len(DILATIONS))], res[-1]


def _gqa_kernel(q_ref, k_ref, v_ref, o_ref):
    tq = q_ref.shape[0]
    ntile = q_ref.shape[1] // LANES
    q = jnp.concatenate([q_ref[:, t * LANES:(t + 1) * LANES] for t in range(ntile)], axis=0)
    first = _first_head(q.shape)
    res = []
    for pick in (first, jnp.logical_not(first)):
        qh = jnp.where(pick, q, jnp.zeros_like(q))
        m = jnp.full((q.shape[0], 1), -jnp.inf, F32)
        acc = jnp.zeros((q.shape[0], 2 * LANES), F32)
        for c in range(SEQ // GQA_KC):
            keys = slice(c * GQA_KC, (c + 1) * GQA_KC)
            s = _dot_nt(qh, k_ref[keys, :])
            m_new = jnp.maximum(m, jnp.max(s, axis=1, keepdims=True))
            p = jnp.exp2(s - m_new).astype(BF16)
            acc = acc * jnp.exp2(m - m_new) + _dot(p, v_ref[keys, :])
            m = m_new
        res.append((acc, m))
    o = _pair_output(res, first)
    for t in range(ntile):
        o_ref[:, t * LANES:(t + 1) * LANES] = o[t * tq:(t + 1) * tq].astype(BF16)


def _gqa_attention(q, k, v, tq=512):
    nq = SEQ // tq
    qw = q.shape[1] // 2
    kw = k.shape[1] // 2
    return pl.pallas_call(
        _gqa_kernel,
        grid=(BATCH, 2, nq),
        in_specs=[pl.BlockSpec((tq, qw), lambda b, p, i: (b * nq + i, p)),
                  pl.BlockSpec((SEQ, kw), lambda b, p, i: (b, p)),
                  pl.BlockSpec((SEQ, 2 * kw), lambda b, p, i: (b, p))],
        out_specs=pl.BlockSpec((tq, qw), lambda b, p, i: (b * nq + i, p)),
        out_shape=jax.ShapeDtypeStruct(q.shape, BF16),
        compiler_params=_params(("parallel", "parallel", "parallel")),
        name="gqa_attention",
    )(q, k, v)


def _band_kernel(q_ref, k_ref, v_ref, o_ref, lse_ref, *, length, seg, win):
    tq = BAND_TQ
    first = _first_head((tq, LANES))
    lane = lax.broadcasted_iota(jnp.int32, (tq, LANES), 1)
    row = lax.broadcasted_iota(jnp.int32, (tq, win), 0)
    col = lax.broadcasted_iota(jnp.int32, (tq, win), 1)
    for u in range(BAND_TILES):
        tile = pl.program_id(1) * BAND_TILES + u
        start = pl.multiple_of(jnp.clip(tile * tq - BAND_RADIUS, 0, length - win), BAND_RADIUS)
        qpos, kpos = tile * tq + row, start + col
        valid = jnp.abs(qpos - kpos) <= BAND_RADIUS
        if seg < length:
            valid = jnp.logical_and(valid, qpos // seg == kpos // seg)
        rows = slice(u * tq, (u + 1) * tq)
        lse = jnp.zeros((tq, LANES), F32)
        for j in range(B_WIDTH // LANES):
            sl = slice(j * LANES, (j + 1) * LANES)
            res = _pair_attention(q_ref[0, rows, sl], k_ref[0, pl.ds(start, win), sl],
                                  _with_ones(v_ref[0, pl.ds(start, win), sl]), first, valid)
            o_ref[0, rows, sl] = _pair_output(res, first).astype(BF16)
            for half, (ol, m) in enumerate(res):
                lse = jnp.where(lane == 2 * j + half, m + jnp.log(ol[:, LANES:]), lse)
        lse_ref[0, rows, :] = lse


def _band_attention(q, k, v, d):
    seg = SEQ // d
    step_rows = BAND_TILES * BAND_TQ
    length = max(seg, step_rows)
    win = BAND_TQ + 2 * BAND_RADIUS
    nblk = N_TOK // length
    q, k, v = (t.reshape(nblk, length, B_WIDTH) for t in (q, k, v))
    seq_spec = pl.BlockSpec((1, length, B_WIDTH), lambda s, i: (s, 0, 0))
    o, lse = pl.pallas_call(
        functools.partial(_band_kernel, length=length, seg=seg, win=win),
        grid=(nblk, length // step_rows),
        in_specs=[pl.BlockSpec((1, step_rows, B_WIDTH), lambda s, i: (s, i, 0)), seq_spec, seq_spec],
        out_specs=[pl.BlockSpec((1, step_rows, B_WIDTH), lambda s, i: (s, i, 0)),
                   pl.BlockSpec((1, step_rows, LANES), lambda s, i: (s, i, 0))],
        out_shape=[jax.ShapeDtypeStruct((nblk, length, B_WIDTH), BF16),
                   jax.ShapeDtypeStruct((nblk, length, LANES), F32)],
        compiler_params=_params(("parallel", "parallel")),
        name=f"band_attention_d{d}",
    )(q, k, v)
    return o.reshape(BATCH, d, seg, B_WIDTH), lse.reshape(BATCH, d, seg, LANES)


def _route(x, wh_ref, wl_ref, rb_ref, tri_ref, xa_ref, cls_ref, rank_ref, cnt_ref, carry_ref):
    tm = x.shape[0]
    xh, xl = _split_bf16(x)
    logits = _dot_nt(wh_ref[...], xh) + _dot_nt(wh_ref[...], xl) + _dot_nt(wl_ref[...], xh)
    aff = 1.0 / (1.0 + jnp.exp(-logits))
    sel = aff + rb_ref[...]
    s = [sel[e:e + 1, :] for e in range(N_EXPERTS)]
    a = [aff[e:e + 1, :] for e in range(N_EXPERTS)]

    def top2_sum(v):
        hi01, lo01 = jnp.maximum(v[0], v[1]), jnp.minimum(v[0], v[1])
        hi23, lo23 = jnp.maximum(v[2], v[3]), jnp.minimum(v[2], v[3])
        return jnp.maximum(hi01, hi23) + jnp.maximum(jnp.minimum(hi01, hi23), jnp.maximum(lo01, lo23))

    gscore = [top2_sum(s[EXPERTS_PER_GROUP * g:EXPERTS_PER_GROUP * (g + 1)]) for g in range(N_EXPERT_GROUPS)]
    best = jnp.zeros((1, tm), jnp.int32)
    best_score = gscore[0]
    for g in range(1, N_EXPERT_GROUPS):
        better = gscore[g] > best_score
        best = jnp.where(better, g, best)
        best_score = jnp.where(better, gscore[g], best_score)

    def pick(rows, j):
        out = rows[j]
        for g in range(1, N_EXPERT_GROUPS):
            out = jnp.where(best == g, rows[EXPERTS_PER_GROUP * g + j], out)
        return out

    t = [pick(s, j) for j in range(EXPERTS_PER_GROUP)]
    w = [pick(a, j) for j in range(EXPERTS_PER_GROUP)]

    def first_max(v):
        mx = jnp.maximum(jnp.maximum(v[0], v[1]), jnp.maximum(v[2], v[3]))
        idx = jnp.full((1, tm), EXPERTS_PER_GROUP - 1, jnp.int32)
        for j in range(EXPERTS_PER_GROUP - 2, -1, -1):
            idx = jnp.where(v[j] == mx, j, idx)
        return idx

    i1 = first_max(t)
    i2 = first_max([jnp.where(i1 == j, -jnp.inf, t[j]) for j in range(EXPERTS_PER_GROUP)])
    lo, hi = jnp.minimum(i1, i2), jnp.maximum(i1, i2)

    def take(rows, idx):
        out = rows[0]
        for j in range(1, EXPERTS_PER_GROUP):
            out = jnp.where(idx == j, rows[j], out)
        return out

    w_lo, w_hi = take(w, lo), take(w, hi)
    den = w_lo + w_hi
    pair = jnp.where(lo == 0, hi - 1, jnp.where(lo == 1, hi + 1, len(PAIRS) - 1))
    cls = best * len(PAIRS) + pair

    onehot = (lax.broadcasted_iota(jnp.int32, (CLASS_ROWS, tm), 0) == cls).astype(F32)
    before = _dot(onehot.astype(BF16), tri_ref[...]) + carry_ref[:, 0:1]
    rank = jnp.sum(onehot * before, axis=0, keepdims=True)
    carry_ref[...] = carry_ref[...] + jnp.sum(onehot, axis=1, keepdims=True)

    cls_ref[0] = cls
    rank_ref[0] = rank.astype(jnp.int32)
    cnt_ref[...] = carry_ref[...]
    gates = jnp.concatenate([w_lo / den, w_hi / den, jnp.zeros((LANES - 2, tm), F32)], axis=0)
    xa_ref[:, :D_MODEL] = x
    xa_ref[:, D_MODEL:] = gates.T


def _post_attn_kernel(*refs, merge):
    if merge:
        o_refs, l_refs, e_ref, refs = refs[:3], refs[3:6], refs[6], refs[7:]
    else:
        mix_ref, refs = refs[0], refs[1:]
    (qm_ref, mk_ref, mv_ref, w1_ref, w2_ref, x_ref, g_ref, b_ref, wh_ref, wl_ref, rb_ref, tri_ref,
     xa_ref, cls_ref, rank_ref, cnt_ref, carry_ref) = refs[:17]

    @pl.when(pl.program_id(0) == 0)
    def _():
        carry_ref[...] = jnp.zeros_like(carry_ref)

    if merge:
        ot_ref, lt_ref = refs[17], refs[18]
        nch = B_WIDTH // LANES
        for g, d in enumerate(DILATIONS):
            n = TOK_TILE // d
            for r in range(d):
                rows = pl.ds(r, n, stride=d) if d > 1 else slice(None)
                og = o_refs[g][0, r].astype(F32)
                for ch in range(nch):
                    ot_ref[g, ch, rows, :] = og[:, ch * LANES:(ch + 1) * LANES]
                lt_ref[g, rows, :] = l_refs[g][0, r]
        lses = [lt_ref[g] for g in range(len(DILATIONS))]
        m = functools.reduce(jnp.maximum, lses)
        es = [jnp.exp(l - m) for l in lses]
        inv = 1.0 / functools.reduce(jnp.add, es)
        expand = e_ref[...]

        def widen(w):
            hi, lo = _split_bf16(w)
            return _dot(hi, expand) + _dot(lo, expand)

        ws = [widen(e * inv) for e in es]
        mix = jnp.concatenate(
            [functools.reduce(jnp.add, [w[:, ch * LANES:(ch + 1) * LANES] * ot_ref[g, ch] for g, w in enumerate(ws)])
             for ch in range(nch)], axis=1).astype(BF16)
    else:
        mix = mix_ref[...]

    first = _first_head((TOK_TILE, LANES))
    memo = []
    for j in range(MEM_W // LANES):
        sl = slice(j * LANES, (j + 1) * LANES)
        memo.append(_pair_output(_pair_attention(qm_ref[:, sl], mk_ref[:, sl], _with_ones(mv_ref[:, sl]), first), first))
    memo = jnp.concatenate(memo, axis=1).astype(BF16)

    attn = _dot(mix, w1_ref[...]) + _dot(memo, w2_ref[...])
    x1 = _layer_norm(ALPHA * x_ref[...] + attn, g_ref[...], b_ref[...])
    _route(x1, wh_ref, wl_ref, rb_ref, tri_ref, xa_ref, cls_ref, rank_ref, cnt_ref, carry_ref)


def _post_attn(mix, qm, mkv, w1, w2, x, g, b, router, expand=None):
    merge = expand is not None
    wh, wl, rb, tri = router
    nblk = N_TOK // TOK_TILE
    row = lambda width: pl.BlockSpec((TOK_TILE, width), lambda i: (i, 0))
    const = lambda shape: pl.BlockSpec(shape, lambda i: (0,) * len(shape))
    row3 = pl.BlockSpec((1, 1, TOK_TILE), lambda i: (i, 0, 0))
    scratch = [pltpu.VMEM((CLASS_ROWS, LANES), F32)]
    if merge:
        outs, lses = mix
        resid = lambda d, width: pl.BlockSpec((1, d, TOK_TILE // d, width),
                                              lambda i: (i // TILES_PER_SEQ, 0, i % TILES_PER_SEQ, 0))
        lead_specs = ([resid(d, B_WIDTH) for d in DILATIONS] + [resid(d, LANES) for d in DILATIONS]
                      + [const(expand.shape)])
        lead = [*outs, *lses, expand]
        scratch += [pltpu.VMEM((len(DILATIONS), B_WIDTH // LANES, TOK_TILE, LANES), F32),
                    pltpu.VMEM((len(DILATIONS), TOK_TILE, LANES), F32)]
    else:
        lead_specs, lead = [row(mix.shape[1])], [mix]
    xa, cls, rank, cnt = pl.pallas_call(
        functools.partial(_post_attn_kernel, merge=merge),
        grid=(nblk,),
        in_specs=lead_specs + [row(MEM_W),
                               pl.BlockSpec((N_MEM, MEM_W), lambda i: (i // TILES_PER_SEQ, 0)),
                               pl.BlockSpec((N_MEM, MEM_W), lambda i: (i // TILES_PER_SEQ, 1)),
                               const(w1.shape), const(w2.shape), row(D_MODEL), const((1, D_MODEL)), const((1, D_MODEL)),
                               const(wh.shape), const(wl.shape), const(rb.shape), const(tri.shape)],
        out_specs=[row(XA_W), row3, row3, const((CLASS_ROWS, LANES))],
        out_shape=[jax.ShapeDtypeStruct((N_TOK, XA_W), F32), jax.ShapeDtypeStruct((nblk, 1, TOK_TILE), jnp.int32),
                   jax.ShapeDtypeStruct((nblk, 1, TOK_TILE), jnp.int32), jax.ShapeDtypeStruct((CLASS_ROWS, LANES), F32)],
        scratch_shapes=scratch,
        compiler_params=_params(("arbitrary",)),
        name="post_attn_merge" if merge else "post_attn",
    )(*lead, qm, mkv, mkv, w1, w2, x, g, b, wh, wl, rb, tri)
    return xa, cls.reshape(N_TOK), rank.reshape(N_TOK), cnt[:N_CLASSES, 0].astype(jnp.int32)


def _dispatch_kernel(pos_ref, xa_ref, init_ref, xs_ref, sem, *, chunk):
    del init_ref
    base = pl.program_id(0) * chunk

    def issue(blk, carry):
        for u in range(ROW_DMA_UNROLL):
            r = blk * ROW_DMA_UNROLL + u
            pltpu.make_async_copy(xa_ref.at[pl.ds(r, 1)], xs_ref.at[pl.ds(pos_ref[base + r], 1)], sem).start(
                priority=u % 2)
        return carry

    def retire(r, carry):
        pltpu.make_async_copy(xa_ref.at[pl.ds(0, 1)], xs_ref.at[pl.ds(0, 1)], sem).wait()
        return carry

    lax.fori_loop(0, chunk // ROW_DMA_UNROLL, issue, 0)
    lax.fori_loop(0, chunk, retire, 0, unroll=8)


def _dispatch(pos, xa, chunk=512):
    init = jnp.zeros((MOE_ROWS, XA_W), F32)
    return pl.pallas_call(
        functools.partial(_dispatch_kernel, chunk=chunk),
        grid_spec=pltpu.PrefetchScalarGridSpec(
            num_scalar_prefetch=1,
            grid=(N_TOK // chunk,),
            in_specs=[pl.BlockSpec((chunk, XA_W), lambda i, p: (i, 0)), pl.BlockSpec(memory_space=pl.ANY)],
            out_specs=pl.BlockSpec(memory_space=pl.ANY),
            scratch_shapes=[pltpu.SemaphoreType.DMA(())],
        ),
        out_shape=jax.ShapeDtypeStruct((MOE_ROWS, XA_W), F32),
        input_output_aliases={2: 0},
        compiler_params=_params(("arbitrary",)),
        name="moe_dispatch",
    )(pos, xa, init)


def _expert_kernel(e1_ref, e2_ref, used_ref, fresh_ref, xs_ref, wg1_ref, wu1_ref, wd1_ref, wg2_ref, wu2_ref, wd2_ref,
                   ys_ref, wg_s, wu_s, wd_s):
    i = pl.program_id(0)

    @pl.when(fresh_ref[i] != 0)
    def _():
        for slot, (wg_ref, wu_ref, wd_ref) in enumerate(((wg1_ref, wu1_ref, wd1_ref), (wg2_ref, wu2_ref, wd2_ref))):
            wg_s[slot] = wg_ref[0].astype(BF16)
            wu_s[slot] = wu_ref[0].astype(BF16)
            wd_s[slot] = wd_ref[0].astype(BF16)

    @pl.when(used_ref[i] != 0)
    def _():
        x = xs_ref[:, :D_MODEL].astype(BF16)
        gates = xs_ref[:, D_MODEL:]

        def expert(slot):
            gate = _dot(x, wg_s[slot])
            h = gate * (1.0 / (1.0 + jnp.exp(-gate))) * _dot(x, wu_s[slot])
            return _dot(h.astype(BF16), wd_s[slot])

        ys_ref[...] = gates[:, 0:1] * expert(0) + gates[:, 1:2] * expert(1)

    @pl.when(used_ref[i] == 0)
    def _():
        ys_ref[...] = jnp.zeros_like(ys_ref)


def _experts(tile_e1, tile_e2, tile_used, tile_fresh, xs, wg, wu, wd):
    up = lambda sel: pl.BlockSpec((1, D_MODEL, D_EXPERT), lambda i, e1, e2, u, f: ((e1, e2)[sel][i], 0, 0))
    down = lambda sel: pl.BlockSpec((1, D_EXPERT, D_MODEL), lambda i, e1, e2, u, f: ((e1, e2)[sel][i], 0, 0))
    return pl.pallas_call(
        _expert_kernel,
        grid_spec=pltpu.PrefetchScalarGridSpec(
            num_scalar_prefetch=4,
            grid=(MOE_TILES,),
            in_specs=[pl.BlockSpec((MOE_TM, XA_W), lambda i, e1, e2, u, f: (i, 0)),
                      up(0), up(0), down(0), up(1), up(1), down(1)],
            out_specs=pl.BlockSpec((MOE_TM, D_MODEL), lambda i, e1, e2, u, f: (i, 0)),
            scratch_shapes=[pltpu.VMEM((2, D_MODEL, D_EXPERT), BF16), pltpu.VMEM((2, D_MODEL, D_EXPERT), BF16),
                            pltpu.VMEM((2, D_EXPERT, D_MODEL), BF16)],
        ),
        out_shape=jax.ShapeDtypeStruct((MOE_ROWS, D_MODEL), F32),
        compiler_params=_params(("arbitrary",)),
        name="moe_experts",
    )(tile_e1, tile_e2, tile_used, tile_fresh, xs, wg, wu, wd, wg, wu, wd)


def _combine_kernel(pos_ref, ys_ref, x_ref, g_ref, b_ref, o_ref, buf, sem, *, tc):
    i = pl.program_id(0)
    slot = i % 2

    def gather(step, into):
        def issue(blk, carry):
            for u in range(ROW_DMA_UNROLL):
                r = blk * ROW_DMA_UNROLL + u
                row = pos_ref[step * tc + r]
                pltpu.make_async_copy(ys_ref.at[pl.ds(row, 1)], buf.at[into, pl.ds(r, 1)], sem.at[into]).start(
                    priority=u % 2)
            return carry

        lax.fori_loop(0, tc // ROW_DMA_UNROLL, issue, 0)

    @pl.when(i == 0)
    def _():
        gather(0, 0)

    @pl.when(i + 1 < pl.num_programs(0))
    def _():
        gather(i + 1, 1 - slot)

    def retire(r, carry):
        pltpu.make_async_copy(ys_ref.at[pl.ds(0, 1)], buf.at[slot, pl.ds(0, 1)], sem.at[slot]).wait()
        return carry

    lax.fori_loop(0, tc, retire, 0, unroll=8)
    o_ref[...] = _layer_norm(ALPHA * x_ref[...] + buf[slot], g_ref[...], b_ref[...])


def _combine_ln(pos, ys, xa, g, b, tc=256):
    return pl.pallas_call(
        functools.partial(_combine_kernel, tc=tc),
        grid_spec=pltpu.PrefetchScalarGridSpec(
            num_scalar_prefetch=1,
            grid=(N_TOK // tc,),
            in_specs=[pl.BlockSpec(memory_space=pl.ANY),
                      pl.BlockSpec((tc, D_MODEL), lambda i, p: (i, 0)),
                      pl.BlockSpec((1, D_MODEL), lambda i, p: (0, 0)),
                      pl.BlockSpec((1, D_MODEL), lambda i, p: (0, 0))],
            out_specs=pl.BlockSpec((tc, D_MODEL), lambda i, p: (i, 0)),
            scratch_shapes=[pltpu.VMEM((2, tc, D_MODEL), F32), pltpu.SemaphoreType.DMA((2,))],
        ),
        out_shape=jax.ShapeDtypeStruct((N_TOK, D_MODEL), F32),
        compiler_params=_params(("arbitrary",)),
        name="moe_combine_ln",
    )(pos, ys, xa, g, b)


_CLASS_E1 = np.array([EXPERTS_PER_GROUP * g + p[0] for g in range(N_EXPERT_GROUPS) for p in PAIRS], np.int32)
_CLASS_E2 = np.array([EXPERTS_PER_GROUP * g + p[1] for g in range(N_EXPERT_GROUPS) for p in PAIRS], np.int32)


def _moe_ln(routed, layer, wg, wu, wd, g, b):
    xa, cls, rank, counts = routed
    padded = (counts + MOE_TM - 1) // MOE_TM * MOE_TM
    ends = jnp.cumsum(padded)
    starts = ends - padded
    pos = starts[cls] + rank
    tile_start = jnp.arange(MOE_TILES, dtype=jnp.int32) * MOE_TM
    tile_used = (tile_start < ends[-1]).astype(jnp.int32)
    last_cls = jnp.max(jnp.where(counts > 0, jnp.arange(N_CLASSES), 0))
    tile_cls = jnp.minimum(jnp.sum(tile_start[:, None] >= ends[None, :], axis=1), last_cls)
    prev_cls = jnp.concatenate([jnp.full((1,), -1, tile_cls.dtype), tile_cls[:-1]])
    tile_fresh = tile_used * (tile_cls != prev_cls).astype(jnp.int32)
    tile_e1 = jnp.asarray(_CLASS_E1 + layer * N_EXPERTS)[tile_cls]
    tile_e2 = jnp.asarray(_CLASS_E2 + layer * N_EXPERTS)[tile_cls]
    xs = _dispatch(pos, xa)
    ys = _experts(tile_e1, tile_e2, tile_used, tile_fresh, xs, wg, wu, wd)
    return _combine_ln(pos, ys, xa, g, b)


def _tile_heads(t64):
    return jnp.concatenate([t64, t64], axis=-1)


def _axial_tables():
    rows = SEQ // GRID_W
    row = jnp.repeat(jnp.arange(rows), GRID_W).astype(F32)
    col = jnp.tile(jnp.arange(GRID_W), rows).astype(F32)
    half = HEAD_DIM // 2
    inv = 1.0 / (AXIAL_THETA ** (jnp.arange(0, half, 2, dtype=F32) / half))
    ar, ac = row[:, None] * inv, col[:, None] * inv
    z = jnp.zeros_like(ar)
    c = jnp.concatenate([jnp.cos(ar), jnp.cos(ar), jnp.cos(ac), jnp.cos(ac)], -1)
    sa = jnp.concatenate([-jnp.sin(ar), z, -jnp.sin(ac), z], -1)
    sb = jnp.concatenate([z, jnp.sin(ar), z, jnp.sin(ac)], -1)
    return tuple(_tile_heads(t) for t in (c, sa, sb))


def _partial_tables():
    n = PARTIAL_ROT_DIMS
    pos = jnp.arange(SEQ, dtype=F32)
    ang = pos[:, None] * (1.0 / (PARTIAL_THETA ** (jnp.arange(0, n, 2, dtype=F32) / n)))
    z = jnp.zeros_like(ang)
    rest = HEAD_DIM - n
    c = jnp.concatenate([jnp.cos(ang), jnp.cos(ang), jnp.ones((SEQ, rest), F32)], -1)
    sa = jnp.concatenate([-jnp.sin(ang), z, jnp.zeros((SEQ, rest), F32)], -1)
    sb = jnp.concatenate([z, jnp.sin(ang), jnp.zeros((SEQ, rest), F32)], -1)
    return tuple(_tile_heads(t) for t in (c, sa, sb))


def _residue_order(tab, d):
    return tab.reshape(SEQ // d, d, LANES).transpose(1, 0, 2)


def kernel(x, mem, w_mem_kv, router_w, router_b, a_w_in, a_w_out, a_q_norm, a_k_norm, b_w_in, b_w_out,
           ln1_g, ln1_b, ln2_g, ln2_b, w_gate, w_up, w_down):
    x = x.reshape(N_TOK, D_MODEL)
    row = lambda v: v.reshape(1, -1)

    lane = np.arange(LANES)
    m128 = jnp.asarray((lane[:, None] // HEAD_DIM == lane[None, :] // HEAD_DIM), BF16)
    expand = jnp.asarray(lane[:, None] == np.arange(B_WIDTH)[None, :] // HEAD_DIM, BF16)
    tri = jnp.asarray(np.triu(np.ones((TOK_TILE, TOK_TILE), np.float32), 1), BF16)
    rw_hi = router_w.T.astype(BF16)
    rw_lo = (router_w.T - rw_hi.astype(F32)).astype(BF16)
    router = (rw_hi, rw_lo, router_b.reshape(N_EXPERTS, 1), tri)

    mkv = _proj(mem.reshape(BATCH * N_MEM, D_MODEL), w_mem_kv.astype(BF16))
    experts = tuple(w.reshape(DEPTH * N_EXPERTS, *w.shape[2:]) for w in (w_gate, w_up, w_down))

    kmix = A_Q_HEADS * HEAD_DIM
    qcols = (np.asarray(_Q_HEAD_ORDER)[:, None] * HEAD_DIM + np.arange(HEAD_DIM)[None, :]).reshape(-1)
    w_in = jnp.concatenate([a_w_in[0][:, qcols], a_w_in[0][:, kmix:]], axis=1).astype(BF16)
    qg = row(jnp.tile(a_q_norm[0], 2) * (SCALE * LOG2_E))
    kg = row(jnp.tile(a_k_norm[0], 2))
    q, k, v, qm = _a_in_proj(x, w_in, m128, qg, kg, _axial_tables())
    mix = _gqa_attention(q, k, v)
    w_out = a_w_out[0].astype(BF16)
    routed = _post_attn(mix, qm, mkv, w_out[qcols], w_out[kmix:], x, row(ln1_g[0]), row(ln1_b[0]), router)
    x = _moe_ln(routed, 0, *experts, row(ln2_g[0]), row(ln2_b[0]))

    ptabs = _partial_tables()
    tabs = [tuple(_residue_order(t, d) for t in ptabs) for d in DILATIONS]
    qkv, qm = _b_in_proj(x, b_w_in[0].astype(BF16), tabs)
    outs, lses = zip(*[_band_attention(*qkv[g], d) for g, d in enumerate(DILATIONS)])
    w_out = b_w_out[0].astype(BF16)
    routed = _post_attn((outs, lses), qm, mkv, w_out[:B_WIDTH], w_out[B_WIDTH:], x, row(ln1_g[1]), row(ln1_b[1]),
                        router, expand)
    x = _moe_ln(routed, 1, *experts, row(ln2_g[1]), row(ln2_b[1]))
    return x.reshape(BATCH, SEQ, D_MODEL)
```

```python
import functools
import math

import numpy as np
import jax
import jax.numpy as jnp
from jax import lax
from jax.experimental import pallas as pl
from jax.experimental.pallas import tpu as pltpu

F32 = jnp.float32
BF16 = jnp.bfloat16

D_MODEL = 1024
BATCH = 8
SEQ = 2048
N_TOK = BATCH * SEQ
HEAD_DIM = 64
GRID_W = 64
N_MEM = 256
MEM_HEADS = 4
A_Q_HEADS = 12
A_KV_HEADS = 4
AXIAL_THETA = 10000.0
B_HEADS = 8
B_WIDTH = B_HEADS * HEAD_DIM
DILATIONS = (1, 4, 16)
BAND_RADIUS = 64
PARTIAL_ROT_DIMS = HEAD_DIM // 4
PARTIAL_THETA = 500000.0
N_EXPERTS = 16
N_EXPERT_GROUPS = 4
EXPERTS_PER_GROUP = 4
D_EXPERT = 512
DEPTH = 2
ALPHA = (2 * DEPTH) ** 0.25
NORM_EPS = 1e-6
NEG_BIG = -1e30
SCALE = HEAD_DIM ** -0.5
LOG2_E = math.log2(math.e)
_Q_HEAD_ORDER = tuple(6 * p + 3 * half + t for p in range(2) for t in range(3) for half in range(2))

LANES = 128
MEM_W = MEM_HEADS * HEAD_DIM
TOK_TILE = 512
TILES_PER_SEQ = SEQ // TOK_TILE
GQA_KC = 256
BAND_TQ = 128
BAND_TILES = 4

PAIRS = ((0, 1), (0, 2), (0, 3), (1, 2), (1, 3), (2, 3))
N_CLASSES = N_EXPERT_GROUPS * len(PAIRS)
CLASS_ROWS = 32
MOE_TM = 256
MOE_ROWS = N_TOK + N_CLASSES * MOE_TM
MOE_TILES = MOE_ROWS // MOE_TM
XA_W = D_MODEL + LANES
ROW_DMA_UNROLL = 8

VMEM_LIMIT = 56 * 1024 * 1024


def _params(sem, vmem=VMEM_LIMIT):
    return pltpu.CompilerParams(dimension_semantics=sem, vmem_limit_bytes=vmem)


def _dot(a, b):
    return jnp.dot(a, b, preferred_element_type=F32)


def _dot_nt(a, b):
    return lax.dot_general(a, b, (((1,), (1,)), ((), ())), preferred_element_type=F32)


def _split_bf16(x):
    hi = x.astype(BF16)
    lo = (x - hi.astype(F32)).astype(BF16)
    return hi, lo


def _layer_norm(z, g, b):
    mu = jnp.mean(z, axis=-1, keepdims=True)
    zc = z - mu
    var = jnp.mean(zc * zc, axis=-1, keepdims=True)
    return zc * lax.rsqrt(var + NORM_EPS) * g + b


def _rope(z, c, sa, sb, shift):
    return z * c + pltpu.roll(z, LANES - shift, 1) * sa + pltpu.roll(z, shift, 1) * sb


def _first_head(shape):
    return lax.broadcasted_iota(jnp.int32, shape, 1) < HEAD_DIM


def _pair_attention(q2, k2, v3, first, valid=None, base2=False):
    res = []
    for pick in (first, jnp.logical_not(first)):
        s = _dot_nt(jnp.where(pick, q2, jnp.zeros_like(q2)), k2)
        if valid is not None:
            s = jnp.where(valid, s, NEG_BIG)
        m = jnp.max(s, axis=1, keepdims=True)
        p = jnp.exp2(s - m) if base2 else jnp.exp(s - m)
        res.append((_dot(p.astype(BF16), v3), m))
    return res


def _pair_output(res, first):
    (ol_a, _), (ol_b, _) = res
    return jnp.where(first, ol_a[:, :LANES], ol_b[:, :LANES]) / jnp.where(first, ol_a[:, LANES:], ol_b[:, LANES:])


def _with_ones(v2):
    return jnp.concatenate([v2, jnp.ones_like(v2)], axis=1)


def _proj_kernel(x_ref, w_ref, o_ref):
    o_ref[...] = _dot(x_ref[...].astype(BF16), w_ref[...]).astype(o_ref.dtype)


def _proj(x, w, tm=TOK_TILE):
    n, k = x.shape
    width = w.shape[1]
    return pl.pallas_call(
        _proj_kernel,
        grid=(n // tm,),
        in_specs=[pl.BlockSpec((tm, k), lambda i: (i, 0)), pl.BlockSpec((k, width), lambda i: (0, 0))],
        out_specs=pl.BlockSpec((tm, width), lambda i: (i, 0)),
        out_shape=jax.ShapeDtypeStruct((n, width), BF16),
        compiler_params=_params(("parallel",)),
        name="proj",
    )(x, w)


def _a_in_kernel(x_ref, w_ref, m_ref, qg_ref, kg_ref, c_ref, sa_ref, sb_ref, q_ref, k_ref, v_ref, qm_ref):
    y = _dot(x_ref[...].astype(BF16), w_ref[...])
    m = m_ref[...]
    c, sa, sb = c_ref[...], sa_ref[...], sb_ref[...]

    def norm_rope(z, gain):
        hi, lo = _split_bf16(z * z)
        ms = (_dot(hi, m) + _dot(lo, m)) * (1.0 / HEAD_DIM)
        z = z * lax.rsqrt(ms + NORM_EPS) * gain
        return _rope(z, c, sa, sb, 16)

    nq = A_Q_HEADS * HEAD_DIM // LANES
    nk = A_KV_HEADS * HEAD_DIM // LANES
    for ch in range(nq):
        q_ref[:, ch * LANES:(ch + 1) * LANES] = norm_rope(y[:, ch * LANES:(ch + 1) * LANES], qg_ref[...]).astype(BF16)
    off = nq * LANES
    for ch in range(nk):
        z = y[:, off + ch * LANES: off + (ch + 1) * LANES]
        k_ref[:, ch * LANES:(ch + 1) * LANES] = norm_rope(z, kg_ref[...]).astype(BF16)
    off += nk * LANES
    ones = jnp.ones((y.shape[0], LANES), BF16)
    for ch in range(nk):
        v_ref[:, 2 * ch * LANES:(2 * ch + 1) * LANES] = y[:, off + ch * LANES: off + (ch + 1) * LANES].astype(BF16)
        v_ref[:, (2 * ch + 1) * LANES:(2 * ch + 2) * LANES] = ones
    off += nk * LANES
    qm_ref[...] = (y[:, off:off + MEM_W] * SCALE).astype(BF16)


def _a_in_proj(x, w, m128, qg, kg, tabs, tm=TOK_TILE):
    c, sa, sb = tabs
    nblk = SEQ // tm
    a_in = w.shape[1]
    tab_spec = pl.BlockSpec((tm, LANES), lambda i: (i % nblk, 0))
    row = lambda width: pl.BlockSpec((tm, width), lambda i: (i, 0))
    const = lambda shape: pl.BlockSpec(shape, lambda i: (0, 0))
    qw, kw = A_Q_HEADS * HEAD_DIM, A_KV_HEADS * HEAD_DIM
    return pl.pallas_call(
        _a_in_kernel,
        grid=(N_TOK // tm,),
        in_specs=[row(D_MODEL), const((D_MODEL, a_in)), const((LANES, LANES)), const((1, LANES)), const((1, LANES)),
                  tab_spec, tab_spec, tab_spec],
        out_specs=[row(qw), row(kw), row(2 * kw), row(MEM_W)],
        out_shape=[jax.ShapeDtypeStruct((N_TOK, qw), BF16), jax.ShapeDtypeStruct((N_TOK, kw), BF16),
                   jax.ShapeDtypeStruct((N_TOK, 2 * kw), BF16), jax.ShapeDtypeStruct((N_TOK, MEM_W), BF16)],
        compiler_params=_params(("parallel",)),
        name="a_in_proj",
    )(x, w, m128, qg, kg, c, sa, sb)


def _b_in_kernel(*refs):
    ng = len(DILATIONS)
    nx = D_MODEL // LANES
    x_refs, w_ref, refs = refs[:nx], refs[nx], refs[nx + 1:]
    tabs, outs, qm_ref = refs[:3 * ng], refs[3 * ng:6 * ng], refs[6 * ng]
    xb1 = None
    for g, d in enumerate(DILATIONS):
        n = TOK_TILE // d
        if d == 1:
            cols = [r[...] for r in x_refs]
        else:
            cols = [jnp.concatenate([r[pl.ds(k, n, stride=d), :] for k in range(d)], axis=0) for r in x_refs]
        xb = jnp.concatenate(cols, axis=1).astype(BF16)
        xb1 = xb if d == 1 else xb1
        c, sa, sb = (t[...].reshape(TOK_TILE, LANES) for t in tabs[3 * g:3 * g + 3])
        q_ref, k_ref, v_ref = outs[3 * g:3 * g + 3]
        yq = _dot(xb, w_ref[:, g * B_WIDTH:(g + 1) * B_WIDTH]) * SCALE
        yk = _dot(xb, w_ref[:, (ng + g) * B_WIDTH:(ng + g + 1) * B_WIDTH])
        yv = _dot(xb, w_ref[:, (2 * ng + g) * B_WIDTH:(2 * ng + g + 1) * B_WIDTH])
        for ch in range(B_WIDTH // LANES):
            sl = slice(ch * LANES, (ch + 1) * LANES)
            q_ref[0, :, :, sl] = _rope(yq[:, sl], c, sa, sb, PARTIAL_ROT_DIMS // 2).reshape(d, n, LANES).astype(BF16)
            k_ref[0, :, :, sl] = _rope(yk[:, sl], c, sa, sb, PARTIAL_ROT_DIMS // 2).reshape(d, n, LANES).astype(BF16)
        v_ref[0] = yv.reshape(d, n, B_WIDTH).astype(BF16)
    qm = _dot(xb1, w_ref[:, 3 * ng * B_WIDTH:3 * ng * B_WIDTH + MEM_W])
    qm_ref[...] = (qm * SCALE).astype(BF16)


def _b_in_proj(x, w, tabs):
    nx = D_MODEL // LANES
    in_specs = [pl.BlockSpec((TOK_TILE, LANES), functools.partial(lambda i, c: (i, c), c=c)) for c in range(nx)]
    in_specs.append(pl.BlockSpec(w.shape, lambda i: (0, 0)))
    out_specs, out_shape = [], []
    for d in DILATIONS:
        n = TOK_TILE // d
        in_specs += [pl.BlockSpec((d, n, LANES), lambda i: (0, i % TILES_PER_SEQ, 0))] * 3
        out_specs += [pl.BlockSpec((1, d, n, B_WIDTH), lambda i: (i // TILES_PER_SEQ, 0, i % TILES_PER_SEQ, 0))] * 3
        out_shape += [jax.ShapeDtypeStruct((BATCH, d, SEQ // d, B_WIDTH), BF16)] * 3
    out_specs.append(pl.BlockSpec((TOK_TILE, MEM_W), lambda i: (i, 0)))
    out_shape.append(jax.ShapeDtypeStruct((N_TOK, MEM_W), BF16))
    res = pl.pallas_call(
        _b_in_kernel,
        grid=(N_TOK // TOK_TILE,),
        in_specs=in_specs,
        out_specs=out_specs,
        out_shape=out_shape,
        compiler_params=_params(("parallel",)),
        name="b_in_proj",
    )(*([x] * nx), w, *[t for group in tabs for t in group])
    return [res[3 * g:3 * g + 3] for g in range(len(DILATIONS))], res[-1]


def _gqa_kernel(q_ref, k_ref, v_ref, o_ref):
    tq = q_ref.shape[0]
    ntile = q_ref.shape[1] // LANES
    q = jnp.concatenate([q_ref[:, t * LANES:(t + 1) * LANES] for t in range(ntile)], axis=0)
    first = _first_head(q.shape)
    res = []
    for pick in (first, jnp.logical_not(first)):
        qh = jnp.where(pick, q, jnp.zeros_like(q))
        m = jnp.full((q.shape[0], 1), -jnp.inf, F32)
        acc = jnp.zeros((q.shape[0], 2 * LANES), F32)
        for c in range(SEQ // GQA_KC):
            keys = slice(c * GQA_KC, (c + 1) * GQA_KC)
            s = _dot_nt(qh, k_ref[keys, :])
            m_new = jnp.maximum(m, jnp.max(s, axis=1, keepdims=True))
            p = jnp.exp2(s - m_new).astype(BF16)
            acc = acc * jnp.exp2(m - m_new) + _dot(p, v_ref[keys, :])
            m = m_new
        res.append((acc, m))
    o = _pair_output(res, first)
    for t in range(ntile):
        o_ref[:, t * LANES:(t + 1) * LANES] = o[t * tq:(t + 1) * tq].astype(BF16)


def _gqa_attention(q, k, v, tq=512):
    nq = SEQ // tq
    qw = q.shape[1] // 2
    kw = k.shape[1] // 2
    return pl.pallas_call(
        _gqa_kernel,
        grid=(BATCH, 2, nq),
        in_specs=[pl.BlockSpec((tq, qw), lambda b, p, i: (b * nq + i, p)),
                  pl.BlockSpec((SEQ, kw), lambda b, p, i: (b, p)),
                  pl.BlockSpec((SEQ, 2 * kw), lambda b, p, i: (b, p))],
        out_specs=pl.BlockSpec((tq, qw), lambda b, p, i: (b * nq + i, p)),
        out_shape=jax.ShapeDtypeStruct(q.shape, BF16),
        compiler_params=_params(("parallel", "parallel", "parallel")),
        name="gqa_attention",
    )(q, k, v)


def _band_kernel(q_ref, k_ref, v_ref, o_ref, lse_ref, *, length, seg, win):
    tq = BAND_TQ
    first = _first_head((tq, LANES))
    lane = lax.broadcasted_iota(jnp.int32, (tq, LANES), 1)
    row = lax.broadcasted_iota(jnp.int32, (tq, win), 0)
    col = lax.broadcasted_iota(jnp.int32, (tq, win), 1)
    for u in range(BAND_TILES):
        tile = pl.program_id(1) * BAND_TILES + u
        start = pl.multiple_of(jnp.clip(tile * tq - BAND_RADIUS, 0, length - win), BAND_RADIUS)
        qpos, kpos = tile * tq + row, start + col
        valid = jnp.abs(qpos - kpos) <= BAND_RADIUS
        if seg < length:
            valid = jnp.logical_and(valid, qpos // seg == kpos // seg)
        rows = slice(u * tq, (u + 1) * tq)
        lse = jnp.zeros((tq, LANES), F32)
        for j in range(B_WIDTH // LANES):
            sl = slice(j * LANES, (j + 1) * LANES)
            res = _pair_attention(q_ref[0, rows, sl], k_ref[0, pl.ds(start, win), sl],
                                  _with_ones(v_ref[0, pl.ds(start, win), sl]), first, valid)
            o_ref[0, rows, sl] = _pair_output(res, first).astype(BF16)
            for half, (ol, m) in enumerate(res):
                lse = jnp.where(lane == 2 * j + half, m + jnp.log(ol[:, LANES:]), lse)
        lse_ref[0, rows, :] = lse


def _band_attention(q, k, v, d):
    seg = SEQ // d
    step_rows = BAND_TILES * BAND_TQ
    length = max(seg, step_rows)
    win = BAND_TQ + 2 * BAND_RADIUS
    nblk = N_TOK // length
    q, k, v = (t.reshape(nblk, length, B_WIDTH) for t in (q, k, v))
    seq_spec = pl.BlockSpec((1, length, B_WIDTH), lambda s, i: (s, 0, 0))
    o, lse = pl.pallas_call(
        functools.partial(_band_kernel, length=length, seg=seg, win=win),
        grid=(nblk, length // step_rows),
        in_specs=[pl.BlockSpec((1, step_rows, B_WIDTH), lambda s, i: (s, i, 0)), seq_spec, seq_spec],
        out_specs=[pl.BlockSpec((1, step_rows, B_WIDTH), lambda s, i: (s, i, 0)),
                   pl.BlockSpec((1, step_rows, LANES), lambda s, i: (s, i, 0))],
        out_shape=[jax.ShapeDtypeStruct((nblk, length, B_WIDTH), BF16),
                   jax.ShapeDtypeStruct((nblk, length, LANES), F32)],
        compiler_params=_params(("parallel", "parallel")),
        name=f"band_attention_d{d}",
    )(q, k, v)
    return o.reshape(BATCH, d, seg, B_WIDTH), lse.reshape(BATCH, d, seg, LANES)


def _route(x, wh_ref, wl_ref, rb_ref, tri_ref, xa_ref, cls_ref, rank_ref, cnt_ref, carry_ref):
    tm = x.shape[0]
    xh, xl = _split_bf16(x)
    logits = _dot_nt(wh_ref[...], xh) + _dot_nt(wh_ref[...], xl) + _dot_nt(wl_ref[...], xh)
    aff = 1.0 / (1.0 + jnp.exp(-logits))
    sel = aff + rb_ref[...]
    s = [sel[e:e + 1, :] for e in range(N_EXPERTS)]
    a = [aff[e:e + 1, :] for e in range(N_EXPERTS)]

    def top2_sum(v):
        hi01, lo01 = jnp.maximum(v[0], v[1]), jnp.minimum(v[0], v[1])
        hi23, lo23 = jnp.maximum(v[2], v[3]), jnp.minimum(v[2], v[3])
        return jnp.maximum(hi01, hi23) + jnp.maximum(jnp.minimum(hi01, hi23), jnp.maximum(lo01, lo23))

    gscore = [top2_sum(s[EXPERTS_PER_GROUP * g:EXPERTS_PER_GROUP * (g + 1)]) for g in range(N_EXPERT_GROUPS)]
    best = jnp.zeros((1, tm), jnp.int32)
    best_score = gscore[0]
    for g in range(1, N_EXPERT_GROUPS):
        better = gscore[g] > best_score
        best = jnp.where(better, g, best)
        best_score = jnp.where(better, gscore[g], best_score)

    def pick(rows, j):
        out = rows[j]
        for g in range(1, N_EXPERT_GROUPS):
            out = jnp.where(best == g, rows[EXPERTS_PER_GROUP * g + j], out)
        return out

    t = [pick(s, j) for j in range(EXPERTS_PER_GROUP)]
    w = [pick(a, j) for j in range(EXPERTS_PER_GROUP)]

    def first_max(v):
        mx = jnp.maximum(jnp.maximum(v[0], v[1]), jnp.maximum(v[2], v[3]))
        idx = jnp.full((1, tm), EXPERTS_PER_GROUP - 1, jnp.int32)
        for j in range(EXPERTS_PER_GROUP - 2, -1, -1):
            idx = jnp.where(v[j] == mx, j, idx)
        return idx

    i1 = first_max(t)
    i2 = first_max([jnp.where(i1 == j, -jnp.inf, t[j]) for j in range(EXPERTS_PER_GROUP)])
    lo, hi = jnp.minimum(i1, i2), jnp.maximum(i1, i2)

    def take(rows, idx):
        out = rows[0]
        for j in range(1, EXPERTS_PER_GROUP):
            out = jnp.where(idx == j, rows[j], out)
        return out

    w_lo, w_hi = take(w, lo), take(w, hi)
    den = w_lo + w_hi
    pair = jnp.where(lo == 0, hi - 1, jnp.where(lo == 1, hi + 1, len(PAIRS) - 1))
    cls = best * len(PAIRS) + pair

    onehot = (lax.broadcasted_iota(jnp.int32, (CLASS_ROWS, tm), 0) == cls).astype(F32)
    before = _dot(onehot.astype(BF16), tri_ref[...]) + carry_ref[:, 0:1]
    rank = jnp.sum(onehot * before, axis=0, keepdims=True)
    carry_ref[...] = carry_ref[...] + jnp.sum(onehot, axis=1, keepdims=True)

    cls_ref[0] = cls
    rank_ref[0] = rank.astype(jnp.int32)
    cnt_ref[...] = carry_ref[...]
    gates = jnp.concatenate([w_lo / den, w_hi / den, jnp.zeros((LANES - 2, tm), F32)], axis=0)
    xa_ref[:, :D_MODEL] = x
    xa_ref[:, D_MODEL:] = gates.T


def _post_attn_kernel(*refs, merge):
    if merge:
        o_refs, l_refs, e_ref, refs = refs[:3], refs[3:6], refs[6], refs[7:]
    else:
        mix_ref, refs = refs[0], refs[1:]
    (qm_ref, mk_ref, mv_ref, w1_ref, w2_ref, x_ref, g_ref, b_ref, wh_ref, wl_ref, rb_ref, tri_ref,
     xa_ref, cls_ref, rank_ref, cnt_ref, carry_ref) = refs[:17]

    @pl.when(pl.program_id(0) == 0)
    def _():
        carry_ref[...] = jnp.zeros_like(carry_ref)

    if merge:
        ot_ref, lt_ref = refs[17], refs[18]
        nch = B_WIDTH // LANES
        for g, d in enumerate(DILATIONS):
            n = TOK_TILE // d
            for r in range(d):
                rows = pl.ds(r, n, stride=d) if d > 1 else slice(None)
                og = o_refs[g][0, r].astype(F32)
                for ch in range(nch):
                    ot_ref[g, ch, rows, :] = og[:, ch * LANES:(ch + 1) * LANES]
                lt_ref[g, rows, :] = l_refs[g][0, r]
        lses = [lt_ref[g] for g in range(len(DILATIONS))]
        m = functools.reduce(jnp.maximum, lses)
        es = [jnp.exp(l - m) for l in lses]
        inv = 1.0 / functools.reduce(jnp.add, es)
        expand = e_ref[...]

        def widen(w):
            hi, lo = _split_bf16(w)
            return _dot(hi, expand) + _dot(lo, expand)

        ws = [widen(e * inv) for e in es]
        mix = jnp.concatenate(
            [functools.reduce(jnp.add, [w[:, ch * LANES:(ch + 1) * LANES] * ot_ref[g, ch] for g, w in enumerate(ws)])
             for ch in range(nch)], axis=1).astype(BF16)
    else:
        mix = mix_ref[...]

    first = _first_head((TOK_TILE, LANES))
    memo = []
    for j in range(MEM_W // LANES):
        sl = slice(j * LANES, (j + 1) * LANES)
        memo.append(_pair_output(_pair_attention(qm_ref[:, sl], mk_ref[:, sl], _with_ones(mv_ref[:, sl]), first), first))
    memo = jnp.concatenate(memo, axis=1).astype(BF16)

    attn = _dot(mix, w1_ref[...]) + _dot(memo, w2_ref[...])
    x1 = _layer_norm(ALPHA * x_ref[...] + attn, g_ref[...], b_ref[...])
    _route(x1, wh_ref, wl_ref, rb_ref, tri_ref, xa_ref, cls_ref, rank_ref, cnt_ref, carry_ref)


def _post_attn(mix, qm, mkv, w1, w2, x, g, b, router, expand=None):
    merge = expand is not None
    wh, wl, rb, tri = router
    nblk = N_TOK // TOK_TILE
    row = lambda width: pl.BlockSpec((TOK_TILE, width), lambda i: (i, 0))
    const = lambda shape: pl.BlockSpec(shape, lambda i: (0,) * len(shape))
    row3 = pl.BlockSpec((1, 1, TOK_TILE), lambda i: (i, 0, 0))
    scratch = [pltpu.VMEM((CLASS_ROWS, LANES), F32)]
    if merge:
        outs, lses = mix
        resid = lambda d, width: pl.BlockSpec((1, d, TOK_TILE // d, width),
                                              lambda i: (i // TILES_PER_SEQ, 0, i % TILES_PER_SEQ, 0))
        lead_specs = ([resid(d, B_WIDTH) for d in DILATIONS] + [resid(d, LANES) for d in DILATIONS]
                      + [const(expand.shape)])
        lead = [*outs, *lses, expand]
        scratch += [pltpu.VMEM((len(DILATIONS), B_WIDTH // LANES, TOK_TILE, LANES), F32),
                    pltpu.VMEM((len(DILATIONS), TOK_TILE, LANES), F32)]
    else:
        lead_specs, lead = [row(mix.shape[1])], [mix]
    xa, cls, rank, cnt = pl.pallas_call(
        functools.partial(_post_attn_kernel, merge=merge),
        grid=(nblk,),
        in_specs=lead_specs + [row(MEM_W),
                               pl.BlockSpec((N_MEM, MEM_W), lambda i: (i // TILES_PER_SEQ, 0)),
                               pl.BlockSpec((N_MEM, MEM_W), lambda i: (i // TILES_PER_SEQ, 1)),
                               const(w1.shape), const(w2.shape), row(D_MODEL), const((1, D_MODEL)), const((1, D_MODEL)),
                               const(wh.shape), const(wl.shape), const(rb.shape), const(tri.shape)],
        out_specs=[row(XA_W), row3, row3, const((CLASS_ROWS, LANES))],
        out_shape=[jax.ShapeDtypeStruct((N_TOK, XA_W), F32), jax.ShapeDtypeStruct((nblk, 1, TOK_TILE), jnp.int32),
                   jax.ShapeDtypeStruct((nblk, 1, TOK_TILE), jnp.int32), jax.ShapeDtypeStruct((CLASS_ROWS, LANES), F32)],
        scratch_shapes=scratch,
        compiler_params=_params(("arbitrary",)),
        name="post_attn_merge" if merge else "post_attn",
    )(*lead, qm, mkv, mkv, w1, w2, x, g, b, wh, wl, rb, tri)
    return xa, cls.reshape(N_TOK), rank.reshape(N_TOK), cnt[:N_CLASSES, 0].astype(jnp.int32)


def _dispatch_kernel(pos_ref, xa_ref, init_ref, xs_ref, sem, *, chunk):
    del init_ref
    base = pl.program_id(0) * chunk

    def issue(blk, carry):
        for u in range(ROW_DMA_UNROLL):
            r = blk * ROW_DMA_UNROLL + u
            pltpu.make_async_copy(xa_ref.at[pl.ds(r, 1)], xs_ref.at[pl.ds(pos_ref[base + r], 1)], sem).start(
                priority=u % 2)
        return carry

    def retire(r, carry):
        pltpu.make_async_copy(xa_ref.at[pl.ds(0, 1)], xs_ref.at[pl.ds(0, 1)], sem).wait()
        return carry

    lax.fori_loop(0, chunk // ROW_DMA_UNROLL, issue, 0)
    lax.fori_loop(0, chunk, retire, 0, unroll=8)


def _dispatch(pos, xa, chunk=512):
    init = jnp.zeros((MOE_ROWS, XA_W), F32)
    return pl.pallas_call(
        functools.partial(_dispatch_kernel, chunk=chunk),
        grid_spec=pltpu.PrefetchScalarGridSpec(
            num_scalar_prefetch=1,
            grid=(N_TOK // chunk,),
            in_specs=[pl.BlockSpec((chunk, XA_W), lambda i, p: (i, 0)), pl.BlockSpec(memory_space=pl.ANY)],
            out_specs=pl.BlockSpec(memory_space=pl.ANY),
            scratch_shapes=[pltpu.SemaphoreType.DMA(())],
        ),
        out_shape=jax.ShapeDtypeStruct((MOE_ROWS, XA_W), F32),
        input_output_aliases={2: 0},
        compiler_params=_params(("arbitrary",)),
        name="moe_dispatch",
    )(pos, xa, init)


def _expert_kernel(slot1_ref, slot2_ref, used_ref, first_ref, grp_ref, next_ref, xs_ref, wg_hbm, wu_hbm, wd_hbm,
                   ys_ref, stage_g, stage_u, stage_d, wg_s, wu_s, wd_s, sem, *, layer):
    i = pl.program_id(0)
    streams = ((wg_hbm, stage_g, wg_s), (wu_hbm, stage_u, wu_s), (wd_hbm, stage_d, wd_s))

    def group_copy(group, k):
        first_expert = layer * N_EXPERTS + group * EXPERTS_PER_GROUP
        hbm, stage, _ = streams[k]
        return pltpu.make_async_copy(hbm.at[pl.ds(first_expert, EXPERTS_PER_GROUP)], stage, sem.at[k])

    @pl.when(i == 0)
    def _():
        for k in range(len(streams)):
            group_copy(grp_ref[0], k).start()

    @pl.when(first_ref[i] != 0)
    def _():
        for k, (_, stage, dst) in enumerate(streams):
            group_copy(grp_ref[i], k).wait()
            for e in range(EXPERTS_PER_GROUP):
                dst[e] = stage[e].astype(BF16)

        @pl.when(next_ref[i] >= 0)
        def _():
            for k in range(len(streams)):
                group_copy(next_ref[i], k).start()

    @pl.when(used_ref[i] != 0)
    def _():
        x = xs_ref[:, :D_MODEL].astype(BF16)
        gates = xs_ref[:, D_MODEL:]

        def expert(slot):
            gate = _dot(x, wg_s[slot])
            h = gate * (1.0 / (1.0 + jnp.exp(-gate))) * _dot(x, wu_s[slot])
            return _dot(h.astype(BF16), wd_s[slot])

        ys_ref[...] = gates[:, 0:1] * expert(slot1_ref[i]) + gates[:, 1:2] * expert(slot2_ref[i])

    @pl.when(used_ref[i] == 0)
    def _():
        ys_ref[...] = jnp.zeros_like(ys_ref)


def _experts(tile_meta, xs, layer, wg, wu, wd):
    nmeta = len(tile_meta)
    up_shape = (EXPERTS_PER_GROUP, D_MODEL, D_EXPERT)
    down_shape = (EXPERTS_PER_GROUP, D_EXPERT, D_MODEL)
    return pl.pallas_call(
        functools.partial(_expert_kernel, layer=layer),
        grid_spec=pltpu.PrefetchScalarGridSpec(
            num_scalar_prefetch=nmeta,
            grid=(MOE_TILES,),
            in_specs=[pl.BlockSpec((MOE_TM, XA_W), lambda i, *_: (i, 0)),
                      pl.BlockSpec(memory_space=pl.ANY), pl.BlockSpec(memory_space=pl.ANY),
                      pl.BlockSpec(memory_space=pl.ANY)],
            out_specs=pl.BlockSpec((MOE_TM, D_MODEL), lambda i, *_: (i, 0)),
            scratch_shapes=[pltpu.VMEM(up_shape, F32), pltpu.VMEM(up_shape, F32), pltpu.VMEM(down_shape, F32),
                            pltpu.VMEM(up_shape, BF16), pltpu.VMEM(up_shape, BF16), pltpu.VMEM(down_shape, BF16),
                            pltpu.SemaphoreType.DMA((3,))],
        ),
        out_shape=jax.ShapeDtypeStruct((MOE_ROWS, D_MODEL), F32),
        compiler_params=_params(("arbitrary",)),
        name="moe_experts",
    )(*tile_meta, xs, wg, wu, wd)


def _combine_kernel(pos_ref, ys_ref, x_ref, g_ref, b_ref, o_ref, buf, sem, *, tc):
    i = pl.program_id(0)
    slot = i % 2

    def gather(step, into):
        def issue(blk, carry):
            for u in range(ROW_DMA_UNROLL):
                r = blk * ROW_DMA_UNROLL + u
                row = pos_ref[step * tc + r]
                pltpu.make_async_copy(ys_ref.at[pl.ds(row, 1)], buf.at[into, pl.ds(r, 1)], sem.at[into]).start(
                    priority=u % 2)
            return carry

        lax.fori_loop(0, tc // ROW_DMA_UNROLL, issue, 0)

    @pl.when(i == 0)
    def _():
        gather(0, 0)

    @pl.when(i + 1 < pl.num_programs(0))
    def _():
        gather(i + 1, 1 - slot)

    def retire(r, carry):
        pltpu.make_async_copy(ys_ref.at[pl.ds(0, 1)], buf.at[slot, pl.ds(0, 1)], sem.at[slot]).wait()
        return carry

    lax.fori_loop(0, tc, retire, 0, unroll=8)
    o_ref[...] = _layer_norm(ALPHA * x_ref[...] + buf[slot], g_ref[...], b_ref[...])


def _combine_ln(pos, ys, xa, g, b, tc=256):
    return pl.pallas_call(
        functools.partial(_combine_kernel, tc=tc),
        grid_spec=pltpu.PrefetchScalarGridSpec(
            num_scalar_prefetch=1,
            grid=(N_TOK // tc,),
            in_specs=[pl.BlockSpec(memory_space=pl.ANY),
                      pl.BlockSpec((tc, D_MODEL), lambda i, p: (i, 0)),
                      pl.BlockSpec((1, D_MODEL), lambda i, p: (0, 0)),
                      pl.BlockSpec((1, D_MODEL), lambda i, p: (0, 0))],
            out_specs=pl.BlockSpec((tc, D_MODEL), lambda i, p: (i, 0)),
            scratch_shapes=[pltpu.VMEM((2, tc, D_MODEL), F32), pltpu.SemaphoreType.DMA((2,))],
        ),
        out_shape=jax.ShapeDtypeStruct((N_TOK, D_MODEL), F32),
        compiler_params=_params(("arbitrary",)),
        name="moe_combine_ln",
    )(pos, ys, xa, g, b)


def _moe_ln(routed, layer, wg, wu, wd, g, b):
    xa, cls, rank, counts = routed
    padded = (counts + MOE_TM - 1) // MOE_TM * MOE_TM
    ends = jnp.cumsum(padded)
    starts = ends - padded
    pos = starts[cls] + rank
    tile_start = jnp.arange(MOE_TILES, dtype=jnp.int32) * MOE_TM
    tile_used = (tile_start < ends[-1]).astype(jnp.int32)
    last_cls = jnp.max(jnp.where(counts > 0, jnp.arange(N_CLASSES), 0))
    tile_cls = jnp.minimum(jnp.sum(tile_start[:, None] >= ends[None, :], axis=1), last_cls)
    tile_cls = tile_cls.astype(jnp.int32)
    tile_grp = tile_cls // len(PAIRS)
    prev_grp = jnp.concatenate([jnp.full((1,), -1, jnp.int32), tile_grp[:-1]])
    tile_first = tile_used * (tile_grp != prev_grp).astype(jnp.int32)
    has = jnp.sum(counts.reshape(N_EXPERT_GROUPS, len(PAIRS)), axis=1) > 0
    gid = jnp.arange(N_EXPERT_GROUPS, dtype=jnp.int32)
    later = jnp.where(jnp.logical_and(has[None, :], gid[None, :] > gid[:, None]), gid[None, :], N_EXPERT_GROUPS)
    next_grp = jnp.min(later, axis=1)
    next_grp = jnp.where(next_grp == N_EXPERT_GROUPS, -1, next_grp)
    pair_slots = jnp.asarray(np.array(PAIRS, np.int32))[tile_cls % len(PAIRS)]
    tile_meta = (pair_slots[:, 0], pair_slots[:, 1], tile_used, tile_first, tile_grp, next_grp[tile_grp])
    xs = _dispatch(pos, xa)
    ys = _experts(tile_meta, xs, layer, wg, wu, wd)
    return _combine_ln(pos, ys, xa, g, b)


def _tile_heads(t64):
    return jnp.concatenate([t64, t64], axis=-1)


def _axial_tables():
    rows = SEQ // GRID_W
    row = jnp.repeat(jnp.arange(rows), GRID_W).astype(F32)
    col = jnp.tile(jnp.arange(GRID_W), rows).astype(F32)
    half = HEAD_DIM // 2
    inv = 1.0 / (AXIAL_THETA ** (jnp.arange(0, half, 2, dtype=F32) / half))
    ar, ac = row[:, None] * inv, col[:, None] * inv
    z = jnp.zeros_like(ar)
    c = jnp.concatenate([jnp.cos(ar), jnp.cos(ar), jnp.cos(ac), jnp.cos(ac)], -1)
    sa = jnp.concatenate([-jnp.sin(ar), z, -jnp.sin(ac), z], -1)
    sb = jnp.concatenate([z, jnp.sin(ar), z, jnp.sin(ac)], -1)
    return tuple(_tile_heads(t) for t in (c, sa, sb))


def _partial_tables():
    n = PARTIAL_ROT_DIMS
    pos = jnp.arange(SEQ, dtype=F32)
    ang = pos[:, None] * (1.0 / (PARTIAL_THETA ** (jnp.arange(0, n, 2, dtype=F32) / n)))
    z = jnp.zeros_like(ang)
    rest = HEAD_DIM - n
    c = jnp.concatenate([jnp.cos(ang), jnp.cos(ang), jnp.ones((SEQ, rest), F32)], -1)
    sa = jnp.concatenate([-jnp.sin(ang), z, jnp.zeros((SEQ, rest), F32)], -1)
    sb = jnp.concatenate([z, jnp.sin(ang), jnp.zeros((SEQ, rest), F32)], -1)
    return tuple(_tile_heads(t) for t in (c, sa, sb))


def _residue_order(tab, d):
    return tab.reshape(SEQ // d, d, LANES).transpose(1, 0, 2)


def kernel(x, mem, w_mem_kv, router_w, router_b, a_w_in, a_w_out, a_q_norm, a_k_norm, b_w_in, b_w_out,
           ln1_g, ln1_b, ln2_g, ln2_b, w_gate, w_up, w_down):
    x = x.reshape(N_TOK, D_MODEL)
    row = lambda v: v.reshape(1, -1)

    lane = np.arange(LANES)
    m128 = jnp.asarray((lane[:, None] // HEAD_DIM == lane[None, :] // HEAD_DIM), BF16)
    expand = jnp.asarray(lane[:, None] == np.arange(B_WIDTH)[None, :] // HEAD_DIM, BF16)
    tri = jnp.asarray(np.triu(np.ones((TOK_TILE, TOK_TILE), np.float32), 1), BF16)
    rw_hi = router_w.T.astype(BF16)
    rw_lo = (router_w.T - rw_hi.astype(F32)).astype(BF16)
    router = (rw_hi, rw_lo, router_b.reshape(N_EXPERTS, 1), tri)

    mkv = _proj(mem.reshape(BATCH * N_MEM, D_MODEL), w_mem_kv.astype(BF16))
    experts = tuple(w.reshape(DEPTH * N_EXPERTS, *w.shape[2:]) for w in (w_gate, w_up, w_down))

    kmix = A_Q_HEADS * HEAD_DIM
    qcols = (np.asarray(_Q_HEAD_ORDER)[:, None] * HEAD_DIM + np.arange(HEAD_DIM)[None, :]).reshape(-1)
    w_in = jnp.concatenate([a_w_in[0][:, qcols], a_w_in[0][:, kmix:]], axis=1).astype(BF16)
    qg = row(jnp.tile(a_q_norm[0], 2) * (SCALE * LOG2_E))
    kg = row(jnp.tile(a_k_norm[0], 2))
    q, k, v, qm = _a_in_proj(x, w_in, m128, qg, kg, _axial_tables())
    mix = _gqa_attention(q, k, v)
    w_out = a_w_out[0].astype(BF16)
    routed = _post_attn(mix, qm, mkv, w_out[qcols], w_out[kmix:], x, row(ln1_g[0]), row(ln1_b[0]), router)
    x = _moe_ln(routed, 0, *experts, row(ln2_g[0]), row(ln2_b[0]))

    ptabs = _partial_tables()
    tabs = [tuple(_residue_order(t, d) for t in ptabs) for d in DILATIONS]
    qkv, qm = _b_in_proj(x, b_w_in[0].astype(BF16), tabs)
    outs, lses = zip(*[_band_attention(*qkv[g], d) for g, d in enumerate(DILATIONS)])
    w_out = b_w_out[0].astype(BF16)
    routed = _post_attn((outs, lses), qm, mkv, w_out[:B_WIDTH], w_out[B_WIDTH:], x, row(ln1_g[1]), row(ln1_b[1]),
                        router, expand)
    x = _moe_ln(routed, 1, *experts, row(ln2_g[1]), row(ln2_b[1]))
    return x.reshape(BATCH, SEQ, D_MODEL)
```

```python
import functools
import math

import numpy as np
import jax
import jax.numpy as jnp
from jax import lax
from jax.experimental import pallas as pl
from jax.experimental.pallas import tpu as pltpu

F32 = jnp.float32
BF16 = jnp.bfloat16

D_MODEL = 1024
BATCH = 8
SEQ = 2048
N_TOK = BATCH * SEQ
HEAD_DIM = 64
GRID_W = 64
N_MEM = 256
MEM_HEADS = 4
A_Q_HEADS = 12
A_KV_HEADS = 4
AXIAL_THETA = 10000.0
B_HEADS = 8
B_WIDTH = B_HEADS * HEAD_DIM
DILATIONS = (1, 4, 16)
BAND_RADIUS = 64
PARTIAL_ROT_DIMS = HEAD_DIM // 4
PARTIAL_THETA = 500000.0
N_EXPERTS = 16
N_EXPERT_GROUPS = 4
EXPERTS_PER_GROUP = 4
D_EXPERT = 512
DEPTH = 2
ALPHA = (2 * DEPTH) ** 0.25
NORM_EPS = 1e-6
NEG_BIG = -1e30
SCALE = HEAD_DIM ** -0.5
LOG2_E = math.log2(math.e)
_Q_HEAD_ORDER = tuple(6 * p + 3 * half + t for p in range(2) for t in range(3) for half in range(2))

LANES = 128
MEM_W = MEM_HEADS * HEAD_DIM
TOK_TILE = 512
TILES_PER_SEQ = SEQ // TOK_TILE
GQA_KC = 256
BAND_TQ = 128
BAND_TILES = 4

PAIRS = ((0, 1), (0, 2), (0, 3), (1, 2), (1, 3), (2, 3))
N_CLASSES = N_EXPERT_GROUPS * len(PAIRS)
CLASS_ROWS = 32
MOE_TM = 256
MOE_ROWS = N_TOK + N_CLASSES * MOE_TM
MOE_TILES = MOE_ROWS // MOE_TM
XA_W = D_MODEL + LANES
ROW_DMA_UNROLL = 8

VMEM_LIMIT = 56 * 1024 * 1024


def _params(sem, vmem=VMEM_LIMIT):
    return pltpu.CompilerParams(dimension_semantics=sem, vmem_limit_bytes=vmem)


def _dot(a, b):
    return jnp.dot(a, b, preferred_element_type=F32)


def _dot_nt(a, b):
    return lax.dot_general(a, b, (((1,), (1,)), ((), ())), preferred_element_type=F32)


def _split_bf16(x):
    hi = x.astype(BF16)
    lo = (x - hi.astype(F32)).astype(BF16)
    return hi, lo


def _layer_norm(z, g, b):
    mu = jnp.mean(z, axis=-1, keepdims=True)
    zc = z - mu
    var = jnp.mean(zc * zc, axis=-1, keepdims=True)
    return zc * lax.rsqrt(var + NORM_EPS) * g + b


def _rope(z, c, sa, sb, shift):
    return z * c + pltpu.roll(z, LANES - shift, 1) * sa + pltpu.roll(z, shift, 1) * sb


def _first_head(shape):
    return lax.broadcasted_iota(jnp.int32, shape, 1) < HEAD_DIM


def _pair_attention(q2, k2, v3, first, valid=None, base2=False):
    res = []
    for pick in (first, jnp.logical_not(first)):
        s = _dot_nt(jnp.where(pick, q2, jnp.zeros_like(q2)), k2)
        if valid is not None:
            s = jnp.where(valid, s, NEG_BIG)
        m = jnp.max(s, axis=1, keepdims=True)
        p = jnp.exp2(s - m) if base2 else jnp.exp(s - m)
        res.append((_dot(p.astype(BF16), v3), m))
    return res


def _pair_output(res, first):
    (ol_a, _), (ol_b, _) = res
    return jnp.where(first, ol_a[:, :LANES], ol_b[:, :LANES]) / jnp.where(first, ol_a[:, LANES:], ol_b[:, LANES:])


def _with_ones(v2):
    return jnp.concatenate([v2, jnp.ones_like(v2)], axis=1)


def _proj_kernel(x_ref, w_ref, o_ref):
    o_ref[...] = _dot(x_ref[...].astype(BF16), w_ref[...]).astype(o_ref.dtype)


def _proj(x, w, tm=TOK_TILE):
    n, k = x.shape
    width = w.shape[1]
    return pl.pallas_call(
        _proj_kernel,
        grid=(n // tm,),
        in_specs=[pl.BlockSpec((tm, k), lambda i: (i, 0)), pl.BlockSpec((k, width), lambda i: (0, 0))],
        out_specs=pl.BlockSpec((tm, width), lambda i: (i, 0)),
        out_shape=jax.ShapeDtypeStruct((n, width), BF16),
        compiler_params=_params(("parallel",)),
        name="proj",
    )(x, w)


def _a_in_kernel(x_ref, w_ref, m_ref, qg_ref, kg_ref, c_ref, sa_ref, sb_ref, q_ref, k_ref, v_ref, qm_ref):
    y = _dot(x_ref[...].astype(BF16), w_ref[...])
    m = m_ref[...]
    c, sa, sb = c_ref[...], sa_ref[...], sb_ref[...]

    def norm_rope(z, gain):
        hi, lo = _split_bf16(z * z)
        ms = (_dot(hi, m) + _dot(lo, m)) * (1.0 / HEAD_DIM)
        z = z * lax.rsqrt(ms + NORM_EPS) * gain
        return _rope(z, c, sa, sb, 16)

    nq = A_Q_HEADS * HEAD_DIM // LANES
    nk = A_KV_HEADS * HEAD_DIM // LANES
    for ch in range(nq):
        q_ref[:, ch * LANES:(ch + 1) * LANES] = norm_rope(y[:, ch * LANES:(ch + 1) * LANES], qg_ref[...]).astype(BF16)
    off = nq * LANES
    for ch in range(nk):
        z = y[:, off + ch * LANES: off + (ch + 1) * LANES]
        k_ref[:, ch * LANES:(ch + 1) * LANES] = norm_rope(z, kg_ref[...]).astype(BF16)
    off += nk * LANES
    ones = jnp.ones((y.shape[0], LANES), BF16)
    for ch in range(nk):
        v_ref[:, 2 * ch * LANES:(2 * ch + 1) * LANES] = y[:, off + ch * LANES: off + (ch + 1) * LANES].astype(BF16)
        v_ref[:, (2 * ch + 1) * LANES:(2 * ch + 2) * LANES] = ones
    off += nk * LANES
    qm_ref[...] = (y[:, off:off + MEM_W] * SCALE).astype(BF16)


def _a_in_proj(x, w, m128, qg, kg, tabs, tm=TOK_TILE):
    c, sa, sb = tabs
    nblk = SEQ // tm
    a_in = w.shape[1]
    tab_spec = pl.BlockSpec((tm, LANES), lambda i: (i % nblk, 0))
    row = lambda width: pl.BlockSpec((tm, width), lambda i: (i, 0))
    const = lambda shape: pl.BlockSpec(shape, lambda i: (0, 0))
    qw, kw = A_Q_HEADS * HEAD_DIM, A_KV_HEADS * HEAD_DIM
    return pl.pallas_call(
        _a_in_kernel,
        grid=(N_TOK // tm,),
        in_specs=[row(D_MODEL), const((D_MODEL, a_in)), const((LANES, LANES)), const((1, LANES)), const((1, LANES)),
                  tab_spec, tab_spec, tab_spec],
        out_specs=[row(qw), row(kw), row(2 * kw), row(MEM_W)],
        out_shape=[jax.ShapeDtypeStruct((N_TOK, qw), BF16), jax.ShapeDtypeStruct((N_TOK, kw), BF16),
                   jax.ShapeDtypeStruct((N_TOK, 2 * kw), BF16), jax.ShapeDtypeStruct((N_TOK, MEM_W), BF16)],
        compiler_params=_params(("parallel",)),
        name="a_in_proj",
    )(x, w, m128, qg, kg, c, sa, sb)


def _b_in_kernel(*refs):
    ng = len(DILATIONS)
    nx = D_MODEL // LANES
    x_refs, w_ref, refs = refs[:nx], refs[nx], refs[nx + 1:]
    tabs, outs, qm_ref = refs[:3 * ng], refs[3 * ng:6 * ng], refs[6 * ng]
    xb1 = None
    for g, d in enumerate(DILATIONS):
        n = TOK_TILE // d
        if d == 1:
            cols = [r[...] for r in x_refs]
        else:
            cols = [jnp.concatenate([r[pl.ds(k, n, stride=d), :] for k in range(d)], axis=0) for r in x_refs]
        xb = jnp.concatenate(cols, axis=1).astype(BF16)
        xb1 = xb if d == 1 else xb1
        c, sa, sb = (t[...].reshape(TOK_TILE, LANES) for t in tabs[3 * g:3 * g + 3])
        q_ref, k_ref, v_ref = outs[3 * g:3 * g + 3]
        yq = _dot(xb, w_ref[:, g * B_WIDTH:(g + 1) * B_WIDTH]) * SCALE
        yk = _dot(xb, w_ref[:, (ng + g) * B_WIDTH:(ng + g + 1) * B_WIDTH])
        yv = _dot(xb, w_ref[:, (2 * ng + g) * B_WIDTH:(2 * ng + g + 1) * B_WIDTH])
        for ch in range(B_WIDTH // LANES):
            sl = slice(ch * LANES, (ch + 1) * LANES)
            q_ref[0, :, :, sl] = _rope(yq[:, sl], c, sa, sb, PARTIAL_ROT_DIMS // 2).reshape(d, n, LANES).astype(BF16)
            k_ref[0, :, :, sl] = _rope(yk[:, sl], c, sa, sb, PARTIAL_ROT_DIMS // 2).reshape(d, n, LANES).astype(BF16)
        v_ref[0] = yv.reshape(d, n, B_WIDTH).astype(BF16)
    qm = _dot(xb1, w_ref[:, 3 * ng * B_WIDTH:3 * ng * B_WIDTH + MEM_W])
    qm_ref[...] = (qm * SCALE).astype(BF16)


def _b_in_proj(x, w, tabs):
    nx = D_MODEL // LANES
    in_specs = [pl.BlockSpec((TOK_TILE, LANES), functools.partial(lambda i, c: (i, c), c=c)) for c in range(nx)]
    in_specs.append(pl.BlockSpec(w.shape, lambda i: (0, 0)))
    out_specs, out_shape = [], []
    for d in DILATIONS:
        n = TOK_TILE // d
        in_specs += [pl.BlockSpec((d, n, LANES), lambda i: (0, i % TILES_PER_SEQ, 0))] * 3
        out_specs += [pl.BlockSpec((1, d, n, B_WIDTH), lambda i: (i // TILES_PER_SEQ, 0, i % TILES_PER_SEQ, 0))] * 3
        out_shape += [jax.ShapeDtypeStruct((BATCH, d, SEQ // d, B_WIDTH), BF16)] * 3
    out_specs.append(pl.BlockSpec((TOK_TILE, MEM_W), lambda i: (i, 0)))
    out_shape.append(jax.ShapeDtypeStruct((N_TOK, MEM_W), BF16))
    res = pl.pallas_call(
        _b_in_kernel,
        grid=(N_TOK // TOK_TILE,),
        in_specs=in_specs,
        out_specs=out_specs,
        out_shape=out_shape,
        compiler_params=_params(("parallel",)),
        name="b_in_proj",
    )(*([x] * nx), w, *[t for group in tabs for t in group])
    return [res[3 * g:3 * g + 3] for g in range(len(DILATIONS))], res[-1]


def _gqa_kernel(q_ref, k_ref, v_ref, o_ref):
    tq = q_ref.shape[0]
    ntile = q_ref.shape[1] // LANES
    q = jnp.concatenate([q_ref[:, t * LANES:(t + 1) * LANES] for t in range(ntile)], axis=0)
    first = _first_head(q.shape)
    res = []
    for pick in (first, jnp.logical_not(first)):
        qh = jnp.where(pick, q, jnp.zeros_like(q))
        m = jnp.full((q.shape[0], 1), -jnp.inf, F32)
        acc = jnp.zeros((q.shape[0], 2 * LANES), F32)
        for c in range(SEQ // GQA_KC):
            keys = slice(c * GQA_KC, (c + 1) * GQA_KC)
            s = _dot_nt(qh, k_ref[keys, :])
            m_new = jnp.maximum(m, jnp.max(s, axis=1, keepdims=True))
            p = jnp.exp2(s - m_new).astype(BF16)
            acc = acc * jnp.exp2(m - m_new) + _dot(p, v_ref[keys, :])
            m = m_new
        res.append((acc, m))
    o = _pair_output(res, first)
    for t in range(ntile):
        o_ref[:, t * LANES:(t + 1) * LANES] = o[t * tq:(t + 1) * tq].astype(BF16)


def _gqa_attention(q, k, v, tq=512):
    nq = SEQ // tq
    qw = q.shape[1] // 2
    kw = k.shape[1] // 2
    return pl.pallas_call(
        _gqa_kernel,
        grid=(BATCH, 2, nq),
        in_specs=[pl.BlockSpec((tq, qw), lambda b, p, i: (b * nq + i, p)),
                  pl.BlockSpec((SEQ, kw), lambda b, p, i: (b, p)),
                  pl.BlockSpec((SEQ, 2 * kw), lambda b, p, i: (b, p))],
        out_specs=pl.BlockSpec((tq, qw), lambda b, p, i: (b * nq + i, p)),
        out_shape=jax.ShapeDtypeStruct(q.shape, BF16),
        compiler_params=_params(("parallel", "parallel", "parallel")),
        name="gqa_attention",
    )(q, k, v)


def _band_kernel(q_ref, k_ref, v_ref, o_ref, lse_ref, *, length, seg, win):
    tq = BAND_TQ
    first = _first_head((tq, LANES))
    lane = lax.broadcasted_iota(jnp.int32, (tq, LANES), 1)
    row = lax.broadcasted_iota(jnp.int32, (tq, win), 0)
    col = lax.broadcasted_iota(jnp.int32, (tq, win), 1)
    for u in range(BAND_TILES):
        tile = pl.program_id(1) * BAND_TILES + u
        start = pl.multiple_of(jnp.clip(tile * tq - BAND_RADIUS, 0, length - win), BAND_RADIUS)
        qpos, kpos = tile * tq + row, start + col
        valid = jnp.abs(qpos - kpos) <= BAND_RADIUS
        if seg < length:
            valid = jnp.logical_and(valid, qpos // seg == kpos // seg)
        rows = slice(u * tq, (u + 1) * tq)
        lse = jnp.zeros((tq, LANES), F32)
        for j in range(B_WIDTH // LANES):
            sl = slice(j * LANES, (j + 1) * LANES)
            res = _pair_attention(q_ref[0, rows, sl], k_ref[0, pl.ds(start, win), sl],
                                  _with_ones(v_ref[0, pl.ds(start, win), sl]), first, valid)
            o_ref[0, rows, sl] = _pair_output(res, first).astype(BF16)
            for half, (ol, m) in enumerate(res):
                lse = jnp.where(lane == 2 * j + half, m + jnp.log(ol[:, LANES:]), lse)
        lse_ref[0, rows, :] = lse


def _band_attention(q, k, v, d):
    seg = SEQ // d
    step_rows = BAND_TILES * BAND_TQ
    length = max(seg, step_rows)
    win = BAND_TQ + 2 * BAND_RADIUS
    nblk = N_TOK // length
    q, k, v = (t.reshape(nblk, length, B_WIDTH) for t in (q, k, v))
    seq_spec = pl.BlockSpec((1, length, B_WIDTH), lambda s, i: (s, 0, 0))
    o, lse = pl.pallas_call(
        functools.partial(_band_kernel, length=length, seg=seg, win=win),
        grid=(nblk, length // step_rows),
        in_specs=[pl.BlockSpec((1, step_rows, B_WIDTH), lambda s, i: (s, i, 0)), seq_spec, seq_spec],
        out_specs=[pl.BlockSpec((1, step_rows, B_WIDTH), lambda s, i: (s, i, 0)),
                   pl.BlockSpec((1, step_rows, LANES), lambda s, i: (s, i, 0))],
        out_shape=[jax.ShapeDtypeStruct((nblk, length, B_WIDTH), BF16),
                   jax.ShapeDtypeStruct((nblk, length, LANES), F32)],
        compiler_params=_params(("parallel", "parallel")),
        name=f"band_attention_d{d}",
    )(q, k, v)
    return o.reshape(BATCH, d, seg, B_WIDTH), lse.reshape(BATCH, d, seg, LANES)


def _route(x, wh_ref, wl_ref, rb_ref, tri_ref, xa_ref, cls_ref, rank_ref, cnt_ref, carry_ref):
    tm = x.shape[0]
    xh, xl = _split_bf16(x)
    logits = _dot_nt(wh_ref[...], xh) + _dot_nt(wh_ref[...], xl) + _dot_nt(wl_ref[...], xh)
    aff = 1.0 / (1.0 + jnp.exp(-logits))
    sel = aff + rb_ref[...]
    s = [sel[e:e + 1, :] for e in range(N_EXPERTS)]
    a = [aff[e:e + 1, :] for e in range(N_EXPERTS)]

    def top2_sum(v):
        hi01, lo01 = jnp.maximum(v[0], v[1]), jnp.minimum(v[0], v[1])
        hi23, lo23 = jnp.maximum(v[2], v[3]), jnp.minimum(v[2], v[3])
        return jnp.maximum(hi01, hi23) + jnp.maximum(jnp.minimum(hi01, hi23), jnp.maximum(lo01, lo23))

    gscore = [top2_sum(s[EXPERTS_PER_GROUP * g:EXPERTS_PER_GROUP * (g + 1)]) for g in range(N_EXPERT_GROUPS)]
    best = jnp.zeros((1, tm), jnp.int32)
    best_score = gscore[0]
    for g in range(1, N_EXPERT_GROUPS):
        better = gscore[g] > best_score
        best = jnp.where(better, g, best)
        best_score = jnp.where(better, gscore[g], best_score)

    def pick(rows, j):
        out = rows[j]
        for g in range(1, N_EXPERT_GROUPS):
            out = jnp.where(best == g, rows[EXPERTS_PER_GROUP * g + j], out)
        return out

    t = [pick(s, j) for j in range(EXPERTS_PER_GROUP)]
    w = [pick(a, j) for j in range(EXPERTS_PER_GROUP)]

    def first_max(v):
        mx = jnp.maximum(jnp.maximum(v[0], v[1]), jnp.maximum(v[2], v[3]))
        idx = jnp.full((1, tm), EXPERTS_PER_GROUP - 1, jnp.int32)
        for j in range(EXPERTS_PER_GROUP - 2, -1, -1):
            idx = jnp.where(v[j] == mx, j, idx)
        return idx

    i1 = first_max(t)
    i2 = first_max([jnp.where(i1 == j, -jnp.inf, t[j]) for j in range(EXPERTS_PER_GROUP)])
    lo, hi = jnp.minimum(i1, i2), jnp.maximum(i1, i2)

    def take(rows, idx):
        out = rows[0]
        for j in range(1, EXPERTS_PER_GROUP):
            out = jnp.where(idx == j, rows[j], out)
        return out

    w_lo, w_hi = take(w, lo), take(w, hi)
    den = w_lo + w_hi
    pair = jnp.where(lo == 0, hi - 1, jnp.where(lo == 1, hi + 1, len(PAIRS) - 1))
    cls = best * len(PAIRS) + pair

    onehot = (lax.broadcasted_iota(jnp.int32, (CLASS_ROWS, tm), 0) == cls).astype(F32)
    before = _dot(onehot.astype(BF16), tri_ref[...]) + carry_ref[:, 0:1]
    rank = jnp.sum(onehot * before, axis=0, keepdims=True)
    carry_ref[...] = carry_ref[...] + jnp.sum(onehot, axis=1, keepdims=True)

    cls_ref[0] = cls
    rank_ref[0] = rank.astype(jnp.int32)
    cnt_ref[...] = carry_ref[...]
    gates = jnp.concatenate([w_lo / den, w_hi / den, jnp.zeros((LANES - 2, tm), F32)], axis=0)
    xa_ref[:, :D_MODEL] = x
    xa_ref[:, D_MODEL:] = gates.T


def _post_attn_kernel(*refs, merge):
    if merge:
        o_refs, l_refs, e_ref, refs = refs[:3], refs[3:6], refs[6], refs[7:]
    else:
        mix_ref, refs = refs[0], refs[1:]
    (qm_ref, mk_ref, mv_ref, w1_ref, w2_ref, x_ref, g_ref, b_ref, wh_ref, wl_ref, rb_ref, tri_ref,
     xa_ref, cls_ref, rank_ref, cnt_ref, carry_ref) = refs[:17]

    @pl.when(pl.program_id(0) == 0)
    def _():
        carry_ref[...] = jnp.zeros_like(carry_ref)

    if merge:
        ot_ref, lt_ref = refs[17], refs[18]
        nch = B_WIDTH // LANES
        for g, d in enumerate(DILATIONS):
            n = TOK_TILE // d
            for r in range(d):
                rows = pl.ds(r, n, stride=d) if d > 1 else slice(None)
                og = o_refs[g][0, r].astype(F32)
                for ch in range(nch):
                    ot_ref[g, ch, rows, :] = og[:, ch * LANES:(ch + 1) * LANES]
                lt_ref[g, rows, :] = l_refs[g][0, r]
        lses = [lt_ref[g] for g in range(len(DILATIONS))]
        m = functools.reduce(jnp.maximum, lses)
        es = [jnp.exp(l - m) for l in lses]
        inv = 1.0 / functools.reduce(jnp.add, es)
        expand = e_ref[...]

        def widen(w):
            hi, lo = _split_bf16(w)
            return _dot(hi, expand) + _dot(lo, expand)

        ws = [widen(e * inv) for e in es]
        mix = jnp.concatenate(
            [functools.reduce(jnp.add, [w[:, ch * LANES:(ch + 1) * LANES] * ot_ref[g, ch] for g, w in enumerate(ws)])
             for ch in range(nch)], axis=1).astype(BF16)
    else:
        mix = mix_ref[...]

    first = _first_head((TOK_TILE, LANES))
    memo = []
    for j in range(MEM_W // LANES):
        sl = slice(j * LANES, (j + 1) * LANES)
        memo.append(_pair_output(_pair_attention(qm_ref[:, sl], mk_ref[:, sl], _with_ones(mv_ref[:, sl]), first), first))
    memo = jnp.concatenate(memo, axis=1).astype(BF16)

    attn = _dot(mix, w1_ref[...]) + _dot(memo, w2_ref[...])
    x1 = _layer_norm(ALPHA * x_ref[...] + attn, g_ref[...], b_ref[...])
    _route(x1, wh_ref, wl_ref, rb_ref, tri_ref, xa_ref, cls_ref, rank_ref, cnt_ref, carry_ref)


def _post_attn(mix, qm, mkv, w1, w2, x, g, b, router, expand=None):
    merge = expand is not None
    wh, wl, rb, tri = router
    nblk = N_TOK // TOK_TILE
    row = lambda width: pl.BlockSpec((TOK_TILE, width), lambda i: (i, 0))
    const = lambda shape: pl.BlockSpec(shape, lambda i: (0,) * len(shape))
    row3 = pl.BlockSpec((1, 1, TOK_TILE), lambda i: (i, 0, 0))
    scratch = [pltpu.VMEM((CLASS_ROWS, LANES), F32)]
    if merge:
        outs, lses = mix
        resid = lambda d, width: pl.BlockSpec((1, d, TOK_TILE // d, width),
                                              lambda i: (i // TILES_PER_SEQ, 0, i % TILES_PER_SEQ, 0))
        lead_specs = ([resid(d, B_WIDTH) for d in DILATIONS] + [resid(d, LANES) for d in DILATIONS]
                      + [const(expand.shape)])
        lead = [*outs, *lses, expand]
        scratch += [pltpu.VMEM((len(DILATIONS), B_WIDTH // LANES, TOK_TILE, LANES), F32),
                    pltpu.VMEM((len(DILATIONS), TOK_TILE, LANES), F32)]
    else:
        lead_specs, lead = [row(mix.shape[1])], [mix]
    xa, cls, rank, cnt = pl.pallas_call(
        functools.partial(_post_attn_kernel, merge=merge),
        grid=(nblk,),
        in_specs=lead_specs + [row(MEM_W),
                               pl.BlockSpec((N_MEM, MEM_W), lambda i: (i // TILES_PER_SEQ, 0)),
                               pl.BlockSpec((N_MEM, MEM_W), lambda i: (i // TILES_PER_SEQ, 1)),
                               const(w1.shape), const(w2.shape), row(D_MODEL), const((1, D_MODEL)), const((1, D_MODEL)),
                               const(wh.shape), const(wl.shape), const(rb.shape), const(tri.shape)],
        out_specs=[row(XA_W), row3, row3, const((CLASS_ROWS, LANES))],
        out_shape=[jax.ShapeDtypeStruct((N_TOK, XA_W), F32), jax.ShapeDtypeStruct((nblk, 1, TOK_TILE), jnp.int32),
                   jax.ShapeDtypeStruct((nblk, 1, TOK_TILE), jnp.int32), jax.ShapeDtypeStruct((CLASS_ROWS, LANES), F32)],
        scratch_shapes=scratch,
        compiler_params=_params(("arbitrary",)),
        name="post_attn_merge" if merge else "post_attn",
    )(*lead, qm, mkv, mkv, w1, w2, x, g, b, wh, wl, rb, tri)
    return xa, cls.reshape(N_TOK), rank.reshape(N_TOK), cnt[:N_CLASSES, 0].astype(jnp.int32)


def _dispatch_kernel(pos_ref, xa_ref, init_ref, xs_ref, sem, *, chunk):
    del init_ref
    base = pl.program_id(0) * chunk

    def issue(blk, carry):
        for u in range(ROW_DMA_UNROLL):
            r = blk * ROW_DMA_UNROLL + u
            pltpu.make_async_copy(xa_ref.at[pl.ds(r, 1)], xs_ref.at[pl.ds(pos_ref[base + r], 1)], sem).start(
                priority=u % 2)
        return carry

    def retire(r, carry):
        pltpu.make_async_copy(xa_ref.at[pl.ds(0, 1)], xs_ref.at[pl.ds(0, 1)], sem).wait()
        return carry

    lax.fori_loop(0, chunk // ROW_DMA_UNROLL, issue, 0)
    lax.fori_loop(0, chunk, retire, 0, unroll=8)


def _dispatch(pos, xa, chunk=512):
    init = jnp.zeros((MOE_ROWS, XA_W), F32)
    return pl.pallas_call(
        functools.partial(_dispatch_kernel, chunk=chunk),
        grid_spec=pltpu.PrefetchScalarGridSpec(
            num_scalar_prefetch=1,
            grid=(N_TOK // chunk,),
            in_specs=[pl.BlockSpec((chunk, XA_W), lambda i, p: (i, 0)), pl.BlockSpec(memory_space=pl.ANY)],
            out_specs=pl.BlockSpec(memory_space=pl.ANY),
            scratch_shapes=[pltpu.SemaphoreType.DMA(())],
        ),
        out_shape=jax.ShapeDtypeStruct((MOE_ROWS, XA_W), F32),
        input_output_aliases={2: 0},
        compiler_params=_params(("arbitrary",)),
        name="moe_dispatch",
    )(pos, xa, init)


def _expert_kernel(slot1_ref, slot2_ref, used_ref, first_ref, grp_ref, next_ref, src_ref, xa_hbm, wg_hbm, wu_hbm, wd_hbm,
                   ys_ref, xbuf, stage_g, stage_u, stage_d, wg_s, wu_s, wd_s, sem, gsem, *, layer):
    i = pl.program_id(0)
    last = pl.num_programs(0) - 1
    slot = i % 2
    streams = ((wg_hbm, stage_g, wg_s), (wu_hbm, stage_u, wu_s), (wd_hbm, stage_d, wd_s))

    def group_copy(group, k):
        first_expert = layer * N_EXPERTS + group * EXPERTS_PER_GROUP
        hbm, stage, _ = streams[k]
        return pltpu.make_async_copy(hbm.at[pl.ds(first_expert, EXPERTS_PER_GROUP)], stage, sem.at[k])

    def gather(tile, into):
        for r in range(MOE_TM):
            row = src_ref[tile * MOE_TM + r]
            pltpu.make_async_copy(xa_hbm.at[pl.ds(row, 1)], xbuf.at[into, pl.ds(r, 1)], gsem.at[into]).start(
                priority=r % 2)

    def gather_wait(into):
        def retire(r, carry):
            pltpu.make_async_copy(xa_hbm.at[pl.ds(0, 1)], xbuf.at[into, pl.ds(0, 1)], gsem.at[into]).wait()
            return carry

        lax.fori_loop(0, MOE_TM, retire, 0, unroll=8)

    @pl.when(i == 0)
    def _():
        for k in range(len(streams)):
            group_copy(grp_ref[0], k).start()
        gather(0, 0)

    @pl.when(first_ref[i] != 0)
    def _():
        for k, (_, stage, dst) in enumerate(streams):
            group_copy(grp_ref[i], k).wait()
            for e in range(EXPERTS_PER_GROUP):
                dst[e] = stage[e].astype(BF16)

        @pl.when(next_ref[i] >= 0)
        def _():
            for k in range(len(streams)):
                group_copy(next_ref[i], k).start()

    @pl.when(used_ref[i] != 0)
    def _():
        gather_wait(slot)
        gather(jnp.minimum(i + 1, last), 1 - slot)
        x = xbuf[slot, :, :D_MODEL].astype(BF16)
        gates = xbuf[slot, :, D_MODEL:]

        def expert(s):
            gate = _dot(x, wg_s[s])
            h = gate * (1.0 / (1.0 + jnp.exp(-gate))) * _dot(x, wu_s[s])
            return _dot(h.astype(BF16), wd_s[s])

        ys_ref[...] = gates[:, 0:1] * expert(slot1_ref[i]) + gates[:, 1:2] * expert(slot2_ref[i])

        @pl.when(i == last)
        def _():
            gather_wait(1 - slot)

    @pl.when(used_ref[i] == 0)
    def _():
        ys_ref[...] = jnp.zeros_like(ys_ref)

        @pl.when(used_ref[jnp.maximum(i - 1, 0)] != 0)
        def _():
            gather_wait(slot)


def _experts(tile_meta, src, xa, layer, wg, wu, wd):
    nmeta = len(tile_meta) + 1
    up_shape = (EXPERTS_PER_GROUP, D_MODEL, D_EXPERT)
    down_shape = (EXPERTS_PER_GROUP, D_EXPERT, D_MODEL)
    anywhere = pl.BlockSpec(memory_space=pl.ANY)
    return pl.pallas_call(
        functools.partial(_expert_kernel, layer=layer),
        grid_spec=pltpu.PrefetchScalarGridSpec(
            num_scalar_prefetch=nmeta,
            grid=(MOE_TILES,),
            in_specs=[anywhere, anywhere, anywhere, anywhere],
            out_specs=pl.BlockSpec((MOE_TM, D_MODEL), lambda i, *_: (i, 0)),
            scratch_shapes=[pltpu.VMEM((2, MOE_TM, XA_W), F32),
                            pltpu.VMEM(up_shape, F32), pltpu.VMEM(up_shape, F32), pltpu.VMEM(down_shape, F32),
                            pltpu.VMEM(up_shape, BF16), pltpu.VMEM(up_shape, BF16), pltpu.VMEM(down_shape, BF16),
                            pltpu.SemaphoreType.DMA((3,)), pltpu.SemaphoreType.DMA((2,))],
        ),
        out_shape=jax.ShapeDtypeStruct((MOE_ROWS, D_MODEL), F32),
        compiler_params=_params(("arbitrary",)),
        name="moe_experts",
    )(*tile_meta, src, xa, wg, wu, wd)


def _combine_kernel(pos_ref, ys_ref, x_ref, g_ref, b_ref, o_ref, buf, sem, *, tc):
    i = pl.program_id(0)
    slot = i % 2

    def gather(step, into):
        def issue(blk, carry):
            for u in range(ROW_DMA_UNROLL):
                r = blk * ROW_DMA_UNROLL + u
                row = pos_ref[step * tc + r]
                pltpu.make_async_copy(ys_ref.at[pl.ds(row, 1)], buf.at[into, pl.ds(r, 1)], sem.at[into]).start(
                    priority=u % 2)
            return carry

        lax.fori_loop(0, tc // ROW_DMA_UNROLL, issue, 0)

    @pl.when(i == 0)
    def _():
        gather(0, 0)

    @pl.when(i + 1 < pl.num_programs(0))
    def _():
        gather(i + 1, 1 - slot)

    def retire(r, carry):
        pltpu.make_async_copy(ys_ref.at[pl.ds(0, 1)], buf.at[slot, pl.ds(0, 1)], sem.at[slot]).wait()
        return carry

    lax.fori_loop(0, tc, retire, 0, unroll=8)
    o_ref[...] = _layer_norm(ALPHA * x_ref[...] + buf[slot], g_ref[...], b_ref[...])


def _combine_ln(pos, ys, xa, g, b, tc=256):
    return pl.pallas_call(
        functools.partial(_combine_kernel, tc=tc),
        grid_spec=pltpu.PrefetchScalarGridSpec(
            num_scalar_prefetch=1,
            grid=(N_TOK // tc,),
            in_specs=[pl.BlockSpec(memory_space=pl.ANY),
                      pl.BlockSpec((tc, D_MODEL), lambda i, p: (i, 0)),
                      pl.BlockSpec((1, D_MODEL), lambda i, p: (0, 0)),
                      pl.BlockSpec((1, D_MODEL), lambda i, p: (0, 0))],
            out_specs=pl.BlockSpec((tc, D_MODEL), lambda i, p: (i, 0)),
            scratch_shapes=[pltpu.VMEM((2, tc, D_MODEL), F32), pltpu.SemaphoreType.DMA((2,))],
        ),
        out_shape=jax.ShapeDtypeStruct((N_TOK, D_MODEL), F32),
        compiler_params=_params(("arbitrary",)),
        name="moe_combine_ln",
    )(pos, ys, xa, g, b)


def _moe_ln(routed, layer, wg, wu, wd, g, b):
    xa, cls, rank, counts = routed
    padded = (counts + MOE_TM - 1) // MOE_TM * MOE_TM
    ends = jnp.cumsum(padded)
    starts = ends - padded
    pos = starts[cls] + rank
    tile_start = jnp.arange(MOE_TILES, dtype=jnp.int32) * MOE_TM
    tile_used = (tile_start < ends[-1]).astype(jnp.int32)
    last_cls = jnp.max(jnp.where(counts > 0, jnp.arange(N_CLASSES), 0))
    tile_cls = jnp.minimum(jnp.sum(tile_start[:, None] >= ends[None, :], axis=1), last_cls)
    tile_cls = tile_cls.astype(jnp.int32)
    tile_grp = tile_cls // len(PAIRS)
    prev_grp = jnp.concatenate([jnp.full((1,), -1, jnp.int32), tile_grp[:-1]])
    tile_first = tile_used * (tile_grp != prev_grp).astype(jnp.int32)
    has = jnp.sum(counts.reshape(N_EXPERT_GROUPS, len(PAIRS)), axis=1) > 0
    gid = jnp.arange(N_EXPERT_GROUPS, dtype=jnp.int32)
    later = jnp.where(jnp.logical_and(has[None, :], gid[None, :] > gid[:, None]), gid[None, :], N_EXPERT_GROUPS)
    next_grp = jnp.min(later, axis=1)
    next_grp = jnp.where(next_grp == N_EXPERT_GROUPS, -1, next_grp)
    pair_slots = jnp.asarray(np.array(PAIRS, np.int32))[tile_cls % len(PAIRS)]
    tile_meta = (pair_slots[:, 0], pair_slots[:, 1], tile_used, tile_first, tile_grp, next_grp[tile_grp])
    src = jnp.zeros((MOE_ROWS,), jnp.int32).at[pos].set(jnp.arange(N_TOK, dtype=jnp.int32), unique_indices=True)
    ys = _experts(tile_meta, src, xa, layer, wg, wu, wd)
    return _combine_ln(pos, ys, xa, g, b)


def _tile_heads(t64):
    return jnp.concatenate([t64, t64], axis=-1)


def _axial_tables():
    rows = SEQ // GRID_W
    row = jnp.repeat(jnp.arange(rows), GRID_W).astype(F32)
    col = jnp.tile(jnp.arange(GRID_W), rows).astype(F32)
    half = HEAD_DIM // 2
    inv = 1.0 / (AXIAL_THETA ** (jnp.arange(0, half, 2, dtype=F32) / half))
    ar, ac = row[:, None] * inv, col[:, None] * inv
    z = jnp.zeros_like(ar)
    c = jnp.concatenate([jnp.cos(ar), jnp.cos(ar), jnp.cos(ac), jnp.cos(ac)], -1)
    sa = jnp.concatenate([-jnp.sin(ar), z, -jnp.sin(ac), z], -1)
    sb = jnp.concatenate([z, jnp.sin(ar), z, jnp.sin(ac)], -1)
    return tuple(_tile_heads(t) for t in (c, sa, sb))


def _partial_tables():
    n = PARTIAL_ROT_DIMS
    pos = jnp.arange(SEQ, dtype=F32)
    ang = pos[:, None] * (1.0 / (PARTIAL_THETA ** (jnp.arange(0, n, 2, dtype=F32) / n)))
    z = jnp.zeros_like(ang)
    rest = HEAD_DIM - n
    c = jnp.concatenate([jnp.cos(ang), jnp.cos(ang), jnp.ones((SEQ, rest), F32)], -1)
    sa = jnp.concatenate([-jnp.sin(ang), z, jnp.zeros((SEQ, rest), F32)], -1)
    sb = jnp.concatenate([z, jnp.sin(ang), jnp.zeros((SEQ, rest), F32)], -1)
    return tuple(_tile_heads(t) for t in (c, sa, sb))


def _residue_order(tab, d):
    return tab.reshape(SEQ // d, d, LANES).transpose(1, 0, 2)


def kernel(x, mem, w_mem_kv, router_w, router_b, a_w_in, a_w_out, a_q_norm, a_k_norm, b_w_in, b_w_out,
           ln1_g, ln1_b, ln2_g, ln2_b, w_gate, w_up, w_down):
    x = x.reshape(N_TOK, D_MODEL)
    row = lambda v: v.reshape(1, -1)

    lane = np.arange(LANES)
    m128 = jnp.asarray((lane[:, None] // HEAD_DIM == lane[None, :] // HEAD_DIM), BF16)
    expand = jnp.asarray(lane[:, None] == np.arange(B_WIDTH)[None, :] // HEAD_DIM, BF16)
    tri = jnp.asarray(np.triu(np.ones((TOK_TILE, TOK_TILE), np.float32), 1), BF16)
    rw_hi = router_w.T.astype(BF16)
    rw_lo = (router_w.T - rw_hi.astype(F32)).astype(BF16)
    router = (rw_hi, rw_lo, router_b.reshape(N_EXPERTS, 1), tri)

    mkv = _proj(mem.reshape(BATCH * N_MEM, D_MODEL), w_mem_kv.astype(BF16))
    experts = tuple(w.reshape(DEPTH * N_EXPERTS, *w.shape[2:]) for w in (w_gate, w_up, w_down))

    kmix = A_Q_HEADS * HEAD_DIM
    qcols = (np.asarray(_Q_HEAD_ORDER)[:, None] * HEAD_DIM + np.arange(HEAD_DIM)[None, :]).reshape(-1)
    w_in = jnp.concatenate([a_w_in[0][:, qcols], a_w_in[0][:, kmix:]], axis=1).astype(BF16)
    qg = row(jnp.tile(a_q_norm[0], 2) * (SCALE * LOG2_E))
    kg = row(jnp.tile(a_k_norm[0], 2))
    q, k, v, qm = _a_in_proj(x, w_in, m128, qg, kg, _axial_tables())
    mix = _gqa_attention(q, k, v)
    w_out = a_w_out[0].astype(BF16)
    routed = _post_attn(mix, qm, mkv, w_out[qcols], w_out[kmix:], x, row(ln1_g[0]), row(ln1_b[0]), router)
    x = _moe_ln(routed, 0, *experts, row(ln2_g[0]), row(ln2_b[0]))

    ptabs = _partial_tables()
    tabs = [tuple(_residue_order(t, d) for t in ptabs) for d in DILATIONS]
    qkv, qm = _b_in_proj(x, b_w_in[0].astype(BF16), tabs)
    outs, lses = zip(*[_band_attention(*qkv[g], d) for g, d in enumerate(DILATIONS)])
    w_out = b_w_out[0].astype(BF16)
    routed = _post_attn((outs, lses), qm, mkv, w_out[:B_WIDTH], w_out[B_WIDTH:], x, row(ln1_g[1]), row(ln1_b[1]),
                        router, expand)
    x = _moe_ln(routed, 1, *experts, row(ln2_g[1]), row(ln2_b[1]))
    return x.reshape(BATCH, SEQ, D_MODEL)
```

```python
import functools
import math

import numpy as np
import jax
import jax.numpy as jnp
from jax import lax
from jax.experimental import pallas as pl
from jax.experimental.pallas import tpu as pltpu

F32 = jnp.float32
BF16 = jnp.bfloat16

D_MODEL = 1024
BATCH = 8
SEQ = 2048
N_TOK = BATCH * SEQ
HEAD_DIM = 64
GRID_W = 64
N_MEM = 256
MEM_HEADS = 4
A_Q_HEADS = 12
A_KV_HEADS = 4
AXIAL_THETA = 10000.0
B_HEADS = 8
B_WIDTH = B_HEADS * HEAD_DIM
DILATIONS = (1, 4, 16)
BAND_RADIUS = 64
PARTIAL_ROT_DIMS = HEAD_DIM // 4
PARTIAL_THETA = 500000.0
N_EXPERTS = 16
N_EXPERT_GROUPS = 4
EXPERTS_PER_GROUP = 4
D_EXPERT = 512
DEPTH = 2
ALPHA = (2 * DEPTH) ** 0.25
NORM_EPS = 1e-6
NEG_BIG = -1e30
SCALE = HEAD_DIM ** -0.5
LOG2_E = math.log2(math.e)
_Q_HEAD_ORDER = tuple(6 * p + 3 * half + t for p in range(2) for t in range(3) for half in range(2))

LANES = 128
MEM_W = MEM_HEADS * HEAD_DIM
TOK_TILE = 512
TILES_PER_SEQ = SEQ // TOK_TILE
GQA_KC = 256
BAND_TQ = 128
BAND_TILES = 4

PAIRS = ((0, 1), (0, 2), (0, 3), (1, 2), (1, 3), (2, 3))
N_CLASSES = N_EXPERT_GROUPS * len(PAIRS)
CLASS_ROWS = 32
MOE_TM = 256
MOE_ROWS = N_TOK + N_CLASSES * MOE_TM
MOE_TILES = MOE_ROWS // MOE_TM
XA_W = D_MODEL + LANES
ROW_DMA_UNROLL = 8

VMEM_LIMIT = 56 * 1024 * 1024


def _params(sem, vmem=VMEM_LIMIT):
    return pltpu.CompilerParams(dimension_semantics=sem, vmem_limit_bytes=vmem)


def _dot(a, b):
    return jnp.dot(a, b, preferred_element_type=F32)


def _dot_nt(a, b):
    return lax.dot_general(a, b, (((1,), (1,)), ((), ())), preferred_element_type=F32)


def _split_bf16(x):
    hi = x.astype(BF16)
    lo = (x - hi.astype(F32)).astype(BF16)
    return hi, lo


def _layer_norm(z, g, b):
    mu = jnp.mean(z, axis=-1, keepdims=True)
    zc = z - mu
    var = jnp.mean(zc * zc, axis=-1, keepdims=True)
    return zc * lax.rsqrt(var + NORM_EPS) * g + b


def _rope(z, c, sa, sb, shift):
    return z * c + pltpu.roll(z, LANES - shift, 1) * sa + pltpu.roll(z, shift, 1) * sb


def _first_head(shape):
    return lax.broadcasted_iota(jnp.int32, shape, 1) < HEAD_DIM


def _pair_attention(q2, k2, v3, first, valid=None, base2=False):
    res = []
    for pick in (first, jnp.logical_not(first)):
        s = _dot_nt(jnp.where(pick, q2, jnp.zeros_like(q2)), k2)
        if valid is not None:
            s = jnp.where(valid, s, NEG_BIG)
        m = jnp.max(s, axis=1, keepdims=True)
        p = jnp.exp2(s - m) if base2 else jnp.exp(s - m)
        res.append((_dot(p.astype(BF16), v3), m))
    return res


def _pair_output(res, first):
    (ol_a, _), (ol_b, _) = res
    return jnp.where(first, ol_a[:, :LANES], ol_b[:, :LANES]) / jnp.where(first, ol_a[:, LANES:], ol_b[:, LANES:])


def _with_ones(v2):
    return jnp.concatenate([v2, jnp.ones_like(v2)], axis=1)


def _proj_kernel(x_ref, w_ref, o_ref):
    o_ref[...] = _dot(x_ref[...].astype(BF16), w_ref[...]).astype(o_ref.dtype)


def _proj(x, w, tm=TOK_TILE):
    n, k = x.shape
    width = w.shape[1]
    return pl.pallas_call(
        _proj_kernel,
        grid=(n // tm,),
        in_specs=[pl.BlockSpec((tm, k), lambda i: (i, 0)), pl.BlockSpec((k, width), lambda i: (0, 0))],
        out_specs=pl.BlockSpec((tm, width), lambda i: (i, 0)),
        out_shape=jax.ShapeDtypeStruct((n, width), BF16),
        compiler_params=_params(("parallel",)),
        name="proj",
    )(x, w)


def _a_in_kernel(x_ref, w_ref, m_ref, qg_ref, kg_ref, c_ref, sa_ref, sb_ref, q_ref, k_ref, v_ref, qm_ref):
    y = _dot(x_ref[...].astype(BF16), w_ref[...])
    m = m_ref[...]
    c, sa, sb = c_ref[...], sa_ref[...], sb_ref[...]

    def norm_rope(z, gain):
        hi, lo = _split_bf16(z * z)
        ms = (_dot(hi, m) + _dot(lo, m)) * (1.0 / HEAD_DIM)
        z = z * lax.rsqrt(ms + NORM_EPS) * gain
        return _rope(z, c, sa, sb, 16)

    nq = A_Q_HEADS * HEAD_DIM // LANES
    nk = A_KV_HEADS * HEAD_DIM // LANES
    for ch in range(nq):
        q_ref[:, ch * LANES:(ch + 1) * LANES] = norm_rope(y[:, ch * LANES:(ch + 1) * LANES], qg_ref[...]).astype(BF16)
    off = nq * LANES
    for ch in range(nk):
        z = y[:, off + ch * LANES: off + (ch + 1) * LANES]
        k_ref[:, ch * LANES:(ch + 1) * LANES] = norm_rope(z, kg_ref[...]).astype(BF16)
    off += nk * LANES
    ones = jnp.ones((y.shape[0], LANES), BF16)
    for ch in range(nk):
        v_ref[:, 2 * ch * LANES:(2 * ch + 1) * LANES] = y[:, off + ch * LANES: off + (ch + 1) * LANES].astype(BF16)
        v_ref[:, (2 * ch + 1) * LANES:(2 * ch + 2) * LANES] = ones
    off += nk * LANES
    qm_ref[...] = (y[:, off:off + MEM_W] * SCALE).astype(BF16)


def _a_in_proj(x, w, m128, qg, kg, tabs, tm=TOK_TILE):
    c, sa, sb = tabs
    nblk = SEQ // tm
    a_in = w.shape[1]
    tab_spec = pl.BlockSpec((tm, LANES), lambda i: (i % nblk, 0))
    row = lambda width: pl.BlockSpec((tm, width), lambda i: (i, 0))
    const = lambda shape: pl.BlockSpec(shape, lambda i: (0, 0))
    qw, kw = A_Q_HEADS * HEAD_DIM, A_KV_HEADS * HEAD_DIM
    return pl.pallas_call(
        _a_in_kernel,
        grid=(N_TOK // tm,),
        in_specs=[row(D_MODEL), const((D_MODEL, a_in)), const((LANES, LANES)), const((1, LANES)), const((1, LANES)),
                  tab_spec, tab_spec, tab_spec],
        out_specs=[row(qw), row(kw), row(2 * kw), row(MEM_W)],
        out_shape=[jax.ShapeDtypeStruct((N_TOK, qw), BF16), jax.ShapeDtypeStruct((N_TOK, kw), BF16),
                   jax.ShapeDtypeStruct((N_TOK, 2 * kw), BF16), jax.ShapeDtypeStruct((N_TOK, MEM_W), BF16)],
        compiler_params=_params(("parallel",)),
        name="a_in_proj",
    )(x, w, m128, qg, kg, c, sa, sb)


def _b_in_kernel(*refs):
    ng = len(DILATIONS)
    nx = D_MODEL // LANES
    x_refs, w_ref, refs = refs[:nx], refs[nx], refs[nx + 1:]
    tabs, outs, qm_ref = refs[:3 * ng], refs[3 * ng:6 * ng], refs[6 * ng]
    xb1 = None
    for g, d in enumerate(DILATIONS):
        n = TOK_TILE // d
        if d == 1:
            cols = [r[...] for r in x_refs]
        else:
            cols = [jnp.concatenate([r[pl.ds(k, n, stride=d), :] for k in range(d)], axis=0) for r in x_refs]
        xb = jnp.concatenate(cols, axis=1).astype(BF16)
        xb1 = xb if d == 1 else xb1
        c, sa, sb = (t[...].reshape(TOK_TILE, LANES) for t in tabs[3 * g:3 * g + 3])
        q_ref, k_ref, v_ref = outs[3 * g:3 * g + 3]
        yq = _dot(xb, w_ref[:, g * B_WIDTH:(g + 1) * B_WIDTH]) * SCALE
        yk = _dot(xb, w_ref[:, (ng + g) * B_WIDTH:(ng + g + 1) * B_WIDTH])
        yv = _dot(xb, w_ref[:, (2 * ng + g) * B_WIDTH:(2 * ng + g + 1) * B_WIDTH])
        for ch in range(B_WIDTH // LANES):
            sl = slice(ch * LANES, (ch + 1) * LANES)
            q_ref[0, :, :, sl] = _rope(yq[:, sl], c, sa, sb, PARTIAL_ROT_DIMS // 2).reshape(d, n, LANES).astype(BF16)
            k_ref[0, :, :, sl] = _rope(yk[:, sl], c, sa, sb, PARTIAL_ROT_DIMS // 2).reshape(d, n, LANES).astype(BF16)
        v_ref[0] = yv.reshape(d, n, B_WIDTH).astype(BF16)
    qm = _dot(xb1, w_ref[:, 3 * ng * B_WIDTH:3 * ng * B_WIDTH + MEM_W])
    qm_ref[...] = (qm * SCALE).astype(BF16)


def _b_in_proj(x, w, tabs):
    nx = D_MODEL // LANES
    in_specs = [pl.BlockSpec((TOK_TILE, LANES), functools.partial(lambda i, c: (i, c), c=c)) for c in range(nx)]
    in_specs.append(pl.BlockSpec(w.shape, lambda i: (0, 0)))
    out_specs, out_shape = [], []
    for d in DILATIONS:
        n = TOK_TILE // d
        in_specs += [pl.BlockSpec((d, n, LANES), lambda i: (0, i % TILES_PER_SEQ, 0))] * 3
        out_specs += [pl.BlockSpec((1, d, n, B_WIDTH), lambda i: (i // TILES_PER_SEQ, 0, i % TILES_PER_SEQ, 0))] * 3
        out_shape += [jax.ShapeDtypeStruct((BATCH, d, SEQ // d, B_WIDTH), BF16)] * 3
    out_specs.append(pl.BlockSpec((TOK_TILE, MEM_W), lambda i: (i, 0)))
    out_shape.append(jax.ShapeDtypeStruct((N_TOK, MEM_W), BF16))
    res = pl.pallas_call(
        _b_in_kernel,
        grid=(N_TOK // TOK_TILE,),
        in_specs=in_specs,
        out_specs=out_specs,
        out_shape=out_shape,
        compiler_params=_params(("parallel",)),
        name="b_in_proj",
    )(*([x] * nx), w, *[t for group in tabs for t in group])
    return [res[3 * g:3 * g + 3] for g in range(len(DILATIONS))], res[-1]


def _gqa_kernel(q_ref, k_ref, v_ref, o_ref):
    tq = q_ref.shape[0]
    ntile = q_ref.shape[1] // LANES
    q = jnp.concatenate([q_ref[:, t * LANES:(t + 1) * LANES] for t in range(ntile)], axis=0)
    first = _first_head(q.shape)
    res = []
    for pick in (first, jnp.logical_not(first)):
        qh = jnp.where(pick, q, jnp.zeros_like(q))
        m = jnp.full((q.shape[0], 1), -jnp.inf, F32)
        acc = jnp.zeros((q.shape[0], 2 * LANES), F32)
        for c in range(SEQ // GQA_KC):
            keys = slice(c * GQA_KC, (c + 1) * GQA_KC)
            s = _dot_nt(qh, k_ref[keys, :])
            m_new = jnp.maximum(m, jnp.max(s, axis=1, keepdims=True))
            p = jnp.exp2(s - m_new).astype(BF16)
            acc = acc * jnp.exp2(m - m_new) + _dot(p, v_ref[keys, :])
            m = m_new
        res.append((acc, m))
    o = _pair_output(res, first)
    for t in range(ntile):
        o_ref[:, t * LANES:(t + 1) * LANES] = o[t * tq:(t + 1) * tq].astype(BF16)


def _gqa_attention(q, k, v, tq=512):
    nq = SEQ // tq
    qw = q.shape[1] // 2
    kw = k.shape[1] // 2
    return pl.pallas_call(
        _gqa_kernel,
        grid=(BATCH, 2, nq),
        in_specs=[pl.BlockSpec((tq, qw), lambda b, p, i: (b * nq + i, p)),
                  pl.BlockSpec((SEQ, kw), lambda b, p, i: (b, p)),
                  pl.BlockSpec((SEQ, 2 * kw), lambda b, p, i: (b, p))],
        out_specs=pl.BlockSpec((tq, qw), lambda b, p, i: (b * nq + i, p)),
        out_shape=jax.ShapeDtypeStruct(q.shape, BF16),
        compiler_params=_params(("parallel", "parallel", "parallel")),
        name="gqa_attention",
    )(q, k, v)


def _band_kernel(q_ref, k_ref, v_ref, o_ref, lse_ref, *, length, seg, win):
    tq = BAND_TQ
    first = _first_head((tq, LANES))
    lane = lax.broadcasted_iota(jnp.int32, (tq, LANES), 1)
    row = lax.broadcasted_iota(jnp.int32, (tq, win), 0)
    col = lax.broadcasted_iota(jnp.int32, (tq, win), 1)
    for u in range(BAND_TILES):
        tile = pl.program_id(1) * BAND_TILES + u
        start = pl.multiple_of(jnp.clip(tile * tq - BAND_RADIUS, 0, length - win), BAND_RADIUS)
        qpos, kpos = tile * tq + row, start + col
        valid = jnp.abs(qpos - kpos) <= BAND_RADIUS
        if seg < length:
            valid = jnp.logical_and(valid, qpos // seg == kpos // seg)
        rows = slice(u * tq, (u + 1) * tq)
        lse = jnp.zeros((tq, LANES), F32)
        for j in range(B_WIDTH // LANES):
            sl = slice(j * LANES, (j + 1) * LANES)
            res = _pair_attention(q_ref[0, rows, sl], k_ref[0, pl.ds(start, win), sl],
                                  _with_ones(v_ref[0, pl.ds(start, win), sl]), first, valid)
            o_ref[0, rows, sl] = _pair_output(res, first).astype(BF16)
            for half, (ol, m) in enumerate(res):
                lse = jnp.where(lane == 2 * j + half, m + jnp.log(ol[:, LANES:]), lse)
        lse_ref[0, rows, :] = lse


def _band_attention(q, k, v, d):
    seg = SEQ // d
    step_rows = BAND_TILES * BAND_TQ
    length = max(seg, step_rows)
    win = BAND_TQ + 2 * BAND_RADIUS
    nblk = N_TOK // length
    q, k, v = (t.reshape(nblk, length, B_WIDTH) for t in (q, k, v))
    seq_spec = pl.BlockSpec((1, length, B_WIDTH), lambda s, i: (s, 0, 0))
    o, lse = pl.pallas_call(
        functools.partial(_band_kernel, length=length, seg=seg, win=win),
        grid=(nblk, length // step_rows),
        in_specs=[pl.BlockSpec((1, step_rows, B_WIDTH), lambda s, i: (s, i, 0)), seq_spec, seq_spec],
        out_specs=[pl.BlockSpec((1, step_rows, B_WIDTH), lambda s, i: (s, i, 0)),
                   pl.BlockSpec((1, step_rows, LANES), lambda s, i: (s, i, 0))],
        out_shape=[jax.ShapeDtypeStruct((nblk, length, B_WIDTH), BF16),
                   jax.ShapeDtypeStruct((nblk, length, LANES), F32)],
        compiler_params=_params(("parallel", "parallel")),
        name=f"band_attention_d{d}",
    )(q, k, v)
    return o.reshape(BATCH, d, seg, B_WIDTH), lse.reshape(BATCH, d, seg, LANES)


def _route(x, wh_ref, wl_ref, rb_ref, tri_ref, xa_ref, cls_ref, rank_ref, cnt_ref, carry_ref):
    tm = x.shape[0]
    xh, xl = _split_bf16(x)
    logits = _dot_nt(wh_ref[...], xh) + _dot_nt(wh_ref[...], xl) + _dot_nt(wl_ref[...], xh)
    aff = 1.0 / (1.0 + jnp.exp(-logits))
    sel = aff + rb_ref[...]
    s = [sel[e:e + 1, :] for e in range(N_EXPERTS)]
    a = [aff[e:e + 1, :] for e in range(N_EXPERTS)]

    def top2_sum(v):
        hi01, lo01 = jnp.maximum(v[0], v[1]), jnp.minimum(v[0], v[1])
        hi23, lo23 = jnp.maximum(v[2], v[3]), jnp.minimum(v[2], v[3])
        return jnp.maximum(hi01, hi23) + jnp.maximum(jnp.minimum(hi01, hi23), jnp.maximum(lo01, lo23))

    gscore = [top2_sum(s[EXPERTS_PER_GROUP * g:EXPERTS_PER_GROUP * (g + 1)]) for g in range(N_EXPERT_GROUPS)]
    best = jnp.zeros((1, tm), jnp.int32)
    best_score = gscore[0]
    for g in range(1, N_EXPERT_GROUPS):
        better = gscore[g] > best_score
        best = jnp.where(better, g, best)
        best_score = jnp.where(better, gscore[g], best_score)

    def pick(rows, j):
        out = rows[j]
        for g in range(1, N_EXPERT_GROUPS):
            out = jnp.where(best == g, rows[EXPERTS_PER_GROUP * g + j], out)
        return out

    t = [pick(s, j) for j in range(EXPERTS_PER_GROUP)]
    w = [pick(a, j) for j in range(EXPERTS_PER_GROUP)]

    def first_max(v):
        mx = jnp.maximum(jnp.maximum(v[0], v[1]), jnp.maximum(v[2], v[3]))
        idx = jnp.full((1, tm), EXPERTS_PER_GROUP - 1, jnp.int32)
        for j in range(EXPERTS_PER_GROUP - 2, -1, -1):
            idx = jnp.where(v[j] == mx, j, idx)
        return idx

    i1 = first_max(t)
    i2 = first_max([jnp.where(i1 == j, -jnp.inf, t[j]) for j in range(EXPERTS_PER_GROUP)])
    lo, hi = jnp.minimum(i1, i2), jnp.maximum(i1, i2)

    def take(rows, idx):
        out = rows[0]
        for j in range(1, EXPERTS_PER_GROUP):
            out = jnp.where(idx == j, rows[j], out)
        return out

    w_lo, w_hi = take(w, lo), take(w, hi)
    den = w_lo + w_hi
    pair = jnp.where(lo == 0, hi - 1, jnp.where(lo == 1, hi + 1, len(PAIRS) - 1))
    cls = best * len(PAIRS) + pair

    onehot = (lax.broadcasted_iota(jnp.int32, (CLASS_ROWS, tm), 0) == cls).astype(F32)
    before = _dot(onehot.astype(BF16), tri_ref[...]) + carry_ref[:, 0:1]
    rank = jnp.sum(onehot * before, axis=0, keepdims=True)
    carry_ref[...] = carry_ref[...] + jnp.sum(onehot, axis=1, keepdims=True)

    cls_ref[0] = cls
    rank_ref[0] = rank.astype(jnp.int32)
    cnt_ref[...] = carry_ref[...]
    gates = jnp.concatenate([w_lo / den, w_hi / den, jnp.zeros((LANES - 2, tm), F32)], axis=0)
    xa_ref[:, :D_MODEL] = x
    xa_ref[:, D_MODEL:] = gates.T


def _post_attn_kernel(*refs, merge):
    if merge:
        o_refs, l_refs, e_ref, refs = refs[:3], refs[3:6], refs[6], refs[7:]
    else:
        mix_ref, refs = refs[0], refs[1:]
    (qm_ref, mk_ref, mv_ref, w1_ref, w2_ref, x_ref, g_ref, b_ref, wh_ref, wl_ref, rb_ref, tri_ref,
     xa_ref, cls_ref, rank_ref, cnt_ref, carry_ref) = refs[:17]

    @pl.when(pl.program_id(0) == 0)
    def _():
        carry_ref[...] = jnp.zeros_like(carry_ref)

    if merge:
        ot_ref, lt_ref = refs[17], refs[18]
        nch = B_WIDTH // LANES
        for g, d in enumerate(DILATIONS):
            n = TOK_TILE // d
            for r in range(d):
                rows = pl.ds(r, n, stride=d) if d > 1 else slice(None)
                og = o_refs[g][0, r].astype(F32)
                for ch in range(nch):
                    ot_ref[g, ch, rows, :] = og[:, ch * LANES:(ch + 1) * LANES]
                lt_ref[g, rows, :] = l_refs[g][0, r]
        lses = [lt_ref[g] for g in range(len(DILATIONS))]
        m = functools.reduce(jnp.maximum, lses)
        es = [jnp.exp(l - m) for l in lses]
        inv = 1.0 / functools.reduce(jnp.add, es)
        expand = e_ref[...]

        def widen(w):
            hi, lo = _split_bf16(w)
            return _dot(hi, expand) + _dot(lo, expand)

        ws = [widen(e * inv) for e in es]
        mix = jnp.concatenate(
            [functools.reduce(jnp.add, [w[:, ch * LANES:(ch + 1) * LANES] * ot_ref[g, ch] for g, w in enumerate(ws)])
             for ch in range(nch)], axis=1).astype(BF16)
    else:
        mix = mix_ref[...]

    first = _first_head((TOK_TILE, LANES))
    memo = []
    for j in range(MEM_W // LANES):
        sl = slice(j * LANES, (j + 1) * LANES)
        memo.append(_pair_output(_pair_attention(qm_ref[:, sl], mk_ref[:, sl], _with_ones(mv_ref[:, sl]), first), first))
    memo = jnp.concatenate(memo, axis=1).astype(BF16)

    attn = _dot(mix, w1_ref[...]) + _dot(memo, w2_ref[...])
    x1 = _layer_norm(ALPHA * x_ref[...] + attn, g_ref[...], b_ref[...])
    _route(x1, wh_ref, wl_ref, rb_ref, tri_ref, xa_ref, cls_ref, rank_ref, cnt_ref, carry_ref)


def _post_attn(mix, qm, mkv, w1, w2, x, g, b, router, expand=None):
    merge = expand is not None
    wh, wl, rb, tri = router
    nblk = N_TOK // TOK_TILE
    row = lambda width: pl.BlockSpec((TOK_TILE, width), lambda i: (i, 0))
    const = lambda shape: pl.BlockSpec(shape, lambda i: (0,) * len(shape))
    row3 = pl.BlockSpec((1, 1, TOK_TILE), lambda i: (i, 0, 0))
    scratch = [pltpu.VMEM((CLASS_ROWS, LANES), F32)]
    if merge:
        outs, lses = mix
        resid = lambda d, width: pl.BlockSpec((1, d, TOK_TILE // d, width),
                                              lambda i: (i // TILES_PER_SEQ, 0, i % TILES_PER_SEQ, 0))
        lead_specs = ([resid(d, B_WIDTH) for d in DILATIONS] + [resid(d, LANES) for d in DILATIONS]
                      + [const(expand.shape)])
        lead = [*outs, *lses, expand]
        scratch += [pltpu.VMEM((len(DILATIONS), B_WIDTH // LANES, TOK_TILE, LANES), F32),
                    pltpu.VMEM((len(DILATIONS), TOK_TILE, LANES), F32)]
    else:
        lead_specs, lead = [row(mix.shape[1])], [mix]
    xa, cls, rank, cnt = pl.pallas_call(
        functools.partial(_post_attn_kernel, merge=merge),
        grid=(nblk,),
        in_specs=lead_specs + [row(MEM_W),
                               pl.BlockSpec((N_MEM, MEM_W), lambda i: (i // TILES_PER_SEQ, 0)),
                               pl.BlockSpec((N_MEM, MEM_W), lambda i: (i // TILES_PER_SEQ, 1)),
                               const(w1.shape), const(w2.shape), row(D_MODEL), const((1, D_MODEL)), const((1, D_MODEL)),
                               const(wh.shape), const(wl.shape), const(rb.shape), const(tri.shape)],
        out_specs=[row(XA_W), row3, row3, const((CLASS_ROWS, LANES))],
        out_shape=[jax.ShapeDtypeStruct((N_TOK, XA_W), F32), jax.ShapeDtypeStruct((nblk, 1, TOK_TILE), jnp.int32),
                   jax.ShapeDtypeStruct((nblk, 1, TOK_TILE), jnp.int32), jax.ShapeDtypeStruct((CLASS_ROWS, LANES), F32)],
        scratch_shapes=scratch,
        compiler_params=_params(("arbitrary",)),
        name="post_attn_merge" if merge else "post_attn",
    )(*lead, qm, mkv, mkv, w1, w2, x, g, b, wh, wl, rb, tri)
    return xa, cls.reshape(N_TOK), rank.reshape(N_TOK), cnt[:N_CLASSES, 0].astype(jnp.int32)


def _dispatch_kernel(pos_ref, xa_ref, init_ref, xs_ref, sem, *, chunk):
    del init_ref
    base = pl.program_id(0) * chunk

    def issue(blk, carry):
        for u in range(ROW_DMA_UNROLL):
            r = blk * ROW_DMA_UNROLL + u
            pltpu.make_async_copy(xa_ref.at[pl.ds(r, 1)], xs_ref.at[pl.ds(pos_ref[base + r], 1)], sem).start(
                priority=u % 2)
        return carry

    def retire(r, carry):
        pltpu.make_async_copy(xa_ref.at[pl.ds(0, 1)], xs_ref.at[pl.ds(0, 1)], sem).wait()
        return carry

    lax.fori_loop(0, chunk // ROW_DMA_UNROLL, issue, 0)
    lax.fori_loop(0, chunk, retire, 0, unroll=8)


def _dispatch(pos, xa, chunk=512):
    init = jnp.zeros((MOE_ROWS, XA_W), F32)
    return pl.pallas_call(
        functools.partial(_dispatch_kernel, chunk=chunk),
        grid_spec=pltpu.PrefetchScalarGridSpec(
            num_scalar_prefetch=1,
            grid=(N_TOK // chunk,),
            in_specs=[pl.BlockSpec((chunk, XA_W), lambda i, p: (i, 0)), pl.BlockSpec(memory_space=pl.ANY)],
            out_specs=pl.BlockSpec(memory_space=pl.ANY),
            scratch_shapes=[pltpu.SemaphoreType.DMA(())],
        ),
        out_shape=jax.ShapeDtypeStruct((MOE_ROWS, XA_W), F32),
        input_output_aliases={2: 0},
        compiler_params=_params(("arbitrary",)),
        name="moe_dispatch",
    )(pos, xa, init)


def _expert_kernel(slot1_ref, slot2_ref, used_ref, first_ref, grp_ref, next_ref, src_ref, xa_hbm, wg_hbm, wu_hbm, wd_hbm,
                   ys_ref, xbuf, stage_g, stage_u, stage_d, wg_s, wu_s, wd_s, sem, gsem, *, layer):
    i = pl.program_id(0)
    last = pl.num_programs(0) - 1
    slot = i % 2
    streams = ((wg_hbm, stage_g, wg_s), (wu_hbm, stage_u, wu_s), (wd_hbm, stage_d, wd_s))

    def group_copy(group, k):
        first_expert = layer * N_EXPERTS + group * EXPERTS_PER_GROUP
        hbm, stage, _ = streams[k]
        return pltpu.make_async_copy(hbm.at[pl.ds(first_expert, EXPERTS_PER_GROUP)], stage, sem.at[k])

    def gather(tile, into):
        for r in range(MOE_TM):
            row = src_ref[tile * MOE_TM + r]
            pltpu.make_async_copy(xa_hbm.at[pl.ds(row, 1)], xbuf.at[into, pl.ds(r, 1)], gsem.at[into]).start()

    def gather_wait(into):
        def retire(r, carry):
            pltpu.make_async_copy(xa_hbm.at[pl.ds(0, 1)], xbuf.at[into, pl.ds(0, 1)], gsem.at[into]).wait()
            return carry

        lax.fori_loop(0, MOE_TM, retire, 0, unroll=8)

    @pl.when(i == 0)
    def _():
        for k in range(len(streams)):
            group_copy(grp_ref[0], k).start(priority=1)
        gather(0, 0)

    @pl.when(first_ref[i] != 0)
    def _():
        for k, (_, stage, dst) in enumerate(streams):
            group_copy(grp_ref[i], k).wait()
            for e in range(EXPERTS_PER_GROUP):
                dst[e] = stage[e].astype(BF16)

        @pl.when(next_ref[i] >= 0)
        def _():
            for k in range(len(streams)):
                group_copy(next_ref[i], k).start(priority=1)

    @pl.when(used_ref[i] != 0)
    def _():
        gather_wait(slot)
        gather(jnp.minimum(i + 1, last), 1 - slot)
        x = xbuf[slot, :, :D_MODEL].astype(BF16)
        gates = xbuf[slot, :, D_MODEL:]

        def expert(s):
            gate = _dot(x, wg_s[s])
            h = gate * (1.0 / (1.0 + jnp.exp(-gate))) * _dot(x, wu_s[s])
            return _dot(h.astype(BF16), wd_s[s])

        ys_ref[...] = gates[:, 0:1] * expert(slot1_ref[i]) + gates[:, 1:2] * expert(slot2_ref[i])

        @pl.when(i == last)
        def _():
            gather_wait(1 - slot)

    @pl.when(used_ref[i] == 0)
    def _():
        ys_ref[...] = jnp.zeros_like(ys_ref)

        @pl.when(used_ref[jnp.maximum(i - 1, 0)] != 0)
        def _():
            gather_wait(slot)


def _experts(tile_meta, src, xa, layer, wg, wu, wd):
    nmeta = len(tile_meta) + 1
    up_shape = (EXPERTS_PER_GROUP, D_MODEL, D_EXPERT)
    down_shape = (EXPERTS_PER_GROUP, D_EXPERT, D_MODEL)
    anywhere = pl.BlockSpec(memory_space=pl.ANY)
    return pl.pallas_call(
        functools.partial(_expert_kernel, layer=layer),
        grid_spec=pltpu.PrefetchScalarGridSpec(
            num_scalar_prefetch=nmeta,
            grid=(MOE_TILES,),
            in_specs=[anywhere, anywhere, anywhere, anywhere],
            out_specs=pl.BlockSpec((MOE_TM, D_MODEL), lambda i, *_: (i, 0)),
            scratch_shapes=[pltpu.VMEM((2, MOE_TM, XA_W), F32),
                            pltpu.VMEM(up_shape, F32), pltpu.VMEM(up_shape, F32), pltpu.VMEM(down_shape, F32),
                            pltpu.VMEM(up_shape, BF16), pltpu.VMEM(up_shape, BF16), pltpu.VMEM(down_shape, BF16),
                            pltpu.SemaphoreType.DMA((3,)), pltpu.SemaphoreType.DMA((2,))],
        ),
        out_shape=jax.ShapeDtypeStruct((MOE_ROWS, D_MODEL), F32),
        compiler_params=_params(("arbitrary",)),
        name="moe_experts",
    )(*tile_meta, src, xa, wg, wu, wd)


def _combine_kernel(pos_ref, ys_ref, x_ref, g_ref, b_ref, o_ref, buf, sem, *, tc):
    i = pl.program_id(0)
    slot = i % 2

    def gather(step, into):
        def issue(blk, carry):
            for u in range(ROW_DMA_UNROLL):
                r = blk * ROW_DMA_UNROLL + u
                row = pos_ref[step * tc + r]
                pltpu.make_async_copy(ys_ref.at[pl.ds(row, 1)], buf.at[into, pl.ds(r, 1)], sem.at[into]).start(
                    priority=u % 2)
            return carry

        lax.fori_loop(0, tc // ROW_DMA_UNROLL, issue, 0)

    @pl.when(i == 0)
    def _():
        gather(0, 0)

    @pl.when(i + 1 < pl.num_programs(0))
    def _():
        gather(i + 1, 1 - slot)

    def retire(r, carry):
        pltpu.make_async_copy(ys_ref.at[pl.ds(0, 1)], buf.at[slot, pl.ds(0, 1)], sem.at[slot]).wait()
        return carry

    lax.fori_loop(0, tc, retire, 0, unroll=8)
    o_ref[...] = _layer_norm(ALPHA * x_ref[...] + buf[slot], g_ref[...], b_ref[...])


def _combine_ln(pos, ys, xa, g, b, tc=256):
    return pl.pallas_call(
        functools.partial(_combine_kernel, tc=tc),
        grid_spec=pltpu.PrefetchScalarGridSpec(
            num_scalar_prefetch=1,
            grid=(N_TOK // tc,),
            in_specs=[pl.BlockSpec(memory_space=pl.ANY),
                      pl.BlockSpec((tc, D_MODEL), lambda i, p: (i, 0)),
                      pl.BlockSpec((1, D_MODEL), lambda i, p: (0, 0)),
                      pl.BlockSpec((1, D_MODEL), lambda i, p: (0, 0))],
            out_specs=pl.BlockSpec((tc, D_MODEL), lambda i, p: (i, 0)),
            scratch_shapes=[pltpu.VMEM((2, tc, D_MODEL), F32), pltpu.SemaphoreType.DMA((2,))],
        ),
        out_shape=jax.ShapeDtypeStruct((N_TOK, D_MODEL), F32),
        compiler_params=_params(("arbitrary",)),
        name="moe_combine_ln",
    )(pos, ys, xa, g, b)


def _moe_ln(routed, layer, wg, wu, wd, g, b):
    xa, cls, rank, counts = routed
    padded = (counts + MOE_TM - 1) // MOE_TM * MOE_TM
    ends = jnp.cumsum(padded)
    starts = ends - padded
    pos = starts[cls] + rank
    tile_start = jnp.arange(MOE_TILES, dtype=jnp.int32) * MOE_TM
    tile_used = (tile_start < ends[-1]).astype(jnp.int32)
    last_cls = jnp.max(jnp.where(counts > 0, jnp.arange(N_CLASSES), 0))
    tile_cls = jnp.minimum(jnp.sum(tile_start[:, None] >= ends[None, :], axis=1), last_cls)
    tile_cls = tile_cls.astype(jnp.int32)
    tile_grp = tile_cls // len(PAIRS)
    prev_grp = jnp.concatenate([jnp.full((1,), -1, jnp.int32), tile_grp[:-1]])
    tile_first = tile_used * (tile_grp != prev_grp).astype(jnp.int32)
    has = jnp.sum(counts.reshape(N_EXPERT_GROUPS, len(PAIRS)), axis=1) > 0
    gid = jnp.arange(N_EXPERT_GROUPS, dtype=jnp.int32)
    later = jnp.where(jnp.logical_and(has[None, :], gid[None, :] > gid[:, None]), gid[None, :], N_EXPERT_GROUPS)
    next_grp = jnp.min(later, axis=1)
    next_grp = jnp.where(next_grp == N_EXPERT_GROUPS, -1, next_grp)
    pair_slots = jnp.asarray(np.array(PAIRS, np.int32))[tile_cls % len(PAIRS)]
    tile_meta = (pair_slots[:, 0], pair_slots[:, 1], tile_used, tile_first, tile_grp, next_grp[tile_grp])
    src = jnp.zeros((MOE_ROWS,), jnp.int32).at[pos].set(jnp.arange(N_TOK, dtype=jnp.int32), unique_indices=True)
    ys = _experts(tile_meta, src, xa, layer, wg, wu, wd)
    return _combine_ln(pos, ys, xa, g, b)


def _tile_heads(t64):
    return jnp.concatenate([t64, t64], axis=-1)


def _axial_tables():
    rows = SEQ // GRID_W
    row = jnp.repeat(jnp.arange(rows), GRID_W).astype(F32)
    col = jnp.tile(jnp.arange(GRID_W), rows).astype(F32)
    half = HEAD_DIM // 2
    inv = 1.0 / (AXIAL_THETA ** (jnp.arange(0, half, 2, dtype=F32) / half))
    ar, ac = row[:, None] * inv, col[:, None] * inv
    z = jnp.zeros_like(ar)
    c = jnp.concatenate([jnp.cos(ar), jnp.cos(ar), jnp.cos(ac), jnp.cos(ac)], -1)
    sa = jnp.concatenate([-jnp.sin(ar), z, -jnp.sin(ac), z], -1)
    sb = jnp.concatenate([z, jnp.sin(ar), z, jnp.sin(ac)], -1)
    return tuple(_tile_heads(t) for t in (c, sa, sb))


def _partial_tables():
    n = PARTIAL_ROT_DIMS
    pos = jnp.arange(SEQ, dtype=F32)
    ang = pos[:, None] * (1.0 / (PARTIAL_THETA ** (jnp.arange(0, n, 2, dtype=F32) / n)))
    z = jnp.zeros_like(ang)
    rest = HEAD_DIM - n
    c = jnp.concatenate([jnp.cos(ang), jnp.cos(ang), jnp.ones((SEQ, rest), F32)], -1)
    sa = jnp.concatenate([-jnp.sin(ang), z, jnp.zeros((SEQ, rest), F32)], -1)
    sb = jnp.concatenate([z, jnp.sin(ang), jnp.zeros((SEQ, rest), F32)], -1)
    return tuple(_tile_heads(t) for t in (c, sa, sb))


def _residue_order(tab, d):
    return tab.reshape(SEQ // d, d, LANES).transpose(1, 0, 2)


def kernel(x, mem, w_mem_kv, router_w, router_b, a_w_in, a_w_out, a_q_norm, a_k_norm, b_w_in, b_w_out,
           ln1_g, ln1_b, ln2_g, ln2_b, w_gate, w_up, w_down):
    x = x.reshape(N_TOK, D_MODEL)
    row = lambda v: v.reshape(1, -1)

    lane = np.arange(LANES)
    m128 = jnp.asarray((lane[:, None] // HEAD_DIM == lane[None, :] // HEAD_DIM), BF16)
    expand = jnp.asarray(lane[:, None] == np.arange(B_WIDTH)[None, :] // HEAD_DIM, BF16)
    tri = jnp.asarray(np.triu(np.ones((TOK_TILE, TOK_TILE), np.float32), 1), BF16)
    rw_hi = router_w.T.astype(BF16)
    rw_lo = (router_w.T - rw_hi.astype(F32)).astype(BF16)
    router = (rw_hi, rw_lo, router_b.reshape(N_EXPERTS, 1), tri)

    mkv = _proj(mem.reshape(BATCH * N_MEM, D_MODEL), w_mem_kv.astype(BF16))
    experts = tuple(w.reshape(DEPTH * N_EXPERTS, *w.shape[2:]) for w in (w_gate, w_up, w_down))

    kmix = A_Q_HEADS * HEAD_DIM
    qcols = (np.asarray(_Q_HEAD_ORDER)[:, None] * HEAD_DIM + np.arange(HEAD_DIM)[None, :]).reshape(-1)
    w_in = jnp.concatenate([a_w_in[0][:, qcols], a_w_in[0][:, kmix:]], axis=1).astype(BF16)
    qg = row(jnp.tile(a_q_norm[0], 2) * (SCALE * LOG2_E))
    kg = row(jnp.tile(a_k_norm[0], 2))
    q, k, v, qm = _a_in_proj(x, w_in, m128, qg, kg, _axial_tables())
    mix = _gqa_attention(q, k, v)
    w_out = a_w_out[0].astype(BF16)
    routed = _post_attn(mix, qm, mkv, w_out[qcols], w_out[kmix:], x, row(ln1_g[0]), row(ln1_b[0]), router)
    x = _moe_ln(routed, 0, *experts, row(ln2_g[0]), row(ln2_b[0]))

    ptabs = _partial_tables()
    tabs = [tuple(_residue_order(t, d) for t in ptabs) for d in DILATIONS]
    qkv, qm = _b_in_proj(x, b_w_in[0].astype(BF16), tabs)
    outs, lses = zip(*[_band_attention(*qkv[g], d) for g, d in enumerate(DILATIONS)])
    w_out = b_w_out[0].astype(BF16)
    routed = _post_attn((outs, lses), qm, mkv, w_out[:B_WIDTH], w_out[B_WIDTH:], x, row(ln1_g[1]), row(ln1_b[1]),
                        router, expand)
    x = _moe_ln(routed, 1, *experts, row(ln2_g[1]), row(ln2_b[1]))
    return x.reshape(BATCH, SEQ, D_MODEL)
```

```python
import functools
import math

import numpy as np
import jax
import jax.numpy as jnp
from jax import lax
from jax.experimental import pallas as pl
from jax.experimental.pallas import tpu as pltpu

F32 = jnp.float32
BF16 = jnp.bfloat16

D_MODEL = 1024
BATCH = 8
SEQ = 2048
N_TOK = BATCH * SEQ
HEAD_DIM = 64
GRID_W = 64
N_MEM = 256
MEM_HEADS = 4
A_Q_HEADS = 12
A_KV_HEADS = 4
AXIAL_THETA = 10000.0
B_HEADS = 8
B_WIDTH = B_HEADS * HEAD_DIM
DILATIONS = (1, 4, 16)
BAND_RADIUS = 64
PARTIAL_ROT_DIMS = HEAD_DIM // 4
PARTIAL_THETA = 500000.0
N_EXPERTS = 16
N_EXPERT_GROUPS = 4
EXPERTS_PER_GROUP = 4
D_EXPERT = 512
DEPTH = 2
ALPHA = (2 * DEPTH) ** 0.25
NORM_EPS = 1e-6
NEG_BIG = -1e30
SCALE = HEAD_DIM ** -0.5
LOG2_E = math.log2(math.e)
_Q_HEAD_ORDER = tuple(6 * p + 3 * half + t for p in range(2) for t in range(3) for half in range(2))

LANES = 128
MEM_W = MEM_HEADS * HEAD_DIM
TOK_TILE = 512
TILES_PER_SEQ = SEQ // TOK_TILE
GQA_KC = 256
BAND_TQ = 128
BAND_TILES = 4

PAIRS = ((0, 1), (0, 2), (0, 3), (1, 2), (1, 3), (2, 3))
N_CLASSES = N_EXPERT_GROUPS * len(PAIRS)
CLASS_ROWS = 32
MOE_TM = 256
MOE_ROWS = N_TOK + N_CLASSES * MOE_TM
MOE_TILES = MOE_ROWS // MOE_TM
XA_W = D_MODEL + LANES
ROW_DMA_UNROLL = 8

VMEM_LIMIT = 56 * 1024 * 1024


def _params(sem, vmem=VMEM_LIMIT):
    return pltpu.CompilerParams(dimension_semantics=sem, vmem_limit_bytes=vmem)


def _dot(a, b):
    return jnp.dot(a, b, preferred_element_type=F32)


def _dot_nt(a, b):
    return lax.dot_general(a, b, (((1,), (1,)), ((), ())), preferred_element_type=F32)


def _split_bf16(x):
    hi = x.astype(BF16)
    lo = (x - hi.astype(F32)).astype(BF16)
    return hi, lo


def _layer_norm(z, g, b):
    mu = jnp.mean(z, axis=-1, keepdims=True)
    zc = z - mu
    var = jnp.mean(zc * zc, axis=-1, keepdims=True)
    return zc * lax.rsqrt(var + NORM_EPS) * g + b


def _rope(z, c, sa, sb, shift):
    return z * c + pltpu.roll(z, LANES - shift, 1) * sa + pltpu.roll(z, shift, 1) * sb


def _first_head(shape):
    return lax.broadcasted_iota(jnp.int32, shape, 1) < HEAD_DIM


def _pair_attention(q2, k2, v3, first, valid=None, base2=False):
    res = []
    for pick in (first, jnp.logical_not(first)):
        s = _dot_nt(jnp.where(pick, q2, jnp.zeros_like(q2)), k2)
        if valid is not None:
            s = jnp.where(valid, s, NEG_BIG)
        m = jnp.max(s, axis=1, keepdims=True)
        p = jnp.exp2(s - m) if base2 else jnp.exp(s - m)
        res.append((_dot(p.astype(BF16), v3), m))
    return res


def _pair_output(res, first):
    (ol_a, _), (ol_b, _) = res
    return jnp.where(first, ol_a[:, :LANES], ol_b[:, :LANES]) / jnp.where(first, ol_a[:, LANES:], ol_b[:, LANES:])


def _with_ones(v2):
    return jnp.concatenate([v2, jnp.ones_like(v2)], axis=1)


def _proj_kernel(x_ref, w_ref, o_ref):
    o_ref[...] = _dot(x_ref[...].astype(BF16), w_ref[...]).astype(o_ref.dtype)


def _proj(x, w, tm=TOK_TILE):
    n, k = x.shape
    width = w.shape[1]
    return pl.pallas_call(
        _proj_kernel,
        grid=(n // tm,),
        in_specs=[pl.BlockSpec((tm, k), lambda i: (i, 0)), pl.BlockSpec((k, width), lambda i: (0, 0))],
        out_specs=pl.BlockSpec((tm, width), lambda i: (i, 0)),
        out_shape=jax.ShapeDtypeStruct((n, width), BF16),
        compiler_params=_params(("parallel",)),
        name="proj",
    )(x, w)


def _a_in_kernel(x_ref, w_ref, m_ref, qg_ref, kg_ref, c_ref, sa_ref, sb_ref, q_ref, k_ref, v_ref, qm_ref):
    y = _dot(x_ref[...].astype(BF16), w_ref[...])
    m = m_ref[...]
    c, sa, sb = c_ref[...], sa_ref[...], sb_ref[...]

    def norm_rope(z, gain):
        hi, lo = _split_bf16(z * z)
        ms = (_dot(hi, m) + _dot(lo, m)) * (1.0 / HEAD_DIM)
        z = z * lax.rsqrt(ms + NORM_EPS) * gain
        return _rope(z, c, sa, sb, 16)

    nq = A_Q_HEADS * HEAD_DIM // LANES
    nk = A_KV_HEADS * HEAD_DIM // LANES
    for ch in range(nq):
        q_ref[:, ch * LANES:(ch + 1) * LANES] = norm_rope(y[:, ch * LANES:(ch + 1) * LANES], qg_ref[...]).astype(BF16)
    off = nq * LANES
    for ch in range(nk):
        z = y[:, off + ch * LANES: off + (ch + 1) * LANES]
        k_ref[:, ch * LANES:(ch + 1) * LANES] = norm_rope(z, kg_ref[...]).astype(BF16)
    off += nk * LANES
    ones = jnp.ones((y.shape[0], LANES), BF16)
    for ch in range(nk):
        v_ref[:, 2 * ch * LANES:(2 * ch + 1) * LANES] = y[:, off + ch * LANES: off + (ch + 1) * LANES].astype(BF16)
        v_ref[:, (2 * ch + 1) * LANES:(2 * ch + 2) * LANES] = ones
    off += nk * LANES
    qm_ref[...] = (y[:, off:off + MEM_W] * SCALE).astype(BF16)


def _a_in_proj(x, w, m128, qg, kg, tabs, tm=TOK_TILE):
    c, sa, sb = tabs
    nblk = SEQ // tm
    a_in = w.shape[1]
    tab_spec = pl.BlockSpec((tm, LANES), lambda i: (i % nblk, 0))
    row = lambda width: pl.BlockSpec((tm, width), lambda i: (i, 0))
    const = lambda shape: pl.BlockSpec(shape, lambda i: (0, 0))
    qw, kw = A_Q_HEADS * HEAD_DIM, A_KV_HEADS * HEAD_DIM
    return pl.pallas_call(
        _a_in_kernel,
        grid=(N_TOK // tm,),
        in_specs=[row(D_MODEL), const((D_MODEL, a_in)), const((LANES, LANES)), const((1, LANES)), const((1, LANES)),
                  tab_spec, tab_spec, tab_spec],
        out_specs=[row(qw), row(kw), row(2 * kw), row(MEM_W)],
        out_shape=[jax.ShapeDtypeStruct((N_TOK, qw), BF16), jax.ShapeDtypeStruct((N_TOK, kw), BF16),
                   jax.ShapeDtypeStruct((N_TOK, 2 * kw), BF16), jax.ShapeDtypeStruct((N_TOK, MEM_W), BF16)],
        compiler_params=_params(("parallel",)),
        name="a_in_proj",
    )(x, w, m128, qg, kg, c, sa, sb)


def _b_in_kernel(*refs):
    ng = len(DILATIONS)
    nx = D_MODEL // LANES
    x_refs, w_ref, refs = refs[:nx], refs[nx], refs[nx + 1:]
    tabs, outs, qm_ref = refs[:3 * ng], refs[3 * ng:6 * ng], refs[6 * ng]
    xb1 = None
    for g, d in enumerate(DILATIONS):
        n = TOK_TILE // d
        if d == 1:
            cols = [r[...] for r in x_refs]
        else:
            cols = [jnp.concatenate([r[pl.ds(k, n, stride=d), :] for k in range(d)], axis=0) for r in x_refs]
        xb = jnp.concatenate(cols, axis=1).astype(BF16)
        xb1 = xb if d == 1 else xb1
        c, sa, sb = (t[...].reshape(TOK_TILE, LANES) for t in tabs[3 * g:3 * g + 3])
        q_ref, k_ref, v_ref = outs[3 * g:3 * g + 3]
        yq = _dot(xb, w_ref[:, g * B_WIDTH:(g + 1) * B_WIDTH]) * SCALE
        yk = _dot(xb, w_ref[:, (ng + g) * B_WIDTH:(ng + g + 1) * B_WIDTH])
        yv = _dot(xb, w_ref[:, (2 * ng + g) * B_WIDTH:(2 * ng + g + 1) * B_WIDTH])
        for ch in range(B_WIDTH // LANES):
            sl = slice(ch * LANES, (ch + 1) * LANES)
            q_ref[0, :, :, sl] = _rope(yq[:, sl], c, sa, sb, PARTIAL_ROT_DIMS // 2).reshape(d, n, LANES).astype(BF16)
            k_ref[0, :, :, sl] = _rope(yk[:, sl], c, sa, sb, PARTIAL_ROT_DIMS // 2).reshape(d, n, LANES).astype(BF16)
        v_ref[0] = yv.reshape(d, n, B_WIDTH).astype(BF16)
    qm = _dot(xb1, w_ref[:, 3 * ng * B_WIDTH:3 * ng * B_WIDTH + MEM_W])
    qm_ref[...] = (qm * SCALE).astype(BF16)


def _b_in_proj(x, w, tabs):
    nx = D_MODEL // LANES
    in_specs = [pl.BlockSpec((TOK_TILE, LANES), functools.partial(lambda i, c: (i, c), c=c)) for c in range(nx)]
    in_specs.append(pl.BlockSpec(w.shape, lambda i: (0, 0)))
    out_specs, out_shape = [], []
    for d in DILATIONS:
        n = TOK_TILE // d
        in_specs += [pl.BlockSpec((d, n, LANES), lambda i: (0, i % TILES_PER_SEQ, 0))] * 3
        out_specs += [pl.BlockSpec((1, d, n, B_WIDTH), lambda i: (i // TILES_PER_SEQ, 0, i % TILES_PER_SEQ, 0))] * 3
        out_shape += [jax.ShapeDtypeStruct((BATCH, d, SEQ // d, B_WIDTH), BF16)] * 3
    out_specs.append(pl.BlockSpec((TOK_TILE, MEM_W), lambda i: (i, 0)))
    out_shape.append(jax.ShapeDtypeStruct((N_TOK, MEM_W), BF16))
    res = pl.pallas_call(
        _b_in_kernel,
        grid=(N_TOK // TOK_TILE,),
        in_specs=in_specs,
        out_specs=out_specs,
        out_shape=out_shape,
        compiler_params=_params(("parallel",)),
        name="b_in_proj",
    )(*([x] * nx), w, *[t for group in tabs for t in group])
    return [res[3 * g:3 * g + 3] for g in range(len(DILATIONS))], res[-1]


def _gqa_kernel(q_ref, k_ref, v_ref, o_ref):
    tq = q_ref.shape[0]
    ntile = q_ref.shape[1] // LANES
    q = jnp.concatenate([q_ref[:, t * LANES:(t + 1) * LANES] for t in range(ntile)], axis=0)
    first = _first_head(q.shape)
    res = []
    for pick in (first, jnp.logical_not(first)):
        qh = jnp.where(pick, q, jnp.zeros_like(q))
        m = jnp.full((q.shape[0], 1), -jnp.inf, F32)
        acc = jnp.zeros((q.shape[0], 2 * LANES), F32)
        for c in range(SEQ // GQA_KC):
            keys = slice(c * GQA_KC, (c + 1) * GQA_KC)
            s = _dot_nt(qh, k_ref[keys, :])
            m_new = jnp.maximum(m, jnp.max(s, axis=1, keepdims=True))
            p = jnp.exp2(s - m_new).astype(BF16)
            acc = acc * jnp.exp2(m - m_new) + _dot(p, v_ref[keys, :])
            m = m_new
        res.append((acc, m))
    o = _pair_output(res, first)
    for t in range(ntile):
        o_ref[:, t * LANES:(t + 1) * LANES] = o[t * tq:(t + 1) * tq].astype(BF16)


def _gqa_attention(q, k, v, tq=512):
    nq = SEQ // tq
    qw = q.shape[1] // 2
    kw = k.shape[1] // 2
    return pl.pallas_call(
        _gqa_kernel,
        grid=(BATCH, 2, nq),
        in_specs=[pl.BlockSpec((tq, qw), lambda b, p, i: (b * nq + i, p)),
                  pl.BlockSpec((SEQ, kw), lambda b, p, i: (b, p)),
                  pl.BlockSpec((SEQ, 2 * kw), lambda b, p, i: (b, p))],
        out_specs=pl.BlockSpec((tq, qw), lambda b, p, i: (b * nq + i, p)),
        out_shape=jax.ShapeDtypeStruct(q.shape, BF16),
        compiler_params=_params(("parallel", "parallel", "parallel")),
        name="gqa_attention",
    )(q, k, v)


def _band_kernel(q_ref, k_ref, v_ref, o_ref, lse_ref, *, length, seg, win):
    tq = BAND_TQ
    first = _first_head((tq, LANES))
    lane = lax.broadcasted_iota(jnp.int32, (tq, LANES), 1)
    row = lax.broadcasted_iota(jnp.int32, (tq, win), 0)
    col = lax.broadcasted_iota(jnp.int32, (tq, win), 1)
    for u in range(BAND_TILES):
        tile = pl.program_id(1) * BAND_TILES + u
        start = pl.multiple_of(jnp.clip(tile * tq - BAND_RADIUS, 0, length - win), BAND_RADIUS)
        qpos, kpos = tile * tq + row, start + col
        valid = jnp.abs(qpos - kpos) <= BAND_RADIUS
        if seg < length:
            valid = jnp.logical_and(valid, qpos // seg == kpos // seg)
        rows = slice(u * tq, (u + 1) * tq)
        lse = jnp.zeros((tq, LANES), F32)
        for j in range(B_WIDTH // LANES):
            sl = slice(j * LANES, (j + 1) * LANES)
            res = _pair_attention(q_ref[0, rows, sl], k_ref[0, pl.ds(start, win), sl],
                                  _with_ones(v_ref[0, pl.ds(start, win), sl]), first, valid)
            o_ref[0, rows, sl] = _pair_output(res, first).astype(BF16)
            for half, (ol, m) in enumerate(res):
                lse = jnp.where(lane == 2 * j + half, m + jnp.log(ol[:, LANES:]), lse)
        lse_ref[0, rows, :] = lse


def _band_attention(q, k, v, d):
    seg = SEQ // d
    step_rows = BAND_TILES * BAND_TQ
    length = max(seg, step_rows)
    win = BAND_TQ + 2 * BAND_RADIUS
    nblk = N_TOK // length
    q, k, v = (t.reshape(nblk, length, B_WIDTH) for t in (q, k, v))
    seq_spec = pl.BlockSpec((1, length, B_WIDTH), lambda s, i: (s, 0, 0))
    o, lse = pl.pallas_call(
        functools.partial(_band_kernel, length=length, seg=seg, win=win),
        grid=(nblk, length // step_rows),
        in_specs=[pl.BlockSpec((1, step_rows, B_WIDTH), lambda s, i: (s, i, 0)), seq_spec, seq_spec],
        out_specs=[pl.BlockSpec((1, step_rows, B_WIDTH), lambda s, i: (s, i, 0)),
                   pl.BlockSpec((1, step_rows, LANES), lambda s, i: (s, i, 0))],
        out_shape=[jax.ShapeDtypeStruct((nblk, length, B_WIDTH), BF16),
                   jax.ShapeDtypeStruct((nblk, length, LANES), F32)],
        compiler_params=_params(("parallel", "parallel")),
        name=f"band_attention_d{d}",
    )(q, k, v)
    return o.reshape(BATCH, d, seg, B_WIDTH), lse.reshape(BATCH, d, seg, LANES)


def _route(x, wh_ref, wl_ref, rb_ref, tri_ref, xa_ref, cls_ref, rank_ref, cnt_ref, carry_ref):
    tm = x.shape[0]
    xh, xl = _split_bf16(x)
    logits = _dot_nt(wh_ref[...], xh) + _dot_nt(wh_ref[...], xl) + _dot_nt(wl_ref[...], xh)
    aff = 1.0 / (1.0 + jnp.exp(-logits))
    sel = aff + rb_ref[...]
    s = [sel[e:e + 1, :] for e in range(N_EXPERTS)]
    a = [aff[e:e + 1, :] for e in range(N_EXPERTS)]

    def top2_sum(v):
        hi01, lo01 = jnp.maximum(v[0], v[1]), jnp.minimum(v[0], v[1])
        hi23, lo23 = jnp.maximum(v[2], v[3]), jnp.minimum(v[2], v[3])
        return jnp.maximum(hi01, hi23) + jnp.maximum(jnp.minimum(hi01, hi23), jnp.maximum(lo01, lo23))

    gscore = [top2_sum(s[EXPERTS_PER_GROUP * g:EXPERTS_PER_GROUP * (g + 1)]) for g in range(N_EXPERT_GROUPS)]
    best = jnp.zeros((1, tm), jnp.int32)
    best_score = gscore[0]
    for g in range(1, N_EXPERT_GROUPS):
        better = gscore[g] > best_score
        best = jnp.where(better, g, best)
        best_score = jnp.where(better, gscore[g], best_score)

    def pick(rows, j):
        out = rows[j]
        for g in range(1, N_EXPERT_GROUPS):
            out = jnp.where(best == g, rows[EXPERTS_PER_GROUP * g + j], out)
        return out

    t = [pick(s, j) for j in range(EXPERTS_PER_GROUP)]
    w = [pick(a, j) for j in range(EXPERTS_PER_GROUP)]

    def first_max(v):
        mx = jnp.maximum(jnp.maximum(v[0], v[1]), jnp.maximum(v[2], v[3]))
        idx = jnp.full((1, tm), EXPERTS_PER_GROUP - 1, jnp.int32)
        for j in range(EXPERTS_PER_GROUP - 2, -1, -1):
            idx = jnp.where(v[j] == mx, j, idx)
        return idx

    i1 = first_max(t)
    i2 = first_max([jnp.where(i1 == j, -jnp.inf, t[j]) for j in range(EXPERTS_PER_GROUP)])
    lo, hi = jnp.minimum(i1, i2), jnp.maximum(i1, i2)

    def take(rows, idx):
        out = rows[0]
        for j in range(1, EXPERTS_PER_GROUP):
            out = jnp.where(idx == j, rows[j], out)
        return out

    w_lo, w_hi = take(w, lo), take(w, hi)
    den = w_lo + w_hi
    pair = jnp.where(lo == 0, hi - 1, jnp.where(lo == 1, hi + 1, len(PAIRS) - 1))
    cls = best * len(PAIRS) + pair

    onehot = (lax.broadcasted_iota(jnp.int32, (CLASS_ROWS, tm), 0) == cls).astype(F32)
    before = _dot(onehot.astype(BF16), tri_ref[...]) + carry_ref[:, 0:1]
    rank = jnp.sum(onehot * before, axis=0, keepdims=True)
    carry_ref[...] = carry_ref[...] + jnp.sum(onehot, axis=1, keepdims=True)

    cls_ref[0] = cls
    rank_ref[0] = rank.astype(jnp.int32)
    cnt_ref[...] = carry_ref[...]
    gates = jnp.concatenate([w_lo / den, w_hi / den, jnp.zeros((LANES - 2, tm), F32)], axis=0)
    xa_ref[:, :D_MODEL] = x
    xa_ref[:, D_MODEL:] = gates.T


def _post_attn_kernel(*refs, merge):
    if merge:
        o_refs, l_refs, e_ref, refs = refs[:3], refs[3:6], refs[6], refs[7:]
    else:
        mix_ref, refs = refs[0], refs[1:]
    (qm_ref, mk_ref, mv_ref, w1_ref, w2_ref, x_ref, g_ref, b_ref, wh_ref, wl_ref, rb_ref, tri_ref,
     xa_ref, cls_ref, rank_ref, cnt_ref, carry_ref) = refs[:17]

    @pl.when(pl.program_id(0) == 0)
    def _():
        carry_ref[...] = jnp.zeros_like(carry_ref)

    if merge:
        ot_ref, lt_ref = refs[17], refs[18]
        nch = B_WIDTH // LANES
        for g, d in enumerate(DILATIONS):
            n = TOK_TILE // d
            for r in range(d):
                rows = pl.ds(r, n, stride=d) if d > 1 else slice(None)
                og = o_refs[g][0, r].astype(F32)
                for ch in range(nch):
                    ot_ref[g, ch, rows, :] = og[:, ch * LANES:(ch + 1) * LANES]
                lt_ref[g, rows, :] = l_refs[g][0, r]
        lses = [lt_ref[g] for g in range(len(DILATIONS))]
        m = functools.reduce(jnp.maximum, lses)
        es = [jnp.exp(l - m) for l in lses]
        inv = 1.0 / functools.reduce(jnp.add, es)
        expand = e_ref[...]

        def widen(w):
            hi, lo = _split_bf16(w)
            return _dot(hi, expand) + _dot(lo, expand)

        ws = [widen(e * inv) for e in es]
        mix = jnp.concatenate(
            [functools.reduce(jnp.add, [w[:, ch * LANES:(ch + 1) * LANES] * ot_ref[g, ch] for g, w in enumerate(ws)])
             for ch in range(nch)], axis=1).astype(BF16)
    else:
        mix = mix_ref[...]

    first = _first_head((TOK_TILE, LANES))
    memo = []
    for j in range(MEM_W // LANES):
        sl = slice(j * LANES, (j + 1) * LANES)
        memo.append(_pair_output(_pair_attention(qm_ref[:, sl], mk_ref[:, sl], _with_ones(mv_ref[:, sl]), first), first))
    memo = jnp.concatenate(memo, axis=1).astype(BF16)

    attn = _dot(mix, w1_ref[...]) + _dot(memo, w2_ref[...])
    x1 = _layer_norm(ALPHA * x_ref[...] + attn, g_ref[...], b_ref[...])
    _route(x1, wh_ref, wl_ref, rb_ref, tri_ref, xa_ref, cls_ref, rank_ref, cnt_ref, carry_ref)


def _post_attn(mix, qm, mkv, w1, w2, x, g, b, router, expand=None):
    merge = expand is not None
    wh, wl, rb, tri = router
    nblk = N_TOK // TOK_TILE
    row = lambda width: pl.BlockSpec((TOK_TILE, width), lambda i: (i, 0))
    const = lambda shape: pl.BlockSpec(shape, lambda i: (0,) * len(shape))
    row3 = pl.BlockSpec((1, 1, TOK_TILE), lambda i: (i, 0, 0))
    scratch = [pltpu.VMEM((CLASS_ROWS, LANES), F32)]
    if merge:
        outs, lses = mix
        resid = lambda d, width: pl.BlockSpec((1, d, TOK_TILE // d, width),
                                              lambda i: (i // TILES_PER_SEQ, 0, i % TILES_PER_SEQ, 0))
        lead_specs = ([resid(d, B_WIDTH) for d in DILATIONS] + [resid(d, LANES) for d in DILATIONS]
                      + [const(expand.shape)])
        lead = [*outs, *lses, expand]
        scratch += [pltpu.VMEM((len(DILATIONS), B_WIDTH // LANES, TOK_TILE, LANES), F32),
                    pltpu.VMEM((len(DILATIONS), TOK_TILE, LANES), F32)]
    else:
        lead_specs, lead = [row(mix.shape[1])], [mix]
    xa, cls, rank, cnt = pl.pallas_call(
        functools.partial(_post_attn_kernel, merge=merge),
        grid=(nblk,),
        in_specs=lead_specs + [row(MEM_W),
                               pl.BlockSpec((N_MEM, MEM_W), lambda i: (i // TILES_PER_SEQ, 0)),
                               pl.BlockSpec((N_MEM, MEM_W), lambda i: (i // TILES_PER_SEQ, 1)),
                               const(w1.shape), const(w2.shape), row(D_MODEL), const((1, D_MODEL)), const((1, D_MODEL)),
                               const(wh.shape), const(wl.shape), const(rb.shape), const(tri.shape)],
        out_specs=[row(XA_W), row3, row3, const((CLASS_ROWS, LANES))],
        out_shape=[jax.ShapeDtypeStruct((N_TOK, XA_W), F32), jax.ShapeDtypeStruct((nblk, 1, TOK_TILE), jnp.int32),
                   jax.ShapeDtypeStruct((nblk, 1, TOK_TILE), jnp.int32), jax.ShapeDtypeStruct((CLASS_ROWS, LANES), F32)],
        scratch_shapes=scratch,
        compiler_params=_params(("arbitrary",)),
        name="post_attn_merge" if merge else "post_attn",
    )(*lead, qm, mkv, mkv, w1, w2, x, g, b, wh, wl, rb, tri)
    return xa, cls.reshape(N_TOK), rank.reshape(N_TOK), cnt[:N_CLASSES, 0].astype(jnp.int32)


def _dispatch_kernel(pos_ref, xa_ref, init_ref, xs_ref, sem, *, chunk):
    del init_ref
    base = pl.program_id(0) * chunk

    def issue(blk, carry):
        for u in range(ROW_DMA_UNROLL):
            r = blk * ROW_DMA_UNROLL + u
            pltpu.make_async_copy(xa_ref.at[pl.ds(r, 1)], xs_ref.at[pl.ds(pos_ref[base + r], 1)], sem).start(
                priority=u % 2)
        return carry

    def retire(r, carry):
        pltpu.make_async_copy(xa_ref.at[pl.ds(0, 1)], xs_ref.at[pl.ds(0, 1)], sem).wait()
        return carry

    lax.fori_loop(0, chunk // ROW_DMA_UNROLL, issue, 0)
    lax.fori_loop(0, chunk, retire, 0, unroll=8)


def _dispatch(pos, xa, chunk=512):
    init = jnp.zeros((MOE_ROWS, XA_W), F32)
    return pl.pallas_call(
        functools.partial(_dispatch_kernel, chunk=chunk),
        grid_spec=pltpu.PrefetchScalarGridSpec(
            num_scalar_prefetch=1,
            grid=(N_TOK // chunk,),
            in_specs=[pl.BlockSpec((chunk, XA_W), lambda i, p: (i, 0)), pl.BlockSpec(memory_space=pl.ANY)],
            out_specs=pl.BlockSpec(memory_space=pl.ANY),
            scratch_shapes=[pltpu.SemaphoreType.DMA(())],
        ),
        out_shape=jax.ShapeDtypeStruct((MOE_ROWS, XA_W), F32),
        input_output_aliases={2: 0},
        compiler_params=_params(("arbitrary",)),
        name="moe_dispatch",
    )(pos, xa, init)


def _expert_kernel(slot1_ref, slot2_ref, used_ref, first_ref, grp_ref, next_ref, src_ref, xa_hbm, wg_hbm, wu_hbm, wd_hbm,
                   ys_ref, xbuf, xb_s, gate_s, h_s, stage_g, stage_u, stage_d, wg_s, wu_s, wd_s, sem, gsem, *, layer):
    i = pl.program_id(0)
    last = pl.num_programs(0) - 1
    slot = i % 2
    streams = ((wg_hbm, stage_g, wg_s), (wu_hbm, stage_u, wu_s), (wd_hbm, stage_d, wd_s))

    def group_copy(group, k):
        first_expert = layer * N_EXPERTS + group * EXPERTS_PER_GROUP
        hbm, stage, _ = streams[k]
        return pltpu.make_async_copy(hbm.at[pl.ds(first_expert, EXPERTS_PER_GROUP)], stage, sem.at[k])

    def gather(tile, into, lo=0, hi=MOE_TM):
        for r in range(lo, hi):
            row = src_ref[tile * MOE_TM + r]
            pltpu.make_async_copy(xa_hbm.at[pl.ds(row, 1)], xbuf.at[into, pl.ds(r, 1)], gsem.at[into]).start()

    def gather_wait(into):
        def retire(r, carry):
            pltpu.make_async_copy(xa_hbm.at[pl.ds(0, 1)], xbuf.at[into, pl.ds(0, 1)], gsem.at[into]).wait()
            return carry

        lax.fori_loop(0, MOE_TM, retire, 0, unroll=8)

    @pl.when(i == 0)
    def _():
        for k in range(len(streams)):
            group_copy(grp_ref[0], k).start(priority=1)
        gather(0, 0)

    @pl.when(first_ref[i] != 0)
    def _():
        for k, (_, stage, dst) in enumerate(streams):
            group_copy(grp_ref[i], k).wait()
            for e in range(EXPERTS_PER_GROUP):
                dst[e] = stage[e].astype(BF16)

        @pl.when(next_ref[i] >= 0)
        def _():
            for k in range(len(streams)):
                group_copy(next_ref[i], k).start(priority=1)

    used = used_ref[i] != 0
    nxt = jnp.minimum(i + 1, last)
    bounds = [MOE_TM * k // 6 for k in range(7)]

    @pl.when(used)
    def _():
        gather_wait(slot)
        xb_s[...] = xbuf[slot, :, :D_MODEL].astype(BF16)

    for e, slot_ref in enumerate((slot1_ref, slot2_ref)):
        @pl.when(used)
        def _():
            gather(nxt, 1 - slot, bounds[3 * e], bounds[3 * e + 1])
            gate_s[...] = _dot(xb_s[...], wg_s[slot_ref[i]])

        @pl.when(used)
        def _():
            gather(nxt, 1 - slot, bounds[3 * e + 1], bounds[3 * e + 2])
            gate = gate_s[...]
            h_s[...] = (gate * (1.0 / (1.0 + jnp.exp(-gate))) * _dot(xb_s[...], wu_s[slot_ref[i]])).astype(BF16)

        @pl.when(used)
        def _():
            gather(nxt, 1 - slot, bounds[3 * e + 2], bounds[3 * e + 3])
            y = xbuf[slot, :, D_MODEL + e:D_MODEL + e + 1] * _dot(h_s[...], wd_s[slot_ref[i]])
            if e == 0:
                ys_ref[...] = y
            else:
                ys_ref[...] += y

    @pl.when(jnp.logical_and(used, i == last))
    def _():
        gather_wait(1 - slot)

    @pl.when(used_ref[i] == 0)
    def _():
        ys_ref[...] = jnp.zeros_like(ys_ref)

        @pl.when(used_ref[jnp.maximum(i - 1, 0)] != 0)
        def _():
            gather_wait(slot)


def _experts(tile_meta, src, xa, layer, wg, wu, wd):
    nmeta = len(tile_meta) + 1
    up_shape = (EXPERTS_PER_GROUP, D_MODEL, D_EXPERT)
    down_shape = (EXPERTS_PER_GROUP, D_EXPERT, D_MODEL)
    anywhere = pl.BlockSpec(memory_space=pl.ANY)
    return pl.pallas_call(
        functools.partial(_expert_kernel, layer=layer),
        grid_spec=pltpu.PrefetchScalarGridSpec(
            num_scalar_prefetch=nmeta,
            grid=(MOE_TILES,),
            in_specs=[anywhere, anywhere, anywhere, anywhere],
            out_specs=pl.BlockSpec((MOE_TM, D_MODEL), lambda i, *_: (i, 0)),
            scratch_shapes=[pltpu.VMEM((2, MOE_TM, XA_W), F32), pltpu.VMEM((MOE_TM, D_MODEL), BF16),
                            pltpu.VMEM((MOE_TM, D_EXPERT), F32), pltpu.VMEM((MOE_TM, D_EXPERT), BF16),
                            pltpu.VMEM(up_shape, F32), pltpu.VMEM(up_shape, F32), pltpu.VMEM(down_shape, F32),
                            pltpu.VMEM(up_shape, BF16), pltpu.VMEM(up_shape, BF16), pltpu.VMEM(down_shape, BF16),
                            pltpu.SemaphoreType.DMA((3,)), pltpu.SemaphoreType.DMA((2,))],
        ),
        out_shape=jax.ShapeDtypeStruct((MOE_ROWS, D_MODEL), F32),
        compiler_params=_params(("arbitrary",)),
        name="moe_experts",
    )(*tile_meta, src, xa, wg, wu, wd)


def _combine_kernel(pos_ref, ys_ref, x_ref, g_ref, b_ref, o_ref, buf, sem, *, tc):
    i = pl.program_id(0)
    slot = i % 2

    def gather(step, into):
        def issue(blk, carry):
            for u in range(ROW_DMA_UNROLL):
                r = blk * ROW_DMA_UNROLL + u
                row = pos_ref[step * tc + r]
                pltpu.make_async_copy(ys_ref.at[pl.ds(row, 1)], buf.at[into, pl.ds(r, 1)], sem.at[into]).start(
                    priority=u % 2)
            return carry

        lax.fori_loop(0, tc // ROW_DMA_UNROLL, issue, 0)

    @pl.when(i == 0)
    def _():
        gather(0, 0)

    @pl.when(i + 1 < pl.num_programs(0))
    def _():
        gather(i + 1, 1 - slot)

    def retire(r, carry):
        pltpu.make_async_copy(ys_ref.at[pl.ds(0, 1)], buf.at[slot, pl.ds(0, 1)], sem.at[slot]).wait()
        return carry

    lax.fori_loop(0, tc, retire, 0, unroll=8)
    o_ref[...] = _layer_norm(ALPHA * x_ref[...] + buf[slot], g_ref[...], b_ref[...])


def _combine_ln(pos, ys, xa, g, b, tc=256):
    return pl.pallas_call(
        functools.partial(_combine_kernel, tc=tc),
        grid_spec=pltpu.PrefetchScalarGridSpec(
            num_scalar_prefetch=1,
            grid=(N_TOK // tc,),
            in_specs=[pl.BlockSpec(memory_space=pl.ANY),
                      pl.BlockSpec((tc, D_MODEL), lambda i, p: (i, 0)),
                      pl.BlockSpec((1, D_MODEL), lambda i, p: (0, 0)),
                      pl.BlockSpec((1, D_MODEL), lambda i, p: (0, 0))],
            out_specs=pl.BlockSpec((tc, D_MODEL), lambda i, p: (i, 0)),
            scratch_shapes=[pltpu.VMEM((2, tc, D_MODEL), F32), pltpu.SemaphoreType.DMA((2,))],
        ),
        out_shape=jax.ShapeDtypeStruct((N_TOK, D_MODEL), F32),
        compiler_params=_params(("arbitrary",)),
        name="moe_combine_ln",
    )(pos, ys, xa, g, b)


def _moe_ln(routed, layer, wg, wu, wd, g, b):
    xa, cls, rank, counts = routed
    padded = (counts + MOE_TM - 1) // MOE_TM * MOE_TM
    ends = jnp.cumsum(padded)
    starts = ends - padded
    pos = starts[cls] + rank
    tile_start = jnp.arange(MOE_TILES, dtype=jnp.int32) * MOE_TM
    tile_used = (tile_start < ends[-1]).astype(jnp.int32)
    last_cls = jnp.max(jnp.where(counts > 0, jnp.arange(N_CLASSES), 0))
    tile_cls = jnp.minimum(jnp.sum(tile_start[:, None] >= ends[None, :], axis=1), last_cls)
    tile_cls = tile_cls.astype(jnp.int32)
    tile_grp = tile_cls // len(PAIRS)
    prev_grp = jnp.concatenate([jnp.full((1,), -1, jnp.int32), tile_grp[:-1]])
    tile_first = tile_used * (tile_grp != prev_grp).astype(jnp.int32)
    has = jnp.sum(counts.reshape(N_EXPERT_GROUPS, len(PAIRS)), axis=1) > 0
    gid = jnp.arange(N_EXPERT_GROUPS, dtype=jnp.int32)
    later = jnp.where(jnp.logical_and(has[None, :], gid[None, :] > gid[:, None]), gid[None, :], N_EXPERT_GROUPS)
    next_grp = jnp.min(later, axis=1)
    next_grp = jnp.where(next_grp == N_EXPERT_GROUPS, -1, next_grp)
    pair_slots = jnp.asarray(np.array(PAIRS, np.int32))[tile_cls % len(PAIRS)]
    tile_meta = (pair_slots[:, 0], pair_slots[:, 1], tile_used, tile_first, tile_grp, next_grp[tile_grp])
    src = jnp.zeros((MOE_ROWS,), jnp.int32).at[pos].set(jnp.arange(N_TOK, dtype=jnp.int32), unique_indices=True)
    ys = _experts(tile_meta, src, xa, layer, wg, wu, wd)
    return _combine_ln(pos, ys, xa, g, b)


def _tile_heads(t64):
    return jnp.concatenate([t64, t64], axis=-1)


def _axial_tables():
    rows = SEQ // GRID_W
    row = jnp.repeat(jnp.arange(rows), GRID_W).astype(F32)
    col = jnp.tile(jnp.arange(GRID_W), rows).astype(F32)
    half = HEAD_DIM // 2
    inv = 1.0 / (AXIAL_THETA ** (jnp.arange(0, half, 2, dtype=F32) / half))
    ar, ac = row[:, None] * inv, col[:, None] * inv
    z = jnp.zeros_like(ar)
    c = jnp.concatenate([jnp.cos(ar), jnp.cos(ar), jnp.cos(ac), jnp.cos(ac)], -1)
    sa = jnp.concatenate([-jnp.sin(ar), z, -jnp.sin(ac), z], -1)
    sb = jnp.concatenate([z, jnp.sin(ar), z, jnp.sin(ac)], -1)
    return tuple(_tile_heads(t) for t in (c, sa, sb))


def _partial_tables():
    n = PARTIAL_ROT_DIMS
    pos = jnp.arange(SEQ, dtype=F32)
    ang = pos[:, None] * (1.0 / (PARTIAL_THETA ** (jnp.arange(0, n, 2, dtype=F32) / n)))
    z = jnp.zeros_like(ang)
    rest = HEAD_DIM - n
    c = jnp.concatenate([jnp.cos(ang), jnp.cos(ang), jnp.ones((SEQ, rest), F32)], -1)
    sa = jnp.concatenate([-jnp.sin(ang), z, jnp.zeros((SEQ, rest), F32)], -1)
    sb = jnp.concatenate([z, jnp.sin(ang), jnp.zeros((SEQ, rest), F32)], -1)
    return tuple(_tile_heads(t) for t in (c, sa, sb))


def _residue_order(tab, d):
    return tab.reshape(SEQ // d, d, LANES).transpose(1, 0, 2)


def kernel(x, mem, w_mem_kv, router_w, router_b, a_w_in, a_w_out, a_q_norm, a_k_norm, b_w_in, b_w_out,
           ln1_g, ln1_b, ln2_g, ln2_b, w_gate, w_up, w_down):
    x = x.reshape(N_TOK, D_MODEL)
    row = lambda v: v.reshape(1, -1)

    lane = np.arange(LANES)
    m128 = jnp.asarray((lane[:, None] // HEAD_DIM == lane[None, :] // HEAD_DIM), BF16)
    expand = jnp.asarray(lane[:, None] == np.arange(B_WIDTH)[None, :] // HEAD_DIM, BF16)
    tri = jnp.asarray(np.triu(np.ones((TOK_TILE, TOK_TILE), np.float32), 1), BF16)
    rw_hi = router_w.T.astype(BF16)
    rw_lo = (router_w.T - rw_hi.astype(F32)).astype(BF16)
    router = (rw_hi, rw_lo, router_b.reshape(N_EXPERTS, 1), tri)

    mkv = _proj(mem.reshape(BATCH * N_MEM, D_MODEL), w_mem_kv.astype(BF16))
    experts = tuple(w.reshape(DEPTH * N_EXPERTS, *w.shape[2:]) for w in (w_gate, w_up, w_down))

    kmix = A_Q_HEADS * HEAD_DIM
    qcols = (np.asarray(_Q_HEAD_ORDER)[:, None] * HEAD_DIM + np.arange(HEAD_DIM)[None, :]).reshape(-1)
    w_in = jnp.concatenate([a_w_in[0][:, qcols], a_w_in[0][:, kmix:]], axis=1).astype(BF16)
    qg = row(jnp.tile(a_q_norm[0], 2) * (SCALE * LOG2_E))
    kg = row(jnp.tile(a_k_norm[0], 2))
    q, k, v, qm = _a_in_proj(x, w_in, m128, qg, kg, _axial_tables())
    mix = _gqa_attention(q, k, v)
    w_out = a_w_out[0].astype(BF16)
    routed = _post_attn(mix, qm, mkv, w_out[qcols], w_out[kmix:], x, row(ln1_g[0]), row(ln1_b[0]), router)
    x = _moe_ln(routed, 0, *experts, row(ln2_g[0]), row(ln2_b[0]))

    ptabs = _partial_tables()
    tabs = [tuple(_residue_order(t, d) for t in ptabs) for d in DILATIONS]
    qkv, qm = _b_in_proj(x, b_w_in[0].astype(BF16), tabs)
    outs, lses = zip(*[_band_attention(*qkv[g], d) for g, d in enumerate(DILATIONS)])
    w_out = b_w_out[0].astype(BF16)
    routed = _post_attn((outs, lses), qm, mkv, w_out[:B_WIDTH], w_out[B_WIDTH:], x, row(ln1_g[1]), row(ln1_b[1]),
                        router, expand)
    x = _moe_ln(routed, 1, *experts, row(ln2_g[1]), row(ln2_b[1]))
    return x.reshape(BATCH, SEQ, D_MODEL)
```

```python
import functools
import math

import numpy as np
import jax
import jax.numpy as jnp
from jax import lax
from jax.experimental import pallas as pl
from jax.experimental.pallas import tpu as pltpu

F32 = jnp.float32
BF16 = jnp.bfloat16

D_MODEL = 1024
BATCH = 8
SEQ = 2048
N_TOK = BATCH * SEQ
HEAD_DIM = 64
GRID_W = 64
N_MEM = 256
MEM_HEADS = 4
A_Q_HEADS = 12
A_KV_HEADS = 4
AXIAL_THETA = 10000.0
B_HEADS = 8
B_WIDTH = B_HEADS * HEAD_DIM
DILATIONS = (1, 4, 16)
BAND_RADIUS = 64
PARTIAL_ROT_DIMS = HEAD_DIM // 4
PARTIAL_THETA = 500000.0
N_EXPERTS = 16
N_EXPERT_GROUPS = 4
EXPERTS_PER_GROUP = 4
D_EXPERT = 512
DEPTH = 2
ALPHA = (2 * DEPTH) ** 0.25
NORM_EPS = 1e-6
NEG_BIG = -1e30
SCALE = HEAD_DIM ** -0.5
LOG2_E = math.log2(math.e)
_Q_HEAD_ORDER = tuple(6 * p + 3 * half + t for p in range(2) for t in range(3) for half in range(2))

LANES = 128
MEM_W = MEM_HEADS * HEAD_DIM
TOK_TILE = 512
TILES_PER_SEQ = SEQ // TOK_TILE
GQA_KC = 256
BAND_TQ = 128
BAND_TILES = 4

PAIRS = ((0, 1), (0, 2), (0, 3), (1, 2), (1, 3), (2, 3))
N_CLASSES = N_EXPERT_GROUPS * len(PAIRS)
CLASS_ROWS = 32
MOE_TM = 256
MOE_ROWS = N_TOK + N_CLASSES * MOE_TM
MOE_TILES = MOE_ROWS // MOE_TM
XA_W = D_MODEL + LANES
ROW_DMA_UNROLL = 8

VMEM_LIMIT = 56 * 1024 * 1024


def _params(sem, vmem=VMEM_LIMIT):
    return pltpu.CompilerParams(dimension_semantics=sem, vmem_limit_bytes=vmem)


def _dot(a, b):
    return jnp.dot(a, b, preferred_element_type=F32)


def _dot_nt(a, b):
    return lax.dot_general(a, b, (((1,), (1,)), ((), ())), preferred_element_type=F32)


def _split_bf16(x):
    hi = x.astype(BF16)
    lo = (x - hi.astype(F32)).astype(BF16)
    return hi, lo


def _layer_norm(z, g, b):
    mu = jnp.mean(z, axis=-1, keepdims=True)
    zc = z - mu
    var = jnp.mean(zc * zc, axis=-1, keepdims=True)
    return zc * lax.rsqrt(var + NORM_EPS) * g + b


def _rope(z, c, sa, sb, shift):
    return z * c + pltpu.roll(z, LANES - shift, 1) * sa + pltpu.roll(z, shift, 1) * sb


def _first_head(shape):
    return lax.broadcasted_iota(jnp.int32, shape, 1) < HEAD_DIM


def _pair_attention(q2, k2, v3, first, valid=None, base2=False):
    res = []
    for pick in (first, jnp.logical_not(first)):
        s = _dot_nt(jnp.where(pick, q2, jnp.zeros_like(q2)), k2)
        if valid is not None:
            s = jnp.where(valid, s, NEG_BIG)
        m = jnp.max(s, axis=1, keepdims=True)
        p = jnp.exp2(s - m) if base2 else jnp.exp(s - m)
        res.append((_dot(p.astype(BF16), v3), m))
    return res


def _pair_output(res, first):
    (ol_a, _), (ol_b, _) = res
    return jnp.where(first, ol_a[:, :LANES], ol_b[:, :LANES]) / jnp.where(first, ol_a[:, LANES:], ol_b[:, LANES:])


def _with_ones(v2):
    return jnp.concatenate([v2, jnp.ones_like(v2)], axis=1)


def _proj_kernel(x_ref, w_ref, o_ref):
    o_ref[...] = _dot(x_ref[...].astype(BF16), w_ref[...]).astype(o_ref.dtype)


def _proj(x, w, tm=TOK_TILE):
    n, k = x.shape
    width = w.shape[1]
    return pl.pallas_call(
        _proj_kernel,
        grid=(n // tm,),
        in_specs=[pl.BlockSpec((tm, k), lambda i: (i, 0)), pl.BlockSpec((k, width), lambda i: (0, 0))],
        out_specs=pl.BlockSpec((tm, width), lambda i: (i, 0)),
        out_shape=jax.ShapeDtypeStruct((n, width), BF16),
        compiler_params=_params(("parallel",)),
        name="proj",
    )(x, w)


def _a_in_kernel(x_ref, w_ref, m_ref, qg_ref, kg_ref, c_ref, sa_ref, sb_ref, q_ref, k_ref, v_ref, qm_ref):
    y = _dot(x_ref[...].astype(BF16), w_ref[...])
    m = m_ref[...]
    c, sa, sb = c_ref[...], sa_ref[...], sb_ref[...]

    def norm_rope(z, gain):
        hi, lo = _split_bf16(z * z)
        ms = (_dot(hi, m) + _dot(lo, m)) * (1.0 / HEAD_DIM)
        z = z * lax.rsqrt(ms + NORM_EPS) * gain
        return _rope(z, c, sa, sb, 16)

    nq = A_Q_HEADS * HEAD_DIM // LANES
    nk = A_KV_HEADS * HEAD_DIM // LANES
    for ch in range(nq):
        q_ref[:, ch * LANES:(ch + 1) * LANES] = norm_rope(y[:, ch * LANES:(ch + 1) * LANES], qg_ref[...]).astype(BF16)
    off = nq * LANES
    for ch in range(nk):
        z = y[:, off + ch * LANES: off + (ch + 1) * LANES]
        k_ref[:, ch * LANES:(ch + 1) * LANES] = norm_rope(z, kg_ref[...]).astype(BF16)
    off += nk * LANES
    ones = jnp.ones((y.shape[0], LANES), BF16)
    for ch in range(nk):
        v_ref[:, 2 * ch * LANES:(2 * ch + 1) * LANES] = y[:, off + ch * LANES: off + (ch + 1) * LANES].astype(BF16)
        v_ref[:, (2 * ch + 1) * LANES:(2 * ch + 2) * LANES] = ones
    off += nk * LANES
    qm_ref[...] = (y[:, off:off + MEM_W] * SCALE).astype(BF16)


def _a_in_proj(x, w, m128, qg, kg, tabs, tm=TOK_TILE):
    c, sa, sb = tabs
    nblk = SEQ // tm
    a_in = w.shape[1]
    tab_spec = pl.BlockSpec((tm, LANES), lambda i: (i % nblk, 0))
    row = lambda width: pl.BlockSpec((tm, width), lambda i: (i, 0))
    const = lambda shape: pl.BlockSpec(shape, lambda i: (0, 0))
    qw, kw = A_Q_HEADS * HEAD_DIM, A_KV_HEADS * HEAD_DIM
    return pl.pallas_call(
        _a_in_kernel,
        grid=(N_TOK // tm,),
        in_specs=[row(D_MODEL), const((D_MODEL, a_in)), const((LANES, LANES)), const((1, LANES)), const((1, LANES)),
                  tab_spec, tab_spec, tab_spec],
        out_specs=[row(qw), row(kw), row(2 * kw), row(MEM_W)],
        out_shape=[jax.ShapeDtypeStruct((N_TOK, qw), BF16), jax.ShapeDtypeStruct((N_TOK, kw), BF16),
                   jax.ShapeDtypeStruct((N_TOK, 2 * kw), BF16), jax.ShapeDtypeStruct((N_TOK, MEM_W), BF16)],
        compiler_params=_params(("parallel",)),
        name="a_in_proj",
    )(x, w, m128, qg, kg, c, sa, sb)


def _b_in_kernel(*refs):
    ng = len(DILATIONS)
    nx = D_MODEL // LANES
    x_refs, w_ref, refs = refs[:nx], refs[nx], refs[nx + 1:]
    tabs, outs, qm_ref = refs[:3 * ng], refs[3 * ng:6 * ng], refs[6 * ng]
    xb1 = None
    for g, d in enumerate(DILATIONS):
        n = TOK_TILE // d
        if d == 1:
            cols = [r[...] for r in x_refs]
        else:
            cols = [jnp.concatenate([r[pl.ds(k, n, stride=d), :] for k in range(d)], axis=0) for r in x_refs]
        xb = jnp.concatenate(cols, axis=1).astype(BF16)
        xb1 = xb if d == 1 else xb1
        c, sa, sb = (t[...].reshape(TOK_TILE, LANES) for t in tabs[3 * g:3 * g + 3])
        q_ref, k_ref, v_ref = outs[3 * g:3 * g + 3]
        yq = _dot(xb, w_ref[:, g * B_WIDTH:(g + 1) * B_WIDTH]) * SCALE
        yk = _dot(xb, w_ref[:, (ng + g) * B_WIDTH:(ng + g + 1) * B_WIDTH])
        yv = _dot(xb, w_ref[:, (2 * ng + g) * B_WIDTH:(2 * ng + g + 1) * B_WIDTH])
        for ch in range(B_WIDTH // LANES):
            sl = slice(ch * LANES, (ch + 1) * LANES)
            q_ref[0, :, :, sl] = _rope(yq[:, sl], c, sa, sb, PARTIAL_ROT_DIMS // 2).reshape(d, n, LANES).astype(BF16)
            k_ref[0, :, :, sl] = _rope(yk[:, sl], c, sa, sb, PARTIAL_ROT_DIMS // 2).reshape(d, n, LANES).astype(BF16)
        v_ref[0] = yv.reshape(d, n, B_WIDTH).astype(BF16)
    qm = _dot(xb1, w_ref[:, 3 * ng * B_WIDTH:3 * ng * B_WIDTH + MEM_W])
    qm_ref[...] = (qm * SCALE).astype(BF16)


def _b_in_proj(x, w, tabs):
    nx = D_MODEL // LANES
    in_specs = [pl.BlockSpec((TOK_TILE, LANES), functools.partial(lambda i, c: (i, c), c=c)) for c in range(nx)]
    in_specs.append(pl.BlockSpec(w.shape, lambda i: (0, 0)))
    out_specs, out_shape = [], []
    for d in DILATIONS:
        n = TOK_TILE // d
        in_specs += [pl.BlockSpec((d, n, LANES), lambda i: (0, i % TILES_PER_SEQ, 0))] * 3
        out_specs += [pl.BlockSpec((1, d, n, B_WIDTH), lambda i: (i // TILES_PER_SEQ, 0, i % TILES_PER_SEQ, 0))] * 3
        out_shape += [jax.ShapeDtypeStruct((BATCH, d, SEQ // d, B_WIDTH), BF16)] * 3
    out_specs.append(pl.BlockSpec((TOK_TILE, MEM_W), lambda i: (i, 0)))
    out_shape.append(jax.ShapeDtypeStruct((N_TOK, MEM_W), BF16))
    res = pl.pallas_call(
        _b_in_kernel,
        grid=(N_TOK // TOK_TILE,),
        in_specs=in_specs,
        out_specs=out_specs,
        out_shape=out_shape,
        compiler_params=_params(("parallel",)),
        name="b_in_proj",
    )(*([x] * nx), w, *[t for group in tabs for t in group])
    return [res[3 * g:3 * g + 3] for g in range(len(DILATIONS))], res[-1]


def _gqa_kernel(q_ref, k_ref, v_ref, o_ref):
    tq = q_ref.shape[0]
    ntile = q_ref.shape[1] // LANES
    q = jnp.concatenate([q_ref[:, t * LANES:(t + 1) * LANES] for t in range(ntile)], axis=0)
    first = _first_head(q.shape)
    res = []
    for pick in (first, jnp.logical_not(first)):
        qh = jnp.where(pick, q, jnp.zeros_like(q))
        m = jnp.full((q.shape[0], 1), -jnp.inf, F32)
        acc = jnp.zeros((q.shape[0], 2 * LANES), F32)
        for c in range(SEQ // GQA_KC):
            keys = slice(c * GQA_KC, (c + 1) * GQA_KC)
            s = _dot_nt(qh, k_ref[keys, :])
            m_new = jnp.maximum(m, jnp.max(s, axis=1, keepdims=True))
            p = jnp.exp2(s - m_new).astype(BF16)
            acc = acc * jnp.exp2(m - m_new) + _dot(p, v_ref[keys, :])
            m = m_new
        res.append((acc, m))
    o = _pair_output(res, first)
    for t in range(ntile):
        o_ref[:, t * LANES:(t + 1) * LANES] = o[t * tq:(t + 1) * tq].astype(BF16)


def _gqa_attention(q, k, v, tq=512):
    nq = SEQ // tq
    qw = q.shape[1] // 2
    kw = k.shape[1] // 2
    return pl.pallas_call(
        _gqa_kernel,
        grid=(BATCH, 2, nq),
        in_specs=[pl.BlockSpec((tq, qw), lambda b, p, i: (b * nq + i, p)),
                  pl.BlockSpec((SEQ, kw), lambda b, p, i: (b, p)),
                  pl.BlockSpec((SEQ, 2 * kw), lambda b, p, i: (b, p))],
        out_specs=pl.BlockSpec((tq, qw), lambda b, p, i: (b * nq + i, p)),
        out_shape=jax.ShapeDtypeStruct(q.shape, BF16),
        compiler_params=_params(("parallel", "parallel", "parallel")),
        name="gqa_attention",
    )(q, k, v)


def _band_kernel(q_ref, k_ref, v_ref, o_ref, lse_ref, *, length, seg, win):
    tq = BAND_TQ
    first = _first_head((tq, LANES))
    lane = lax.broadcasted_iota(jnp.int32, (tq, LANES), 1)
    row = lax.broadcasted_iota(jnp.int32, (tq, win), 0)
    col = lax.broadcasted_iota(jnp.int32, (tq, win), 1)
    for u in range(BAND_TILES):
        tile = pl.program_id(1) * BAND_TILES + u
        start = pl.multiple_of(jnp.clip(tile * tq - BAND_RADIUS, 0, length - win), BAND_RADIUS)
        qpos, kpos = tile * tq + row, start + col
        valid = jnp.abs(qpos - kpos) <= BAND_RADIUS
        if seg < length:
            valid = jnp.logical_and(valid, qpos // seg == kpos // seg)
        rows = slice(u * tq, (u + 1) * tq)
        lse = jnp.zeros((tq, LANES), F32)
        for j in range(B_WIDTH // LANES):
            sl = slice(j * LANES, (j + 1) * LANES)
            res = _pair_attention(q_ref[0, rows, sl], k_ref[0, pl.ds(start, win), sl],
                                  _with_ones(v_ref[0, pl.ds(start, win), sl]), first, valid)
            o_ref[0, rows, sl] = _pair_output(res, first).astype(BF16)
            for half, (ol, m) in enumerate(res):
                lse = jnp.where(lane == 2 * j + half, m + jnp.log(ol[:, LANES:]), lse)
        lse_ref[0, rows, :] = lse


def _band_attention(q, k, v, d):
    seg = SEQ // d
    step_rows = BAND_TILES * BAND_TQ
    length = max(seg, step_rows)
    win = BAND_TQ + 2 * BAND_RADIUS
    nblk = N_TOK // length
    q, k, v = (t.reshape(nblk, length, B_WIDTH) for t in (q, k, v))
    seq_spec = pl.BlockSpec((1, length, B_WIDTH), lambda s, i: (s, 0, 0))
    o, lse = pl.pallas_call(
        functools.partial(_band_kernel, length=length, seg=seg, win=win),
        grid=(nblk, length // step_rows),
        in_specs=[pl.BlockSpec((1, step_rows, B_WIDTH), lambda s, i: (s, i, 0)), seq_spec, seq_spec],
        out_specs=[pl.BlockSpec((1, step_rows, B_WIDTH), lambda s, i: (s, i, 0)),
                   pl.BlockSpec((1, step_rows, LANES), lambda s, i: (s, i, 0))],
        out_shape=[jax.ShapeDtypeStruct((nblk, length, B_WIDTH), BF16),
                   jax.ShapeDtypeStruct((nblk, length, LANES), F32)],
        compiler_params=_params(("parallel", "parallel")),
        name=f"band_attention_d{d}",
    )(q, k, v)
    return o.reshape(BATCH, d, seg, B_WIDTH), lse.reshape(BATCH, d, seg, LANES)


def _route(x, wh_ref, wl_ref, rb_ref, tri_ref, xa_ref, cls_ref, rank_ref, cnt_ref, carry_ref):
    tm = x.shape[0]
    xh, xl = _split_bf16(x)
    logits = _dot_nt(wh_ref[...], xh) + _dot_nt(wh_ref[...], xl) + _dot_nt(wl_ref[...], xh)
    aff = 1.0 / (1.0 + jnp.exp(-logits))
    sel = aff + rb_ref[...]
    s = [sel[e:e + 1, :] for e in range(N_EXPERTS)]
    a = [aff[e:e + 1, :] for e in range(N_EXPERTS)]

    def top2_sum(v):
        hi01, lo01 = jnp.maximum(v[0], v[1]), jnp.minimum(v[0], v[1])
        hi23, lo23 = jnp.maximum(v[2], v[3]), jnp.minimum(v[2], v[3])
        return jnp.maximum(hi01, hi23) + jnp.maximum(jnp.minimum(hi01, hi23), jnp.maximum(lo01, lo23))

    gscore = [top2_sum(s[EXPERTS_PER_GROUP * g:EXPERTS_PER_GROUP * (g + 1)]) for g in range(N_EXPERT_GROUPS)]
    best = jnp.zeros((1, tm), jnp.int32)
    best_score = gscore[0]
    for g in range(1, N_EXPERT_GROUPS):
        better = gscore[g] > best_score
        best = jnp.where(better, g, best)
        best_score = jnp.where(better, gscore[g], best_score)

    def pick(rows, j):
        out = rows[j]
        for g in range(1, N_EXPERT_GROUPS):
            out = jnp.where(best == g, rows[EXPERTS_PER_GROUP * g + j], out)
        return out

    t = [pick(s, j) for j in range(EXPERTS_PER_GROUP)]
    w = [pick(a, j) for j in range(EXPERTS_PER_GROUP)]

    def first_max(v):
        mx = jnp.maximum(jnp.maximum(v[0], v[1]), jnp.maximum(v[2], v[3]))
        idx = jnp.full((1, tm), EXPERTS_PER_GROUP - 1, jnp.int32)
        for j in range(EXPERTS_PER_GROUP - 2, -1, -1):
            idx = jnp.where(v[j] == mx, j, idx)
        return idx

    i1 = first_max(t)
    i2 = first_max([jnp.where(i1 == j, -jnp.inf, t[j]) for j in range(EXPERTS_PER_GROUP)])
    lo, hi = jnp.minimum(i1, i2), jnp.maximum(i1, i2)

    def take(rows, idx):
        out = rows[0]
        for j in range(1, EXPERTS_PER_GROUP):
            out = jnp.where(idx == j, rows[j], out)
        return out

    w_lo, w_hi = take(w, lo), take(w, hi)
    den = w_lo + w_hi
    pair = jnp.where(lo == 0, hi - 1, jnp.where(lo == 1, hi + 1, len(PAIRS) - 1))
    cls = best * len(PAIRS) + pair

    onehot = (lax.broadcasted_iota(jnp.int32, (CLASS_ROWS, tm), 0) == cls).astype(F32)
    before = _dot(onehot.astype(BF16), tri_ref[...]) + carry_ref[:, 0:1]
    rank = jnp.sum(onehot * before, axis=0, keepdims=True)
    carry_ref[...] = carry_ref[...] + jnp.sum(onehot, axis=1, keepdims=True)

    cls_ref[0] = cls
    rank_ref[0] = rank.astype(jnp.int32)
    cnt_ref[...] = carry_ref[...]
    gates = jnp.concatenate([w_lo / den, w_hi / den, jnp.zeros((LANES - 2, tm), F32)], axis=0)
    xa_ref[:, :D_MODEL] = x
    xa_ref[:, D_MODEL:] = gates.T


def _post_attn_kernel(*refs, merge):
    if merge:
        o_refs, l_refs, e_ref, refs = refs[:3], refs[3:6], refs[6], refs[7:]
    else:
        mix_ref, refs = refs[0], refs[1:]
    (qm_ref, mk_ref, mv_ref, w1_ref, w2_ref, x_ref, g_ref, b_ref, wh_ref, wl_ref, rb_ref, tri_ref,
     xa_ref, cls_ref, rank_ref, cnt_ref, carry_ref) = refs[:17]

    @pl.when(pl.program_id(0) == 0)
    def _():
        carry_ref[...] = jnp.zeros_like(carry_ref)

    if merge:
        ot_ref, lt_ref = refs[17], refs[18]
        nch = B_WIDTH // LANES
        for g, d in enumerate(DILATIONS):
            n = TOK_TILE // d
            for r in range(d):
                rows = pl.ds(r, n, stride=d) if d > 1 else slice(None)
                og = o_refs[g][0, r].astype(F32)
                for ch in range(nch):
                    ot_ref[g, ch, rows, :] = og[:, ch * LANES:(ch + 1) * LANES]
                lt_ref[g, rows, :] = l_refs[g][0, r]
        lses = [lt_ref[g] for g in range(len(DILATIONS))]
        m = functools.reduce(jnp.maximum, lses)
        es = [jnp.exp(l - m) for l in lses]
        inv = 1.0 / functools.reduce(jnp.add, es)
        expand = e_ref[...]

        def widen(w):
            hi, lo = _split_bf16(w)
            return _dot(hi, expand) + _dot(lo, expand)

        ws = [widen(e * inv) for e in es]
        mix = jnp.concatenate(
            [functools.reduce(jnp.add, [w[:, ch * LANES:(ch + 1) * LANES] * ot_ref[g, ch] for g, w in enumerate(ws)])
             for ch in range(nch)], axis=1).astype(BF16)
    else:
        mix = mix_ref[...]

    first = _first_head((TOK_TILE, LANES))
    memo = []
    for j in range(MEM_W // LANES):
        sl = slice(j * LANES, (j + 1) * LANES)
        memo.append(_pair_output(_pair_attention(qm_ref[:, sl], mk_ref[:, sl], _with_ones(mv_ref[:, sl]), first), first))
    memo = jnp.concatenate(memo, axis=1).astype(BF16)

    attn = _dot(mix, w1_ref[...]) + _dot(memo, w2_ref[...])
    x1 = _layer_norm(ALPHA * x_ref[...] + attn, g_ref[...], b_ref[...])
    _route(x1, wh_ref, wl_ref, rb_ref, tri_ref, xa_ref, cls_ref, rank_ref, cnt_ref, carry_ref)


def _post_attn(mix, qm, mkv, w1, w2, x, g, b, router, expand=None):
    merge = expand is not None
    wh, wl, rb, tri = router
    nblk = N_TOK // TOK_TILE
    row = lambda width: pl.BlockSpec((TOK_TILE, width), lambda i: (i, 0))
    const = lambda shape: pl.BlockSpec(shape, lambda i: (0,) * len(shape))
    row3 = pl.BlockSpec((1, 1, TOK_TILE), lambda i: (i, 0, 0))
    scratch = [pltpu.VMEM((CLASS_ROWS, LANES), F32)]
    if merge:
        outs, lses = mix
        resid = lambda d, width: pl.BlockSpec((1, d, TOK_TILE // d, width),
                                              lambda i: (i // TILES_PER_SEQ, 0, i % TILES_PER_SEQ, 0))
        lead_specs = ([resid(d, B_WIDTH) for d in DILATIONS] + [resid(d, LANES) for d in DILATIONS]
                      + [const(expand.shape)])
        lead = [*outs, *lses, expand]
        scratch += [pltpu.VMEM((len(DILATIONS), B_WIDTH // LANES, TOK_TILE, LANES), F32),
                    pltpu.VMEM((len(DILATIONS), TOK_TILE, LANES), F32)]
    else:
        lead_specs, lead = [row(mix.shape[1])], [mix]
    xa, cls, rank, cnt = pl.pallas_call(
        functools.partial(_post_attn_kernel, merge=merge),
        grid=(nblk,),
        in_specs=lead_specs + [row(MEM_W),
                               pl.BlockSpec((N_MEM, MEM_W), lambda i: (i // TILES_PER_SEQ, 0)),
                               pl.BlockSpec((N_MEM, MEM_W), lambda i: (i // TILES_PER_SEQ, 1)),
                               const(w1.shape), const(w2.shape), row(D_MODEL), const((1, D_MODEL)), const((1, D_MODEL)),
                               const(wh.shape), const(wl.shape), const(rb.shape), const(tri.shape)],
        out_specs=[row(XA_W), row3, row3, const((CLASS_ROWS, LANES))],
        out_shape=[jax.ShapeDtypeStruct((N_TOK, XA_W), F32), jax.ShapeDtypeStruct((nblk, 1, TOK_TILE), jnp.int32),
                   jax.ShapeDtypeStruct((nblk, 1, TOK_TILE), jnp.int32), jax.ShapeDtypeStruct((CLASS_ROWS, LANES), F32)],
        scratch_shapes=scratch,
        compiler_params=_params(("arbitrary",)),
        name="post_attn_merge" if merge else "post_attn",
    )(*lead, qm, mkv, mkv, w1, w2, x, g, b, wh, wl, rb, tri)
    return xa, cls.reshape(N_TOK), rank.reshape(N_TOK), cnt[:N_CLASSES, 0].astype(jnp.int32)


def _dispatch_kernel(pos_ref, xa_ref, init_ref, xs_ref, sem, *, chunk):
    del init_ref
    base = pl.program_id(0) * chunk

    def issue(blk, carry):
        for u in range(ROW_DMA_UNROLL):
            r = blk * ROW_DMA_UNROLL + u
            pltpu.make_async_copy(xa_ref.at[pl.ds(r, 1)], xs_ref.at[pl.ds(pos_ref[base + r], 1)], sem).start(
                priority=u % 2)
        return carry

    def retire(r, carry):
        pltpu.make_async_copy(xa_ref.at[pl.ds(0, 1)], xs_ref.at[pl.ds(0, 1)], sem).wait()
        return carry

    lax.fori_loop(0, chunk // ROW_DMA_UNROLL, issue, 0)
    lax.fori_loop(0, chunk, retire, 0, unroll=8)


def _dispatch(pos, xa, chunk=512):
    init = jnp.zeros((MOE_ROWS, XA_W), F32)
    return pl.pallas_call(
        functools.partial(_dispatch_kernel, chunk=chunk),
        grid_spec=pltpu.PrefetchScalarGridSpec(
            num_scalar_prefetch=1,
            grid=(N_TOK // chunk,),
            in_specs=[pl.BlockSpec((chunk, XA_W), lambda i, p: (i, 0)), pl.BlockSpec(memory_space=pl.ANY)],
            out_specs=pl.BlockSpec(memory_space=pl.ANY),
            scratch_shapes=[pltpu.SemaphoreType.DMA(())],
        ),
        out_shape=jax.ShapeDtypeStruct((MOE_ROWS, XA_W), F32),
        input_output_aliases={2: 0},
        compiler_params=_params(("arbitrary",)),
        name="moe_dispatch",
    )(pos, xa, init)


def _expert_kernel(slot1_ref, slot2_ref, used_ref, first_ref, grp_ref, next_ref, xs_ref, wg_hbm, wu_hbm, wd_hbm,
                   ys_ref, stage_g, stage_u, stage_d, wg_s, wu_s, wd_s, sem, *, layer):
    i = pl.program_id(0)
    streams = ((wg_hbm, stage_g, wg_s), (wu_hbm, stage_u, wu_s), (wd_hbm, stage_d, wd_s))

    def group_copy(group, k):
        first_expert = layer * N_EXPERTS + group * EXPERTS_PER_GROUP
        hbm, stage, _ = streams[k]
        return pltpu.make_async_copy(hbm.at[pl.ds(first_expert, EXPERTS_PER_GROUP)], stage, sem.at[k])

    @pl.when(i == 0)
    def _():
        for k in range(len(streams)):
            group_copy(grp_ref[0], k).start()

    @pl.when(first_ref[i] != 0)
    def _():
        for k, (_, stage, dst) in enumerate(streams):
            group_copy(grp_ref[i], k).wait()
            for e in range(EXPERTS_PER_GROUP):
                dst[e] = stage[e].astype(BF16)

        @pl.when(next_ref[i] >= 0)
        def _():
            for k in range(len(streams)):
                group_copy(next_ref[i], k).start()

    @pl.when(used_ref[i] != 0)
    def _():
        x = xs_ref[:, :D_MODEL].astype(BF16)
        gates = xs_ref[:, D_MODEL:]

        def expert(slot):
            gate = _dot(x, wg_s[slot])
            h = gate * (1.0 / (1.0 + jnp.exp(-gate))) * _dot(x, wu_s[slot])
            return _dot(h.astype(BF16), wd_s[slot])

        y = gates[:, 0:1] * expert(slot1_ref[i]) + gates[:, 1:2] * expert(slot2_ref[i])
        for j in range(D_MODEL // LANES):
            ys_ref[:, j, :] = y[:, j * LANES:(j + 1) * LANES]

    @pl.when(used_ref[i] == 0)
    def _():
        ys_ref[...] = jnp.zeros_like(ys_ref)


def _experts(tile_meta, xs, layer, wg, wu, wd):
    nmeta = len(tile_meta)
    up_shape = (EXPERTS_PER_GROUP, D_MODEL, D_EXPERT)
    down_shape = (EXPERTS_PER_GROUP, D_EXPERT, D_MODEL)
    return pl.pallas_call(
        functools.partial(_expert_kernel, layer=layer),
        grid_spec=pltpu.PrefetchScalarGridSpec(
            num_scalar_prefetch=nmeta,
            grid=(MOE_TILES,),
            in_specs=[pl.BlockSpec((MOE_TM, XA_W), lambda i, *_: (i, 0)),
                      pl.BlockSpec(memory_space=pl.ANY), pl.BlockSpec(memory_space=pl.ANY),
                      pl.BlockSpec(memory_space=pl.ANY)],
            out_specs=pl.BlockSpec((MOE_TM, D_MODEL // LANES, LANES), lambda i, *_: (i, 0, 0)),
            scratch_shapes=[pltpu.VMEM(up_shape, F32), pltpu.VMEM(up_shape, F32), pltpu.VMEM(down_shape, F32),
                            pltpu.VMEM(up_shape, BF16), pltpu.VMEM(up_shape, BF16), pltpu.VMEM(down_shape, BF16),
                            pltpu.SemaphoreType.DMA((3,))],
        ),
        out_shape=jax.ShapeDtypeStruct((MOE_ROWS, D_MODEL // LANES, LANES), F32),
        compiler_params=_params(("arbitrary",)),
        name="moe_experts",
    )(*tile_meta, xs, wg, wu, wd)


def _combine_kernel(pos_ref, ys_ref, x_ref, g_ref, b_ref, o_ref, buf, sem, *, tc):
    i = pl.program_id(0)
    slot = i % 2

    def gather(step, into):
        def issue(blk, carry):
            for u in range(ROW_DMA_UNROLL):
                r = blk * ROW_DMA_UNROLL + u
                row = pos_ref[step * tc + r]
                pltpu.make_async_copy(ys_ref.at[pl.ds(row, 1)], buf.at[into, pl.ds(r, 1)], sem.at[into]).start(
                    priority=u % 2)
            return carry

        lax.fori_loop(0, tc // ROW_DMA_UNROLL, issue, 0)

    @pl.when(i == 0)
    def _():
        gather(0, 0)

    @pl.when(i + 1 < pl.num_programs(0))
    def _():
        gather(i + 1, 1 - slot)

    def retire(r, carry):
        pltpu.make_async_copy(ys_ref.at[pl.ds(0, 1)], buf.at[slot, pl.ds(0, 1)], sem.at[slot]).wait()
        return carry

    lax.fori_loop(0, tc, retire, 0, unroll=8)
    ffn = jnp.concatenate([buf[slot, :, j, :] for j in range(D_MODEL // LANES)], axis=1)
    o_ref[...] = _layer_norm(ALPHA * x_ref[...] + ffn, g_ref[...], b_ref[...])


def _combine_ln(pos, ys, xa, g, b, tc=256):
    return pl.pallas_call(
        functools.partial(_combine_kernel, tc=tc),
        grid_spec=pltpu.PrefetchScalarGridSpec(
            num_scalar_prefetch=1,
            grid=(N_TOK // tc,),
            in_specs=[pl.BlockSpec(memory_space=pl.ANY),
                      pl.BlockSpec((tc, D_MODEL), lambda i, p: (i, 0)),
                      pl.BlockSpec((1, D_MODEL), lambda i, p: (0, 0)),
                      pl.BlockSpec((1, D_MODEL), lambda i, p: (0, 0))],
            out_specs=pl.BlockSpec((tc, D_MODEL), lambda i, p: (i, 0)),
            scratch_shapes=[pltpu.VMEM((2, tc, D_MODEL // LANES, LANES), F32), pltpu.SemaphoreType.DMA((2,))],
        ),
        out_shape=jax.ShapeDtypeStruct((N_TOK, D_MODEL), F32),
        compiler_params=_params(("arbitrary",)),
        name="moe_combine_ln",
    )(pos, ys, xa, g, b)


def _moe_ln(routed, layer, wg, wu, wd, g, b):
    xa, cls, rank, counts = routed
    padded = (counts + MOE_TM - 1) // MOE_TM * MOE_TM
    ends = jnp.cumsum(padded)
    starts = ends - padded
    pos = starts[cls] + rank
    tile_start = jnp.arange(MOE_TILES, dtype=jnp.int32) * MOE_TM
    tile_used = (tile_start < ends[-1]).astype(jnp.int32)
    last_cls = jnp.max(jnp.where(counts > 0, jnp.arange(N_CLASSES), 0))
    tile_cls = jnp.minimum(jnp.sum(tile_start[:, None] >= ends[None, :], axis=1), last_cls)
    tile_cls = tile_cls.astype(jnp.int32)
    tile_grp = tile_cls // len(PAIRS)
    prev_grp = jnp.concatenate([jnp.full((1,), -1, jnp.int32), tile_grp[:-1]])
    tile_first = tile_used * (tile_grp != prev_grp).astype(jnp.int32)
    has = jnp.sum(counts.reshape(N_EXPERT_GROUPS, len(PAIRS)), axis=1) > 0
    gid = jnp.arange(N_EXPERT_GROUPS, dtype=jnp.int32)
    later = jnp.where(jnp.logical_and(has[None, :], gid[None, :] > gid[:, None]), gid[None, :], N_EXPERT_GROUPS)
    next_grp = jnp.min(later, axis=1)
    next_grp = jnp.where(next_grp == N_EXPERT_GROUPS, -1, next_grp)
    pair_slots = jnp.asarray(np.array(PAIRS, np.int32))[tile_cls % len(PAIRS)]
    tile_meta = (pair_slots[:, 0], pair_slots[:, 1], tile_used, tile_first, tile_grp, next_grp[tile_grp])
    xs = _dispatch(pos, xa)
    ys = _experts(tile_meta, xs, layer, wg, wu, wd)
    return _combine_ln(pos, ys, xa, g, b)


def _tile_heads(t64):
    return jnp.concatenate([t64, t64], axis=-1)


def _axial_tables():
    rows = SEQ // GRID_W
    row = jnp.repeat(jnp.arange(rows), GRID_W).astype(F32)
    col = jnp.tile(jnp.arange(GRID_W), rows).astype(F32)
    half = HEAD_DIM // 2
    inv = 1.0 / (AXIAL_THETA ** (jnp.arange(0, half, 2, dtype=F32) / half))
    ar, ac = row[:, None] * inv, col[:, None] * inv
    z = jnp.zeros_like(ar)
    c = jnp.concatenate([jnp.cos(ar), jnp.cos(ar), jnp.cos(ac), jnp.cos(ac)], -1)
    sa = jnp.concatenate([-jnp.sin(ar), z, -jnp.sin(ac), z], -1)
    sb = jnp.concatenate([z, jnp.sin(ar), z, jnp.sin(ac)], -1)
    return tuple(_tile_heads(t) for t in (c, sa, sb))


def _partial_tables():
    n = PARTIAL_ROT_DIMS
    pos = jnp.arange(SEQ, dtype=F32)
    ang = pos[:, None] * (1.0 / (PARTIAL_THETA ** (jnp.arange(0, n, 2, dtype=F32) / n)))
    z = jnp.zeros_like(ang)
    rest = HEAD_DIM - n
    c = jnp.concatenate([jnp.cos(ang), jnp.cos(ang), jnp.ones((SEQ, rest), F32)], -1)
    sa = jnp.concatenate([-jnp.sin(ang), z, jnp.zeros((SEQ, rest), F32)], -1)
    sb = jnp.concatenate([z, jnp.sin(ang), jnp.zeros((SEQ, rest), F32)], -1)
    return tuple(_tile_heads(t) for t in (c, sa, sb))


def _residue_order(tab, d):
    return tab.reshape(SEQ // d, d, LANES).transpose(1, 0, 2)


def kernel(x, mem, w_mem_kv, router_w, router_b, a_w_in, a_w_out, a_q_norm, a_k_norm, b_w_in, b_w_out,
           ln1_g, ln1_b, ln2_g, ln2_b, w_gate, w_up, w_down):
    x = x.reshape(N_TOK, D_MODEL)
    row = lambda v: v.reshape(1, -1)

    lane = np.arange(LANES)
    m128 = jnp.asarray((lane[:, None] // HEAD_DIM == lane[None, :] // HEAD_DIM), BF16)
    expand = jnp.asarray(lane[:, None] == np.arange(B_WIDTH)[None, :] // HEAD_DIM, BF16)
    tri = jnp.asarray(np.triu(np.ones((TOK_TILE, TOK_TILE), np.float32), 1), BF16)
    rw_hi = router_w.T.astype(BF16)
    rw_lo = (router_w.T - rw_hi.astype(F32)).astype(BF16)
    router = (rw_hi, rw_lo, router_b.reshape(N_EXPERTS, 1), tri)

    mkv = _proj(mem.reshape(BATCH * N_MEM, D_MODEL), w_mem_kv.astype(BF16))
    experts = tuple(w.reshape(DEPTH * N_EXPERTS, *w.shape[2:]) for w in (w_gate, w_up, w_down))

    kmix = A_Q_HEADS * HEAD_DIM
    qcols = (np.asarray(_Q_HEAD_ORDER)[:, None] * HEAD_DIM + np.arange(HEAD_DIM)[None, :]).reshape(-1)
    w_in = jnp.concatenate([a_w_in[0][:, qcols], a_w_in[0][:, kmix:]], axis=1).astype(BF16)
    qg = row(jnp.tile(a_q_norm[0], 2) * (SCALE * LOG2_E))
    kg = row(jnp.tile(a_k_norm[0], 2))
    q, k, v, qm = _a_in_proj(x, w_in, m128, qg, kg, _axial_tables())
    mix = _gqa_attention(q, k, v)
    w_out = a_w_out[0].astype(BF16)
    routed = _post_attn(mix, qm, mkv, w_out[qcols], w_out[kmix:], x, row(ln1_g[0]), row(ln1_b[0]), router)
    x = _moe_ln(routed, 0, *experts, row(ln2_g[0]), row(ln2_b[0]))

    ptabs = _partial_tables()
    tabs = [tuple(_residue_order(t, d) for t in ptabs) for d in DILATIONS]
    qkv, qm = _b_in_proj(x, b_w_in[0].astype(BF16), tabs)
    outs, lses = zip(*[_band_attention(*qkv[g], d) for g, d in enumerate(DILATIONS)])
    w_out = b_w_out[0].astype(BF16)
    routed = _post_attn((outs, lses), qm, mkv, w_out[:B_WIDTH], w_out[B_WIDTH:], x, row(ln1_g[1]), row(ln1_b[1]),
                        router, expand)
    x = _moe_ln(routed, 1, *experts, row(ln2_g[1]), row(ln2_b[1]))
    return x.reshape(BATCH, SEQ, D_MODEL)
```

```python
import functools
import math

import numpy as np
import jax
import jax.numpy as jnp
from jax import lax
from jax.experimental import pallas as pl
from jax.experimental.pallas import tpu as pltpu

F32 = jnp.float32
BF16 = jnp.bfloat16

D_MODEL = 1024
BATCH = 8
SEQ = 2048
N_TOK = BATCH * SEQ
HEAD_DIM = 64
GRID_W = 64
N_MEM = 256
MEM_HEADS = 4
A_Q_HEADS = 12
A_KV_HEADS = 4
AXIAL_THETA = 10000.0
B_HEADS = 8
B_WIDTH = B_HEADS * HEAD_DIM
DILATIONS = (1, 4, 16)
BAND_RADIUS = 64
PARTIAL_ROT_DIMS = HEAD_DIM // 4
PARTIAL_THETA = 500000.0
N_EXPERTS = 16
N_EXPERT_GROUPS = 4
EXPERTS_PER_GROUP = 4
D_EXPERT = 512
DEPTH = 2
ALPHA = (2 * DEPTH) ** 0.25
NORM_EPS = 1e-6
NEG_BIG = -1e30
SCALE = HEAD_DIM ** -0.5
LOG2_E = math.log2(math.e)
_Q_HEAD_ORDER = tuple(6 * p + 3 * half + t for p in range(2) for t in range(3) for half in range(2))

LANES = 128
MEM_W = MEM_HEADS * HEAD_DIM
TOK_TILE = 512
TILES_PER_SEQ = SEQ // TOK_TILE
GQA_KC = 256
BAND_TQ = 128
BAND_TILES = 4

PAIRS = ((0, 1), (0, 2), (0, 3), (1, 2), (1, 3), (2, 3))
N_CLASSES = N_EXPERT_GROUPS * len(PAIRS)
CLASS_ROWS = 32
MOE_TM = 256
MOE_ROWS = N_TOK + N_CLASSES * MOE_TM
MOE_TILES = MOE_ROWS // MOE_TM
XA_W = D_MODEL + LANES
ROW_DMA_UNROLL = 8

VMEM_LIMIT = 56 * 1024 * 1024


def _params(sem, vmem=VMEM_LIMIT):
    return pltpu.CompilerParams(dimension_semantics=sem, vmem_limit_bytes=vmem)


def _dot(a, b):
    return jnp.dot(a, b, preferred_element_type=F32)


def _dot_nt(a, b):
    return lax.dot_general(a, b, (((1,), (1,)), ((), ())), preferred_element_type=F32)


def _split_bf16(x):
    hi = x.astype(BF16)
    lo = (x - hi.astype(F32)).astype(BF16)
    return hi, lo


def _layer_norm(z, g, b):
    mu = jnp.mean(z, axis=-1, keepdims=True)
    zc = z - mu
    var = jnp.mean(zc * zc, axis=-1, keepdims=True)
    return zc * lax.rsqrt(var + NORM_EPS) * g + b


def _rope(z, c, sa, sb, shift):
    return z * c + pltpu.roll(z, LANES - shift, 1) * sa + pltpu.roll(z, shift, 1) * sb


def _first_head(shape):
    return lax.broadcasted_iota(jnp.int32, shape, 1) < HEAD_DIM


def _pair_attention(q2, k2, v3, first, valid=None, base2=False):
    res = []
    for pick in (first, jnp.logical_not(first)):
        s = _dot_nt(jnp.where(pick, q2, jnp.zeros_like(q2)), k2)
        if valid is not None:
            s = jnp.where(valid, s, NEG_BIG)
        m = jnp.max(s, axis=1, keepdims=True)
        p = jnp.exp2(s - m) if base2 else jnp.exp(s - m)
        res.append((_dot(p.astype(BF16), v3), m))
    return res


def _pair_output(res, first):
    (ol_a, _), (ol_b, _) = res
    return jnp.where(first, ol_a[:, :LANES], ol_b[:, :LANES]) / jnp.where(first, ol_a[:, LANES:], ol_b[:, LANES:])


def _with_ones(v2):
    return jnp.concatenate([v2, jnp.ones_like(v2)], axis=1)


def _proj_kernel(x_ref, w_ref, o_ref):
    o_ref[...] = _dot(x_ref[...].astype(BF16), w_ref[...]).astype(o_ref.dtype)


def _proj(x, w, tm=TOK_TILE):
    n, k = x.shape
    width = w.shape[1]
    return pl.pallas_call(
        _proj_kernel,
        grid=(n // tm,),
        in_specs=[pl.BlockSpec((tm, k), lambda i: (i, 0)), pl.BlockSpec((k, width), lambda i: (0, 0))],
        out_specs=pl.BlockSpec((tm, width), lambda i: (i, 0)),
        out_shape=jax.ShapeDtypeStruct((n, width), BF16),
        compiler_params=_params(("parallel",)),
        name="proj",
    )(x, w)


def _a_in_kernel(x_ref, w_ref, m_ref, qg_ref, kg_ref, c_ref, sa_ref, sb_ref, q_ref, k_ref, v_ref, qm_ref):
    y = _dot(x_ref[...].astype(BF16), w_ref[...])
    m = m_ref[...]
    c, sa, sb = c_ref[...], sa_ref[...], sb_ref[...]

    def norm_rope(z, gain):
        hi, lo = _split_bf16(z * z)
        ms = (_dot(hi, m) + _dot(lo, m)) * (1.0 / HEAD_DIM)
        z = z * lax.rsqrt(ms + NORM_EPS) * gain
        return _rope(z, c, sa, sb, 16)

    nq = A_Q_HEADS * HEAD_DIM // LANES
    nk = A_KV_HEADS * HEAD_DIM // LANES
    for ch in range(nq):
        q_ref[:, ch * LANES:(ch + 1) * LANES] = norm_rope(y[:, ch * LANES:(ch + 1) * LANES], qg_ref[...]).astype(BF16)
    off = nq * LANES
    for ch in range(nk):
        z = y[:, off + ch * LANES: off + (ch + 1) * LANES]
        k_ref[:, ch * LANES:(ch + 1) * LANES] = norm_rope(z, kg_ref[...]).astype(BF16)
    off += nk * LANES
    ones = jnp.ones((y.shape[0], LANES), BF16)
    for ch in range(nk):
        v_ref[:, 2 * ch * LANES:(2 * ch + 1) * LANES] = y[:, off + ch * LANES: off + (ch + 1) * LANES].astype(BF16)
        v_ref[:, (2 * ch + 1) * LANES:(2 * ch + 2) * LANES] = ones
    off += nk * LANES
    qm_ref[...] = (y[:, off:off + MEM_W] * SCALE).astype(BF16)


def _a_in_proj(x, w, m128, qg, kg, tabs, tm=TOK_TILE):
    c, sa, sb = tabs
    nblk = SEQ // tm
    a_in = w.shape[1]
    tab_spec = pl.BlockSpec((tm, LANES), lambda i: (i % nblk, 0))
    row = lambda width: pl.BlockSpec((tm, width), lambda i: (i, 0))
    const = lambda shape: pl.BlockSpec(shape, lambda i: (0, 0))
    qw, kw = A_Q_HEADS * HEAD_DIM, A_KV_HEADS * HEAD_DIM
    return pl.pallas_call(
        _a_in_kernel,
        grid=(N_TOK // tm,),
        in_specs=[row(D_MODEL), const((D_MODEL, a_in)), const((LANES, LANES)), const((1, LANES)), const((1, LANES)),
                  tab_spec, tab_spec, tab_spec],
        out_specs=[row(qw), row(kw), row(2 * kw), row(MEM_W)],
        out_shape=[jax.ShapeDtypeStruct((N_TOK, qw), BF16), jax.ShapeDtypeStruct((N_TOK, kw), BF16),
                   jax.ShapeDtypeStruct((N_TOK, 2 * kw), BF16), jax.ShapeDtypeStruct((N_TOK, MEM_W), BF16)],
        compiler_params=_params(("parallel",)),
        name="a_in_proj",
    )(x, w, m128, qg, kg, c, sa, sb)


def _b_in_kernel(*refs):
    ng = len(DILATIONS)
    nx = D_MODEL // LANES
    x_refs, w_ref, refs = refs[:nx], refs[nx], refs[nx + 1:]
    tabs, outs, qm_ref = refs[:3 * ng], refs[3 * ng:6 * ng], refs[6 * ng]
    xb1 = None
    for g, d in enumerate(DILATIONS):
        n = TOK_TILE // d
        if d == 1:
            cols = [r[...] for r in x_refs]
        else:
            cols = [jnp.concatenate([r[pl.ds(k, n, stride=d), :] for k in range(d)], axis=0) for r in x_refs]
        xb = jnp.concatenate(cols, axis=1).astype(BF16)
        xb1 = xb if d == 1 else xb1
        c, sa, sb = (t[...].reshape(TOK_TILE, LANES) for t in tabs[3 * g:3 * g + 3])
        q_ref, k_ref, v_ref = outs[3 * g:3 * g + 3]
        yq = _dot(xb, w_ref[:, g * B_WIDTH:(g + 1) * B_WIDTH]) * SCALE
        yk = _dot(xb, w_ref[:, (ng + g) * B_WIDTH:(ng + g + 1) * B_WIDTH])
        yv = _dot(xb, w_ref[:, (2 * ng + g) * B_WIDTH:(2 * ng + g + 1) * B_WIDTH])
        for ch in range(B_WIDTH // LANES):
            sl = slice(ch * LANES, (ch + 1) * LANES)
            q_ref[0, :, :, sl] = _rope(yq[:, sl], c, sa, sb, PARTIAL_ROT_DIMS // 2).reshape(d, n, LANES).astype(BF16)
            k_ref[0, :, :, sl] = _rope(yk[:, sl], c, sa, sb, PARTIAL_ROT_DIMS // 2).reshape(d, n, LANES).astype(BF16)
        v_ref[0] = yv.reshape(d, n, B_WIDTH).astype(BF16)
    qm = _dot(xb1, w_ref[:, 3 * ng * B_WIDTH:3 * ng * B_WIDTH + MEM_W])
    qm_ref[...] = (qm * SCALE).astype(BF16)


def _b_in_proj(x, w, tabs):
    nx = D_MODEL // LANES
    in_specs = [pl.BlockSpec((TOK_TILE, LANES), functools.partial(lambda i, c: (i, c), c=c)) for c in range(nx)]
    in_specs.append(pl.BlockSpec(w.shape, lambda i: (0, 0)))
    out_specs, out_shape = [], []
    for d in DILATIONS:
        n = TOK_TILE // d
        in_specs += [pl.BlockSpec((d, n, LANES), lambda i: (0, i % TILES_PER_SEQ, 0))] * 3
        out_specs += [pl.BlockSpec((1, d, n, B_WIDTH), lambda i: (i // TILES_PER_SEQ, 0, i % TILES_PER_SEQ, 0))] * 3
        out_shape += [jax.ShapeDtypeStruct((BATCH, d, SEQ // d, B_WIDTH), BF16)] * 3
    out_specs.append(pl.BlockSpec((TOK_TILE, MEM_W), lambda i: (i, 0)))
    out_shape.append(jax.ShapeDtypeStruct((N_TOK, MEM_W), BF16))
    res = pl.pallas_call(
        _b_in_kernel,
        grid=(N_TOK // TOK_TILE,),
        in_specs=in_specs,
        out_specs=out_specs,
        out_shape=out_shape,
        compiler_params=_params(("parallel",)),
        name="b_in_proj",
    )(*([x] * nx), w, *[t for group in tabs for t in group])
    return [res[3 * g:3 * g + 3] for g in range(len(DILATIONS))], res[-1]


def _gqa_kernel(q_ref, k_ref, v_ref, o_ref):
    tq = q_ref.shape[0]
    ntile = q_ref.shape[1] // LANES
    q = jnp.concatenate([q_ref[:, t * LANES:(t + 1) * LANES] for t in range(ntile)], axis=0)
    first = _first_head(q.shape)
    res = []
    for pick in (first, jnp.logical_not(first)):
        qh = jnp.where(pick, q, jnp.zeros_like(q))
        m = jnp.full((q.shape[0], 1), -jnp.inf, F32)
        acc = jnp.zeros((q.shape[0], 2 * LANES), F32)
        for c in range(SEQ // GQA_KC):
            keys = slice(c * GQA_KC, (c + 1) * GQA_KC)
            s = _dot_nt(qh, k_ref[keys, :])
            m_new = jnp.maximum(m, jnp.max(s, axis=1, keepdims=True))
            p = jnp.exp2(s - m_new).astype(BF16)
            acc = acc * jnp.exp2(m - m_new) + _dot(p, v_ref[keys, :])
            m = m_new
        res.append((acc, m))
    o = _pair_output(res, first)
    for t in range(ntile):
        o_ref[:, t * LANES:(t + 1) * LANES] = o[t * tq:(t + 1) * tq].astype(BF16)


def _gqa_attention(q, k, v, tq=512):
    nq = SEQ // tq
    qw = q.shape[1] // 2
    kw = k.shape[1] // 2
    return pl.pallas_call(
        _gqa_kernel,
        grid=(BATCH, 2, nq),
        in_specs=[pl.BlockSpec((tq, qw), lambda b, p, i: (b * nq + i, p)),
                  pl.BlockSpec((SEQ, kw), lambda b, p, i: (b, p)),
                  pl.BlockSpec((SEQ, 2 * kw), lambda b, p, i: (b, p))],
        out_specs=pl.BlockSpec((tq, qw), lambda b, p, i: (b * nq + i, p)),
        out_shape=jax.ShapeDtypeStruct(q.shape, BF16),
        compiler_params=_params(("parallel", "parallel", "parallel")),
        name="gqa_attention",
    )(q, k, v)


def _band_kernel(q_ref, k_ref, v_ref, o_ref, lse_ref, *, length, seg, win):
    tq = BAND_TQ
    first = _first_head((tq, LANES))
    lane = lax.broadcasted_iota(jnp.int32, (tq, LANES), 1)
    row = lax.broadcasted_iota(jnp.int32, (tq, win), 0)
    col = lax.broadcasted_iota(jnp.int32, (tq, win), 1)
    for u in range(BAND_TILES):
        tile = pl.program_id(1) * BAND_TILES + u
        start = pl.multiple_of(jnp.clip(tile * tq - BAND_RADIUS, 0, length - win), BAND_RADIUS)
        qpos, kpos = tile * tq + row, start + col
        valid = jnp.abs(qpos - kpos) <= BAND_RADIUS
        if seg < length:
            valid = jnp.logical_and(valid, qpos // seg == kpos // seg)
        rows = slice(u * tq, (u + 1) * tq)
        lse = jnp.zeros((tq, LANES), F32)
        for j in range(B_WIDTH // LANES):
            sl = slice(j * LANES, (j + 1) * LANES)
            res = _pair_attention(q_ref[0, rows, sl], k_ref[0, pl.ds(start, win), sl],
                                  _with_ones(v_ref[0, pl.ds(start, win), sl]), first, valid)
            o_ref[0, rows, sl] = _pair_output(res, first).astype(BF16)
            for half, (ol, m) in enumerate(res):
                lse = jnp.where(lane == 2 * j + half, m + jnp.log(ol[:, LANES:]), lse)
        lse_ref[0, rows, :] = lse


def _band_attention(q, k, v, d):
    seg = SEQ // d
    step_rows = BAND_TILES * BAND_TQ
    length = max(seg, step_rows)
    win = BAND_TQ + 2 * BAND_RADIUS
    nblk = N_TOK // length
    q, k, v = (t.reshape(nblk, length, B_WIDTH) for t in (q, k, v))
    seq_spec = pl.BlockSpec((1, length, B_WIDTH), lambda s, i: (s, 0, 0))
    o, lse = pl.pallas_call(
        functools.partial(_band_kernel, length=length, seg=seg, win=win),
        grid=(nblk, length // step_rows),
        in_specs=[pl.BlockSpec((1, step_rows, B_WIDTH), lambda s, i: (s, i, 0)), seq_spec, seq_spec],
        out_specs=[pl.BlockSpec((1, step_rows, B_WIDTH), lambda s, i: (s, i, 0)),
                   pl.BlockSpec((1, step_rows, LANES), lambda s, i: (s, i, 0))],
        out_shape=[jax.ShapeDtypeStruct((nblk, length, B_WIDTH), BF16),
                   jax.ShapeDtypeStruct((nblk, length, LANES), F32)],
        compiler_params=_params(("parallel", "parallel")),
        name=f"band_attention_d{d}",
    )(q, k, v)
    return o.reshape(BATCH, d, seg, B_WIDTH), lse.reshape(BATCH, d, seg, LANES)


def _route(x, wh_ref, wl_ref, rb_ref, tri_ref, xa_ref, cls_ref, rank_ref, cnt_ref, carry_ref):
    tm = x.shape[0]
    xh, xl = _split_bf16(x)
    logits = _dot_nt(wh_ref[...], xh) + _dot_nt(wh_ref[...], xl) + _dot_nt(wl_ref[...], xh)
    aff = 1.0 / (1.0 + jnp.exp(-logits))
    sel = aff + rb_ref[...]
    s = [sel[e:e + 1, :] for e in range(N_EXPERTS)]
    a = [aff[e:e + 1, :] for e in range(N_EXPERTS)]

    def top2_sum(v):
        hi01, lo01 = jnp.maximum(v[0], v[1]), jnp.minimum(v[0], v[1])
        hi23, lo23 = jnp.maximum(v[2], v[3]), jnp.minimum(v[2], v[3])
        return jnp.maximum(hi01, hi23) + jnp.maximum(jnp.minimum(hi01, hi23), jnp.maximum(lo01, lo23))

    gscore = [top2_sum(s[EXPERTS_PER_GROUP * g:EXPERTS_PER_GROUP * (g + 1)]) for g in range(N_EXPERT_GROUPS)]
    best = jnp.zeros((1, tm), jnp.int32)
    best_score = gscore[0]
    for g in range(1, N_EXPERT_GROUPS):
        better = gscore[g] > best_score
        best = jnp.where(better, g, best)
        best_score = jnp.where(better, gscore[g], best_score)

    def pick(rows, j):
        out = rows[j]
        for g in range(1, N_EXPERT_GROUPS):
            out = jnp.where(best == g, rows[EXPERTS_PER_GROUP * g + j], out)
        return out

    t = [pick(s, j) for j in range(EXPERTS_PER_GROUP)]
    w = [pick(a, j) for j in range(EXPERTS_PER_GROUP)]

    def first_max(v):
        mx = jnp.maximum(jnp.maximum(v[0], v[1]), jnp.maximum(v[2], v[3]))
        idx = jnp.full((1, tm), EXPERTS_PER_GROUP - 1, jnp.int32)
        for j in range(EXPERTS_PER_GROUP - 2, -1, -1):
            idx = jnp.where(v[j] == mx, j, idx)
        return idx

    i1 = first_max(t)
    i2 = first_max([jnp.where(i1 == j, -jnp.inf, t[j]) for j in range(EXPERTS_PER_GROUP)])
    lo, hi = jnp.minimum(i1, i2), jnp.maximum(i1, i2)

    def take(rows, idx):
        out = rows[0]
        for j in range(1, EXPERTS_PER_GROUP):
            out = jnp.where(idx == j, rows[j], out)
        return out

    w_lo, w_hi = take(w, lo), take(w, hi)
    den = w_lo + w_hi
    pair = jnp.where(lo == 0, hi - 1, jnp.where(lo == 1, hi + 1, len(PAIRS) - 1))
    cls = best * len(PAIRS) + pair

    onehot = (lax.broadcasted_iota(jnp.int32, (CLASS_ROWS, tm), 0) == cls).astype(F32)
    before = _dot(onehot.astype(BF16), tri_ref[...]) + carry_ref[:, 0:1]
    rank = jnp.sum(onehot * before, axis=0, keepdims=True)
    carry_ref[...] = carry_ref[...] + jnp.sum(onehot, axis=1, keepdims=True)

    cls_ref[0] = cls
    rank_ref[0] = rank.astype(jnp.int32)
    cnt_ref[...] = carry_ref[...]
    gates = jnp.concatenate([w_lo / den, w_hi / den, jnp.zeros((LANES - 2, tm), F32)], axis=0)
    xa_ref[:, :D_MODEL] = x
    xa_ref[:, D_MODEL:] = gates.T


def _post_attn_kernel(*refs, merge):
    if merge:
        o_refs, l_refs, e_ref, refs = refs[:3], refs[3:6], refs[6], refs[7:]
    else:
        mix_ref, refs = refs[0], refs[1:]
    (qm_ref, mk_ref, mv_ref, w1_ref, w2_ref, x_ref, g_ref, b_ref, wh_ref, wl_ref, rb_ref, tri_ref,
     xa_ref, cls_ref, rank_ref, cnt_ref, carry_ref) = refs[:17]

    @pl.when(pl.program_id(0) == 0)
    def _():
        carry_ref[...] = jnp.zeros_like(carry_ref)

    if merge:
        ot_ref, lt_ref = refs[17], refs[18]
        nch = B_WIDTH // LANES
        for g, d in enumerate(DILATIONS):
            n = TOK_TILE // d
            for r in range(d):
                rows = pl.ds(r, n, stride=d) if d > 1 else slice(None)
                og = o_refs[g][0, r].astype(F32)
                for ch in range(nch):
                    ot_ref[g, ch, rows, :] = og[:, ch * LANES:(ch + 1) * LANES]
                lt_ref[g, rows, :] = l_refs[g][0, r]
        lses = [lt_ref[g] for g in range(len(DILATIONS))]
        m = functools.reduce(jnp.maximum, lses)
        es = [jnp.exp(l - m) for l in lses]
        inv = 1.0 / functools.reduce(jnp.add, es)
        expand = e_ref[...]

        def widen(w):
            hi, lo = _split_bf16(w)
            return _dot(hi, expand) + _dot(lo, expand)

        ws = [widen(e * inv) for e in es]
        mix = jnp.concatenate(
            [functools.reduce(jnp.add, [w[:, ch * LANES:(ch + 1) * LANES] * ot_ref[g, ch] for g, w in enumerate(ws)])
             for ch in range(nch)], axis=1).astype(BF16)
    else:
        mix = mix_ref[...]

    first = _first_head((TOK_TILE, LANES))
    memo = []
    for j in range(MEM_W // LANES):
        sl = slice(j * LANES, (j + 1) * LANES)
        memo.append(_pair_output(_pair_attention(qm_ref[:, sl], mk_ref[:, sl], _with_ones(mv_ref[:, sl]), first), first))
    memo = jnp.concatenate(memo, axis=1).astype(BF16)

    attn = _dot(mix, w1_ref[...]) + _dot(memo, w2_ref[...])
    x1 = _layer_norm(ALPHA * x_ref[...] + attn, g_ref[...], b_ref[...])
    _route(x1, wh_ref, wl_ref, rb_ref, tri_ref, xa_ref, cls_ref, rank_ref, cnt_ref, carry_ref)


def _post_attn(mix, qm, mkv, w1, w2, x, g, b, router, expand=None):
    merge = expand is not None
    wh, wl, rb, tri = router
    nblk = N_TOK // TOK_TILE
    row = lambda width: pl.BlockSpec((TOK_TILE, width), lambda i: (i, 0))
    const = lambda shape: pl.BlockSpec(shape, lambda i: (0,) * len(shape))
    row3 = pl.BlockSpec((1, 1, TOK_TILE), lambda i: (i, 0, 0))
    scratch = [pltpu.VMEM((CLASS_ROWS, LANES), F32)]
    if merge:
        outs, lses = mix
        resid = lambda d, width: pl.BlockSpec((1, d, TOK_TILE // d, width),
                                              lambda i: (i // TILES_PER_SEQ, 0, i % TILES_PER_SEQ, 0))
        lead_specs = ([resid(d, B_WIDTH) for d in DILATIONS] + [resid(d, LANES) for d in DILATIONS]
                      + [const(expand.shape)])
        lead = [*outs, *lses, expand]
        scratch += [pltpu.VMEM((len(DILATIONS), B_WIDTH // LANES, TOK_TILE, LANES), F32),
                    pltpu.VMEM((len(DILATIONS), TOK_TILE, LANES), F32)]
    else:
        lead_specs, lead = [row(mix.shape[1])], [mix]
    xa, cls, rank, cnt = pl.pallas_call(
        functools.partial(_post_attn_kernel, merge=merge),
        grid=(nblk,),
        in_specs=lead_specs + [row(MEM_W),
                               pl.BlockSpec((N_MEM, MEM_W), lambda i: (i // TILES_PER_SEQ, 0)),
                               pl.BlockSpec((N_MEM, MEM_W), lambda i: (i // TILES_PER_SEQ, 1)),
                               const(w1.shape), const(w2.shape), row(D_MODEL), const((1, D_MODEL)), const((1, D_MODEL)),
                               const(wh.shape), const(wl.shape), const(rb.shape), const(tri.shape)],
        out_specs=[row(XA_W), row3, row3, const((CLASS_ROWS, LANES))],
        out_shape=[jax.ShapeDtypeStruct((N_TOK, XA_W), F32), jax.ShapeDtypeStruct((nblk, 1, TOK_TILE), jnp.int32),
                   jax.ShapeDtypeStruct((nblk, 1, TOK_TILE), jnp.int32), jax.ShapeDtypeStruct((CLASS_ROWS, LANES), F32)],
        scratch_shapes=scratch,
        compiler_params=_params(("arbitrary",)),
        name="post_attn_merge" if merge else "post_attn",
    )(*lead, qm, mkv, mkv, w1, w2, x, g, b, wh, wl, rb, tri)
    return xa, cls.reshape(N_TOK), rank.reshape(N_TOK), cnt[:N_CLASSES, 0].astype(jnp.int32)


def _dispatch_kernel(pos_ref, ends_ref, xa_ref, xs_ref, zeros, sem, zsem, *, chunk):
    base = pl.program_id(0) * chunk

    @pl.when(pl.program_id(0) == 0)
    def _():
        zeros[...] = jnp.zeros_like(zeros)

        def tail(c):
            start = pl.multiple_of(ends_ref[c] - MOE_TM, MOE_TM)
            return pltpu.make_async_copy(zeros, xs_ref.at[pl.ds(start, MOE_TM)], zsem)

        def nonempty(c):
            return ends_ref[c] > (ends_ref[c - 1] if c else 0)

        def spare(t):
            return pltpu.make_async_copy(zeros, xs_ref.at[pl.ds(pl.multiple_of(t * MOE_TM, MOE_TM), MOE_TM)], zsem)

        first_spare = ends_ref[N_CLASSES - 1] // MOE_TM
        for c in range(N_CLASSES):
            pl.when(nonempty(c))(lambda c=c: tail(c).start())
        lax.fori_loop(first_spare, MOE_TILES, lambda t, carry: (spare(t).start(), carry)[1], 0)
        for c in range(N_CLASSES):
            pl.when(nonempty(c))(lambda c=c: tail(c).wait())
        lax.fori_loop(first_spare, MOE_TILES, lambda t, carry: (spare(t).wait(), carry)[1], 0)

    def issue(blk, carry):
        for u in range(ROW_DMA_UNROLL):
            r = blk * ROW_DMA_UNROLL + u
            pltpu.make_async_copy(xa_ref.at[pl.ds(r, 1)], xs_ref.at[pl.ds(pos_ref[base + r], 1)], sem).start(
                priority=u % 2)
        return carry

    def retire(r, carry):
        pltpu.make_async_copy(xa_ref.at[pl.ds(0, 1)], xs_ref.at[pl.ds(0, 1)], sem).wait()
        return carry

    lax.fori_loop(0, chunk // ROW_DMA_UNROLL, issue, 0)
    lax.fori_loop(0, chunk, retire, 0, unroll=8)


def _dispatch(pos, ends, xa, chunk=512):
    return pl.pallas_call(
        functools.partial(_dispatch_kernel, chunk=chunk),
        grid_spec=pltpu.PrefetchScalarGridSpec(
            num_scalar_prefetch=2,
            grid=(N_TOK // chunk,),
            in_specs=[pl.BlockSpec((chunk, XA_W), lambda i, p, e: (i, 0))],
            out_specs=pl.BlockSpec(memory_space=pl.ANY),
            scratch_shapes=[pltpu.VMEM((MOE_TM, XA_W), F32), pltpu.SemaphoreType.DMA(()),
                            pltpu.SemaphoreType.DMA(())],
        ),
        out_shape=jax.ShapeDtypeStruct((MOE_ROWS, XA_W), F32),
        compiler_params=_params(("arbitrary",)),
        name="moe_dispatch",
    )(pos, ends, xa)


def _expert_kernel(slot1_ref, slot2_ref, used_ref, first_ref, grp_ref, next_ref, xs_ref, wg_hbm, wu_hbm, wd_hbm,
                   ys_ref, stage_g, stage_u, stage_d, wg_s, wu_s, wd_s, sem, *, layer):
    i = pl.program_id(0)
    streams = ((wg_hbm, stage_g, wg_s), (wu_hbm, stage_u, wu_s), (wd_hbm, stage_d, wd_s))

    def group_copy(group, k):
        first_expert = layer * N_EXPERTS + group * EXPERTS_PER_GROUP
        hbm, stage, _ = streams[k]
        return pltpu.make_async_copy(hbm.at[pl.ds(first_expert, EXPERTS_PER_GROUP)], stage, sem.at[k])

    @pl.when(i == 0)
    def _():
        for k in range(len(streams)):
            group_copy(grp_ref[0], k).start()

    @pl.when(first_ref[i] != 0)
    def _():
        for k, (_, stage, dst) in enumerate(streams):
            group_copy(grp_ref[i], k).wait()
            for e in range(EXPERTS_PER_GROUP):
                dst[e] = stage[e].astype(BF16)

        @pl.when(next_ref[i] >= 0)
        def _():
            for k in range(len(streams)):
                group_copy(next_ref[i], k).start()

    @pl.when(used_ref[i] != 0)
    def _():
        x = xs_ref[:, :D_MODEL].astype(BF16)
        gates = xs_ref[:, D_MODEL:]

        def expert(slot):
            gate = _dot(x, wg_s[slot])
            h = gate * (1.0 / (1.0 + jnp.exp(-gate))) * _dot(x, wu_s[slot])
            return _dot(h.astype(BF16), wd_s[slot])

        ys_ref[...] = gates[:, 0:1] * expert(slot1_ref[i]) + gates[:, 1:2] * expert(slot2_ref[i])

    @pl.when(used_ref[i] == 0)
    def _():
        ys_ref[...] = jnp.zeros_like(ys_ref)


def _experts(tile_meta, xs, layer, wg, wu, wd):
    nmeta = len(tile_meta)
    up_shape = (EXPERTS_PER_GROUP, D_MODEL, D_EXPERT)
    down_shape = (EXPERTS_PER_GROUP, D_EXPERT, D_MODEL)
    return pl.pallas_call(
        functools.partial(_expert_kernel, layer=layer),
        grid_spec=pltpu.PrefetchScalarGridSpec(
            num_scalar_prefetch=nmeta,
            grid=(MOE_TILES,),
            in_specs=[pl.BlockSpec((MOE_TM, XA_W), lambda i, *_: (i, 0)),
                      pl.BlockSpec(memory_space=pl.ANY), pl.BlockSpec(memory_space=pl.ANY),
                      pl.BlockSpec(memory_space=pl.ANY)],
            out_specs=pl.BlockSpec((MOE_TM, D_MODEL), lambda i, *_: (i, 0)),
            scratch_shapes=[pltpu.VMEM(up_shape, F32), pltpu.VMEM(up_shape, F32), pltpu.VMEM(down_shape, F32),
                            pltpu.VMEM(up_shape, BF16), pltpu.VMEM(up_shape, BF16), pltpu.VMEM(down_shape, BF16),
                            pltpu.SemaphoreType.DMA((3,))],
        ),
        out_shape=jax.ShapeDtypeStruct((MOE_ROWS, D_MODEL), F32),
        compiler_params=_params(("arbitrary",)),
        name="moe_experts",
    )(*tile_meta, xs, wg, wu, wd)


def _combine_kernel(pos_ref, ys_ref, x_ref, g_ref, b_ref, o_ref, buf, sem, *, tc):
    i = pl.program_id(0)
    slot = i % 2

    def gather(step, into):
        def issue(blk, carry):
            for u in range(ROW_DMA_UNROLL):
                r = blk * ROW_DMA_UNROLL + u
                row = pos_ref[step * tc + r]
                pltpu.make_async_copy(ys_ref.at[pl.ds(row, 1)], buf.at[into, pl.ds(r, 1)], sem.at[into]).start(
                    priority=u % 2)
            return carry

        lax.fori_loop(0, tc // ROW_DMA_UNROLL, issue, 0)

    @pl.when(i == 0)
    def _():
        gather(0, 0)

    @pl.when(i + 1 < pl.num_programs(0))
    def _():
        gather(i + 1, 1 - slot)

    def retire(r, carry):
        pltpu.make_async_copy(ys_ref.at[pl.ds(0, 1)], buf.at[slot, pl.ds(0, 1)], sem.at[slot]).wait()
        return carry

    lax.fori_loop(0, tc, retire, 0, unroll=8)
    o_ref[...] = _layer_norm(ALPHA * x_ref[...] + buf[slot], g_ref[...], b_ref[...])


def _combine_ln(pos, ys, xa, g, b, tc=256):
    return pl.pallas_call(
        functools.partial(_combine_kernel, tc=tc),
        grid_spec=pltpu.PrefetchScalarGridSpec(
            num_scalar_prefetch=1,
            grid=(N_TOK // tc,),
            in_specs=[pl.BlockSpec(memory_space=pl.ANY),
                      pl.BlockSpec((tc, D_MODEL), lambda i, p: (i, 0)),
                      pl.BlockSpec((1, D_MODEL), lambda i, p: (0, 0)),
                      pl.BlockSpec((1, D_MODEL), lambda i, p: (0, 0))],
            out_specs=pl.BlockSpec((tc, D_MODEL), lambda i, p: (i, 0)),
            scratch_shapes=[pltpu.VMEM((2, tc, D_MODEL), F32), pltpu.SemaphoreType.DMA((2,))],
        ),
        out_shape=jax.ShapeDtypeStruct((N_TOK, D_MODEL), F32),
        compiler_params=_params(("arbitrary",)),
        name="moe_combine_ln",
    )(pos, ys, xa, g, b)


def _moe_ln(routed, layer, wg, wu, wd, g, b):
    xa, cls, rank, counts = routed
    padded = (counts + MOE_TM - 1) // MOE_TM * MOE_TM
    ends = jnp.cumsum(padded)
    starts = ends - padded
    pos = starts[cls] + rank
    tile_start = jnp.arange(MOE_TILES, dtype=jnp.int32) * MOE_TM
    tile_used = (tile_start < ends[-1]).astype(jnp.int32)
    last_cls = jnp.max(jnp.where(counts > 0, jnp.arange(N_CLASSES), 0))
    tile_cls = jnp.minimum(jnp.sum(tile_start[:, None] >= ends[None, :], axis=1), last_cls)
    tile_cls = tile_cls.astype(jnp.int32)
    tile_grp = tile_cls // len(PAIRS)
    prev_grp = jnp.concatenate([jnp.full((1,), -1, jnp.int32), tile_grp[:-1]])
    tile_first = tile_used * (tile_grp != prev_grp).astype(jnp.int32)
    has = jnp.sum(counts.reshape(N_EXPERT_GROUPS, len(PAIRS)), axis=1) > 0
    gid = jnp.arange(N_EXPERT_GROUPS, dtype=jnp.int32)
    later = jnp.where(jnp.logical_and(has[None, :], gid[None, :] > gid[:, None]), gid[None, :], N_EXPERT_GROUPS)
    next_grp = jnp.min(later, axis=1)
    next_grp = jnp.where(next_grp == N_EXPERT_GROUPS, -1, next_grp)
    pair_slots = jnp.asarray(np.array(PAIRS, np.int32))[tile_cls % len(PAIRS)]
    tile_meta = (pair_slots[:, 0], pair_slots[:, 1], tile_used, tile_first, tile_grp, next_grp[tile_grp])
    xs = _dispatch(pos, ends.astype(jnp.int32), xa)
    ys = _experts(tile_meta, xs, layer, wg, wu, wd)
    return _combine_ln(pos, ys, xa, g, b)


def _tables(c, sa, sb):
    return tuple(np.concatenate([t, t], axis=-1).astype(np.float32) for t in (c, sa, sb))


def _axial_tables():
    rows = SEQ // GRID_W
    row = np.repeat(np.arange(rows), GRID_W).astype(np.float64)
    col = np.tile(np.arange(GRID_W), rows).astype(np.float64)
    half = HEAD_DIM // 2
    inv = 1.0 / (AXIAL_THETA ** (np.arange(0, half, 2, dtype=np.float64) / half))
    ar, ac = row[:, None] * inv, col[:, None] * inv
    z = np.zeros_like(ar)
    c = np.concatenate([np.cos(ar), np.cos(ar), np.cos(ac), np.cos(ac)], -1)
    sa = np.concatenate([-np.sin(ar), z, -np.sin(ac), z], -1)
    sb = np.concatenate([z, np.sin(ar), z, np.sin(ac)], -1)
    return _tables(c, sa, sb)


def _partial_tables():
    n = PARTIAL_ROT_DIMS
    pos = np.arange(SEQ, dtype=np.float64)
    ang = pos[:, None] * (1.0 / (PARTIAL_THETA ** (np.arange(0, n, 2, dtype=np.float64) / n)))
    z = np.zeros_like(ang)
    rest = np.zeros((SEQ, HEAD_DIM - n))
    c = np.concatenate([np.cos(ang), np.cos(ang), rest + 1.0], -1)
    sa = np.concatenate([-np.sin(ang), z, rest], -1)
    sb = np.concatenate([z, np.sin(ang), rest], -1)
    return _tables(c, sa, sb)


def _residue_order(tab, d):
    return np.ascontiguousarray(tab.reshape(SEQ // d, d, LANES).transpose(1, 0, 2))


def kernel(x, mem, w_mem_kv, router_w, router_b, a_w_in, a_w_out, a_q_norm, a_k_norm, b_w_in, b_w_out,
           ln1_g, ln1_b, ln2_g, ln2_b, w_gate, w_up, w_down):
    x = x.reshape(N_TOK, D_MODEL)
    row = lambda v: v.reshape(1, -1)

    lane = np.arange(LANES)
    m128 = jnp.asarray((lane[:, None] // HEAD_DIM == lane[None, :] // HEAD_DIM), BF16)
    expand = jnp.asarray(lane[:, None] == np.arange(B_WIDTH)[None, :] // HEAD_DIM, BF16)
    tri = jnp.asarray(np.triu(np.ones((TOK_TILE, TOK_TILE), np.float32), 1), BF16)
    rw_hi = router_w.T.astype(BF16)
    rw_lo = (router_w.T - rw_hi.astype(F32)).astype(BF16)
    router = (rw_hi, rw_lo, router_b.reshape(N_EXPERTS, 1), tri)

    mkv = _proj(mem.reshape(BATCH * N_MEM, D_MODEL), w_mem_kv.astype(BF16))
    experts = tuple(w.reshape(DEPTH * N_EXPERTS, *w.shape[2:]) for w in (w_gate, w_up, w_down))

    kmix = A_Q_HEADS * HEAD_DIM
    qcols = (np.asarray(_Q_HEAD_ORDER)[:, None] * HEAD_DIM + np.arange(HEAD_DIM)[None, :]).reshape(-1)
    w_in = jnp.concatenate([a_w_in[0][:, qcols], a_w_in[0][:, kmix:]], axis=1).astype(BF16)
    qg = row(jnp.tile(a_q_norm[0], 2) * (SCALE * LOG2_E))
    kg = row(jnp.tile(a_k_norm[0], 2))
    q, k, v, qm = _a_in_proj(x, w_in, m128, qg, kg, _axial_tables())
    mix = _gqa_attention(q, k, v)
    w_out = a_w_out[0].astype(BF16)
    routed = _post_attn(mix, qm, mkv, w_out[qcols], w_out[kmix:], x, row(ln1_g[0]), row(ln1_b[0]), router)
    x = _moe_ln(routed, 0, *experts, row(ln2_g[0]), row(ln2_b[0]))

    ptabs = _partial_tables()
    tabs = [tuple(_residue_order(t, d) for t in ptabs) for d in DILATIONS]
    qkv, qm = _b_in_proj(x, b_w_in[0].astype(BF16), tabs)
    outs, lses = zip(*[_band_attention(*qkv[g], d) for g, d in enumerate(DILATIONS)])
    w_out = b_w_out[0].astype(BF16)
    routed = _post_attn((outs, lses), qm, mkv, w_out[:B_WIDTH], w_out[B_WIDTH:], x, row(ln1_g[1]), row(ln1_b[1]),
                        router, expand)
    x = _moe_ln(routed, 1, *experts, row(ln2_g[1]), row(ln2_b[1]))
    return x.reshape(BATCH, SEQ, D_MODEL)
```

```python
import functools
import math

import numpy as np
import jax
import jax.numpy as jnp
from jax import lax
from jax.experimental import pallas as pl
from jax.experimental.pallas import tpu as pltpu

F32 = jnp.float32
BF16 = jnp.bfloat16

D_MODEL = 1024
BATCH = 8
SEQ = 2048
N_TOK = BATCH * SEQ
HEAD_DIM = 64
GRID_W = 64
N_MEM = 256
MEM_HEADS = 4
A_Q_HEADS = 12
A_KV_HEADS = 4
AXIAL_THETA = 10000.0
B_HEADS = 8
B_WIDTH = B_HEADS * HEAD_DIM
DILATIONS = (1, 4, 16)
BAND_RADIUS = 64
PARTIAL_ROT_DIMS = HEAD_DIM // 4
PARTIAL_THETA = 500000.0
N_EXPERTS = 16
N_EXPERT_GROUPS = 4
EXPERTS_PER_GROUP = 4
D_EXPERT = 512
DEPTH = 2
ALPHA = (2 * DEPTH) ** 0.25
NORM_EPS = 1e-6
NEG_BIG = -1e30
SCALE = HEAD_DIM ** -0.5
LOG2_E = math.log2(math.e)
_Q_HEAD_ORDER = tuple(6 * p + 3 * half + t for p in range(2) for t in range(3) for half in range(2))

LANES = 128
MEM_W = MEM_HEADS * HEAD_DIM
TOK_TILE = 512
TILES_PER_SEQ = SEQ // TOK_TILE
POST_SUB = 2
GQA_KC = 256
BAND_TQ = 128
BAND_TILES = 4

PAIRS = ((0, 1), (0, 2), (0, 3), (1, 2), (1, 3), (2, 3))
N_CLASSES = N_EXPERT_GROUPS * len(PAIRS)
CLASS_ROWS = 32
MOE_TM = 256
MOE_ROWS = N_TOK + N_CLASSES * MOE_TM
MOE_TILES = MOE_ROWS // MOE_TM
XA_W = D_MODEL + LANES
ROW_DMA_UNROLL = 8

VMEM_LIMIT = 56 * 1024 * 1024


def _params(sem, vmem=VMEM_LIMIT):
    return pltpu.CompilerParams(dimension_semantics=sem, vmem_limit_bytes=vmem)


def _dot(a, b):
    return jnp.dot(a, b, preferred_element_type=F32)


def _dot_nt(a, b):
    return lax.dot_general(a, b, (((1,), (1,)), ((), ())), preferred_element_type=F32)


def _split_bf16(x):
    hi = x.astype(BF16)
    lo = (x - hi.astype(F32)).astype(BF16)
    return hi, lo


def _layer_norm(z, g, b):
    mu = jnp.mean(z, axis=-1, keepdims=True)
    zc = z - mu
    var = jnp.mean(zc * zc, axis=-1, keepdims=True)
    return zc * lax.rsqrt(var + NORM_EPS) * g + b


def _rope(z, c, sa, sb, shift):
    return z * c + pltpu.roll(z, LANES - shift, 1) * sa + pltpu.roll(z, shift, 1) * sb


def _first_head(shape):
    return lax.broadcasted_iota(jnp.int32, shape, 1) < HEAD_DIM


def _pair_attention(q2, k2, v3, first, valid=None, base2=False):
    res = []
    for pick in (first, jnp.logical_not(first)):
        s = _dot_nt(jnp.where(pick, q2, jnp.zeros_like(q2)), k2)
        if valid is not None:
            s = jnp.where(valid, s, NEG_BIG)
        m = jnp.max(s, axis=1, keepdims=True)
        p = jnp.exp2(s - m) if base2 else jnp.exp(s - m)
        res.append((_dot(p.astype(BF16), v3), m))
    return res


def _pair_output(res, first):
    (ol_a, _), (ol_b, _) = res
    return jnp.where(first, ol_a[:, :LANES], ol_b[:, :LANES]) / jnp.where(first, ol_a[:, LANES:], ol_b[:, LANES:])


def _with_ones(v2):
    return jnp.concatenate([v2, jnp.ones_like(v2)], axis=1)


def _proj_kernel(x_ref, w_ref, o_ref):
    o_ref[...] = _dot(x_ref[...].astype(BF16), w_ref[...]).astype(o_ref.dtype)


def _proj(x, w, tm=TOK_TILE):
    n, k = x.shape
    width = w.shape[1]
    return pl.pallas_call(
        _proj_kernel,
        grid=(n // tm,),
        in_specs=[pl.BlockSpec((tm, k), lambda i: (i, 0)), pl.BlockSpec((k, width), lambda i: (0, 0))],
        out_specs=pl.BlockSpec((tm, width), lambda i: (i, 0)),
        out_shape=jax.ShapeDtypeStruct((n, width), BF16),
        compiler_params=_params(("parallel",)),
        name="proj",
    )(x, w)


def _a_in_kernel(x_ref, w_ref, m_ref, qg_ref, kg_ref, c_ref, sa_ref, sb_ref, q_ref, k_ref, v_ref, qm_ref):
    y = _dot(x_ref[...].astype(BF16), w_ref[...])
    m = m_ref[...]
    c, sa, sb = c_ref[...], sa_ref[...], sb_ref[...]

    def norm_rope(z, gain):
        hi, lo = _split_bf16(z * z)
        ms = (_dot(hi, m) + _dot(lo, m)) * (1.0 / HEAD_DIM)
        z = z * lax.rsqrt(ms + NORM_EPS) * gain
        return _rope(z, c, sa, sb, 16)

    nq = A_Q_HEADS * HEAD_DIM // LANES
    nk = A_KV_HEADS * HEAD_DIM // LANES
    for ch in range(nq):
        q_ref[:, ch * LANES:(ch + 1) * LANES] = norm_rope(y[:, ch * LANES:(ch + 1) * LANES], qg_ref[...]).astype(BF16)
    off = nq * LANES
    for ch in range(nk):
        z = y[:, off + ch * LANES: off + (ch + 1) * LANES]
        k_ref[:, ch * LANES:(ch + 1) * LANES] = norm_rope(z, kg_ref[...]).astype(BF16)
    off += nk * LANES
    ones = jnp.ones((y.shape[0], LANES), BF16)
    for ch in range(nk):
        v_ref[:, 2 * ch * LANES:(2 * ch + 1) * LANES] = y[:, off + ch * LANES: off + (ch + 1) * LANES].astype(BF16)
        v_ref[:, (2 * ch + 1) * LANES:(2 * ch + 2) * LANES] = ones
    off += nk * LANES
    qm_ref[...] = (y[:, off:off + MEM_W] * SCALE).astype(BF16)


def _a_in_proj(x, w, m128, qg, kg, tabs, tm=TOK_TILE):
    c, sa, sb = tabs
    nblk = SEQ // tm
    a_in = w.shape[1]
    tab_spec = pl.BlockSpec((tm, LANES), lambda i: (i % nblk, 0))
    row = lambda width: pl.BlockSpec((tm, width), lambda i: (i, 0))
    const = lambda shape: pl.BlockSpec(shape, lambda i: (0, 0))
    qw, kw = A_Q_HEADS * HEAD_DIM, A_KV_HEADS * HEAD_DIM
    return pl.pallas_call(
        _a_in_kernel,
        grid=(N_TOK // tm,),
        in_specs=[row(D_MODEL), const((D_MODEL, a_in)), const((LANES, LANES)), const((1, LANES)), const((1, LANES)),
                  tab_spec, tab_spec, tab_spec],
        out_specs=[row(qw), row(kw), row(2 * kw), row(MEM_W)],
        out_shape=[jax.ShapeDtypeStruct((N_TOK, qw), BF16), jax.ShapeDtypeStruct((N_TOK, kw), BF16),
                   jax.ShapeDtypeStruct((N_TOK, 2 * kw), BF16), jax.ShapeDtypeStruct((N_TOK, MEM_W), BF16)],
        compiler_params=_params(("parallel",)),
        name="a_in_proj",
    )(x, w, m128, qg, kg, c, sa, sb)


def _b_in_kernel(*refs):
    ng = len(DILATIONS)
    nx = D_MODEL // LANES
    x_refs, w_ref, refs = refs[:nx], refs[nx], refs[nx + 1:]
    tabs, outs, qm_ref = refs[:3 * ng], refs[3 * ng:6 * ng], refs[6 * ng]
    xb1 = None
    for g, d in enumerate(DILATIONS):
        n = TOK_TILE // d
        if d == 1:
            cols = [r[...] for r in x_refs]
        else:
            cols = [jnp.concatenate([r[pl.ds(k, n, stride=d), :] for k in range(d)], axis=0) for r in x_refs]
        xb = jnp.concatenate(cols, axis=1).astype(BF16)
        xb1 = xb if d == 1 else xb1
        c, sa, sb = (t[...].reshape(TOK_TILE, LANES) for t in tabs[3 * g:3 * g + 3])
        q_ref, k_ref, v_ref = outs[3 * g:3 * g + 3]
        yq = _dot(xb, w_ref[:, g * B_WIDTH:(g + 1) * B_WIDTH]) * SCALE
        yk = _dot(xb, w_ref[:, (ng + g) * B_WIDTH:(ng + g + 1) * B_WIDTH])
        yv = _dot(xb, w_ref[:, (2 * ng + g) * B_WIDTH:(2 * ng + g + 1) * B_WIDTH])
        for ch in range(B_WIDTH // LANES):
            sl = slice(ch * LANES, (ch + 1) * LANES)
            q_ref[0, :, :, sl] = _rope(yq[:, sl], c, sa, sb, PARTIAL_ROT_DIMS // 2).reshape(d, n, LANES).astype(BF16)
            k_ref[0, :, :, sl] = _rope(yk[:, sl], c, sa, sb, PARTIAL_ROT_DIMS // 2).reshape(d, n, LANES).astype(BF16)
        v_ref[0] = yv.reshape(d, n, B_WIDTH).astype(BF16)
    qm = _dot(xb1, w_ref[:, 3 * ng * B_WIDTH:3 * ng * B_WIDTH + MEM_W])
    qm_ref[...] = (qm * SCALE).astype(BF16)


def _b_in_proj(x, w, tabs):
    nx = D_MODEL // LANES
    in_specs = [pl.BlockSpec((TOK_TILE, LANES), functools.partial(lambda i, c: (i, c), c=c)) for c in range(nx)]
    in_specs.append(pl.BlockSpec(w.shape, lambda i: (0, 0)))
    out_specs, out_shape = [], []
    for d in DILATIONS:
        n = TOK_TILE // d
        in_specs += [pl.BlockSpec((d, n, LANES), lambda i: (0, i % TILES_PER_SEQ, 0))] * 3
        out_specs += [pl.BlockSpec((1, d, n, B_WIDTH), lambda i: (i // TILES_PER_SEQ, 0, i % TILES_PER_SEQ, 0))] * 3
        out_shape += [jax.ShapeDtypeStruct((BATCH, d, SEQ // d, B_WIDTH), BF16)] * 3
    out_specs.append(pl.BlockSpec((TOK_TILE, MEM_W), lambda i: (i, 0)))
    out_shape.append(jax.ShapeDtypeStruct((N_TOK, MEM_W), BF16))
    res = pl.pallas_call(
        _b_in_kernel,
        grid=(N_TOK // TOK_TILE,),
        in_specs=in_specs,
        out_specs=out_specs,
        out_shape=out_shape,
        compiler_params=_params(("parallel",)),
        name="b_in_proj",
    )(*([x] * nx), w, *[t for group in tabs for t in group])
    return [res[3 * g:3 * g + 3] for g in range(len(DILATIONS))], res[-1]


def _gqa_kernel(q_ref, k_ref, v_ref, o_ref):
    tq = q_ref.shape[0]
    ntile = q_ref.shape[1] // LANES
    q = jnp.concatenate([q_ref[:, t * LANES:(t + 1) * LANES] for t in range(ntile)], axis=0)
    first = _first_head(q.shape)
    res = []
    for pick in (first, jnp.logical_not(first)):
        qh = jnp.where(pick, q, jnp.zeros_like(q))
        m = jnp.full((q.shape[0], 1), -jnp.inf, F32)
        acc = jnp.zeros((q.shape[0], 2 * LANES), F32)
        for c in range(SEQ // GQA_KC):
            keys = slice(c * GQA_KC, (c + 1) * GQA_KC)
            s = _dot_nt(qh, k_ref[keys, :])
            m_new = jnp.maximum(m, jnp.max(s, axis=1, keepdims=True))
            p = jnp.exp2(s - m_new).astype(BF16)
            acc = acc * jnp.exp2(m - m_new) + _dot(p, v_ref[keys, :])
            m = m_new
        res.append((acc, m))
    o = _pair_output(res, first)
    for t in range(ntile):
        o_ref[:, t * LANES:(t + 1) * LANES] = o[t * tq:(t + 1) * tq].astype(BF16)


def _gqa_attention(q, k, v, tq=1024):
    nq = SEQ // tq
    qw = q.shape[1] // 2
    kw = k.shape[1] // 2
    return pl.pallas_call(
        _gqa_kernel,
        grid=(BATCH, 2, nq),
        in_specs=[pl.BlockSpec((tq, qw), lambda b, p, i: (b * nq + i, p)),
                  pl.BlockSpec((SEQ, kw), lambda b, p, i: (b, p)),
                  pl.BlockSpec((SEQ, 2 * kw), lambda b, p, i: (b, p))],
        out_specs=pl.BlockSpec((tq, qw), lambda b, p, i: (b * nq + i, p)),
        out_shape=jax.ShapeDtypeStruct(q.shape, BF16),
        compiler_params=_params(("parallel", "parallel", "parallel")),
        name="gqa_attention",
    )(q, k, v)


def _band_kernel(q_ref, k_ref, v_ref, o_ref, lse_ref, *, length, seg, win):
    tq = BAND_TQ
    first = _first_head((tq, LANES))
    lane = lax.broadcasted_iota(jnp.int32, (tq, LANES), 1)
    row = lax.broadcasted_iota(jnp.int32, (tq, win), 0)
    col = lax.broadcasted_iota(jnp.int32, (tq, win), 1)
    for u in range(BAND_TILES):
        tile = pl.program_id(1) * BAND_TILES + u
        start = pl.multiple_of(jnp.clip(tile * tq - BAND_RADIUS, 0, length - win), BAND_RADIUS)
        qpos, kpos = tile * tq + row, start + col
        valid = jnp.abs(qpos - kpos) <= BAND_RADIUS
        if seg < length:
            valid = jnp.logical_and(valid, qpos // seg == kpos // seg)
        rows = slice(u * tq, (u + 1) * tq)
        lse = jnp.zeros((tq, LANES), F32)
        for j in range(B_WIDTH // LANES):
            sl = slice(j * LANES, (j + 1) * LANES)
            res = _pair_attention(q_ref[0, rows, sl], k_ref[0, pl.ds(start, win), sl],
                                  _with_ones(v_ref[0, pl.ds(start, win), sl]), first, valid)
            o_ref[0, rows, sl] = _pair_output(res, first).astype(BF16)
            for half, (ol, m) in enumerate(res):
                lse = jnp.where(lane == 2 * j + half, m + jnp.log(ol[:, LANES:]), lse)
        lse_ref[0, rows, :] = lse


def _band_attention(q, k, v, d):
    seg = SEQ // d
    step_rows = BAND_TILES * BAND_TQ
    length = max(seg, step_rows)
    win = BAND_TQ + 2 * BAND_RADIUS
    nblk = N_TOK // length
    q, k, v = (t.reshape(nblk, length, B_WIDTH) for t in (q, k, v))
    seq_spec = pl.BlockSpec((1, length, B_WIDTH), lambda s, i: (s, 0, 0))
    o, lse = pl.pallas_call(
        functools.partial(_band_kernel, length=length, seg=seg, win=win),
        grid=(nblk, length // step_rows),
        in_specs=[pl.BlockSpec((1, step_rows, B_WIDTH), lambda s, i: (s, i, 0)), seq_spec, seq_spec],
        out_specs=[pl.BlockSpec((1, step_rows, B_WIDTH), lambda s, i: (s, i, 0)),
                   pl.BlockSpec((1, step_rows, LANES), lambda s, i: (s, i, 0))],
        out_shape=[jax.ShapeDtypeStruct((nblk, length, B_WIDTH), BF16),
                   jax.ShapeDtypeStruct((nblk, length, LANES), F32)],
        compiler_params=_params(("parallel", "parallel")),
        name=f"band_attention_d{d}",
    )(q, k, v)
    return o.reshape(BATCH, d, seg, B_WIDTH), lse.reshape(BATCH, d, seg, LANES)


def _route(x, rows, wh_ref, wl_ref, rb_ref, tri_ref, xa_ref, cls_ref, rank_ref, cnt_ref, carry_ref):
    tm = x.shape[0]
    xh, xl = _split_bf16(x)
    logits = _dot_nt(wh_ref[...], xh) + _dot_nt(wh_ref[...], xl) + _dot_nt(wl_ref[...], xh)
    aff = 1.0 / (1.0 + jnp.exp(-logits))
    sel = aff + rb_ref[...]
    s = [sel[e:e + 1, :] for e in range(N_EXPERTS)]
    a = [aff[e:e + 1, :] for e in range(N_EXPERTS)]

    def top2_sum(v):
        hi01, lo01 = jnp.maximum(v[0], v[1]), jnp.minimum(v[0], v[1])
        hi23, lo23 = jnp.maximum(v[2], v[3]), jnp.minimum(v[2], v[3])
        return jnp.maximum(hi01, hi23) + jnp.maximum(jnp.minimum(hi01, hi23), jnp.maximum(lo01, lo23))

    gscore = [top2_sum(s[EXPERTS_PER_GROUP * g:EXPERTS_PER_GROUP * (g + 1)]) for g in range(N_EXPERT_GROUPS)]
    best = jnp.zeros((1, tm), jnp.int32)
    best_score = gscore[0]
    for g in range(1, N_EXPERT_GROUPS):
        better = gscore[g] > best_score
        best = jnp.where(better, g, best)
        best_score = jnp.where(better, gscore[g], best_score)

    def pick(rows, j):
        out = rows[j]
        for g in range(1, N_EXPERT_GROUPS):
            out = jnp.where(best == g, rows[EXPERTS_PER_GROUP * g + j], out)
        return out

    t = [pick(s, j) for j in range(EXPERTS_PER_GROUP)]
    w = [pick(a, j) for j in range(EXPERTS_PER_GROUP)]

    def first_max(v):
        mx = jnp.maximum(jnp.maximum(v[0], v[1]), jnp.maximum(v[2], v[3]))
        idx = jnp.full((1, tm), EXPERTS_PER_GROUP - 1, jnp.int32)
        for j in range(EXPERTS_PER_GROUP - 2, -1, -1):
            idx = jnp.where(v[j] == mx, j, idx)
        return idx

    i1 = first_max(t)
    i2 = first_max([jnp.where(i1 == j, -jnp.inf, t[j]) for j in range(EXPERTS_PER_GROUP)])
    lo, hi = jnp.minimum(i1, i2), jnp.maximum(i1, i2)

    def take(rows, idx):
        out = rows[0]
        for j in range(1, EXPERTS_PER_GROUP):
            out = jnp.where(idx == j, rows[j], out)
        return out

    w_lo, w_hi = take(w, lo), take(w, hi)
    den = w_lo + w_hi
    pair = jnp.where(lo == 0, hi - 1, jnp.where(lo == 1, hi + 1, len(PAIRS) - 1))
    cls = best * len(PAIRS) + pair

    onehot = (lax.broadcasted_iota(jnp.int32, (CLASS_ROWS, tm), 0) == cls).astype(F32)
    before = _dot(onehot.astype(BF16), tri_ref[...]) + carry_ref[:, 0:1]
    rank = jnp.sum(onehot * before, axis=0, keepdims=True)
    carry_ref[...] = carry_ref[...] + jnp.sum(onehot, axis=1, keepdims=True)

    cls_ref[0, :, rows] = cls
    rank_ref[0, :, rows] = rank.astype(jnp.int32)
    cnt_ref[...] = carry_ref[...]
    gates = jnp.concatenate([w_lo / den, w_hi / den, jnp.zeros((LANES - 2, tm), F32)], axis=0)
    xa_ref[rows, :D_MODEL] = x
    xa_ref[rows, D_MODEL:] = gates.T


def _post_attn_kernel(*refs, merge):
    if merge:
        o_refs, l_refs, e_ref, refs = refs[:3], refs[3:6], refs[6], refs[7:]
    else:
        mix_ref, refs = refs[0], refs[1:]
    (qm_ref, mk_ref, mv_ref, w1_ref, w2_ref, x_ref, g_ref, b_ref, wh_ref, wl_ref, rb_ref, tri_ref,
     xa_ref, cls_ref, rank_ref, cnt_ref, carry_ref) = refs[:17]

    @pl.when(pl.program_id(0) == 0)
    def _():
        carry_ref[...] = jnp.zeros_like(carry_ref)

    first = _first_head((TOK_TILE, LANES))
    for sub in range(POST_SUB):
        rows = slice(sub * TOK_TILE, (sub + 1) * TOK_TILE)
        if merge:
            ot_ref, lt_ref = refs[17], refs[18]
            nch = B_WIDTH // LANES
            for g, d in enumerate(DILATIONS):
                n = TOK_TILE // d
                for r in range(d):
                    dst = pl.ds(r, n, stride=d) if d > 1 else slice(None)
                    og = o_refs[g][0, r, sub * n:(sub + 1) * n].astype(F32)
                    for ch in range(nch):
                        ot_ref[sub, g, ch, dst, :] = og[:, ch * LANES:(ch + 1) * LANES]
                    lt_ref[sub, g, dst, :] = l_refs[g][0, r, sub * n:(sub + 1) * n]
            lses = [lt_ref[sub, g] for g in range(len(DILATIONS))]
            m = functools.reduce(jnp.maximum, lses)
            es = [jnp.exp(l - m) for l in lses]
            inv = 1.0 / functools.reduce(jnp.add, es)
            expand = e_ref[...]

            def widen(w):
                hi, lo = _split_bf16(w)
                return _dot(hi, expand) + _dot(lo, expand)

            ws = [widen(e * inv) for e in es]
            mix = jnp.concatenate(
                [functools.reduce(jnp.add, [w[:, ch * LANES:(ch + 1) * LANES] * ot_ref[sub, g, ch]
                                            for g, w in enumerate(ws)]) for ch in range(nch)], axis=1).astype(BF16)
        else:
            mix = mix_ref[rows, :]

        memo = []
        for j in range(MEM_W // LANES):
            sl = slice(j * LANES, (j + 1) * LANES)
            memo.append(_pair_output(_pair_attention(qm_ref[rows, sl], mk_ref[:, sl], _with_ones(mv_ref[:, sl]), first),
                                     first))
        memo = jnp.concatenate(memo, axis=1).astype(BF16)

        attn = _dot(mix, w1_ref[...]) + _dot(memo, w2_ref[...])
        x1 = _layer_norm(ALPHA * x_ref[rows, :] + attn, g_ref[...], b_ref[...])
        _route(x1, rows, wh_ref, wl_ref, rb_ref, tri_ref, xa_ref, cls_ref, rank_ref, cnt_ref, carry_ref)


def _post_attn(mix, qm, mkv, w1, w2, x, g, b, router, expand=None):
    merge = expand is not None
    wh, wl, rb, tri = router
    blk = POST_SUB * TOK_TILE
    nblk = N_TOK // blk
    per_seq = SEQ // blk
    row = lambda width: pl.BlockSpec((blk, width), lambda i: (i, 0))
    const = lambda shape: pl.BlockSpec(shape, lambda i: (0,) * len(shape))
    row3 = pl.BlockSpec((1, 1, blk), lambda i: (i, 0, 0))
    scratch = [pltpu.VMEM((CLASS_ROWS, LANES), F32)]
    if merge:
        outs, lses = mix
        resid = lambda d, width: pl.BlockSpec((1, d, blk // d, width), lambda i: (i // per_seq, 0, i % per_seq, 0))
        lead_specs = ([resid(d, B_WIDTH) for d in DILATIONS] + [resid(d, LANES) for d in DILATIONS]
                      + [const(expand.shape)])
        lead = [*outs, *lses, expand]
        scratch += [pltpu.VMEM((POST_SUB, len(DILATIONS), B_WIDTH // LANES, TOK_TILE, LANES), F32),
                    pltpu.VMEM((POST_SUB, len(DILATIONS), TOK_TILE, LANES), F32)]
    else:
        lead_specs, lead = [row(mix.shape[1])], [mix]
    xa, cls, rank, cnt = pl.pallas_call(
        functools.partial(_post_attn_kernel, merge=merge),
        grid=(nblk,),
        in_specs=lead_specs + [row(MEM_W),
                               pl.BlockSpec((N_MEM, MEM_W), lambda i: (i // per_seq, 0)),
                               pl.BlockSpec((N_MEM, MEM_W), lambda i: (i // per_seq, 1)),
                               const(w1.shape), const(w2.shape), row(D_MODEL), const((1, D_MODEL)), const((1, D_MODEL)),
                               const(wh.shape), const(wl.shape), const(rb.shape), const(tri.shape)],
        out_specs=[row(XA_W), row3, row3, const((CLASS_ROWS, LANES))],
        out_shape=[jax.ShapeDtypeStruct((N_TOK, XA_W), F32), jax.ShapeDtypeStruct((nblk, 1, blk), jnp.int32),
                   jax.ShapeDtypeStruct((nblk, 1, blk), jnp.int32), jax.ShapeDtypeStruct((CLASS_ROWS, LANES), F32)],
        scratch_shapes=scratch,
        compiler_params=_params(("arbitrary",)),
        name="post_attn_merge" if merge else "post_attn",
    )(*lead, qm, mkv, mkv, w1, w2, x, g, b, wh, wl, rb, tri)
    return xa, cls.reshape(N_TOK), rank.reshape(N_TOK), cnt[:N_CLASSES, 0].astype(jnp.int32)


def _dispatch_kernel(pos_ref, ends_ref, xa_ref, xs_ref, zeros, sem, zsem, *, chunk):
    base = pl.program_id(0) * chunk

    @pl.when(pl.program_id(0) == 0)
    def _():
        zeros[...] = jnp.zeros_like(zeros)

        def tail(c):
            start = pl.multiple_of(ends_ref[c] - MOE_TM, MOE_TM)
            return pltpu.make_async_copy(zeros, xs_ref.at[pl.ds(start, MOE_TM)], zsem)

        def nonempty(c):
            return ends_ref[c] > (ends_ref[c - 1] if c else 0)

        def spare(t):
            return pltpu.make_async_copy(zeros, xs_ref.at[pl.ds(pl.multiple_of(t * MOE_TM, MOE_TM), MOE_TM)], zsem)

        first_spare = ends_ref[N_CLASSES - 1] // MOE_TM
        for c in range(N_CLASSES):
            pl.when(nonempty(c))(lambda c=c: tail(c).start())
        lax.fori_loop(first_spare, MOE_TILES, lambda t, carry: (spare(t).start(), carry)[1], 0)
        for c in range(N_CLASSES):
            pl.when(nonempty(c))(lambda c=c: tail(c).wait())
        lax.fori_loop(first_spare, MOE_TILES, lambda t, carry: (spare(t).wait(), carry)[1], 0)

    def issue(blk, carry):
        for u in range(ROW_DMA_UNROLL):
            r = blk * ROW_DMA_UNROLL + u
            pltpu.make_async_copy(xa_ref.at[pl.ds(r, 1)], xs_ref.at[pl.ds(pos_ref[base + r], 1)], sem).start(
                priority=u % 2)
        return carry

    def retire(r, carry):
        pltpu.make_async_copy(xa_ref.at[pl.ds(0, 1)], xs_ref.at[pl.ds(0, 1)], sem).wait()
        return carry

    lax.fori_loop(0, chunk // ROW_DMA_UNROLL, issue, 0)
    lax.fori_loop(0, chunk, retire, 0, unroll=8)


def _dispatch(pos, ends, xa, chunk=512):
    return pl.pallas_call(
        functools.partial(_dispatch_kernel, chunk=chunk),
        grid_spec=pltpu.PrefetchScalarGridSpec(
            num_scalar_prefetch=2,
            grid=(N_TOK // chunk,),
            in_specs=[pl.BlockSpec((chunk, XA_W), lambda i, p, e: (i, 0))],
            out_specs=pl.BlockSpec(memory_space=pl.ANY),
            scratch_shapes=[pltpu.VMEM((MOE_TM, XA_W), F32), pltpu.SemaphoreType.DMA(()),
                            pltpu.SemaphoreType.DMA(())],
        ),
        out_shape=jax.ShapeDtypeStruct((MOE_ROWS, XA_W), F32),
        compiler_params=_params(("arbitrary",)),
        name="moe_dispatch",
    )(pos, ends, xa)


def _expert_kernel(slot1_ref, slot2_ref, used_ref, first_ref, grp_ref, next_ref, xs_ref, wg_hbm, wu_hbm, wd_hbm,
                   ys_ref, stage_g, stage_u, stage_d, wg_s, wu_s, wd_s, sem, *, layer):
    i = pl.program_id(0)
    streams = ((wg_hbm, stage_g, wg_s), (wu_hbm, stage_u, wu_s), (wd_hbm, stage_d, wd_s))

    def group_copy(group, k):
        first_expert = layer * N_EXPERTS + group * EXPERTS_PER_GROUP
        hbm, stage, _ = streams[k]
        return pltpu.make_async_copy(hbm.at[pl.ds(first_expert, EXPERTS_PER_GROUP)], stage, sem.at[k])

    @pl.when(i == 0)
    def _():
        for k in range(len(streams)):
            group_copy(grp_ref[0], k).start()

    @pl.when(first_ref[i] != 0)
    def _():
        for k, (_, stage, dst) in enumerate(streams):
            group_copy(grp_ref[i], k).wait()
            for e in range(EXPERTS_PER_GROUP):
                dst[e] = stage[e].astype(BF16)

        @pl.when(next_ref[i] >= 0)
        def _():
            for k in range(len(streams)):
                group_copy(next_ref[i], k).start()

    @pl.when(used_ref[i] != 0)
    def _():
        x = xs_ref[:, :D_MODEL].astype(BF16)
        gates = xs_ref[:, D_MODEL:]

        def expert(slot):
            gate = _dot(x, wg_s[slot])
            h = gate * (1.0 / (1.0 + jnp.exp(-gate))) * _dot(x, wu_s[slot])
            return _dot(h.astype(BF16), wd_s[slot])

        ys_ref[...] = gates[:, 0:1] * expert(slot1_ref[i]) + gates[:, 1:2] * expert(slot2_ref[i])

    @pl.when(used_ref[i] == 0)
    def _():
        ys_ref[...] = jnp.zeros_like(ys_ref)


def _experts(tile_meta, xs, layer, wg, wu, wd):
    nmeta = len(tile_meta)
    up_shape = (EXPERTS_PER_GROUP, D_MODEL, D_EXPERT)
    down_shape = (EXPERTS_PER_GROUP, D_EXPERT, D_MODEL)
    return pl.pallas_call(
        functools.partial(_expert_kernel, layer=layer),
        grid_spec=pltpu.PrefetchScalarGridSpec(
            num_scalar_prefetch=nmeta,
            grid=(MOE_TILES,),
            in_specs=[pl.BlockSpec((MOE_TM, XA_W), lambda i, *_: (i, 0)),
                      pl.BlockSpec(memory_space=pl.ANY), pl.BlockSpec(memory_space=pl.ANY),
                      pl.BlockSpec(memory_space=pl.ANY)],
            out_specs=pl.BlockSpec((MOE_TM, D_MODEL), lambda i, *_: (i, 0)),
            scratch_shapes=[pltpu.VMEM(up_shape, F32), pltpu.VMEM(up_shape, F32), pltpu.VMEM(down_shape, F32),
                            pltpu.VMEM(up_shape, BF16), pltpu.VMEM(up_shape, BF16), pltpu.VMEM(down_shape, BF16),
                            pltpu.SemaphoreType.DMA((3,))],
        ),
        out_shape=jax.ShapeDtypeStruct((MOE_ROWS, D_MODEL), F32),
        compiler_params=_params(("arbitrary",)),
        name="moe_experts",
    )(*tile_meta, xs, wg, wu, wd)


def _combine_kernel(pos_ref, ys_ref, x_ref, g_ref, b_ref, o_ref, buf, sem, *, tc):
    i = pl.program_id(0)
    slot = i % 2

    def gather(step, into):
        def issue(blk, carry):
            for u in range(ROW_DMA_UNROLL):
                r = blk * ROW_DMA_UNROLL + u
                row = pos_ref[step * tc + r]
                pltpu.make_async_copy(ys_ref.at[pl.ds(row, 1)], buf.at[into, pl.ds(r, 1)], sem.at[into]).start(
                    priority=u % 2)
            return carry

        lax.fori_loop(0, tc // ROW_DMA_UNROLL, issue, 0)

    @pl.when(i == 0)
    def _():
        gather(0, 0)

    @pl.when(i + 1 < pl.num_programs(0))
    def _():
        gather(i + 1, 1 - slot)

    def retire(r, carry):
        pltpu.make_async_copy(ys_ref.at[pl.ds(0, 1)], buf.at[slot, pl.ds(0, 1)], sem.at[slot]).wait()
        return carry

    lax.fori_loop(0, tc, retire, 0, unroll=8)
    o_ref[...] = _layer_norm(ALPHA * x_ref[...] + buf[slot], g_ref[...], b_ref[...])


def _combine_ln(pos, ys, xa, g, b, tc=256):
    return pl.pallas_call(
        functools.partial(_combine_kernel, tc=tc),
        grid_spec=pltpu.PrefetchScalarGridSpec(
            num_scalar_prefetch=1,
            grid=(N_TOK // tc,),
            in_specs=[pl.BlockSpec(memory_space=pl.ANY),
                      pl.BlockSpec((tc, D_MODEL), lambda i, p: (i, 0)),
                      pl.BlockSpec((1, D_MODEL), lambda i, p: (0, 0)),
                      pl.BlockSpec((1, D_MODEL), lambda i, p: (0, 0))],
            out_specs=pl.BlockSpec((tc, D_MODEL), lambda i, p: (i, 0)),
            scratch_shapes=[pltpu.VMEM((2, tc, D_MODEL), F32), pltpu.SemaphoreType.DMA((2,))],
        ),
        out_shape=jax.ShapeDtypeStruct((N_TOK, D_MODEL), F32),
        compiler_params=_params(("arbitrary",)),
        name="moe_combine_ln",
    )(pos, ys, xa, g, b)


def _moe_ln(routed, layer, wg, wu, wd, g, b):
    xa, cls, rank, counts = routed
    padded = (counts + MOE_TM - 1) // MOE_TM * MOE_TM
    ends = jnp.cumsum(padded)
    starts = ends - padded
    pos = starts[cls] + rank
    tile_start = jnp.arange(MOE_TILES, dtype=jnp.int32) * MOE_TM
    tile_used = (tile_start < ends[-1]).astype(jnp.int32)
    last_cls = jnp.max(jnp.where(counts > 0, jnp.arange(N_CLASSES), 0))
    tile_cls = jnp.minimum(jnp.sum(tile_start[:, None] >= ends[None, :], axis=1), last_cls)
    tile_cls = tile_cls.astype(jnp.int32)
    tile_grp = tile_cls // len(PAIRS)
    prev_grp = jnp.concatenate([jnp.full((1,), -1, jnp.int32), tile_grp[:-1]])
    tile_first = tile_used * (tile_grp != prev_grp).astype(jnp.int32)
    has = jnp.sum(counts.reshape(N_EXPERT_GROUPS, len(PAIRS)), axis=1) > 0
    gid = jnp.arange(N_EXPERT_GROUPS, dtype=jnp.int32)
    later = jnp.where(jnp.logical_and(has[None, :], gid[None, :] > gid[:, None]), gid[None, :], N_EXPERT_GROUPS)
    next_grp = jnp.min(later, axis=1)
    next_grp = jnp.where(next_grp == N_EXPERT_GROUPS, -1, next_grp)
    pair_slots = jnp.asarray(np.array(PAIRS, np.int32))[tile_cls % len(PAIRS)]
    tile_meta = (pair_slots[:, 0], pair_slots[:, 1], tile_used, tile_first, tile_grp, next_grp[tile_grp])
    xs = _dispatch(pos, ends.astype(jnp.int32), xa)
    ys = _experts(tile_meta, xs, layer, wg, wu, wd)
    return _combine_ln(pos, ys, xa, g, b)


def _tables(c, sa, sb):
    return tuple(np.concatenate([t, t], axis=-1).astype(np.float32) for t in (c, sa, sb))


def _axial_tables():
    rows = SEQ // GRID_W
    row = np.repeat(np.arange(rows), GRID_W).astype(np.float64)
    col = np.tile(np.arange(GRID_W), rows).astype(np.float64)
    half = HEAD_DIM // 2
    inv = 1.0 / (AXIAL_THETA ** (np.arange(0, half, 2, dtype=np.float64) / half))
    ar, ac = row[:, None] * inv, col[:, None] * inv
    z = np.zeros_like(ar)
    c = np.concatenate([np.cos(ar), np.cos(ar), np.cos(ac), np.cos(ac)], -1)
    sa = np.concatenate([-np.sin(ar), z, -np.sin(ac), z], -1)
    sb = np.concatenate([z, np.sin(ar), z, np.sin(ac)], -1)
    return _tables(c, sa, sb)


def _partial_tables():
    n = PARTIAL_ROT_DIMS
    pos = np.arange(SEQ, dtype=np.float64)
    ang = pos[:, None] * (1.0 / (PARTIAL_THETA ** (np.arange(0, n, 2, dtype=np.float64) / n)))
    z = np.zeros_like(ang)
    rest = np.zeros((SEQ, HEAD_DIM - n))
    c = np.concatenate([np.cos(ang), np.cos(ang), rest + 1.0], -1)
    sa = np.concatenate([-np.sin(ang), z, rest], -1)
    sb = np.concatenate([z, np.sin(ang), rest], -1)
    return _tables(c, sa, sb)


def _residue_order(tab, d):
    return np.ascontiguousarray(tab.reshape(SEQ // d, d, LANES).transpose(1, 0, 2))


def kernel(x, mem, w_mem_kv, router_w, router_b, a_w_in, a_w_out, a_q_norm, a_k_norm, b_w_in, b_w_out,
           ln1_g, ln1_b, ln2_g, ln2_b, w_gate, w_up, w_down):
    x = x.reshape(N_TOK, D_MODEL)
    row = lambda v: v.reshape(1, -1)

    lane = np.arange(LANES)
    m128 = jnp.asarray((lane[:, None] // HEAD_DIM == lane[None, :] // HEAD_DIM), BF16)
    expand = jnp.asarray(lane[:, None] == np.arange(B_WIDTH)[None, :] // HEAD_DIM, BF16)
    tri = jnp.asarray(np.triu(np.ones((TOK_TILE, TOK_TILE), np.float32), 1), BF16)
    rw_hi = router_w.T.astype(BF16)
    rw_lo = (router_w.T - rw_hi.astype(F32)).astype(BF16)
    router = (rw_hi, rw_lo, router_b.reshape(N_EXPERTS, 1), tri)

    mkv = _proj(mem.reshape(BATCH * N_MEM, D_MODEL), w_mem_kv.astype(BF16))
    experts = tuple(w.reshape(DEPTH * N_EXPERTS, *w.shape[2:]) for w in (w_gate, w_up, w_down))

    kmix = A_Q_HEADS * HEAD_DIM
    qcols = (np.asarray(_Q_HEAD_ORDER)[:, None] * HEAD_DIM + np.arange(HEAD_DIM)[None, :]).reshape(-1)
    w_in = jnp.concatenate([a_w_in[0][:, qcols], a_w_in[0][:, kmix:]], axis=1).astype(BF16)
    qg = row(jnp.tile(a_q_norm[0], 2) * (SCALE * LOG2_E))
    kg = row(jnp.tile(a_k_norm[0], 2))
    q, k, v, qm = _a_in_proj(x, w_in, m128, qg, kg, _axial_tables())
    mix = _gqa_attention(q, k, v)
    w_out = a_w_out[0].astype(BF16)
    routed = _post_attn(mix, qm, mkv, w_out[qcols], w_out[kmix:], x, row(ln1_g[0]), row(ln1_b[0]), router)
    x = _moe_ln(routed, 0, *experts, row(ln2_g[0]), row(ln2_b[0]))

    ptabs = _partial_tables()
    tabs = [tuple(_residue_order(t, d) for t in ptabs) for d in DILATIONS]
    qkv, qm = _b_in_proj(x, b_w_in[0].astype(BF16), tabs)
    outs, lses = zip(*[_band_attention(*qkv[g], d) for g, d in enumerate(DILATIONS)])
    w_out = b_w_out[0].astype(BF16)
    routed = _post_attn((outs, lses), qm, mkv, w_out[:B_WIDTH], w_out[B_WIDTH:], x, row(ln1_g[1]), row(ln1_b[1]),
                        router, expand)
    x = _moe_ln(routed, 1, *experts, row(ln2_g[1]), row(ln2_b[1]))
    return x.reshape(BATCH, SEQ, D_MODEL)
```

```python
import functools
import math

import numpy as np
import jax
import jax.numpy as jnp
from jax import lax
from jax.experimental import pallas as pl
from jax.experimental.pallas import tpu as pltpu

F32 = jnp.float32
BF16 = jnp.bfloat16

D_MODEL = 1024
BATCH = 8
SEQ = 2048
N_TOK = BATCH * SEQ
HEAD_DIM = 64
GRID_W = 64
N_MEM = 256
MEM_HEADS = 4
A_Q_HEADS = 12
A_KV_HEADS = 4
AXIAL_THETA = 10000.0
B_HEADS = 8
B_WIDTH = B_HEADS * HEAD_DIM
DILATIONS = (1, 4, 16)
BAND_RADIUS = 64
PARTIAL_ROT_DIMS = HEAD_DIM // 4
PARTIAL_THETA = 500000.0
N_EXPERTS = 16
N_EXPERT_GROUPS = 4
EXPERTS_PER_GROUP = 4
D_EXPERT = 512
DEPTH = 2
ALPHA = (2 * DEPTH) ** 0.25
NORM_EPS = 1e-6
NEG_BIG = -1e30
SCALE = HEAD_DIM ** -0.5
LOG2_E = math.log2(math.e)
_Q_HEAD_ORDER = tuple(6 * p + 3 * half + t for p in range(2) for t in range(3) for half in range(2))

LANES = 128
MEM_W = MEM_HEADS * HEAD_DIM
TOK_TILE = 512
TILES_PER_SEQ = SEQ // TOK_TILE
POST_SUB = 2
GQA_KC = 256
BAND_TQ = 128
BAND_TILES = 8

PAIRS = ((0, 1), (0, 2), (0, 3), (1, 2), (1, 3), (2, 3))
N_CLASSES = N_EXPERT_GROUPS * len(PAIRS)
CLASS_ROWS = 32
MOE_TM = 256
MOE_ROWS = N_TOK + N_CLASSES * MOE_TM
MOE_TILES = MOE_ROWS // MOE_TM
XA_W = D_MODEL + LANES
ROW_DMA_UNROLL = 8

VMEM_LIMIT = 56 * 1024 * 1024


def _params(sem, vmem=VMEM_LIMIT):
    return pltpu.CompilerParams(dimension_semantics=sem, vmem_limit_bytes=vmem)


def _dot(a, b):
    return jnp.dot(a, b, preferred_element_type=F32)


def _dot_nt(a, b):
    return lax.dot_general(a, b, (((1,), (1,)), ((), ())), preferred_element_type=F32)


def _split_bf16(x):
    hi = x.astype(BF16)
    lo = (x - hi.astype(F32)).astype(BF16)
    return hi, lo


def _layer_norm(z, g, b):
    mu = jnp.mean(z, axis=-1, keepdims=True)
    zc = z - mu
    var = jnp.mean(zc * zc, axis=-1, keepdims=True)
    return zc * lax.rsqrt(var + NORM_EPS) * g + b


def _rope(z, c, sa, sb, shift):
    return z * c + pltpu.roll(z, LANES - shift, 1) * sa + pltpu.roll(z, shift, 1) * sb


def _first_head(shape):
    return lax.broadcasted_iota(jnp.int32, shape, 1) < HEAD_DIM


def _pair_attention(q2, k2, v3, first, valid=None, base2=False):
    res = []
    for pick in (first, jnp.logical_not(first)):
        s = _dot_nt(jnp.where(pick, q2, jnp.zeros_like(q2)), k2)
        if valid is not None:
            s = jnp.where(valid, s, NEG_BIG)
        m = jnp.max(s, axis=1, keepdims=True)
        p = jnp.exp2(s - m) if base2 else jnp.exp(s - m)
        res.append((_dot(p.astype(BF16), v3), m))
    return res


def _pair_output(res, first):
    (ol_a, _), (ol_b, _) = res
    return jnp.where(first, ol_a[:, :LANES], ol_b[:, :LANES]) / jnp.where(first, ol_a[:, LANES:], ol_b[:, LANES:])


def _with_ones(v2):
    return jnp.concatenate([v2, jnp.ones_like(v2)], axis=1)


def _proj_kernel(x_ref, w_ref, o_ref):
    o_ref[...] = _dot(x_ref[...].astype(BF16), w_ref[...]).astype(o_ref.dtype)


def _proj(x, w, tm=TOK_TILE):
    n, k = x.shape
    width = w.shape[1]
    return pl.pallas_call(
        _proj_kernel,
        grid=(n // tm,),
        in_specs=[pl.BlockSpec((tm, k), lambda i: (i, 0)), pl.BlockSpec((k, width), lambda i: (0, 0))],
        out_specs=pl.BlockSpec((tm, width), lambda i: (i, 0)),
        out_shape=jax.ShapeDtypeStruct((n, width), BF16),
        compiler_params=_params(("parallel",)),
        name="proj",
    )(x, w)


def _a_in_kernel(x_ref, w_ref, m_ref, qg_ref, kg_ref, c_ref, sa_ref, sb_ref, q_ref, k_ref, v_ref, qm_ref):
    y = _dot(x_ref[...].astype(BF16), w_ref[...])
    m = m_ref[...]
    c, sa, sb = c_ref[...], sa_ref[...], sb_ref[...]

    def norm_rope(z, gain):
        hi, lo = _split_bf16(z * z)
        ms = (_dot(hi, m) + _dot(lo, m)) * (1.0 / HEAD_DIM)
        z = z * lax.rsqrt(ms + NORM_EPS) * gain
        return _rope(z, c, sa, sb, 16)

    nq = A_Q_HEADS * HEAD_DIM // LANES
    nk = A_KV_HEADS * HEAD_DIM // LANES
    for ch in range(nq):
        q_ref[:, ch * LANES:(ch + 1) * LANES] = norm_rope(y[:, ch * LANES:(ch + 1) * LANES], qg_ref[...]).astype(BF16)
    off = nq * LANES
    for ch in range(nk):
        z = y[:, off + ch * LANES: off + (ch + 1) * LANES]
        k_ref[:, ch * LANES:(ch + 1) * LANES] = norm_rope(z, kg_ref[...]).astype(BF16)
    off += nk * LANES
    ones = jnp.ones((y.shape[0], LANES), BF16)
    for ch in range(nk):
        v_ref[:, 2 * ch * LANES:(2 * ch + 1) * LANES] = y[:, off + ch * LANES: off + (ch + 1) * LANES].astype(BF16)
        v_ref[:, (2 * ch + 1) * LANES:(2 * ch + 2) * LANES] = ones
    off += nk * LANES
    qm_ref[...] = (y[:, off:off + MEM_W] * SCALE).astype(BF16)


def _a_in_proj(x, w, m128, qg, kg, tabs, tm=2 * TOK_TILE):
    c, sa, sb = tabs
    nblk = SEQ // tm
    a_in = w.shape[1]
    tab_spec = pl.BlockSpec((tm, LANES), lambda i: (i % nblk, 0))
    row = lambda width: pl.BlockSpec((tm, width), lambda i: (i, 0))
    const = lambda shape: pl.BlockSpec(shape, lambda i: (0, 0))
    qw, kw = A_Q_HEADS * HEAD_DIM, A_KV_HEADS * HEAD_DIM
    return pl.pallas_call(
        _a_in_kernel,
        grid=(N_TOK // tm,),
        in_specs=[row(D_MODEL), const((D_MODEL, a_in)), const((LANES, LANES)), const((1, LANES)), const((1, LANES)),
                  tab_spec, tab_spec, tab_spec],
        out_specs=[row(qw), row(kw), row(2 * kw), row(MEM_W)],
        out_shape=[jax.ShapeDtypeStruct((N_TOK, qw), BF16), jax.ShapeDtypeStruct((N_TOK, kw), BF16),
                   jax.ShapeDtypeStruct((N_TOK, 2 * kw), BF16), jax.ShapeDtypeStruct((N_TOK, MEM_W), BF16)],
        compiler_params=_params(("parallel",)),
        name="a_in_proj",
    )(x, w, m128, qg, kg, c, sa, sb)


def _b_in_kernel(*refs):
    ng = len(DILATIONS)
    nx = D_MODEL // LANES
    x_refs, w_ref, refs = refs[:nx], refs[nx], refs[nx + 1:]
    tabs, outs, qm_ref = refs[:3 * ng], refs[3 * ng:6 * ng], refs[6 * ng]
    xb1 = None
    for g, d in enumerate(DILATIONS):
        n = TOK_TILE // d
        if d == 1:
            cols = [r[...] for r in x_refs]
        else:
            cols = [jnp.concatenate([r[pl.ds(k, n, stride=d), :] for k in range(d)], axis=0) for r in x_refs]
        xb = jnp.concatenate(cols, axis=1).astype(BF16)
        xb1 = xb if d == 1 else xb1
        c, sa, sb = (t[...].reshape(TOK_TILE, LANES) for t in tabs[3 * g:3 * g + 3])
        q_ref, k_ref, v_ref = outs[3 * g:3 * g + 3]
        yq = _dot(xb, w_ref[:, g * B_WIDTH:(g + 1) * B_WIDTH]) * SCALE
        yk = _dot(xb, w_ref[:, (ng + g) * B_WIDTH:(ng + g + 1) * B_WIDTH])
        yv = _dot(xb, w_ref[:, (2 * ng + g) * B_WIDTH:(2 * ng + g + 1) * B_WIDTH])
        for ch in range(B_WIDTH // LANES):
            sl = slice(ch * LANES, (ch + 1) * LANES)
            q_ref[0, :, :, sl] = _rope(yq[:, sl], c, sa, sb, PARTIAL_ROT_DIMS // 2).reshape(d, n, LANES).astype(BF16)
            k_ref[0, :, :, sl] = _rope(yk[:, sl], c, sa, sb, PARTIAL_ROT_DIMS // 2).reshape(d, n, LANES).astype(BF16)
        v_ref[0] = yv.reshape(d, n, B_WIDTH).astype(BF16)
    qm = _dot(xb1, w_ref[:, 3 * ng * B_WIDTH:3 * ng * B_WIDTH + MEM_W])
    qm_ref[...] = (qm * SCALE).astype(BF16)


def _b_in_proj(x, w, tabs):
    nx = D_MODEL // LANES
    in_specs = [pl.BlockSpec((TOK_TILE, LANES), functools.partial(lambda i, c: (i, c), c=c)) for c in range(nx)]
    in_specs.append(pl.BlockSpec(w.shape, lambda i: (0, 0)))
    out_specs, out_shape = [], []
    for d in DILATIONS:
        n = TOK_TILE // d
        in_specs += [pl.BlockSpec((d, n, LANES), lambda i: (0, i % TILES_PER_SEQ, 0))] * 3
        out_specs += [pl.BlockSpec((1, d, n, B_WIDTH), lambda i: (i // TILES_PER_SEQ, 0, i % TILES_PER_SEQ, 0))] * 3
        out_shape += [jax.ShapeDtypeStruct((BATCH, d, SEQ // d, B_WIDTH), BF16)] * 3
    out_specs.append(pl.BlockSpec((TOK_TILE, MEM_W), lambda i: (i, 0)))
    out_shape.append(jax.ShapeDtypeStruct((N_TOK, MEM_W), BF16))
    res = pl.pallas_call(
        _b_in_kernel,
        grid=(N_TOK // TOK_TILE,),
        in_specs=in_specs,
        out_specs=out_specs,
        out_shape=out_shape,
        compiler_params=_params(("parallel",)),
        name="b_in_proj",
    )(*([x] * nx), w, *[t for group in tabs for t in group])
    return [res[3 * g:3 * g + 3] for g in range(len(DILATIONS))], res[-1]


def _gqa_kernel(q_ref, k_ref, v_ref, o_ref):
    tq = q_ref.shape[0]
    ntile = q_ref.shape[1] // LANES
    q = jnp.concatenate([q_ref[:, t * LANES:(t + 1) * LANES] for t in range(ntile)], axis=0)
    first = _first_head(q.shape)
    res = []
    for pick in (first, jnp.logical_not(first)):
        qh = jnp.where(pick, q, jnp.zeros_like(q))
        m = jnp.full((q.shape[0], 1), -jnp.inf, F32)
        acc = jnp.zeros((q.shape[0], 2 * LANES), F32)
        for c in range(SEQ // GQA_KC):
            keys = slice(c * GQA_KC, (c + 1) * GQA_KC)
            s = _dot_nt(qh, k_ref[keys, :])
            m_new = jnp.maximum(m, jnp.max(s, axis=1, keepdims=True))
            p = jnp.exp2(s - m_new).astype(BF16)
            acc = acc * jnp.exp2(m - m_new) + _dot(p, v_ref[keys, :])
            m = m_new
        res.append((acc, m))
    o = _pair_output(res, first)
    for t in range(ntile):
        o_ref[:, t * LANES:(t + 1) * LANES] = o[t * tq:(t + 1) * tq].astype(BF16)


def _gqa_attention(q, k, v, tq=1024):
    nq = SEQ // tq
    qw = q.shape[1] // 2
    kw = k.shape[1] // 2
    return pl.pallas_call(
        _gqa_kernel,
        grid=(BATCH, 2, nq),
        in_specs=[pl.BlockSpec((tq, qw), lambda b, p, i: (b * nq + i, p)),
                  pl.BlockSpec((SEQ, kw), lambda b, p, i: (b, p)),
                  pl.BlockSpec((SEQ, 2 * kw), lambda b, p, i: (b, p))],
        out_specs=pl.BlockSpec((tq, qw), lambda b, p, i: (b * nq + i, p)),
        out_shape=jax.ShapeDtypeStruct(q.shape, BF16),
        compiler_params=_params(("parallel", "parallel", "parallel")),
        name="gqa_attention",
    )(q, k, v)


def _band_kernel(q_ref, k_ref, v_ref, o_ref, lse_ref, *, length, seg, win):
    tq = BAND_TQ
    first = _first_head((tq, LANES))
    lane = lax.broadcasted_iota(jnp.int32, (tq, LANES), 1)
    row = lax.broadcasted_iota(jnp.int32, (tq, win), 0)
    col = lax.broadcasted_iota(jnp.int32, (tq, win), 1)
    for u in range(BAND_TILES):
        tile = pl.program_id(1) * BAND_TILES + u
        start = pl.multiple_of(jnp.clip(tile * tq - BAND_RADIUS, 0, length - win), BAND_RADIUS)
        qpos, kpos = tile * tq + row, start + col
        valid = jnp.abs(qpos - kpos) <= BAND_RADIUS
        if seg < length:
            valid = jnp.logical_and(valid, qpos // seg == kpos // seg)
        rows = slice(u * tq, (u + 1) * tq)
        lse = jnp.zeros((tq, LANES), F32)
        for j in range(B_WIDTH // LANES):
            sl = slice(j * LANES, (j + 1) * LANES)
            res = _pair_attention(q_ref[0, rows, sl], k_ref[0, pl.ds(start, win), sl],
                                  _with_ones(v_ref[0, pl.ds(start, win), sl]), first, valid)
            o_ref[0, rows, sl] = _pair_output(res, first).astype(BF16)
            for half, (ol, m) in enumerate(res):
                lse = jnp.where(lane == 2 * j + half, m + jnp.log(ol[:, LANES:]), lse)
        lse_ref[0, rows, :] = lse


def _band_attention(q, k, v, d):
    seg = SEQ // d
    step_rows = BAND_TILES * BAND_TQ
    length = max(seg, step_rows)
    win = BAND_TQ + 2 * BAND_RADIUS
    nblk = N_TOK // length
    q, k, v = (t.reshape(nblk, length, B_WIDTH) for t in (q, k, v))
    seq_spec = pl.BlockSpec((1, length, B_WIDTH), lambda s, i: (s, 0, 0))
    o, lse = pl.pallas_call(
        functools.partial(_band_kernel, length=length, seg=seg, win=win),
        grid=(nblk, length // step_rows),
        in_specs=[pl.BlockSpec((1, step_rows, B_WIDTH), lambda s, i: (s, i, 0)), seq_spec, seq_spec],
        out_specs=[pl.BlockSpec((1, step_rows, B_WIDTH), lambda s, i: (s, i, 0)),
                   pl.BlockSpec((1, step_rows, LANES), lambda s, i: (s, i, 0))],
        out_shape=[jax.ShapeDtypeStruct((nblk, length, B_WIDTH), BF16),
                   jax.ShapeDtypeStruct((nblk, length, LANES), F32)],
        compiler_params=_params(("parallel", "parallel")),
        name=f"band_attention_d{d}",
    )(q, k, v)
    return o.reshape(BATCH, d, seg, B_WIDTH), lse.reshape(BATCH, d, seg, LANES)


def _route(x, rows, wh_ref, wl_ref, rb_ref, tri_ref, xa_ref, cls_ref, rank_ref, cnt_ref, carry_ref):
    tm = x.shape[0]
    xh, xl = _split_bf16(x)
    logits = _dot_nt(wh_ref[...], xh) + _dot_nt(wh_ref[...], xl) + _dot_nt(wl_ref[...], xh)
    aff = 1.0 / (1.0 + jnp.exp(-logits))
    sel = aff + rb_ref[...]
    s = [sel[e:e + 1, :] for e in range(N_EXPERTS)]
    a = [aff[e:e + 1, :] for e in range(N_EXPERTS)]

    def top2_sum(v):
        hi01, lo01 = jnp.maximum(v[0], v[1]), jnp.minimum(v[0], v[1])
        hi23, lo23 = jnp.maximum(v[2], v[3]), jnp.minimum(v[2], v[3])
        return jnp.maximum(hi01, hi23) + jnp.maximum(jnp.minimum(hi01, hi23), jnp.maximum(lo01, lo23))

    gscore = [top2_sum(s[EXPERTS_PER_GROUP * g:EXPERTS_PER_GROUP * (g + 1)]) for g in range(N_EXPERT_GROUPS)]
    best = jnp.zeros((1, tm), jnp.int32)
    best_score = gscore[0]
    for g in range(1, N_EXPERT_GROUPS):
        better = gscore[g] > best_score
        best = jnp.where(better, g, best)
        best_score = jnp.where(better, gscore[g], best_score)

    def pick(rows, j):
        out = rows[j]
        for g in range(1, N_EXPERT_GROUPS):
            out = jnp.where(best == g, rows[EXPERTS_PER_GROUP * g + j], out)
        return out

    t = [pick(s, j) for j in range(EXPERTS_PER_GROUP)]
    w = [pick(a, j) for j in range(EXPERTS_PER_GROUP)]

    def first_max(v):
        mx = jnp.maximum(jnp.maximum(v[0], v[1]), jnp.maximum(v[2], v[3]))
        idx = jnp.full((1, tm), EXPERTS_PER_GROUP - 1, jnp.int32)
        for j in range(EXPERTS_PER_GROUP - 2, -1, -1):
            idx = jnp.where(v[j] == mx, j, idx)
        return idx

    i1 = first_max(t)
    i2 = first_max([jnp.where(i1 == j, -jnp.inf, t[j]) for j in range(EXPERTS_PER_GROUP)])
    lo, hi = jnp.minimum(i1, i2), jnp.maximum(i1, i2)

    def take(rows, idx):
        out = rows[0]
        for j in range(1, EXPERTS_PER_GROUP):
            out = jnp.where(idx == j, rows[j], out)
        return out

    w_lo, w_hi = take(w, lo), take(w, hi)
    den = w_lo + w_hi
    pair = jnp.where(lo == 0, hi - 1, jnp.where(lo == 1, hi + 1, len(PAIRS) - 1))
    cls = best * len(PAIRS) + pair

    onehot = (lax.broadcasted_iota(jnp.int32, (CLASS_ROWS, tm), 0) == cls).astype(F32)
    before = _dot(onehot.astype(BF16), tri_ref[...]) + carry_ref[:, 0:1]
    rank = jnp.sum(onehot * before, axis=0, keepdims=True)
    carry_ref[...] = carry_ref[...] + jnp.sum(onehot, axis=1, keepdims=True)

    cls_ref[0, :, rows] = cls
    rank_ref[0, :, rows] = rank.astype(jnp.int32)
    cnt_ref[...] = carry_ref[...]
    gates = jnp.concatenate([w_lo / den, w_hi / den, jnp.zeros((LANES - 2, tm), F32)], axis=0)
    xa_ref[rows, :D_MODEL] = x
    xa_ref[rows, D_MODEL:] = gates.T


def _post_attn_kernel(*refs, merge):
    if merge:
        o_refs, l_refs, e_ref, refs = refs[:3], refs[3:6], refs[6], refs[7:]
    else:
        mix_ref, refs = refs[0], refs[1:]
    (qm_ref, mk_ref, mv_ref, w1_ref, w2_ref, x_ref, g_ref, b_ref, wh_ref, wl_ref, rb_ref, tri_ref,
     xa_ref, cls_ref, rank_ref, cnt_ref, carry_ref) = refs[:17]

    @pl.when(pl.program_id(0) == 0)
    def _():
        carry_ref[...] = jnp.zeros_like(carry_ref)

    first = _first_head((TOK_TILE, LANES))
    for sub in range(POST_SUB):
        rows = slice(sub * TOK_TILE, (sub + 1) * TOK_TILE)
        if merge:
            ot_ref, lt_ref = refs[17], refs[18]
            nch = B_WIDTH // LANES
            for g, d in enumerate(DILATIONS):
                n = TOK_TILE // d
                for r in range(d):
                    dst = pl.ds(r, n, stride=d) if d > 1 else slice(None)
                    og = o_refs[g][0, r, sub * n:(sub + 1) * n].astype(F32)
                    for ch in range(nch):
                        ot_ref[sub, g, ch, dst, :] = og[:, ch * LANES:(ch + 1) * LANES]
                    lt_ref[sub, g, dst, :] = l_refs[g][0, r, sub * n:(sub + 1) * n]
            lses = [lt_ref[sub, g] for g in range(len(DILATIONS))]
            m = functools.reduce(jnp.maximum, lses)
            es = [jnp.exp(l - m) for l in lses]
            inv = 1.0 / functools.reduce(jnp.add, es)
            expand = e_ref[...]

            def widen(w):
                hi, lo = _split_bf16(w)
                return _dot(hi, expand) + _dot(lo, expand)

            ws = [widen(e * inv) for e in es]
            mix = jnp.concatenate(
                [functools.reduce(jnp.add, [w[:, ch * LANES:(ch + 1) * LANES] * ot_ref[sub, g, ch]
                                            for g, w in enumerate(ws)]) for ch in range(nch)], axis=1).astype(BF16)
        else:
            mix = mix_ref[rows, :]

        memo = []
        for j in range(MEM_W // LANES):
            sl = slice(j * LANES, (j + 1) * LANES)
            memo.append(_pair_output(_pair_attention(qm_ref[rows, sl], mk_ref[:, sl], _with_ones(mv_ref[:, sl]), first),
                                     first))
        memo = jnp.concatenate(memo, axis=1).astype(BF16)

        attn = _dot(mix, w1_ref[...]) + _dot(memo, w2_ref[...])
        x1 = _layer_norm(ALPHA * x_ref[rows, :] + attn, g_ref[...], b_ref[...])
        _route(x1, rows, wh_ref, wl_ref, rb_ref, tri_ref, xa_ref, cls_ref, rank_ref, cnt_ref, carry_ref)


def _post_attn(mix, qm, mkv, w1, w2, x, g, b, router, expand=None):
    merge = expand is not None
    wh, wl, rb, tri = router
    blk = POST_SUB * TOK_TILE
    nblk = N_TOK // blk
    per_seq = SEQ // blk
    row = lambda width: pl.BlockSpec((blk, width), lambda i: (i, 0))
    const = lambda shape: pl.BlockSpec(shape, lambda i: (0,) * len(shape))
    row3 = pl.BlockSpec((1, 1, blk), lambda i: (i, 0, 0))
    scratch = [pltpu.VMEM((CLASS_ROWS, LANES), F32)]
    if merge:
        outs, lses = mix
        resid = lambda d, width: pl.BlockSpec((1, d, blk // d, width), lambda i: (i // per_seq, 0, i % per_seq, 0))
        lead_specs = ([resid(d, B_WIDTH) for d in DILATIONS] + [resid(d, LANES) for d in DILATIONS]
                      + [const(expand.shape)])
        lead = [*outs, *lses, expand]
        scratch += [pltpu.VMEM((POST_SUB, len(DILATIONS), B_WIDTH // LANES, TOK_TILE, LANES), F32),
                    pltpu.VMEM((POST_SUB, len(DILATIONS), TOK_TILE, LANES), F32)]
    else:
        lead_specs, lead = [row(mix.shape[1])], [mix]
    xa, cls, rank, cnt = pl.pallas_call(
        functools.partial(_post_attn_kernel, merge=merge),
        grid=(nblk,),
        in_specs=lead_specs + [row(MEM_W),
                               pl.BlockSpec((N_MEM, MEM_W), lambda i: (i // per_seq, 0)),
                               pl.BlockSpec((N_MEM, MEM_W), lambda i: (i // per_seq, 1)),
                               const(w1.shape), const(w2.shape), row(D_MODEL), const((1, D_MODEL)), const((1, D_MODEL)),
                               const(wh.shape), const(wl.shape), const(rb.shape), const(tri.shape)],
        out_specs=[row(XA_W), row3, row3, const((CLASS_ROWS, LANES))],
        out_shape=[jax.ShapeDtypeStruct((N_TOK, XA_W), F32), jax.ShapeDtypeStruct((nblk, 1, blk), jnp.int32),
                   jax.ShapeDtypeStruct((nblk, 1, blk), jnp.int32), jax.ShapeDtypeStruct((CLASS_ROWS, LANES), F32)],
        scratch_shapes=scratch,
        compiler_params=_params(("arbitrary",)),
        name="post_attn_merge" if merge else "post_attn",
    )(*lead, qm, mkv, mkv, w1, w2, x, g, b, wh, wl, rb, tri)
    return xa, cls.reshape(N_TOK), rank.reshape(N_TOK), cnt[:N_CLASSES, 0].astype(jnp.int32)


def _dispatch_kernel(pos_ref, ends_ref, xa_ref, xs_ref, zeros, sem, zsem, *, chunk):
    base = pl.program_id(0) * chunk

    @pl.when(pl.program_id(0) == 0)
    def _():
        zeros[...] = jnp.zeros_like(zeros)

        def tail(c):
            start = pl.multiple_of(ends_ref[c] - MOE_TM, MOE_TM)
            return pltpu.make_async_copy(zeros, xs_ref.at[pl.ds(start, MOE_TM)], zsem)

        def nonempty(c):
            return ends_ref[c] > (ends_ref[c - 1] if c else 0)

        def spare(t):
            return pltpu.make_async_copy(zeros, xs_ref.at[pl.ds(pl.multiple_of(t * MOE_TM, MOE_TM), MOE_TM)], zsem)

        first_spare = ends_ref[N_CLASSES - 1] // MOE_TM
        for c in range(N_CLASSES):
            pl.when(nonempty(c))(lambda c=c: tail(c).start())
        lax.fori_loop(first_spare, MOE_TILES, lambda t, carry: (spare(t).start(), carry)[1], 0)
        for c in range(N_CLASSES):
            pl.when(nonempty(c))(lambda c=c: tail(c).wait())
        lax.fori_loop(first_spare, MOE_TILES, lambda t, carry: (spare(t).wait(), carry)[1], 0)

    def issue(blk, carry):
        for u in range(ROW_DMA_UNROLL):
            r = blk * ROW_DMA_UNROLL + u
            pltpu.make_async_copy(xa_ref.at[pl.ds(r, 1)], xs_ref.at[pl.ds(pos_ref[base + r], 1)], sem).start(
                priority=u % 2)
        return carry

    def retire(r, carry):
        pltpu.make_async_copy(xa_ref.at[pl.ds(0, 1)], xs_ref.at[pl.ds(0, 1)], sem).wait()
        return carry

    lax.fori_loop(0, chunk // ROW_DMA_UNROLL, issue, 0)
    lax.fori_loop(0, chunk, retire, 0, unroll=8)


def _dispatch(pos, ends, xa, chunk=1024):
    return pl.pallas_call(
        functools.partial(_dispatch_kernel, chunk=chunk),
        grid_spec=pltpu.PrefetchScalarGridSpec(
            num_scalar_prefetch=2,
            grid=(N_TOK // chunk,),
            in_specs=[pl.BlockSpec((chunk, XA_W), lambda i, p, e: (i, 0))],
            out_specs=pl.BlockSpec(memory_space=pl.ANY),
            scratch_shapes=[pltpu.VMEM((MOE_TM, XA_W), F32), pltpu.SemaphoreType.DMA(()),
                            pltpu.SemaphoreType.DMA(())],
        ),
        out_shape=jax.ShapeDtypeStruct((MOE_ROWS, XA_W), F32),
        compiler_params=_params(("arbitrary",)),
        name="moe_dispatch",
    )(pos, ends, xa)


def _expert_kernel(slot1_ref, slot2_ref, used_ref, first_ref, grp_ref, next_ref, xs_ref, wg_hbm, wu_hbm, wd_hbm,
                   ys_ref, stage_g, stage_u, stage_d, wg_s, wu_s, wd_s, sem, *, layer):
    i = pl.program_id(0)
    streams = ((wg_hbm, stage_g, wg_s), (wu_hbm, stage_u, wu_s), (wd_hbm, stage_d, wd_s))

    def group_copy(group, k):
        first_expert = layer * N_EXPERTS + group * EXPERTS_PER_GROUP
        hbm, stage, _ = streams[k]
        return pltpu.make_async_copy(hbm.at[pl.ds(first_expert, EXPERTS_PER_GROUP)], stage, sem.at[k])

    @pl.when(i == 0)
    def _():
        for k in range(len(streams)):
            group_copy(grp_ref[0], k).start()

    @pl.when(first_ref[i] != 0)
    def _():
        for k, (_, stage, dst) in enumerate(streams):
            group_copy(grp_ref[i], k).wait()
            for e in range(EXPERTS_PER_GROUP):
                dst[e] = stage[e].astype(BF16)

        @pl.when(next_ref[i] >= 0)
        def _():
            for k in range(len(streams)):
                group_copy(next_ref[i], k).start()

    @pl.when(used_ref[i] != 0)
    def _():
        x = xs_ref[:, :D_MODEL].astype(BF16)
        gates = xs_ref[:, D_MODEL:]

        def expert(slot):
            gate = _dot(x, wg_s[slot])
            h = gate * (1.0 / (1.0 + jnp.exp(-gate))) * _dot(x, wu_s[slot])
            return _dot(h.astype(BF16), wd_s[slot])

        ys_ref[...] = gates[:, 0:1] * expert(slot1_ref[i]) + gates[:, 1:2] * expert(slot2_ref[i])

    @pl.when(used_ref[i] == 0)
    def _():
        ys_ref[...] = jnp.zeros_like(ys_ref)


def _experts(tile_meta, xs, layer, wg, wu, wd):
    nmeta = len(tile_meta)
    up_shape = (EXPERTS_PER_GROUP, D_MODEL, D_EXPERT)
    down_shape = (EXPERTS_PER_GROUP, D_EXPERT, D_MODEL)
    return pl.pallas_call(
        functools.partial(_expert_kernel, layer=layer),
        grid_spec=pltpu.PrefetchScalarGridSpec(
            num_scalar_prefetch=nmeta,
            grid=(MOE_TILES,),
            in_specs=[pl.BlockSpec((MOE_TM, XA_W), lambda i, *_: (i, 0)),
                      pl.BlockSpec(memory_space=pl.ANY), pl.BlockSpec(memory_space=pl.ANY),
                      pl.BlockSpec(memory_space=pl.ANY)],
            out_specs=pl.BlockSpec((MOE_TM, D_MODEL), lambda i, *_: (i, 0)),
            scratch_shapes=[pltpu.VMEM(up_shape, F32), pltpu.VMEM(up_shape, F32), pltpu.VMEM(down_shape, F32),
                            pltpu.VMEM(up_shape, BF16), pltpu.VMEM(up_shape, BF16), pltpu.VMEM(down_shape, BF16),
                            pltpu.SemaphoreType.DMA((3,))],
        ),
        out_shape=jax.ShapeDtypeStruct((MOE_ROWS, D_MODEL), F32),
        compiler_params=_params(("arbitrary",)),
        name="moe_experts",
    )(*tile_meta, xs, wg, wu, wd)


def _combine_kernel(pos_ref, ys_ref, x_ref, g_ref, b_ref, o_ref, buf, sem, *, tc):
    i = pl.program_id(0)
    slot = i % 2

    def gather(step, into):
        def issue(blk, carry):
            for u in range(ROW_DMA_UNROLL):
                r = blk * ROW_DMA_UNROLL + u
                row = pos_ref[step * tc + r]
                pltpu.make_async_copy(ys_ref.at[pl.ds(row, 1)], buf.at[into, pl.ds(r, 1)], sem.at[into]).start(
                    priority=u % 2)
            return carry

        lax.fori_loop(0, tc // ROW_DMA_UNROLL, issue, 0)

    @pl.when(i == 0)
    def _():
        gather(0, 0)

    @pl.when(i + 1 < pl.num_programs(0))
    def _():
        gather(i + 1, 1 - slot)

    def retire(r, carry):
        pltpu.make_async_copy(ys_ref.at[pl.ds(0, 1)], buf.at[slot, pl.ds(0, 1)], sem.at[slot]).wait()
        return carry

    lax.fori_loop(0, tc, retire, 0, unroll=8)
    o_ref[...] = _layer_norm(ALPHA * x_ref[...] + buf[slot], g_ref[...], b_ref[...])


def _combine_ln(pos, ys, xa, g, b, tc=512):
    return pl.pallas_call(
        functools.partial(_combine_kernel, tc=tc),
        grid_spec=pltpu.PrefetchScalarGridSpec(
            num_scalar_prefetch=1,
            grid=(N_TOK // tc,),
            in_specs=[pl.BlockSpec(memory_space=pl.ANY),
                      pl.BlockSpec((tc, D_MODEL), lambda i, p: (i, 0)),
                      pl.BlockSpec((1, D_MODEL), lambda i, p: (0, 0)),
                      pl.BlockSpec((1, D_MODEL), lambda i, p: (0, 0))],
            out_specs=pl.BlockSpec((tc, D_MODEL), lambda i, p: (i, 0)),
            scratch_shapes=[pltpu.VMEM((2, tc, D_MODEL), F32), pltpu.SemaphoreType.DMA((2,))],
        ),
        out_shape=jax.ShapeDtypeStruct((N_TOK, D_MODEL), F32),
        compiler_params=_params(("arbitrary",)),
        name="moe_combine_ln",
    )(pos, ys, xa, g, b)


def _moe_ln(routed, layer, wg, wu, wd, g, b):
    xa, cls, rank, counts = routed
    padded = (counts + MOE_TM - 1) // MOE_TM * MOE_TM
    ends = jnp.cumsum(padded)
    starts = ends - padded
    pos = starts[cls] + rank
    tile_start = jnp.arange(MOE_TILES, dtype=jnp.int32) * MOE_TM
    tile_used = (tile_start < ends[-1]).astype(jnp.int32)
    last_cls = jnp.max(jnp.where(counts > 0, jnp.arange(N_CLASSES), 0))
    tile_cls = jnp.minimum(jnp.sum(tile_start[:, None] >= ends[None, :], axis=1), last_cls)
    tile_cls = tile_cls.astype(jnp.int32)
    tile_grp = tile_cls // len(PAIRS)
    prev_grp = jnp.concatenate([jnp.full((1,), -1, jnp.int32), tile_grp[:-1]])
    tile_first = tile_used * (tile_grp != prev_grp).astype(jnp.int32)
    has = jnp.sum(counts.reshape(N_EXPERT_GROUPS, len(PAIRS)), axis=1) > 0
    gid = jnp.arange(N_EXPERT_GROUPS, dtype=jnp.int32)
    later = jnp.where(jnp.logical_and(has[None, :], gid[None, :] > gid[:, None]), gid[None, :], N_EXPERT_GROUPS)
    next_grp = jnp.min(later, axis=1)
    next_grp = jnp.where(next_grp == N_EXPERT_GROUPS, -1, next_grp)
    pair_slots = jnp.asarray(np.array(PAIRS, np.int32))[tile_cls % len(PAIRS)]
    tile_meta = (pair_slots[:, 0], pair_slots[:, 1], tile_used, tile_first, tile_grp, next_grp[tile_grp])
    xs = _dispatch(pos, ends.astype(jnp.int32), xa)
    ys = _experts(tile_meta, xs, layer, wg, wu, wd)
    return _combine_ln(pos, ys, xa, g, b)


def _tables(c, sa, sb):
    return tuple(np.concatenate([t, t], axis=-1).astype(np.float32) for t in (c, sa, sb))


def _axial_tables():
    rows = SEQ // GRID_W
    row = np.repeat(np.arange(rows), GRID_W).astype(np.float64)
    col = np.tile(np.arange(GRID_W), rows).astype(np.float64)
    half = HEAD_DIM // 2
    inv = 1.0 / (AXIAL_THETA ** (np.arange(0, half, 2, dtype=np.float64) / half))
    ar, ac = row[:, None] * inv, col[:, None] * inv
    z = np.zeros_like(ar)
    c = np.concatenate([np.cos(ar), np.cos(ar), np.cos(ac), np.cos(ac)], -1)
    sa = np.concatenate([-np.sin(ar), z, -np.sin(ac), z], -1)
    sb = np.concatenate([z, np.sin(ar), z, np.sin(ac)], -1)
    return _tables(c, sa, sb)


def _partial_tables():
    n = PARTIAL_ROT_DIMS
    pos = np.arange(SEQ, dtype=np.float64)
    ang = pos[:, None] * (1.0 / (PARTIAL_THETA ** (np.arange(0, n, 2, dtype=np.float64) / n)))
    z = np.zeros_like(ang)
    rest = np.zeros((SEQ, HEAD_DIM - n))
    c = np.concatenate([np.cos(ang), np.cos(ang), rest + 1.0], -1)
    sa = np.concatenate([-np.sin(ang), z, rest], -1)
    sb = np.concatenate([z, np.sin(ang), rest], -1)
    return _tables(c, sa, sb)


def _residue_order(tab, d):
    return np.ascontiguousarray(tab.reshape(SEQ // d, d, LANES).transpose(1, 0, 2))


def kernel(x, mem, w_mem_kv, router_w, router_b, a_w_in, a_w_out, a_q_norm, a_k_norm, b_w_in, b_w_out,
           ln1_g, ln1_b, ln2_g, ln2_b, w_gate, w_up, w_down):
    x = x.reshape(N_TOK, D_MODEL)
    row = lambda v: v.reshape(1, -1)

    lane = np.arange(LANES)
    m128 = jnp.asarray((lane[:, None] // HEAD_DIM == lane[None, :] // HEAD_DIM), BF16)
    expand = jnp.asarray(lane[:, None] == np.arange(B_WIDTH)[None, :] // HEAD_DIM, BF16)
    tri = jnp.asarray(np.triu(np.ones((TOK_TILE, TOK_TILE), np.float32), 1), BF16)
    rw_hi = router_w.T.astype(BF16)
    rw_lo = (router_w.T - rw_hi.astype(F32)).astype(BF16)
    router = (rw_hi, rw_lo, router_b.reshape(N_EXPERTS, 1), tri)

    mkv = _proj(mem.reshape(BATCH * N_MEM, D_MODEL), w_mem_kv.astype(BF16))
    experts = tuple(w.reshape(DEPTH * N_EXPERTS, *w.shape[2:]) for w in (w_gate, w_up, w_down))

    kmix = A_Q_HEADS * HEAD_DIM
    qcols = (np.asarray(_Q_HEAD_ORDER)[:, None] * HEAD_DIM + np.arange(HEAD_DIM)[None, :]).reshape(-1)
    w_in = jnp.concatenate([a_w_in[0][:, qcols], a_w_in[0][:, kmix:]], axis=1).astype(BF16)
    qg = row(jnp.tile(a_q_norm[0], 2) * (SCALE * LOG2_E))
    kg = row(jnp.tile(a_k_norm[0], 2))
    q, k, v, qm = _a_in_proj(x, w_in, m128, qg, kg, _axial_tables())
    mix = _gqa_attention(q, k, v)
    w_out = a_w_out[0].astype(BF16)
    routed = _post_attn(mix, qm, mkv, w_out[qcols], w_out[kmix:], x, row(ln1_g[0]), row(ln1_b[0]), router)
    x = _moe_ln(routed, 0, *experts, row(ln2_g[0]), row(ln2_b[0]))

    ptabs = _partial_tables()
    tabs = [tuple(_residue_order(t, d) for t in ptabs) for d in DILATIONS]
    qkv, qm = _b_in_proj(x, b_w_in[0].astype(BF16), tabs)
    outs, lses = zip(*[_band_attention(*qkv[g], d) for g, d in enumerate(DILATIONS)])
    w_out = b_w_out[0].astype(BF16)
    routed = _post_attn((outs, lses), qm, mkv, w_out[:B_WIDTH], w_out[B_WIDTH:], x, row(ln1_g[1]), row(ln1_b[1]),
                        router, expand)
    x = _moe_ln(routed, 1, *experts, row(ln2_g[1]), row(ln2_b[1]))
    return x.reshape(BATCH, SEQ, D_MODEL)
```

```python
import functools
import math

import numpy as np
import jax
import jax.numpy as jnp
from jax import lax
from jax.experimental import pallas as pl
from jax.experimental.pallas import tpu as pltpu

F32 = jnp.float32
BF16 = jnp.bfloat16

D_MODEL = 1024
BATCH = 8
SEQ = 2048
N_TOK = BATCH * SEQ
HEAD_DIM = 64
GRID_W = 64
N_MEM = 256
MEM_HEADS = 4
A_Q_HEADS = 12
A_KV_HEADS = 4
AXIAL_THETA = 10000.0
B_HEADS = 8
B_WIDTH = B_HEADS * HEAD_DIM
DILATIONS = (1, 4, 16)
BAND_RADIUS = 64
PARTIAL_ROT_DIMS = HEAD_DIM // 4
PARTIAL_THETA = 500000.0
N_EXPERTS = 16
N_EXPERT_GROUPS = 4
EXPERTS_PER_GROUP = 4
D_EXPERT = 512
DEPTH = 2
ALPHA = (2 * DEPTH) ** 0.25
NORM_EPS = 1e-6
NEG_BIG = -1e30
SCALE = HEAD_DIM ** -0.5
LOG2_E = math.log2(math.e)
_Q_HEAD_ORDER = tuple(6 * p + 3 * half + t for p in range(2) for t in range(3) for half in range(2))

LANES = 128
MEM_W = MEM_HEADS * HEAD_DIM
TOK_TILE = 512
TILES_PER_SEQ = SEQ // TOK_TILE
POST_SUB = 2
GQA_KC = 256
BAND_TQ = 128
BAND_TILES = 8

PAIRS = ((0, 1), (0, 2), (0, 3), (1, 2), (1, 3), (2, 3))
N_CLASSES = N_EXPERT_GROUPS * len(PAIRS)
CLASS_ROWS = 32
MOE_TM = 256
MOE_ROWS = N_TOK + N_CLASSES * MOE_TM
MOE_TILES = MOE_ROWS // MOE_TM
XA_W = D_MODEL + LANES
ROW_DMA_UNROLL = 8

VMEM_LIMIT = 56 * 1024 * 1024


def _params(sem, vmem=VMEM_LIMIT):
    return pltpu.CompilerParams(dimension_semantics=sem, vmem_limit_bytes=vmem)


def _dot(a, b):
    return jnp.dot(a, b, preferred_element_type=F32)


def _dot_nt(a, b):
    return lax.dot_general(a, b, (((1,), (1,)), ((), ())), preferred_element_type=F32)


def _split_bf16(x):
    hi = x.astype(BF16)
    lo = (x - hi.astype(F32)).astype(BF16)
    return hi, lo


def _layer_norm(z, g, b):
    mu = jnp.mean(z, axis=-1, keepdims=True)
    zc = z - mu
    var = jnp.mean(zc * zc, axis=-1, keepdims=True)
    return zc * lax.rsqrt(var + NORM_EPS) * g + b


def _rope(z, c, sa, sb, shift):
    return z * c + pltpu.roll(z, LANES - shift, 1) * sa + pltpu.roll(z, shift, 1) * sb


def _first_head(shape):
    return lax.broadcasted_iota(jnp.int32, shape, 1) < HEAD_DIM


def _pair_attention(q2, k2, v3, first, valid=None, base2=False):
    res = []
    for pick in (first, jnp.logical_not(first)):
        s = _dot_nt(jnp.where(pick, q2, jnp.zeros_like(q2)), k2)
        if valid is not None:
            s = jnp.where(valid, s, NEG_BIG)
        m = jnp.max(s, axis=1, keepdims=True)
        p = jnp.exp2(s - m) if base2 else jnp.exp(s - m)
        res.append((_dot(p.astype(BF16), v3), m))
    return res


def _pair_output(res, first):
    (ol_a, _), (ol_b, _) = res
    return jnp.where(first, ol_a[:, :LANES], ol_b[:, :LANES]) / jnp.where(first, ol_a[:, LANES:], ol_b[:, LANES:])


def _with_ones(v2):
    return jnp.concatenate([v2, jnp.ones_like(v2)], axis=1)


def _proj_kernel(x_ref, w_ref, o_ref):
    o_ref[...] = _dot(x_ref[...].astype(BF16), w_ref[...]).astype(o_ref.dtype)


def _proj(x, w, tm=TOK_TILE):
    n, k = x.shape
    width = w.shape[1]
    return pl.pallas_call(
        _proj_kernel,
        grid=(n // tm,),
        in_specs=[pl.BlockSpec((tm, k), lambda i: (i, 0)), pl.BlockSpec((k, width), lambda i: (0, 0))],
        out_specs=pl.BlockSpec((tm, width), lambda i: (i, 0)),
        out_shape=jax.ShapeDtypeStruct((n, width), BF16),
        compiler_params=_params(("parallel",)),
        name="proj",
    )(x, w)


def _a_in_kernel(x_ref, w_ref, m_ref, qg_ref, kg_ref, c_ref, sa_ref, sb_ref, q_ref, k_ref, v_ref, qm_ref):
    y = _dot(x_ref[...].astype(BF16), w_ref[...])
    m = m_ref[...]
    c, sa, sb = c_ref[...], sa_ref[...], sb_ref[...]

    def norm_rope(z, gain):
        hi, lo = _split_bf16(z * z)
        ms = (_dot(hi, m) + _dot(lo, m)) * (1.0 / HEAD_DIM)
        z = z * lax.rsqrt(ms + NORM_EPS) * gain
        return _rope(z, c, sa, sb, 16)

    nq = A_Q_HEADS * HEAD_DIM // LANES
    nk = A_KV_HEADS * HEAD_DIM // LANES
    for ch in range(nq):
        q_ref[:, ch * LANES:(ch + 1) * LANES] = norm_rope(y[:, ch * LANES:(ch + 1) * LANES], qg_ref[...]).astype(BF16)
    off = nq * LANES
    for ch in range(nk):
        z = y[:, off + ch * LANES: off + (ch + 1) * LANES]
        k_ref[:, ch * LANES:(ch + 1) * LANES] = norm_rope(z, kg_ref[...]).astype(BF16)
    off += nk * LANES
    ones = jnp.ones((y.shape[0], LANES), BF16)
    for ch in range(nk):
        v_ref[:, 2 * ch * LANES:(2 * ch + 1) * LANES] = y[:, off + ch * LANES: off + (ch + 1) * LANES].astype(BF16)
        v_ref[:, (2 * ch + 1) * LANES:(2 * ch + 2) * LANES] = ones
    off += nk * LANES
    qm_ref[...] = (y[:, off:off + MEM_W] * SCALE).astype(BF16)


def _a_in_proj(x, w, m128, qg, kg, tabs, tm=2 * TOK_TILE):
    c, sa, sb = tabs
    nblk = SEQ // tm
    a_in = w.shape[1]
    tab_spec = pl.BlockSpec((tm, LANES), lambda i: (i % nblk, 0))
    row = lambda width: pl.BlockSpec((tm, width), lambda i: (i, 0))
    const = lambda shape: pl.BlockSpec(shape, lambda i: (0, 0))
    qw, kw = A_Q_HEADS * HEAD_DIM, A_KV_HEADS * HEAD_DIM
    return pl.pallas_call(
        _a_in_kernel,
        grid=(N_TOK // tm,),
        in_specs=[row(D_MODEL), const((D_MODEL, a_in)), const((LANES, LANES)), const((1, LANES)), const((1, LANES)),
                  tab_spec, tab_spec, tab_spec],
        out_specs=[row(qw), row(kw), row(2 * kw), row(MEM_W)],
        out_shape=[jax.ShapeDtypeStruct((N_TOK, qw), BF16), jax.ShapeDtypeStruct((N_TOK, kw), BF16),
                   jax.ShapeDtypeStruct((N_TOK, 2 * kw), BF16), jax.ShapeDtypeStruct((N_TOK, MEM_W), BF16)],
        compiler_params=_params(("parallel",)),
        name="a_in_proj",
    )(x, w, m128, qg, kg, c, sa, sb)


def _b_in_kernel(*refs):
    ng = len(DILATIONS)
    nx = D_MODEL // LANES
    x_refs, w_ref, refs = refs[:nx], refs[nx], refs[nx + 1:]
    tabs, outs, qm_ref = refs[:3 * ng], refs[3 * ng:6 * ng], refs[6 * ng]
    xb1 = None
    for g, d in enumerate(DILATIONS):
        n = TOK_TILE // d
        if d == 1:
            cols = [r[...] for r in x_refs]
        else:
            cols = [jnp.concatenate([r[pl.ds(k, n, stride=d), :] for k in range(d)], axis=0) for r in x_refs]
        xb = jnp.concatenate(cols, axis=1).astype(BF16)
        xb1 = xb if d == 1 else xb1
        part = pl.ds(pl.multiple_of((pl.program_id(0) % TILES_PER_SEQ) * n, n), n)
        c, sa, sb = (t[:, part, :].reshape(TOK_TILE, LANES) for t in tabs[3 * g:3 * g + 3])
        q_ref, k_ref, v_ref = outs[3 * g:3 * g + 3]
        yq = _dot(xb, w_ref[:, g * B_WIDTH:(g + 1) * B_WIDTH]) * SCALE
        yk = _dot(xb, w_ref[:, (ng + g) * B_WIDTH:(ng + g + 1) * B_WIDTH])
        yv = _dot(xb, w_ref[:, (2 * ng + g) * B_WIDTH:(2 * ng + g + 1) * B_WIDTH])
        for ch in range(B_WIDTH // LANES):
            sl = slice(ch * LANES, (ch + 1) * LANES)
            q_ref[0, :, :, sl] = _rope(yq[:, sl], c, sa, sb, PARTIAL_ROT_DIMS // 2).reshape(d, n, LANES).astype(BF16)
            k_ref[0, :, :, sl] = _rope(yk[:, sl], c, sa, sb, PARTIAL_ROT_DIMS // 2).reshape(d, n, LANES).astype(BF16)
        v_ref[0] = yv.reshape(d, n, B_WIDTH).astype(BF16)
    qm = _dot(xb1, w_ref[:, 3 * ng * B_WIDTH:3 * ng * B_WIDTH + MEM_W])
    qm_ref[...] = (qm * SCALE).astype(BF16)


def _b_in_proj(x, w, tabs):
    nx = D_MODEL // LANES
    in_specs = [pl.BlockSpec((TOK_TILE, LANES), functools.partial(lambda i, c: (i, c), c=c)) for c in range(nx)]
    resident = pl.Buffered(1)
    in_specs.append(pl.BlockSpec(w.shape, lambda i: (0, 0), pipeline_mode=resident))
    out_specs, out_shape = [], []
    for d in DILATIONS:
        n = TOK_TILE // d
        in_specs += [pl.BlockSpec((d, SEQ // d, LANES), lambda i: (0, 0, 0), pipeline_mode=resident)] * 3
        out_specs += [pl.BlockSpec((1, d, n, B_WIDTH), lambda i: (i // TILES_PER_SEQ, 0, i % TILES_PER_SEQ, 0))] * 3
        out_shape += [jax.ShapeDtypeStruct((BATCH, d, SEQ // d, B_WIDTH), BF16)] * 3
    out_specs.append(pl.BlockSpec((TOK_TILE, MEM_W), lambda i: (i, 0)))
    out_shape.append(jax.ShapeDtypeStruct((N_TOK, MEM_W), BF16))
    res = pl.pallas_call(
        _b_in_kernel,
        grid=(N_TOK // TOK_TILE,),
        in_specs=in_specs,
        out_specs=out_specs,
        out_shape=out_shape,
        compiler_params=_params(("parallel",)),
        name="b_in_proj",
    )(*([x] * nx), w, *[t for group in tabs for t in group])
    return [res[3 * g:3 * g + 3] for g in range(len(DILATIONS))], res[-1]


def _gqa_kernel(q_ref, k_ref, v_ref, o_ref):
    tq = q_ref.shape[0]
    ntile = q_ref.shape[1] // LANES
    q = jnp.concatenate([q_ref[:, t * LANES:(t + 1) * LANES] for t in range(ntile)], axis=0)
    first = _first_head(q.shape)
    res = []
    for pick in (first, jnp.logical_not(first)):
        qh = jnp.where(pick, q, jnp.zeros_like(q))
        m = jnp.full((q.shape[0], 1), -jnp.inf, F32)
        acc = jnp.zeros((q.shape[0], 2 * LANES), F32)
        for c in range(SEQ // GQA_KC):
            keys = slice(c * GQA_KC, (c + 1) * GQA_KC)
            s = _dot_nt(qh, k_ref[keys, :])
            m_new = jnp.maximum(m, jnp.max(s, axis=1, keepdims=True))
            p = jnp.exp2(s - m_new).astype(BF16)
            acc = acc * jnp.exp2(m - m_new) + _dot(p, v_ref[keys, :])
            m = m_new
        res.append((acc, m))
    o = _pair_output(res, first)
    for t in range(ntile):
        o_ref[:, t * LANES:(t + 1) * LANES] = o[t * tq:(t + 1) * tq].astype(BF16)


def _gqa_attention(q, k, v, tq=1024):
    nq = SEQ // tq
    qw = q.shape[1] // 2
    kw = k.shape[1] // 2
    return pl.pallas_call(
        _gqa_kernel,
        grid=(BATCH, 2, nq),
        in_specs=[pl.BlockSpec((tq, qw), lambda b, p, i: (b * nq + i, p)),
                  pl.BlockSpec((SEQ, kw), lambda b, p, i: (b, p)),
                  pl.BlockSpec((SEQ, 2 * kw), lambda b, p, i: (b, p))],
        out_specs=pl.BlockSpec((tq, qw), lambda b, p, i: (b * nq + i, p)),
        out_shape=jax.ShapeDtypeStruct(q.shape, BF16),
        compiler_params=_params(("parallel", "parallel", "parallel")),
        name="gqa_attention",
    )(q, k, v)


def _band_kernel(q_ref, k_ref, v_ref, o_ref, lse_ref, *, length, seg, win):
    tq = BAND_TQ
    first = _first_head((tq, LANES))
    lane = lax.broadcasted_iota(jnp.int32, (tq, LANES), 1)
    row = lax.broadcasted_iota(jnp.int32, (tq, win), 0)
    col = lax.broadcasted_iota(jnp.int32, (tq, win), 1)
    for u in range(BAND_TILES):
        tile = pl.program_id(1) * BAND_TILES + u
        start = pl.multiple_of(jnp.clip(tile * tq - BAND_RADIUS, 0, length - win), BAND_RADIUS)
        qpos, kpos = tile * tq + row, start + col
        valid = jnp.abs(qpos - kpos) <= BAND_RADIUS
        if seg < length:
            valid = jnp.logical_and(valid, qpos // seg == kpos // seg)
        rows = slice(u * tq, (u + 1) * tq)
        lse = jnp.zeros((tq, LANES), F32)
        for j in range(B_WIDTH // LANES):
            sl = slice(j * LANES, (j + 1) * LANES)
            res = _pair_attention(q_ref[0, rows, sl], k_ref[0, pl.ds(start, win), sl],
                                  _with_ones(v_ref[0, pl.ds(start, win), sl]), first, valid)
            o_ref[0, rows, sl] = _pair_output(res, first).astype(BF16)
            for half, (ol, m) in enumerate(res):
                lse = jnp.where(lane == 2 * j + half, m + jnp.log(ol[:, LANES:]), lse)
        lse_ref[0, rows, :] = lse


def _band_attention(q, k, v, d):
    seg = SEQ // d
    step_rows = BAND_TILES * BAND_TQ
    length = max(seg, step_rows)
    win = BAND_TQ + 2 * BAND_RADIUS
    nblk = N_TOK // length
    q, k, v = (t.reshape(nblk, length, B_WIDTH) for t in (q, k, v))
    seq_spec = pl.BlockSpec((1, length, B_WIDTH), lambda s, i: (s, 0, 0))
    o, lse = pl.pallas_call(
        functools.partial(_band_kernel, length=length, seg=seg, win=win),
        grid=(nblk, length // step_rows),
        in_specs=[pl.BlockSpec((1, step_rows, B_WIDTH), lambda s, i: (s, i, 0)), seq_spec, seq_spec],
        out_specs=[pl.BlockSpec((1, step_rows, B_WIDTH), lambda s, i: (s, i, 0)),
                   pl.BlockSpec((1, step_rows, LANES), lambda s, i: (s, i, 0))],
        out_shape=[jax.ShapeDtypeStruct((nblk, length, B_WIDTH), BF16),
                   jax.ShapeDtypeStruct((nblk, length, LANES), F32)],
        compiler_params=_params(("parallel", "parallel")),
        name=f"band_attention_d{d}",
    )(q, k, v)
    return o.reshape(BATCH, d, seg, B_WIDTH), lse.reshape(BATCH, d, seg, LANES)


def _route(x, rows, wh_ref, wl_ref, rb_ref, tri_ref, xa_ref, cls_ref, rank_ref, cnt_ref, carry_ref):
    tm = x.shape[0]
    xh, xl = _split_bf16(x)
    logits = _dot_nt(wh_ref[...], xh) + _dot_nt(wh_ref[...], xl) + _dot_nt(wl_ref[...], xh)
    aff = 1.0 / (1.0 + jnp.exp(-logits))
    sel = aff + rb_ref[...]
    s = [sel[e:e + 1, :] for e in range(N_EXPERTS)]
    a = [aff[e:e + 1, :] for e in range(N_EXPERTS)]

    def top2_sum(v):
        hi01, lo01 = jnp.maximum(v[0], v[1]), jnp.minimum(v[0], v[1])
        hi23, lo23 = jnp.maximum(v[2], v[3]), jnp.minimum(v[2], v[3])
        return jnp.maximum(hi01, hi23) + jnp.maximum(jnp.minimum(hi01, hi23), jnp.maximum(lo01, lo23))

    gscore = [top2_sum(s[EXPERTS_PER_GROUP * g:EXPERTS_PER_GROUP * (g + 1)]) for g in range(N_EXPERT_GROUPS)]
    best = jnp.zeros((1, tm), jnp.int32)
    best_score = gscore[0]
    for g in range(1, N_EXPERT_GROUPS):
        better = gscore[g] > best_score
        best = jnp.where(better, g, best)
        best_score = jnp.where(better, gscore[g], best_score)

    def pick(rows, j):
        out = rows[j]
        for g in range(1, N_EXPERT_GROUPS):
            out = jnp.where(best == g, rows[EXPERTS_PER_GROUP * g + j], out)
        return out

    t = [pick(s, j) for j in range(EXPERTS_PER_GROUP)]
    w = [pick(a, j) for j in range(EXPERTS_PER_GROUP)]

    def first_max(v):
        mx = jnp.maximum(jnp.maximum(v[0], v[1]), jnp.maximum(v[2], v[3]))
        idx = jnp.full((1, tm), EXPERTS_PER_GROUP - 1, jnp.int32)
        for j in range(EXPERTS_PER_GROUP - 2, -1, -1):
            idx = jnp.where(v[j] == mx, j, idx)
        return idx

    i1 = first_max(t)
    i2 = first_max([jnp.where(i1 == j, -jnp.inf, t[j]) for j in range(EXPERTS_PER_GROUP)])
    lo, hi = jnp.minimum(i1, i2), jnp.maximum(i1, i2)

    def take(rows, idx):
        out = rows[0]
        for j in range(1, EXPERTS_PER_GROUP):
            out = jnp.where(idx == j, rows[j], out)
        return out

    w_lo, w_hi = take(w, lo), take(w, hi)
    den = w_lo + w_hi
    pair = jnp.where(lo == 0, hi - 1, jnp.where(lo == 1, hi + 1, len(PAIRS) - 1))
    cls = best * len(PAIRS) + pair

    onehot = (lax.broadcasted_iota(jnp.int32, (CLASS_ROWS, tm), 0) == cls).astype(F32)
    before = _dot(onehot.astype(BF16), tri_ref[...]) + carry_ref[:, 0:1]
    rank = jnp.sum(onehot * before, axis=0, keepdims=True)
    carry_ref[...] = carry_ref[...] + jnp.sum(onehot, axis=1, keepdims=True)

    cls_ref[0, :, rows] = cls
    rank_ref[0, :, rows] = rank.astype(jnp.int32)
    cnt_ref[...] = carry_ref[...]
    gates = jnp.concatenate([w_lo / den, w_hi / den, jnp.zeros((LANES - 2, tm), F32)], axis=0)
    xa_ref[rows, :D_MODEL] = x
    xa_ref[rows, D_MODEL:] = gates.T


def _post_attn_kernel(*refs, merge):
    if merge:
        o_refs, l_refs, e_ref, refs = refs[:3], refs[3:6], refs[6], refs[7:]
    else:
        mix_ref, refs = refs[0], refs[1:]
    (qm_ref, mk_ref, mv_ref, w1_ref, w2_ref, x_ref, g_ref, b_ref, wh_ref, wl_ref, rb_ref, tri_ref,
     xa_ref, cls_ref, rank_ref, cnt_ref, carry_ref) = refs[:17]

    @pl.when(pl.program_id(0) == 0)
    def _():
        carry_ref[...] = jnp.zeros_like(carry_ref)

    first = _first_head((TOK_TILE, LANES))
    for sub in range(POST_SUB):
        rows = slice(sub * TOK_TILE, (sub + 1) * TOK_TILE)
        if merge:
            ot_ref, lt_ref = refs[17], refs[18]
            nch = B_WIDTH // LANES
            for g, d in enumerate(DILATIONS):
                n = TOK_TILE // d
                for r in range(d):
                    dst = pl.ds(r, n, stride=d) if d > 1 else slice(None)
                    og = o_refs[g][0, r, sub * n:(sub + 1) * n].astype(F32)
                    for ch in range(nch):
                        ot_ref[sub, g, ch, dst, :] = og[:, ch * LANES:(ch + 1) * LANES]
                    lt_ref[sub, g, dst, :] = l_refs[g][0, r, sub * n:(sub + 1) * n]
            lses = [lt_ref[sub, g] for g in range(len(DILATIONS))]
            m = functools.reduce(jnp.maximum, lses)
            es = [jnp.exp(l - m) for l in lses]
            inv = 1.0 / functools.reduce(jnp.add, es)
            expand = e_ref[...]

            def widen(w):
                hi, lo = _split_bf16(w)
                return _dot(hi, expand) + _dot(lo, expand)

            ws = [widen(e * inv) for e in es]
            mix = jnp.concatenate(
                [functools.reduce(jnp.add, [w[:, ch * LANES:(ch + 1) * LANES] * ot_ref[sub, g, ch]
                                            for g, w in enumerate(ws)]) for ch in range(nch)], axis=1).astype(BF16)
        else:
            mix = mix_ref[rows, :]

        memo = []
        for j in range(MEM_W // LANES):
            sl = slice(j * LANES, (j + 1) * LANES)
            memo.append(_pair_output(_pair_attention(qm_ref[rows, sl], mk_ref[:, sl], _with_ones(mv_ref[:, sl]), first),
                                     first))
        memo = jnp.concatenate(memo, axis=1).astype(BF16)

        attn = _dot(mix, w1_ref[...]) + _dot(memo, w2_ref[...])
        x1 = _layer_norm(ALPHA * x_ref[rows, :] + attn, g_ref[...], b_ref[...])
        _route(x1, rows, wh_ref, wl_ref, rb_ref, tri_ref, xa_ref, cls_ref, rank_ref, cnt_ref, carry_ref)


def _post_attn(mix, qm, mkv, w1, w2, x, g, b, router, expand=None):
    merge = expand is not None
    wh, wl, rb, tri = router
    blk = POST_SUB * TOK_TILE
    nblk = N_TOK // blk
    per_seq = SEQ // blk
    row = lambda width: pl.BlockSpec((blk, width), lambda i: (i, 0))
    const = lambda shape: pl.BlockSpec(shape, lambda i: (0,) * len(shape))
    row3 = pl.BlockSpec((1, 1, blk), lambda i: (i, 0, 0))
    scratch = [pltpu.VMEM((CLASS_ROWS, LANES), F32)]
    if merge:
        outs, lses = mix
        resid = lambda d, width: pl.BlockSpec((1, d, blk // d, width), lambda i: (i // per_seq, 0, i % per_seq, 0))
        lead_specs = ([resid(d, B_WIDTH) for d in DILATIONS] + [resid(d, LANES) for d in DILATIONS]
                      + [const(expand.shape)])
        lead = [*outs, *lses, expand]
        scratch += [pltpu.VMEM((POST_SUB, len(DILATIONS), B_WIDTH // LANES, TOK_TILE, LANES), F32),
                    pltpu.VMEM((POST_SUB, len(DILATIONS), TOK_TILE, LANES), F32)]
    else:
        lead_specs, lead = [row(mix.shape[1])], [mix]
    xa, cls, rank, cnt = pl.pallas_call(
        functools.partial(_post_attn_kernel, merge=merge),
        grid=(nblk,),
        in_specs=lead_specs + [row(MEM_W),
                               pl.BlockSpec((N_MEM, MEM_W), lambda i: (i // per_seq, 0)),
                               pl.BlockSpec((N_MEM, MEM_W), lambda i: (i // per_seq, 1)),
                               const(w1.shape), const(w2.shape), row(D_MODEL), const((1, D_MODEL)), const((1, D_MODEL)),
                               const(wh.shape), const(wl.shape), const(rb.shape), const(tri.shape)],
        out_specs=[row(XA_W), row3, row3, const((CLASS_ROWS, LANES))],
        out_shape=[jax.ShapeDtypeStruct((N_TOK, XA_W), F32), jax.ShapeDtypeStruct((nblk, 1, blk), jnp.int32),
                   jax.ShapeDtypeStruct((nblk, 1, blk), jnp.int32), jax.ShapeDtypeStruct((CLASS_ROWS, LANES), F32)],
        scratch_shapes=scratch,
        compiler_params=_params(("arbitrary",)),
        name="post_attn_merge" if merge else "post_attn",
    )(*lead, qm, mkv, mkv, w1, w2, x, g, b, wh, wl, rb, tri)
    return xa, cls.reshape(N_TOK), rank.reshape(N_TOK), cnt[:N_CLASSES, 0].astype(jnp.int32)


def _dispatch_kernel(pos_ref, ends_ref, xa_ref, xs_ref, zeros, sem, zsem, *, chunk):
    base = pl.program_id(0) * chunk

    @pl.when(pl.program_id(0) == 0)
    def _():
        zeros[...] = jnp.zeros_like(zeros)

        def tail(c):
            start = pl.multiple_of(ends_ref[c] - MOE_TM, MOE_TM)
            return pltpu.make_async_copy(zeros, xs_ref.at[pl.ds(start, MOE_TM)], zsem)

        def nonempty(c):
            return ends_ref[c] > (ends_ref[c - 1] if c else 0)

        def spare(t):
            return pltpu.make_async_copy(zeros, xs_ref.at[pl.ds(pl.multiple_of(t * MOE_TM, MOE_TM), MOE_TM)], zsem)

        first_spare = ends_ref[N_CLASSES - 1] // MOE_TM
        for c in range(N_CLASSES):
            pl.when(nonempty(c))(lambda c=c: tail(c).start())
        lax.fori_loop(first_spare, MOE_TILES, lambda t, carry: (spare(t).start(), carry)[1], 0)
        for c in range(N_CLASSES):
            pl.when(nonempty(c))(lambda c=c: tail(c).wait())
        lax.fori_loop(first_spare, MOE_TILES, lambda t, carry: (spare(t).wait(), carry)[1], 0)

    def issue(blk, carry):
        for u in range(ROW_DMA_UNROLL):
            r = blk * ROW_DMA_UNROLL + u
            pltpu.make_async_copy(xa_ref.at[pl.ds(r, 1)], xs_ref.at[pl.ds(pos_ref[base + r], 1)], sem).start(
                priority=u % 2)
        return carry

    def retire(r, carry):
        pltpu.make_async_copy(xa_ref.at[pl.ds(0, 1)], xs_ref.at[pl.ds(0, 1)], sem).wait()
        return carry

    lax.fori_loop(0, chunk // ROW_DMA_UNROLL, issue, 0)
    lax.fori_loop(0, chunk, retire, 0, unroll=8)


def _dispatch(pos, ends, xa, chunk=1024):
    return pl.pallas_call(
        functools.partial(_dispatch_kernel, chunk=chunk),
        grid_spec=pltpu.PrefetchScalarGridSpec(
            num_scalar_prefetch=2,
            grid=(N_TOK // chunk,),
            in_specs=[pl.BlockSpec((chunk, XA_W), lambda i, p, e: (i, 0))],
            out_specs=pl.BlockSpec(memory_space=pl.ANY),
            scratch_shapes=[pltpu.VMEM((MOE_TM, XA_W), F32), pltpu.SemaphoreType.DMA(()),
                            pltpu.SemaphoreType.DMA(())],
        ),
        out_shape=jax.ShapeDtypeStruct((MOE_ROWS, XA_W), F32),
        compiler_params=_params(("arbitrary",)),
        name="moe_dispatch",
    )(pos, ends, xa)


def _expert_kernel(slot1_ref, slot2_ref, used_ref, first_ref, grp_ref, next_ref, xs_ref, wg_hbm, wu_hbm, wd_hbm,
                   ys_ref, stage_g, stage_u, stage_d, wg_s, wu_s, wd_s, sem, *, layer):
    i = pl.program_id(0)
    streams = ((wg_hbm, stage_g, wg_s), (wu_hbm, stage_u, wu_s), (wd_hbm, stage_d, wd_s))

    def group_copy(group, k):
        first_expert = layer * N_EXPERTS + group * EXPERTS_PER_GROUP
        hbm, stage, _ = streams[k]
        return pltpu.make_async_copy(hbm.at[pl.ds(first_expert, EXPERTS_PER_GROUP)], stage, sem.at[k])

    @pl.when(i == 0)
    def _():
        for k in range(len(streams)):
            group_copy(grp_ref[0], k).start()

    @pl.when(first_ref[i] != 0)
    def _():
        for k, (_, stage, dst) in enumerate(streams):
            group_copy(grp_ref[i], k).wait()
            for e in range(EXPERTS_PER_GROUP):
                dst[e] = stage[e].astype(BF16)

        @pl.when(next_ref[i] >= 0)
        def _():
            for k in range(len(streams)):
                group_copy(next_ref[i], k).start()

    @pl.when(used_ref[i] != 0)
    def _():
        x = xs_ref[:, :D_MODEL].astype(BF16)
        gates = xs_ref[:, D_MODEL:]

        def expert(slot):
            gate = _dot(x, wg_s[slot])
            h = gate * (1.0 / (1.0 + jnp.exp(-gate))) * _dot(x, wu_s[slot])
            return _dot(h.astype(BF16), wd_s[slot])

        ys_ref[...] = gates[:, 0:1] * expert(slot1_ref[i]) + gates[:, 1:2] * expert(slot2_ref[i])

    @pl.when(used_ref[i] == 0)
    def _():
        ys_ref[...] = jnp.zeros_like(ys_ref)


def _experts(tile_meta, xs, layer, wg, wu, wd):
    nmeta = len(tile_meta)
    up_shape = (EXPERTS_PER_GROUP, D_MODEL, D_EXPERT)
    down_shape = (EXPERTS_PER_GROUP, D_EXPERT, D_MODEL)
    return pl.pallas_call(
        functools.partial(_expert_kernel, layer=layer),
        grid_spec=pltpu.PrefetchScalarGridSpec(
            num_scalar_prefetch=nmeta,
            grid=(MOE_TILES,),
            in_specs=[pl.BlockSpec((MOE_TM, XA_W), lambda i, *_: (i, 0)),
                      pl.BlockSpec(memory_space=pl.ANY), pl.BlockSpec(memory_space=pl.ANY),
                      pl.BlockSpec(memory_space=pl.ANY)],
            out_specs=pl.BlockSpec((MOE_TM, D_MODEL), lambda i, *_: (i, 0)),
            scratch_shapes=[pltpu.VMEM(up_shape, F32), pltpu.VMEM(up_shape, F32), pltpu.VMEM(down_shape, F32),
                            pltpu.VMEM(up_shape, BF16), pltpu.VMEM(up_shape, BF16), pltpu.VMEM(down_shape, BF16),
                            pltpu.SemaphoreType.DMA((3,))],
        ),
        out_shape=jax.ShapeDtypeStruct((MOE_ROWS, D_MODEL), F32),
        compiler_params=_params(("arbitrary",)),
        name="moe_experts",
    )(*tile_meta, xs, wg, wu, wd)


def _combine_kernel(pos_ref, ys_ref, x_ref, g_ref, b_ref, o_ref, buf, sem, *, tc):
    i = pl.program_id(0)
    slot = i % 2

    def gather(step, into):
        def issue(blk, carry):
            for u in range(ROW_DMA_UNROLL):
                r = blk * ROW_DMA_UNROLL + u
                row = pos_ref[step * tc + r]
                pltpu.make_async_copy(ys_ref.at[pl.ds(row, 1)], buf.at[into, pl.ds(r, 1)], sem.at[into]).start(
                    priority=u % 2)
            return carry

        lax.fori_loop(0, tc // ROW_DMA_UNROLL, issue, 0)

    @pl.when(i == 0)
    def _():
        gather(0, 0)

    @pl.when(i + 1 < pl.num_programs(0))
    def _():
        gather(i + 1, 1 - slot)

    def retire(r, carry):
        pltpu.make_async_copy(ys_ref.at[pl.ds(0, 1)], buf.at[slot, pl.ds(0, 1)], sem.at[slot]).wait()
        return carry

    lax.fori_loop(0, tc, retire, 0, unroll=8)
    o_ref[...] = _layer_norm(ALPHA * x_ref[...] + buf[slot], g_ref[...], b_ref[...])


def _combine_ln(pos, ys, xa, g, b, tc=512):
    return pl.pallas_call(
        functools.partial(_combine_kernel, tc=tc),
        grid_spec=pltpu.PrefetchScalarGridSpec(
            num_scalar_prefetch=1,
            grid=(N_TOK // tc,),
            in_specs=[pl.BlockSpec(memory_space=pl.ANY),
                      pl.BlockSpec((tc, D_MODEL), lambda i, p: (i, 0)),
                      pl.BlockSpec((1, D_MODEL), lambda i, p: (0, 0)),
                      pl.BlockSpec((1, D_MODEL), lambda i, p: (0, 0))],
            out_specs=pl.BlockSpec((tc, D_MODEL), lambda i, p: (i, 0)),
            scratch_shapes=[pltpu.VMEM((2, tc, D_MODEL), F32), pltpu.SemaphoreType.DMA((2,))],
        ),
        out_shape=jax.ShapeDtypeStruct((N_TOK, D_MODEL), F32),
        compiler_params=_params(("arbitrary",)),
        name="moe_combine_ln",
    )(pos, ys, xa, g, b)


def _moe_ln(routed, layer, wg, wu, wd, g, b):
    xa, cls, rank, counts = routed
    padded = (counts + MOE_TM - 1) // MOE_TM * MOE_TM
    ends = jnp.cumsum(padded)
    starts = ends - padded
    pos = starts[cls] + rank
    tile_start = jnp.arange(MOE_TILES, dtype=jnp.int32) * MOE_TM
    tile_used = (tile_start < ends[-1]).astype(jnp.int32)
    last_cls = jnp.max(jnp.where(counts > 0, jnp.arange(N_CLASSES), 0))
    tile_cls = jnp.minimum(jnp.sum(tile_start[:, None] >= ends[None, :], axis=1), last_cls)
    tile_cls = tile_cls.astype(jnp.int32)
    tile_grp = tile_cls // len(PAIRS)
    prev_grp = jnp.concatenate([jnp.full((1,), -1, jnp.int32), tile_grp[:-1]])
    tile_first = tile_used * (tile_grp != prev_grp).astype(jnp.int32)
    has = jnp.sum(counts.reshape(N_EXPERT_GROUPS, len(PAIRS)), axis=1) > 0
    gid = jnp.arange(N_EXPERT_GROUPS, dtype=jnp.int32)
    later = jnp.where(jnp.logical_and(has[None, :], gid[None, :] > gid[:, None]), gid[None, :], N_EXPERT_GROUPS)
    next_grp = jnp.min(later, axis=1)
    next_grp = jnp.where(next_grp == N_EXPERT_GROUPS, -1, next_grp)
    pair_slots = jnp.asarray(np.array(PAIRS, np.int32))[tile_cls % len(PAIRS)]
    tile_meta = (pair_slots[:, 0], pair_slots[:, 1], tile_used, tile_first, tile_grp, next_grp[tile_grp])
    xs = _dispatch(pos, ends.astype(jnp.int32), xa)
    ys = _experts(tile_meta, xs, layer, wg, wu, wd)
    return _combine_ln(pos, ys, xa, g, b)


def _tables(c, sa, sb):
    return tuple(np.concatenate([t, t], axis=-1).astype(np.float32) for t in (c, sa, sb))


def _axial_tables():
    rows = SEQ // GRID_W
    row = np.repeat(np.arange(rows), GRID_W).astype(np.float64)
    col = np.tile(np.arange(GRID_W), rows).astype(np.float64)
    half = HEAD_DIM // 2
    inv = 1.0 / (AXIAL_THETA ** (np.arange(0, half, 2, dtype=np.float64) / half))
    ar, ac = row[:, None] * inv, col[:, None] * inv
    z = np.zeros_like(ar)
    c = np.concatenate([np.cos(ar), np.cos(ar), np.cos(ac), np.cos(ac)], -1)
    sa = np.concatenate([-np.sin(ar), z, -np.sin(ac), z], -1)
    sb = np.concatenate([z, np.sin(ar), z, np.sin(ac)], -1)
    return _tables(c, sa, sb)


def _partial_tables():
    n = PARTIAL_ROT_DIMS
    pos = np.arange(SEQ, dtype=np.float64)
    ang = pos[:, None] * (1.0 / (PARTIAL_THETA ** (np.arange(0, n, 2, dtype=np.float64) / n)))
    z = np.zeros_like(ang)
    rest = np.zeros((SEQ, HEAD_DIM - n))
    c = np.concatenate([np.cos(ang), np.cos(ang), rest + 1.0], -1)
    sa = np.concatenate([-np.sin(ang), z, rest], -1)
    sb = np.concatenate([z, np.sin(ang), rest], -1)
    return _tables(c, sa, sb)


def _residue_order(tab, d):
    return np.ascontiguousarray(tab.reshape(SEQ // d, d, LANES).transpose(1, 0, 2))


def kernel(x, mem, w_mem_kv, router_w, router_b, a_w_in, a_w_out, a_q_norm, a_k_norm, b_w_in, b_w_out,
           ln1_g, ln1_b, ln2_g, ln2_b, w_gate, w_up, w_down):
    x = x.reshape(N_TOK, D_MODEL)
    row = lambda v: v.reshape(1, -1)

    lane = np.arange(LANES)
    m128 = jnp.asarray((lane[:, None] // HEAD_DIM == lane[None, :] // HEAD_DIM), BF16)
    expand = jnp.asarray(lane[:, None] == np.arange(B_WIDTH)[None, :] // HEAD_DIM, BF16)
    tri = jnp.asarray(np.triu(np.ones((TOK_TILE, TOK_TILE), np.float32), 1), BF16)
    rw_hi = router_w.T.astype(BF16)
    rw_lo = (router_w.T - rw_hi.astype(F32)).astype(BF16)
    router = (rw_hi, rw_lo, router_b.reshape(N_EXPERTS, 1), tri)

    mkv = _proj(mem.reshape(BATCH * N_MEM, D_MODEL), w_mem_kv.astype(BF16))
    experts = tuple(w.reshape(DEPTH * N_EXPERTS, *w.shape[2:]) for w in (w_gate, w_up, w_down))

    kmix = A_Q_HEADS * HEAD_DIM
    qcols = (np.asarray(_Q_HEAD_ORDER)[:, None] * HEAD_DIM + np.arange(HEAD_DIM)[None, :]).reshape(-1)
    w_in = jnp.concatenate([a_w_in[0][:, qcols], a_w_in[0][:, kmix:]], axis=1).astype(BF16)
    qg = row(jnp.tile(a_q_norm[0], 2) * (SCALE * LOG2_E))
    kg = row(jnp.tile(a_k_norm[0], 2))
    q, k, v, qm = _a_in_proj(x, w_in, m128, qg, kg, _axial_tables())
    mix = _gqa_attention(q, k, v)
    w_out = a_w_out[0].astype(BF16)
    routed = _post_attn(mix, qm, mkv, w_out[qcols], w_out[kmix:], x, row(ln1_g[0]), row(ln1_b[0]), router)
    x = _moe_ln(routed, 0, *experts, row(ln2_g[0]), row(ln2_b[0]))

    ptabs = _partial_tables()
    tabs = [tuple(_residue_order(t, d) for t in ptabs) for d in DILATIONS]
    qkv, qm = _b_in_proj(x, b_w_in[0].astype(BF16), tabs)
    outs, lses = zip(*[_band_attention(*qkv[g], d) for g, d in enumerate(DILATIONS)])
    w_out = b_w_out[0].astype(BF16)
    routed = _post_attn((outs, lses), qm, mkv, w_out[:B_WIDTH], w_out[B_WIDTH:], x, row(ln1_g[1]), row(ln1_b[1]),
                        router, expand)
    x = _moe_ln(routed, 1, *experts, row(ln2_g[1]), row(ln2_b[1]))
    return x.reshape(BATCH, SEQ, D_MODEL)
```

```python
import functools
import math

import numpy as np
import jax
import jax.numpy as jnp
from jax import lax
from jax.experimental import pallas as pl
from jax.experimental.pallas import tpu as pltpu

F32 = jnp.float32
BF16 = jnp.bfloat16

D_MODEL = 1024
BATCH = 8
SEQ = 2048
N_TOK = BATCH * SEQ
HEAD_DIM = 64
GRID_W = 64
N_MEM = 256
MEM_HEADS = 4
A_Q_HEADS = 12
A_KV_HEADS = 4
AXIAL_THETA = 10000.0
B_HEADS = 8
B_WIDTH = B_HEADS * HEAD_DIM
DILATIONS = (1, 4, 16)
BAND_RADIUS = 64
PARTIAL_ROT_DIMS = HEAD_DIM // 4
PARTIAL_THETA = 500000.0
N_EXPERTS = 16
N_EXPERT_GROUPS = 4
EXPERTS_PER_GROUP = 4
D_EXPERT = 512
DEPTH = 2
ALPHA = (2 * DEPTH) ** 0.25
NORM_EPS = 1e-6
NEG_BIG = -1e30
SCALE = HEAD_DIM ** -0.5
LOG2_E = math.log2(math.e)
_Q_HEAD_ORDER = tuple(6 * p + 3 * half + t for p in range(2) for t in range(3) for half in range(2))

LANES = 128
MEM_W = MEM_HEADS * HEAD_DIM
TOK_TILE = 512
TILES_PER_SEQ = SEQ // TOK_TILE
POST_SUB = 2
GQA_KC = 256
BAND_TQ = 128
BAND_TILES = 8

PAIRS = ((0, 1), (0, 2), (0, 3), (1, 2), (1, 3), (2, 3))
N_CLASSES = N_EXPERT_GROUPS * len(PAIRS)
CLASS_ROWS = 32
MOE_TM = 256
MOE_ROWS = N_TOK + N_CLASSES * MOE_TM
MOE_TILES = MOE_ROWS // MOE_TM
XA_W = D_MODEL + LANES
VMEM_LIMIT = 56 * 1024 * 1024


def _params(sem, vmem=VMEM_LIMIT):
    return pltpu.CompilerParams(dimension_semantics=sem, vmem_limit_bytes=vmem)


def _dot(a, b):
    return jnp.dot(a, b, preferred_element_type=F32)


def _dot_nt(a, b):
    return lax.dot_general(a, b, (((1,), (1,)), ((), ())), preferred_element_type=F32)


def _split_bf16(x):
    hi = x.astype(BF16)
    lo = (x - hi.astype(F32)).astype(BF16)
    return hi, lo


def _layer_norm(z, g, b):
    mu = jnp.mean(z, axis=-1, keepdims=True)
    zc = z - mu
    var = jnp.mean(zc * zc, axis=-1, keepdims=True)
    return zc * lax.rsqrt(var + NORM_EPS) * g + b


def _rope(z, c, sa, sb, shift):
    return z * c + pltpu.roll(z, LANES - shift, 1) * sa + pltpu.roll(z, shift, 1) * sb


def _first_head(shape):
    return lax.broadcasted_iota(jnp.int32, shape, 1) < HEAD_DIM


def _pair_attention(q2, k2, v3, first, valid=None, base2=False):
    res = []
    for pick in (first, jnp.logical_not(first)):
        s = _dot_nt(jnp.where(pick, q2, jnp.zeros_like(q2)), k2)
        if valid is not None:
            s = jnp.where(valid, s, NEG_BIG)
        m = jnp.max(s, axis=1, keepdims=True)
        p = jnp.exp2(s - m) if base2 else jnp.exp(s - m)
        res.append((_dot(p.astype(BF16), v3), m))
    return res


def _pair_output(res, first):
    (ol_a, _), (ol_b, _) = res
    return jnp.where(first, ol_a[:, :LANES], ol_b[:, :LANES]) / jnp.where(first, ol_a[:, LANES:], ol_b[:, LANES:])


def _with_ones(v2):
    return jnp.concatenate([v2, jnp.ones_like(v2)], axis=1)


def _proj_kernel(x_ref, w_ref, o_ref):
    o_ref[...] = _dot(x_ref[...].astype(BF16), w_ref[...]).astype(o_ref.dtype)


def _proj(x, w, tm=TOK_TILE):
    n, k = x.shape
    width = w.shape[1]
    return pl.pallas_call(
        _proj_kernel,
        grid=(n // tm,),
        in_specs=[pl.BlockSpec((tm, k), lambda i: (i, 0)), pl.BlockSpec((k, width), lambda i: (0, 0))],
        out_specs=pl.BlockSpec((tm, width), lambda i: (i, 0)),
        out_shape=jax.ShapeDtypeStruct((n, width), BF16),
        compiler_params=_params(("parallel",)),
        name="proj",
    )(x, w)


def _a_in_kernel(x_ref, w_ref, m_ref, qg_ref, kg_ref, c_ref, sa_ref, sb_ref, q_ref, k_ref, v_ref, qm_ref):
    y = _dot(x_ref[...].astype(BF16), w_ref[...])
    m = m_ref[...]
    c, sa, sb = c_ref[...], sa_ref[...], sb_ref[...]

    def norm_rope(z, gain):
        hi, lo = _split_bf16(z * z)
        ms = (_dot(hi, m) + _dot(lo, m)) * (1.0 / HEAD_DIM)
        z = z * lax.rsqrt(ms + NORM_EPS) * gain
        return _rope(z, c, sa, sb, 16)

    nq = A_Q_HEADS * HEAD_DIM // LANES
    nk = A_KV_HEADS * HEAD_DIM // LANES
    for ch in range(nq):
        q_ref[:, ch * LANES:(ch + 1) * LANES] = norm_rope(y[:, ch * LANES:(ch + 1) * LANES], qg_ref[...]).astype(BF16)
    off = nq * LANES
    for ch in range(nk):
        z = y[:, off + ch * LANES: off + (ch + 1) * LANES]
        k_ref[:, ch * LANES:(ch + 1) * LANES] = norm_rope(z, kg_ref[...]).astype(BF16)
    off += nk * LANES
    ones = jnp.ones((y.shape[0], LANES), BF16)
    for ch in range(nk):
        v_ref[:, 2 * ch * LANES:(2 * ch + 1) * LANES] = y[:, off + ch * LANES: off + (ch + 1) * LANES].astype(BF16)
        v_ref[:, (2 * ch + 1) * LANES:(2 * ch + 2) * LANES] = ones
    off += nk * LANES
    qm_ref[...] = (y[:, off:off + MEM_W] * SCALE).astype(BF16)


def _a_in_proj(x, w, m128, qg, kg, tabs, tm=2 * TOK_TILE):
    c, sa, sb = tabs
    nblk = SEQ // tm
    a_in = w.shape[1]
    tab_spec = pl.BlockSpec((tm, LANES), lambda i: (i % nblk, 0))
    row = lambda width: pl.BlockSpec((tm, width), lambda i: (i, 0))
    const = lambda shape: pl.BlockSpec(shape, lambda i: (0, 0))
    qw, kw = A_Q_HEADS * HEAD_DIM, A_KV_HEADS * HEAD_DIM
    return pl.pallas_call(
        _a_in_kernel,
        grid=(N_TOK // tm,),
        in_specs=[row(D_MODEL), const((D_MODEL, a_in)), const((LANES, LANES)), const((1, LANES)), const((1, LANES)),
                  tab_spec, tab_spec, tab_spec],
        out_specs=[row(qw), row(kw), row(2 * kw), row(MEM_W)],
        out_shape=[jax.ShapeDtypeStruct((N_TOK, qw), BF16), jax.ShapeDtypeStruct((N_TOK, kw), BF16),
                   jax.ShapeDtypeStruct((N_TOK, 2 * kw), BF16), jax.ShapeDtypeStruct((N_TOK, MEM_W), BF16)],
        compiler_params=_params(("parallel",)),
        name="a_in_proj",
    )(x, w, m128, qg, kg, c, sa, sb)


def _b_in_kernel(*refs):
    ng = len(DILATIONS)
    nx = D_MODEL // LANES
    x_refs, w_ref, refs = refs[:nx], refs[nx], refs[nx + 1:]
    tabs, outs, qm_ref = refs[:3 * ng], refs[3 * ng:6 * ng], refs[6 * ng]
    xb1 = None
    for g, d in enumerate(DILATIONS):
        n = TOK_TILE // d
        if d == 1:
            cols = [r[...] for r in x_refs]
        else:
            cols = [jnp.concatenate([r[pl.ds(k, n, stride=d), :] for k in range(d)], axis=0) for r in x_refs]
        xb = jnp.concatenate(cols, axis=1).astype(BF16)
        xb1 = xb if d == 1 else xb1
        part = pl.ds(pl.multiple_of((pl.program_id(0) % TILES_PER_SEQ) * n, n), n)
        c, sa, sb = (t[:, part, :].reshape(TOK_TILE, LANES) for t in tabs[3 * g:3 * g + 3])
        q_ref, k_ref, v_ref = outs[3 * g:3 * g + 3]
        yq = _dot(xb, w_ref[:, g * B_WIDTH:(g + 1) * B_WIDTH]) * SCALE
        yk = _dot(xb, w_ref[:, (ng + g) * B_WIDTH:(ng + g + 1) * B_WIDTH])
        yv = _dot(xb, w_ref[:, (2 * ng + g) * B_WIDTH:(2 * ng + g + 1) * B_WIDTH])
        for ch in range(B_WIDTH // LANES):
            sl = slice(ch * LANES, (ch + 1) * LANES)
            q_ref[0, :, :, sl] = _rope(yq[:, sl], c, sa, sb, PARTIAL_ROT_DIMS // 2).reshape(d, n, LANES).astype(BF16)
            k_ref[0, :, :, sl] = _rope(yk[:, sl], c, sa, sb, PARTIAL_ROT_DIMS // 2).reshape(d, n, LANES).astype(BF16)
        v_ref[0] = yv.reshape(d, n, B_WIDTH).astype(BF16)
    qm = _dot(xb1, w_ref[:, 3 * ng * B_WIDTH:3 * ng * B_WIDTH + MEM_W])
    qm_ref[...] = (qm * SCALE).astype(BF16)


def _b_in_proj(x, w, tabs):
    nx = D_MODEL // LANES
    in_specs = [pl.BlockSpec((TOK_TILE, LANES), functools.partial(lambda i, c: (i, c), c=c)) for c in range(nx)]
    resident = pl.Buffered(1)
    in_specs.append(pl.BlockSpec(w.shape, lambda i: (0, 0), pipeline_mode=resident))
    out_specs, out_shape = [], []
    for d in DILATIONS:
        n = TOK_TILE // d
        in_specs += [pl.BlockSpec((d, SEQ // d, LANES), lambda i: (0, 0, 0), pipeline_mode=resident)] * 3
        out_specs += [pl.BlockSpec((1, d, n, B_WIDTH), lambda i: (i // TILES_PER_SEQ, 0, i % TILES_PER_SEQ, 0))] * 3
        out_shape += [jax.ShapeDtypeStruct((BATCH, d, SEQ // d, B_WIDTH), BF16)] * 3
    out_specs.append(pl.BlockSpec((TOK_TILE, MEM_W), lambda i: (i, 0)))
    out_shape.append(jax.ShapeDtypeStruct((N_TOK, MEM_W), BF16))
    res = pl.pallas_call(
        _b_in_kernel,
        grid=(N_TOK // TOK_TILE,),
        in_specs=in_specs,
        out_specs=out_specs,
        out_shape=out_shape,
        compiler_params=_params(("parallel",)),
        name="b_in_proj",
    )(*([x] * nx), w, *[t for group in tabs for t in group])
    return [res[3 * g:3 * g + 3] for g in range(len(DILATIONS))], res[-1]


def _gqa_kernel(q_ref, k_ref, v_ref, o_ref):
    tq = q_ref.shape[0]
    ntile = q_ref.shape[1] // LANES
    q = jnp.concatenate([q_ref[:, t * LANES:(t + 1) * LANES] for t in range(ntile)], axis=0)
    first = _first_head(q.shape)
    res = []
    for pick in (first, jnp.logical_not(first)):
        qh = jnp.where(pick, q, jnp.zeros_like(q))
        m = jnp.full((q.shape[0], 1), -jnp.inf, F32)
        acc = jnp.zeros((q.shape[0], 2 * LANES), F32)
        for c in range(SEQ // GQA_KC):
            keys = slice(c * GQA_KC, (c + 1) * GQA_KC)
            s = _dot_nt(qh, k_ref[keys, :])
            m_new = jnp.maximum(m, jnp.max(s, axis=1, keepdims=True))
            p = jnp.exp2(s - m_new).astype(BF16)
            acc = acc * jnp.exp2(m - m_new) + _dot(p, v_ref[keys, :])
            m = m_new
        res.append((acc, m))
    o = _pair_output(res, first)
    for t in range(ntile):
        o_ref[:, t * LANES:(t + 1) * LANES] = o[t * tq:(t + 1) * tq].astype(BF16)


def _gqa_attention(q, k, v, tq=1024):
    nq = SEQ // tq
    qw = q.shape[1] // 2
    kw = k.shape[1] // 2
    return pl.pallas_call(
        _gqa_kernel,
        grid=(BATCH, 2, nq),
        in_specs=[pl.BlockSpec((tq, qw), lambda b, p, i: (b * nq + i, p)),
                  pl.BlockSpec((SEQ, kw), lambda b, p, i: (b, p)),
                  pl.BlockSpec((SEQ, 2 * kw), lambda b, p, i: (b, p))],
        out_specs=pl.BlockSpec((tq, qw), lambda b, p, i: (b * nq + i, p)),
        out_shape=jax.ShapeDtypeStruct(q.shape, BF16),
        compiler_params=_params(("parallel", "parallel", "parallel")),
        name="gqa_attention",
    )(q, k, v)


def _band_kernel(q_ref, k_ref, v_ref, o_ref, lse_ref, *, length, seg, win):
    tq = BAND_TQ
    first = _first_head((tq, LANES))
    lane = lax.broadcasted_iota(jnp.int32, (tq, LANES), 1)
    row = lax.broadcasted_iota(jnp.int32, (tq, win), 0)
    col = lax.broadcasted_iota(jnp.int32, (tq, win), 1)
    for u in range(BAND_TILES):
        tile = pl.program_id(1) * BAND_TILES + u
        start = pl.multiple_of(jnp.clip(tile * tq - BAND_RADIUS, 0, length - win), BAND_RADIUS)
        qpos, kpos = tile * tq + row, start + col
        valid = jnp.abs(qpos - kpos) <= BAND_RADIUS
        if seg < length:
            valid = jnp.logical_and(valid, qpos // seg == kpos // seg)
        rows = slice(u * tq, (u + 1) * tq)
        lse = jnp.zeros((tq, LANES), F32)
        for j in range(B_WIDTH // LANES):
            sl = slice(j * LANES, (j + 1) * LANES)
            res = _pair_attention(q_ref[0, rows, sl], k_ref[0, pl.ds(start, win), sl],
                                  _with_ones(v_ref[0, pl.ds(start, win), sl]), first, valid)
            o_ref[0, rows, sl] = _pair_output(res, first).astype(BF16)
            for half, (ol, m) in enumerate(res):
                lse = jnp.where(lane == 2 * j + half, m + jnp.log(ol[:, LANES:]), lse)
        lse_ref[0, rows, :] = lse


def _band_attention(q, k, v, d):
    seg = SEQ // d
    step_rows = BAND_TILES * BAND_TQ
    length = max(seg, step_rows)
    win = BAND_TQ + 2 * BAND_RADIUS
    nblk = N_TOK // length
    q, k, v = (t.reshape(nblk, length, B_WIDTH) for t in (q, k, v))
    seq_spec = pl.BlockSpec((1, length, B_WIDTH), lambda s, i: (s, 0, 0))
    o, lse = pl.pallas_call(
        functools.partial(_band_kernel, length=length, seg=seg, win=win),
        grid=(nblk, length // step_rows),
        in_specs=[pl.BlockSpec((1, step_rows, B_WIDTH), lambda s, i: (s, i, 0)), seq_spec, seq_spec],
        out_specs=[pl.BlockSpec((1, step_rows, B_WIDTH), lambda s, i: (s, i, 0)),
                   pl.BlockSpec((1, step_rows, LANES), lambda s, i: (s, i, 0))],
        out_shape=[jax.ShapeDtypeStruct((nblk, length, B_WIDTH), BF16),
                   jax.ShapeDtypeStruct((nblk, length, LANES), F32)],
        compiler_params=_params(("parallel", "parallel")),
        name=f"band_attention_d{d}",
    )(q, k, v)
    return o.reshape(BATCH, d, seg, B_WIDTH), lse.reshape(BATCH, d, seg, LANES)


def _route(x, rows, wh_ref, wl_ref, rb_ref, tri_ref, xa_ref, cls_ref, rank_ref, cnt_ref, carry_ref):
    tm = x.shape[0]
    xh, xl = _split_bf16(x)
    logits = _dot_nt(wh_ref[...], xh) + _dot_nt(wh_ref[...], xl) + _dot_nt(wl_ref[...], xh)
    aff = 1.0 / (1.0 + jnp.exp(-logits))
    sel = aff + rb_ref[...]
    s = [sel[e:e + 1, :] for e in range(N_EXPERTS)]
    a = [aff[e:e + 1, :] for e in range(N_EXPERTS)]

    def top2_sum(v):
        hi01, lo01 = jnp.maximum(v[0], v[1]), jnp.minimum(v[0], v[1])
        hi23, lo23 = jnp.maximum(v[2], v[3]), jnp.minimum(v[2], v[3])
        return jnp.maximum(hi01, hi23) + jnp.maximum(jnp.minimum(hi01, hi23), jnp.maximum(lo01, lo23))

    gscore = [top2_sum(s[EXPERTS_PER_GROUP * g:EXPERTS_PER_GROUP * (g + 1)]) for g in range(N_EXPERT_GROUPS)]
    best = jnp.zeros((1, tm), jnp.int32)
    best_score = gscore[0]
    for g in range(1, N_EXPERT_GROUPS):
        better = gscore[g] > best_score
        best = jnp.where(better, g, best)
        best_score = jnp.where(better, gscore[g], best_score)

    def pick(rows, j):
        out = rows[j]
        for g in range(1, N_EXPERT_GROUPS):
            out = jnp.where(best == g, rows[EXPERTS_PER_GROUP * g + j], out)
        return out

    t = [pick(s, j) for j in range(EXPERTS_PER_GROUP)]
    w = [pick(a, j) for j in range(EXPERTS_PER_GROUP)]

    def first_max(v):
        mx = jnp.maximum(jnp.maximum(v[0], v[1]), jnp.maximum(v[2], v[3]))
        idx = jnp.full((1, tm), EXPERTS_PER_GROUP - 1, jnp.int32)
        for j in range(EXPERTS_PER_GROUP - 2, -1, -1):
            idx = jnp.where(v[j] == mx, j, idx)
        return idx

    i1 = first_max(t)
    i2 = first_max([jnp.where(i1 == j, -jnp.inf, t[j]) for j in range(EXPERTS_PER_GROUP)])
    lo, hi = jnp.minimum(i1, i2), jnp.maximum(i1, i2)

    def take(rows, idx):
        out = rows[0]
        for j in range(1, EXPERTS_PER_GROUP):
            out = jnp.where(idx == j, rows[j], out)
        return out

    w_lo, w_hi = take(w, lo), take(w, hi)
    den = w_lo + w_hi
    pair = jnp.where(lo == 0, hi - 1, jnp.where(lo == 1, hi + 1, len(PAIRS) - 1))
    cls = best * len(PAIRS) + pair

    onehot = (lax.broadcasted_iota(jnp.int32, (CLASS_ROWS, tm), 0) == cls).astype(F32)
    before = _dot(onehot.astype(BF16), tri_ref[...]) + carry_ref[:, 0:1]
    rank = jnp.sum(onehot * before, axis=0, keepdims=True)
    carry_ref[...] = carry_ref[...] + jnp.sum(onehot, axis=1, keepdims=True)

    cls_ref[0, :, rows] = cls
    rank_ref[0, :, rows] = rank.astype(jnp.int32)
    cnt_ref[...] = carry_ref[...]
    gates = jnp.concatenate([w_lo / den, w_hi / den, jnp.zeros((LANES - 2, tm), F32)], axis=0)
    xa_ref[rows, :D_MODEL] = x
    xa_ref[rows, D_MODEL:] = gates.T


def _post_attn_kernel(*refs, merge):
    if merge:
        o_refs, l_refs, e_ref, refs = refs[:3], refs[3:6], refs[6], refs[7:]
    else:
        mix_ref, refs = refs[0], refs[1:]
    (qm_ref, mk_ref, mv_ref, w1_ref, w2_ref, x_ref, g_ref, b_ref, wh_ref, wl_ref, rb_ref, tri_ref,
     xa_ref, cls_ref, rank_ref, cnt_ref, carry_ref) = refs[:17]

    @pl.when(pl.program_id(0) == 0)
    def _():
        carry_ref[...] = jnp.zeros_like(carry_ref)

    first = _first_head((TOK_TILE, LANES))
    for sub in range(POST_SUB):
        rows = slice(sub * TOK_TILE, (sub + 1) * TOK_TILE)
        if merge:
            ot_ref, lt_ref = refs[17], refs[18]
            nch = B_WIDTH // LANES
            for g, d in enumerate(DILATIONS):
                n = TOK_TILE // d
                for r in range(d):
                    dst = pl.ds(r, n, stride=d) if d > 1 else slice(None)
                    og = o_refs[g][0, r, sub * n:(sub + 1) * n].astype(F32)
                    for ch in range(nch):
                        ot_ref[sub, g, ch, dst, :] = og[:, ch * LANES:(ch + 1) * LANES]
                    lt_ref[sub, g, dst, :] = l_refs[g][0, r, sub * n:(sub + 1) * n]
            lses = [lt_ref[sub, g] for g in range(len(DILATIONS))]
            m = functools.reduce(jnp.maximum, lses)
            es = [jnp.exp(l - m) for l in lses]
            inv = 1.0 / functools.reduce(jnp.add, es)
            expand = e_ref[...]

            def widen(w):
                hi, lo = _split_bf16(w)
                return _dot(hi, expand) + _dot(lo, expand)

            ws = [widen(e * inv) for e in es]
            mix = jnp.concatenate(
                [functools.reduce(jnp.add, [w[:, ch * LANES:(ch + 1) * LANES] * ot_ref[sub, g, ch]
                                            for g, w in enumerate(ws)]) for ch in range(nch)], axis=1).astype(BF16)
        else:
            mix = mix_ref[rows, :]

        memo = []
        for j in range(MEM_W // LANES):
            sl = slice(j * LANES, (j + 1) * LANES)
            memo.append(_pair_output(_pair_attention(qm_ref[rows, sl], mk_ref[:, sl], _with_ones(mv_ref[:, sl]), first),
                                     first))
        memo = jnp.concatenate(memo, axis=1).astype(BF16)

        attn = _dot(mix, w1_ref[...]) + _dot(memo, w2_ref[...])
        x1 = _layer_norm(ALPHA * x_ref[rows, :] + attn, g_ref[...], b_ref[...])
        _route(x1, rows, wh_ref, wl_ref, rb_ref, tri_ref, xa_ref, cls_ref, rank_ref, cnt_ref, carry_ref)


def _post_attn(mix, qm, mkv, w1, w2, x, g, b, router, expand=None):
    merge = expand is not None
    wh, wl, rb, tri = router
    blk = POST_SUB * TOK_TILE
    nblk = N_TOK // blk
    per_seq = SEQ // blk
    row = lambda width: pl.BlockSpec((blk, width), lambda i: (i, 0))
    const = lambda shape: pl.BlockSpec(shape, lambda i: (0,) * len(shape))
    row3 = pl.BlockSpec((1, 1, blk), lambda i: (i, 0, 0))
    scratch = [pltpu.VMEM((CLASS_ROWS, LANES), F32)]
    if merge:
        outs, lses = mix
        resid = lambda d, width: pl.BlockSpec((1, d, blk // d, width), lambda i: (i // per_seq, 0, i % per_seq, 0))
        lead_specs = ([resid(d, B_WIDTH) for d in DILATIONS] + [resid(d, LANES) for d in DILATIONS]
                      + [const(expand.shape)])
        lead = [*outs, *lses, expand]
        scratch += [pltpu.VMEM((POST_SUB, len(DILATIONS), B_WIDTH // LANES, TOK_TILE, LANES), F32),
                    pltpu.VMEM((POST_SUB, len(DILATIONS), TOK_TILE, LANES), F32)]
    else:
        lead_specs, lead = [row(mix.shape[1])], [mix]
    xa, cls, rank, cnt = pl.pallas_call(
        functools.partial(_post_attn_kernel, merge=merge),
        grid=(nblk,),
        in_specs=lead_specs + [row(MEM_W),
                               pl.BlockSpec((N_MEM, MEM_W), lambda i: (i // per_seq, 0)),
                               pl.BlockSpec((N_MEM, MEM_W), lambda i: (i // per_seq, 1)),
                               const(w1.shape), const(w2.shape), row(D_MODEL), const((1, D_MODEL)), const((1, D_MODEL)),
                               const(wh.shape), const(wl.shape), const(rb.shape), const(tri.shape)],
        out_specs=[row(XA_W), row3, row3, const((CLASS_ROWS, LANES))],
        out_shape=[jax.ShapeDtypeStruct((N_TOK, XA_W), F32), jax.ShapeDtypeStruct((nblk, 1, blk), jnp.int32),
                   jax.ShapeDtypeStruct((nblk, 1, blk), jnp.int32), jax.ShapeDtypeStruct((CLASS_ROWS, LANES), F32)],
        scratch_shapes=scratch,
        compiler_params=_params(("arbitrary",)),
        name="post_attn_merge" if merge else "post_attn",
    )(*lead, qm, mkv, mkv, w1, w2, x, g, b, wh, wl, rb, tri)
    return xa, cls.reshape(N_TOK), rank.reshape(N_TOK), cnt[:N_CLASSES, 0].astype(jnp.int32)


def _dispatch_kernel(pos_ref, ends_ref, xa_ref, xs_ref, zeros, sem, zsem, *, chunk):
    base = pl.program_id(0) * chunk

    @pl.when(pl.program_id(0) == 0)
    def _():
        zeros[...] = jnp.zeros_like(zeros)

        def tail(c):
            start = pl.multiple_of(ends_ref[c] - MOE_TM, MOE_TM)
            return pltpu.make_async_copy(zeros, xs_ref.at[pl.ds(start, MOE_TM)], zsem)

        def nonempty(c):
            return ends_ref[c] > (ends_ref[c - 1] if c else 0)

        def spare(t):
            return pltpu.make_async_copy(zeros, xs_ref.at[pl.ds(pl.multiple_of(t * MOE_TM, MOE_TM), MOE_TM)], zsem)

        first_spare = ends_ref[N_CLASSES - 1] // MOE_TM
        for c in range(N_CLASSES):
            pl.when(nonempty(c))(lambda c=c: tail(c).start())
        lax.fori_loop(first_spare, MOE_TILES, lambda t, carry: (spare(t).start(), carry)[1], 0)
        for c in range(N_CLASSES):
            pl.when(nonempty(c))(lambda c=c: tail(c).wait())
        lax.fori_loop(first_spare, MOE_TILES, lambda t, carry: (spare(t).wait(), carry)[1], 0)

    def retire(r, carry):
        pltpu.make_async_copy(xa_ref.at[pl.ds(0, 1)], xs_ref.at[pl.ds(0, 1)], sem).wait()
        return carry

    for r in range(chunk):
        pltpu.make_async_copy(xa_ref.at[pl.ds(r, 1)], xs_ref.at[pl.ds(pos_ref[base + r], 1)], sem).start(priority=r % 2)
    lax.fori_loop(0, chunk, retire, 0, unroll=8)


def _dispatch(pos, ends, xa, chunk=1024):
    return pl.pallas_call(
        functools.partial(_dispatch_kernel, chunk=chunk),
        grid_spec=pltpu.PrefetchScalarGridSpec(
            num_scalar_prefetch=2,
            grid=(N_TOK // chunk,),
            in_specs=[pl.BlockSpec((chunk, XA_W), lambda i, p, e: (i, 0))],
            out_specs=pl.BlockSpec(memory_space=pl.ANY),
            scratch_shapes=[pltpu.VMEM((MOE_TM, XA_W), F32), pltpu.SemaphoreType.DMA(()),
                            pltpu.SemaphoreType.DMA(())],
        ),
        out_shape=jax.ShapeDtypeStruct((MOE_ROWS, XA_W), F32),
        compiler_params=_params(("arbitrary",)),
        name="moe_dispatch",
    )(pos, ends, xa)


def _expert_kernel(slot1_ref, slot2_ref, used_ref, first_ref, grp_ref, next_ref, xs_ref, wg_hbm, wu_hbm, wd_hbm,
                   ys_ref, stage_g, stage_u, stage_d, wg_s, wu_s, wd_s, sem, *, layer):
    i = pl.program_id(0)
    streams = ((wg_hbm, stage_g, wg_s), (wu_hbm, stage_u, wu_s), (wd_hbm, stage_d, wd_s))

    def group_copy(group, k):
        first_expert = layer * N_EXPERTS + group * EXPERTS_PER_GROUP
        hbm, stage, _ = streams[k]
        return pltpu.make_async_copy(hbm.at[pl.ds(first_expert, EXPERTS_PER_GROUP)], stage, sem.at[k])

    @pl.when(i == 0)
    def _():
        for k in range(len(streams)):
            group_copy(grp_ref[0], k).start()

    @pl.when(first_ref[i] != 0)
    def _():
        for k, (_, stage, dst) in enumerate(streams):
            group_copy(grp_ref[i], k).wait()
            for e in range(EXPERTS_PER_GROUP):
                dst[e] = stage[e].astype(BF16)

        @pl.when(next_ref[i] >= 0)
        def _():
            for k in range(len(streams)):
                group_copy(next_ref[i], k).start()

    @pl.when(used_ref[i] != 0)
    def _():
        x = xs_ref[:, :D_MODEL].astype(BF16)
        gates = xs_ref[:, D_MODEL:]

        def expert(slot):
            gate = _dot(x, wg_s[slot])
            h = gate * (1.0 / (1.0 + jnp.exp(-gate))) * _dot(x, wu_s[slot])
            return _dot(h.astype(BF16), wd_s[slot])

        ys_ref[...] = gates[:, 0:1] * expert(slot1_ref[i]) + gates[:, 1:2] * expert(slot2_ref[i])

    @pl.when(used_ref[i] == 0)
    def _():
        ys_ref[...] = jnp.zeros_like(ys_ref)


def _experts(tile_meta, xs, layer, wg, wu, wd):
    nmeta = len(tile_meta)
    up_shape = (EXPERTS_PER_GROUP, D_MODEL, D_EXPERT)
    down_shape = (EXPERTS_PER_GROUP, D_EXPERT, D_MODEL)
    return pl.pallas_call(
        functools.partial(_expert_kernel, layer=layer),
        grid_spec=pltpu.PrefetchScalarGridSpec(
            num_scalar_prefetch=nmeta,
            grid=(MOE_TILES,),
            in_specs=[pl.BlockSpec((MOE_TM, XA_W), lambda i, *_: (i, 0)),
                      pl.BlockSpec(memory_space=pl.ANY), pl.BlockSpec(memory_space=pl.ANY),
                      pl.BlockSpec(memory_space=pl.ANY)],
            out_specs=pl.BlockSpec((MOE_TM, D_MODEL), lambda i, *_: (i, 0)),
            scratch_shapes=[pltpu.VMEM(up_shape, F32), pltpu.VMEM(up_shape, F32), pltpu.VMEM(down_shape, F32),
                            pltpu.VMEM(up_shape, BF16), pltpu.VMEM(up_shape, BF16), pltpu.VMEM(down_shape, BF16),
                            pltpu.SemaphoreType.DMA((3,))],
        ),
        out_shape=jax.ShapeDtypeStruct((MOE_ROWS, D_MODEL), F32),
        compiler_params=_params(("arbitrary",)),
        name="moe_experts",
    )(*tile_meta, xs, wg, wu, wd)


def _combine_kernel(pos_ref, ys_ref, x_ref, g_ref, b_ref, o_ref, buf, sem, *, tc):
    i = pl.program_id(0)
    slot = i % 2

    def gather(step, into):
        for r in range(tc):
            row = pos_ref[step * tc + r]
            pltpu.make_async_copy(ys_ref.at[pl.ds(row, 1)], buf.at[into, pl.ds(r, 1)], sem.at[into]).start(
                priority=r % 2)

    @pl.when(i == 0)
    def _():
        gather(0, 0)

    @pl.when(i + 1 < pl.num_programs(0))
    def _():
        gather(i + 1, 1 - slot)

    def retire(r, carry):
        pltpu.make_async_copy(ys_ref.at[pl.ds(0, 1)], buf.at[slot, pl.ds(0, 1)], sem.at[slot]).wait()
        return carry

    lax.fori_loop(0, tc, retire, 0, unroll=8)
    o_ref[...] = _layer_norm(ALPHA * x_ref[...] + buf[slot], g_ref[...], b_ref[...])


def _combine_ln(pos, ys, xa, g, b, tc=512):
    return pl.pallas_call(
        functools.partial(_combine_kernel, tc=tc),
        grid_spec=pltpu.PrefetchScalarGridSpec(
            num_scalar_prefetch=1,
            grid=(N_TOK // tc,),
            in_specs=[pl.BlockSpec(memory_space=pl.ANY),
                      pl.BlockSpec((tc, D_MODEL), lambda i, p: (i, 0)),
                      pl.BlockSpec((1, D_MODEL), lambda i, p: (0, 0)),
                      pl.BlockSpec((1, D_MODEL), lambda i, p: (0, 0))],
            out_specs=pl.BlockSpec((tc, D_MODEL), lambda i, p: (i, 0)),
            scratch_shapes=[pltpu.VMEM((2, tc, D_MODEL), F32), pltpu.SemaphoreType.DMA((2,))],
        ),
        out_shape=jax.ShapeDtypeStruct((N_TOK, D_MODEL), F32),
        compiler_params=_params(("arbitrary",)),
        name="moe_combine_ln",
    )(pos, ys, xa, g, b)


def _moe_ln(routed, layer, wg, wu, wd, g, b):
    xa, cls, rank, counts = routed
    padded = (counts + MOE_TM - 1) // MOE_TM * MOE_TM
    ends = jnp.cumsum(padded)
    starts = ends - padded
    pos = starts[cls] + rank
    tile_start = jnp.arange(MOE_TILES, dtype=jnp.int32) * MOE_TM
    tile_used = (tile_start < ends[-1]).astype(jnp.int32)
    last_cls = jnp.max(jnp.where(counts > 0, jnp.arange(N_CLASSES), 0))
    tile_cls = jnp.minimum(jnp.sum(tile_start[:, None] >= ends[None, :], axis=1), last_cls)
    tile_cls = tile_cls.astype(jnp.int32)
    tile_grp = tile_cls // len(PAIRS)
    prev_grp = jnp.concatenate([jnp.full((1,), -1, jnp.int32), tile_grp[:-1]])
    tile_first = tile_used * (tile_grp != prev_grp).astype(jnp.int32)
    has = jnp.sum(counts.reshape(N_EXPERT_GROUPS, len(PAIRS)), axis=1) > 0
    gid = jnp.arange(N_EXPERT_GROUPS, dtype=jnp.int32)
    later = jnp.where(jnp.logical_and(has[None, :], gid[None, :] > gid[:, None]), gid[None, :], N_EXPERT_GROUPS)
    next_grp = jnp.min(later, axis=1)
    next_grp = jnp.where(next_grp == N_EXPERT_GROUPS, -1, next_grp)
    pair_slots = jnp.asarray(np.array(PAIRS, np.int32))[tile_cls % len(PAIRS)]
    tile_meta = (pair_slots[:, 0], pair_slots[:, 1], tile_used, tile_first, tile_grp, next_grp[tile_grp])
    xs = _dispatch(pos, ends.astype(jnp.int32), xa)
    ys = _experts(tile_meta, xs, layer, wg, wu, wd)
    return _combine_ln(pos, ys, xa, g, b)


def _tables(c, sa, sb):
    return tuple(np.concatenate([t, t], axis=-1).astype(np.float32) for t in (c, sa, sb))


def _axial_tables():
    rows = SEQ // GRID_W
    row = np.repeat(np.arange(rows), GRID_W).astype(np.float64)
    col = np.tile(np.arange(GRID_W), rows).astype(np.float64)
    half = HEAD_DIM // 2
    inv = 1.0 / (AXIAL_THETA ** (np.arange(0, half, 2, dtype=np.float64) / half))
    ar, ac = row[:, None] * inv, col[:, None] * inv
    z = np.zeros_like(ar)
    c = np.concatenate([np.cos(ar), np.cos(ar), np.cos(ac), np.cos(ac)], -1)
    sa = np.concatenate([-np.sin(ar), z, -np.sin(ac), z], -1)
    sb = np.concatenate([z, np.sin(ar), z, np.sin(ac)], -1)
    return _tables(c, sa, sb)


def _partial_tables():
    n = PARTIAL_ROT_DIMS
    pos = np.arange(SEQ, dtype=np.float64)
    ang = pos[:, None] * (1.0 / (PARTIAL_THETA ** (np.arange(0, n, 2, dtype=np.float64) / n)))
    z = np.zeros_like(ang)
    rest = np.zeros((SEQ, HEAD_DIM - n))
    c = np.concatenate([np.cos(ang), np.cos(ang), rest + 1.0], -1)
    sa = np.concatenate([-np.sin(ang), z, rest], -1)
    sb = np.concatenate([z, np.sin(ang), rest], -1)
    return _tables(c, sa, sb)


def _residue_order(tab, d):
    return np.ascontiguousarray(tab.reshape(SEQ // d, d, LANES).transpose(1, 0, 2))


def kernel(x, mem, w_mem_kv, router_w, router_b, a_w_in, a_w_out, a_q_norm, a_k_norm, b_w_in, b_w_out,
           ln1_g, ln1_b, ln2_g, ln2_b, w_gate, w_up, w_down):
    x = x.reshape(N_TOK, D_MODEL)
    row = lambda v: v.reshape(1, -1)

    lane = np.arange(LANES)
    m128 = jnp.asarray((lane[:, None] // HEAD_DIM == lane[None, :] // HEAD_DIM), BF16)
    expand = jnp.asarray(lane[:, None] == np.arange(B_WIDTH)[None, :] // HEAD_DIM, BF16)
    tri = jnp.asarray(np.triu(np.ones((TOK_TILE, TOK_TILE), np.float32), 1), BF16)
    rw_hi = router_w.T.astype(BF16)
    rw_lo = (router_w.T - rw_hi.astype(F32)).astype(BF16)
    router = (rw_hi, rw_lo, router_b.reshape(N_EXPERTS, 1), tri)

    mkv = _proj(mem.reshape(BATCH * N_MEM, D_MODEL), w_mem_kv.astype(BF16))
    experts = tuple(w.reshape(DEPTH * N_EXPERTS, *w.shape[2:]) for w in (w_gate, w_up, w_down))

    kmix = A_Q_HEADS * HEAD_DIM
    qcols = (np.asarray(_Q_HEAD_ORDER)[:, None] * HEAD_DIM + np.arange(HEAD_DIM)[None, :]).reshape(-1)
    w_in = jnp.concatenate([a_w_in[0][:, qcols], a_w_in[0][:, kmix:]], axis=1).astype(BF16)
    qg = row(jnp.tile(a_q_norm[0], 2) * (SCALE * LOG2_E))
    kg = row(jnp.tile(a_k_norm[0], 2))
    q, k, v, qm = _a_in_proj(x, w_in, m128, qg, kg, _axial_tables())
    mix = _gqa_attention(q, k, v)
    w_out = a_w_out[0].astype(BF16)
    routed = _post_attn(mix, qm, mkv, w_out[qcols], w_out[kmix:], x, row(ln1_g[0]), row(ln1_b[0]), router)
    x = _moe_ln(routed, 0, *experts, row(ln2_g[0]), row(ln2_b[0]))

    ptabs = _partial_tables()
    tabs = [tuple(_residue_order(t, d) for t in ptabs) for d in DILATIONS]
    qkv, qm = _b_in_proj(x, b_w_in[0].astype(BF16), tabs)
    outs, lses = zip(*[_band_attention(*qkv[g], d) for g, d in enumerate(DILATIONS)])
    w_out = b_w_out[0].astype(BF16)
    routed = _post_attn((outs, lses), qm, mkv, w_out[:B_WIDTH], w_out[B_WIDTH:], x, row(ln1_g[1]), row(ln1_b[1]),
                        router, expand)
    x = _moe_ln(routed, 1, *experts, row(ln2_g[1]), row(ln2_b[1]))
    return x.reshape(BATCH, SEQ, D_MODEL)
```

```python
import functools
import math

import numpy as np
import jax
import jax.numpy as jnp
from jax import lax
from jax.experimental import pallas as pl
from jax.experimental.pallas import tpu as pltpu

F32 = jnp.float32
BF16 = jnp.bfloat16

D_MODEL = 1024
BATCH = 8
SEQ = 2048
N_TOK = BATCH * SEQ
HEAD_DIM = 64
GRID_W = 64
N_MEM = 256
MEM_HEADS = 4
A_Q_HEADS = 12
A_KV_HEADS = 4
AXIAL_THETA = 10000.0
B_HEADS = 8
B_WIDTH = B_HEADS * HEAD_DIM
DILATIONS = (1, 4, 16)
BAND_RADIUS = 64
PARTIAL_ROT_DIMS = HEAD_DIM // 4
PARTIAL_THETA = 500000.0
N_EXPERTS = 16
N_EXPERT_GROUPS = 4
EXPERTS_PER_GROUP = 4
D_EXPERT = 512
DEPTH = 2
ALPHA = (2 * DEPTH) ** 0.25
NORM_EPS = 1e-6
NEG_BIG = -1e30
SCALE = HEAD_DIM ** -0.5
LOG2_E = math.log2(math.e)
_Q_HEAD_ORDER = tuple(6 * p + 3 * half + t for p in range(2) for t in range(3) for half in range(2))

LANES = 128
MEM_W = MEM_HEADS * HEAD_DIM
TOK_TILE = 512
TILES_PER_SEQ = SEQ // TOK_TILE
POST_SUB = 2
GQA_KC = 256
BAND_TQ = 128
BAND_TILES = 8

PAIRS = ((0, 1), (0, 2), (0, 3), (1, 2), (1, 3), (2, 3))
N_CLASSES = N_EXPERT_GROUPS * len(PAIRS)
CLASS_ROWS = 32
MOE_TM = 256
MOE_ROWS = N_TOK + N_CLASSES * MOE_TM
MOE_TILES = MOE_ROWS // MOE_TM
XA_W = D_MODEL + LANES
VMEM_LIMIT = 56 * 1024 * 1024


def _params(sem, vmem=VMEM_LIMIT):
    return pltpu.CompilerParams(dimension_semantics=sem, vmem_limit_bytes=vmem)


def _dot(a, b):
    return jnp.dot(a, b, preferred_element_type=F32)


def _dot_nt(a, b):
    return lax.dot_general(a, b, (((1,), (1,)), ((), ())), preferred_element_type=F32)


def _split_bf16(x):
    hi = x.astype(BF16)
    lo = (x - hi.astype(F32)).astype(BF16)
    return hi, lo


def _layer_norm(z, g, b):
    mu = jnp.mean(z, axis=-1, keepdims=True)
    zc = z - mu
    var = jnp.mean(zc * zc, axis=-1, keepdims=True)
    return zc * lax.rsqrt(var + NORM_EPS) * g + b


def _rope(z, c, sa, sb, shift):
    return z * c + pltpu.roll(z, LANES - shift, 1) * sa + pltpu.roll(z, shift, 1) * sb


def _first_head(shape):
    return lax.broadcasted_iota(jnp.int32, shape, 1) < HEAD_DIM


def _pair_attention(q2, k2, v3, first, valid=None, base2=False):
    res = []
    for pick in (first, jnp.logical_not(first)):
        s = _dot_nt(jnp.where(pick, q2, jnp.zeros_like(q2)), k2)
        if valid is not None:
            s = jnp.where(valid, s, NEG_BIG)
        m = jnp.max(s, axis=1, keepdims=True)
        p = jnp.exp2(s - m) if base2 else jnp.exp(s - m)
        res.append((_dot(p.astype(BF16), v3), m))
    return res


def _pair_output(res, first):
    (ol_a, _), (ol_b, _) = res
    return jnp.where(first, ol_a[:, :LANES], ol_b[:, :LANES]) / jnp.where(first, ol_a[:, LANES:], ol_b[:, LANES:])


def _with_ones(v2):
    return jnp.concatenate([v2, jnp.ones_like(v2)], axis=1)


def _proj_kernel(x_ref, w_ref, o_ref):
    o_ref[...] = _dot(x_ref[...].astype(BF16), w_ref[...]).astype(o_ref.dtype)


def _proj(x, w, tm=TOK_TILE):
    n, k = x.shape
    width = w.shape[1]
    return pl.pallas_call(
        _proj_kernel,
        grid=(n // tm,),
        in_specs=[pl.BlockSpec((tm, k), lambda i: (i, 0)), pl.BlockSpec((k, width), lambda i: (0, 0))],
        out_specs=pl.BlockSpec((tm, width), lambda i: (i, 0)),
        out_shape=jax.ShapeDtypeStruct((n, width), BF16),
        compiler_params=_params(("parallel",)),
        name="proj",
    )(x, w)


def _a_in_kernel(x_ref, w_ref, m_ref, qg_ref, kg_ref, c_ref, sa_ref, sb_ref, q_ref, k_ref, v_ref, qm_ref):
    y = _dot(x_ref[...].astype(BF16), w_ref[...])
    m = m_ref[...]
    c, sa, sb = c_ref[...], sa_ref[...], sb_ref[...]

    def norm_rope(z, gain):
        hi, lo = _split_bf16(z * z)
        ms = (_dot(hi, m) + _dot(lo, m)) * (1.0 / HEAD_DIM)
        z = z * lax.rsqrt(ms + NORM_EPS) * gain
        return _rope(z, c, sa, sb, 16)

    nq = A_Q_HEADS * HEAD_DIM // LANES
    nk = A_KV_HEADS * HEAD_DIM // LANES
    for ch in range(nq):
        q_ref[:, ch * LANES:(ch + 1) * LANES] = norm_rope(y[:, ch * LANES:(ch + 1) * LANES], qg_ref[...]).astype(BF16)
    off = nq * LANES
    for ch in range(nk):
        z = y[:, off + ch * LANES: off + (ch + 1) * LANES]
        k_ref[:, ch * LANES:(ch + 1) * LANES] = norm_rope(z, kg_ref[...]).astype(BF16)
    off += nk * LANES
    ones = jnp.ones((y.shape[0], LANES), BF16)
    for ch in range(nk):
        v_ref[:, 2 * ch * LANES:(2 * ch + 1) * LANES] = y[:, off + ch * LANES: off + (ch + 1) * LANES].astype(BF16)
        v_ref[:, (2 * ch + 1) * LANES:(2 * ch + 2) * LANES] = ones
    off += nk * LANES
    qm_ref[...] = (y[:, off:off + MEM_W] * SCALE).astype(BF16)


def _a_in_proj(x, w, m128, qg, kg, tabs, tm=2 * TOK_TILE):
    c, sa, sb = tabs
    nblk = SEQ // tm
    a_in = w.shape[1]
    tab_spec = pl.BlockSpec((tm, LANES), lambda i: (i % nblk, 0))
    row = lambda width: pl.BlockSpec((tm, width), lambda i: (i, 0))
    const = lambda shape: pl.BlockSpec(shape, lambda i: (0, 0))
    qw, kw = A_Q_HEADS * HEAD_DIM, A_KV_HEADS * HEAD_DIM
    return pl.pallas_call(
        _a_in_kernel,
        grid=(N_TOK // tm,),
        in_specs=[row(D_MODEL), const((D_MODEL, a_in)), const((LANES, LANES)), const((1, LANES)), const((1, LANES)),
                  tab_spec, tab_spec, tab_spec],
        out_specs=[row(qw), row(kw), row(2 * kw), row(MEM_W)],
        out_shape=[jax.ShapeDtypeStruct((N_TOK, qw), BF16), jax.ShapeDtypeStruct((N_TOK, kw), BF16),
                   jax.ShapeDtypeStruct((N_TOK, 2 * kw), BF16), jax.ShapeDtypeStruct((N_TOK, MEM_W), BF16)],
        compiler_params=_params(("parallel",)),
        name="a_in_proj",
    )(x, w, m128, qg, kg, c, sa, sb)


def _b_in_kernel(*refs):
    ng = len(DILATIONS)
    nx = D_MODEL // LANES
    x_refs, w_ref, refs = refs[:nx], refs[nx], refs[nx + 1:]
    tabs, outs, qm_ref = refs[:3 * ng], refs[3 * ng:6 * ng], refs[6 * ng]
    xb1 = None
    for g, d in enumerate(DILATIONS):
        n = TOK_TILE // d
        if d == 1:
            cols = [r[...] for r in x_refs]
        else:
            cols = [jnp.concatenate([r[pl.ds(k, n, stride=d), :] for k in range(d)], axis=0) for r in x_refs]
        xb = jnp.concatenate(cols, axis=1).astype(BF16)
        xb1 = xb if d == 1 else xb1
        part = pl.ds(pl.multiple_of((pl.program_id(0) % TILES_PER_SEQ) * n, n), n)
        c, sa, sb = (t[:, part, :].reshape(TOK_TILE, LANES) for t in tabs[3 * g:3 * g + 3])
        q_ref, k_ref, v_ref = outs[3 * g:3 * g + 3]
        yq = _dot(xb, w_ref[:, g * B_WIDTH:(g + 1) * B_WIDTH]) * SCALE
        yk = _dot(xb, w_ref[:, (ng + g) * B_WIDTH:(ng + g + 1) * B_WIDTH])
        yv = _dot(xb, w_ref[:, (2 * ng + g) * B_WIDTH:(2 * ng + g + 1) * B_WIDTH])
        for ch in range(B_WIDTH // LANES):
            sl = slice(ch * LANES, (ch + 1) * LANES)
            q_ref[0, :, :, sl] = _rope(yq[:, sl], c, sa, sb, PARTIAL_ROT_DIMS // 2).reshape(d, n, LANES).astype(BF16)
            k_ref[0, :, :, sl] = _rope(yk[:, sl], c, sa, sb, PARTIAL_ROT_DIMS // 2).reshape(d, n, LANES).astype(BF16)
        v_ref[0] = yv.reshape(d, n, B_WIDTH).astype(BF16)
    qm = _dot(xb1, w_ref[:, 3 * ng * B_WIDTH:3 * ng * B_WIDTH + MEM_W])
    qm_ref[...] = (qm * SCALE).astype(BF16)


def _b_in_proj(x, w, tabs):
    nx = D_MODEL // LANES
    in_specs = [pl.BlockSpec((TOK_TILE, LANES), functools.partial(lambda i, c: (i, c), c=c)) for c in range(nx)]
    resident = pl.Buffered(1)
    in_specs.append(pl.BlockSpec(w.shape, lambda i: (0, 0), pipeline_mode=resident))
    out_specs, out_shape = [], []
    for d in DILATIONS:
        n = TOK_TILE // d
        in_specs += [pl.BlockSpec((d, SEQ // d, LANES), lambda i: (0, 0, 0), pipeline_mode=resident)] * 3
        out_specs += [pl.BlockSpec((1, d, n, B_WIDTH), lambda i: (i // TILES_PER_SEQ, 0, i % TILES_PER_SEQ, 0))] * 3
        out_shape += [jax.ShapeDtypeStruct((BATCH, d, SEQ // d, B_WIDTH), BF16)] * 3
    out_specs.append(pl.BlockSpec((TOK_TILE, MEM_W), lambda i: (i, 0)))
    out_shape.append(jax.ShapeDtypeStruct((N_TOK, MEM_W), BF16))
    res = pl.pallas_call(
        _b_in_kernel,
        grid=(N_TOK // TOK_TILE,),
        in_specs=in_specs,
        out_specs=out_specs,
        out_shape=out_shape,
        compiler_params=_params(("parallel",)),
        name="b_in_proj",
    )(*([x] * nx), w, *[t for group in tabs for t in group])
    return [res[3 * g:3 * g + 3] for g in range(len(DILATIONS))], res[-1]


def _gqa_kernel(q_ref, k_ref, v_ref, o_ref):
    tq = q_ref.shape[0]
    ntile = q_ref.shape[1] // LANES
    q = jnp.concatenate([q_ref[:, t * LANES:(t + 1) * LANES] for t in range(ntile)], axis=0)
    first = _first_head(q.shape)
    res = []
    for pick in (first, jnp.logical_not(first)):
        qh = jnp.where(pick, q, jnp.zeros_like(q))
        m = jnp.full((q.shape[0], 1), -jnp.inf, F32)
        acc = jnp.zeros((q.shape[0], 2 * LANES), F32)
        for c in range(SEQ // GQA_KC):
            keys = slice(c * GQA_KC, (c + 1) * GQA_KC)
            s = _dot_nt(qh, k_ref[keys, :])
            m_new = jnp.maximum(m, jnp.max(s, axis=1, keepdims=True))
            p = jnp.exp2(s - m_new).astype(BF16)
            acc = acc * jnp.exp2(m - m_new) + _dot(p, v_ref[keys, :])
            m = m_new
        res.append((acc, m))
    o = _pair_output(res, first)
    for t in range(ntile):
        o_ref[:, t * LANES:(t + 1) * LANES] = o[t * tq:(t + 1) * tq].astype(BF16)


def _gqa_attention(q, k, v, tq=1024):
    nq = SEQ // tq
    qw = q.shape[1] // 2
    kw = k.shape[1] // 2
    return pl.pallas_call(
        _gqa_kernel,
        grid=(BATCH, 2, nq),
        in_specs=[pl.BlockSpec((tq, qw), lambda b, p, i: (b * nq + i, p)),
                  pl.BlockSpec((SEQ, kw), lambda b, p, i: (b, p)),
                  pl.BlockSpec((SEQ, 2 * kw), lambda b, p, i: (b, p))],
        out_specs=pl.BlockSpec((tq, qw), lambda b, p, i: (b * nq + i, p)),
        out_shape=jax.ShapeDtypeStruct(q.shape, BF16),
        compiler_params=_params(("parallel", "parallel", "parallel")),
        name="gqa_attention",
    )(q, k, v)


def _band_kernel(q_ref, k_ref, v_ref, o_ref, lse_ref, *, length, seg, win):
    tq = BAND_TQ
    first = _first_head((tq, LANES))
    lane = lax.broadcasted_iota(jnp.int32, (tq, LANES), 1)
    row = lax.broadcasted_iota(jnp.int32, (tq, win), 0)
    col = lax.broadcasted_iota(jnp.int32, (tq, win), 1)
    for u in range(BAND_TILES):
        tile = pl.program_id(1) * BAND_TILES + u
        start = pl.multiple_of(jnp.clip(tile * tq - BAND_RADIUS, 0, length - win), BAND_RADIUS)
        qpos, kpos = tile * tq + row, start + col
        valid = jnp.abs(qpos - kpos) <= BAND_RADIUS
        if seg < length:
            valid = jnp.logical_and(valid, qpos // seg == kpos // seg)
        rows = slice(u * tq, (u + 1) * tq)
        lse = jnp.zeros((tq, LANES), F32)
        for j in range(B_WIDTH // LANES):
            sl = slice(j * LANES, (j + 1) * LANES)
            res = _pair_attention(q_ref[0, rows, sl], k_ref[0, pl.ds(start, win), sl],
                                  _with_ones(v_ref[0, pl.ds(start, win), sl]), first, valid)
            o_ref[0, rows, sl] = _pair_output(res, first).astype(BF16)
            for half, (ol, m) in enumerate(res):
                lse = jnp.where(lane == 2 * j + half, m + jnp.log(ol[:, LANES:]), lse)
        lse_ref[0, rows, :] = lse


def _band_attention(q, k, v, d):
    seg = SEQ // d
    step_rows = BAND_TILES * BAND_TQ
    length = max(seg, step_rows)
    win = BAND_TQ + 2 * BAND_RADIUS
    nblk = N_TOK // length
    q, k, v = (t.reshape(nblk, length, B_WIDTH) for t in (q, k, v))
    seq_spec = pl.BlockSpec((1, length, B_WIDTH), lambda s, i: (s, 0, 0))
    o, lse = pl.pallas_call(
        functools.partial(_band_kernel, length=length, seg=seg, win=win),
        grid=(nblk, length // step_rows),
        in_specs=[pl.BlockSpec((1, step_rows, B_WIDTH), lambda s, i: (s, i, 0)), seq_spec, seq_spec],
        out_specs=[pl.BlockSpec((1, step_rows, B_WIDTH), lambda s, i: (s, i, 0)),
                   pl.BlockSpec((1, step_rows, LANES), lambda s, i: (s, i, 0))],
        out_shape=[jax.ShapeDtypeStruct((nblk, length, B_WIDTH), BF16),
                   jax.ShapeDtypeStruct((nblk, length, LANES), F32)],
        compiler_params=_params(("parallel", "parallel")),
        name=f"band_attention_d{d}",
    )(q, k, v)
    return o.reshape(BATCH, d, seg, B_WIDTH), lse.reshape(BATCH, d, seg, LANES)


def _route(x, rows, wh_ref, wl_ref, rb_ref, tri_ref, xa_ref, cls_ref, rank_ref, cnt_ref, carry_ref):
    tm = x.shape[0]
    xh, xl = _split_bf16(x)
    both = _dot_nt(jnp.concatenate([wh_ref[...], wl_ref[...]], axis=0), xh)
    logits = both[:N_EXPERTS] + _dot_nt(wh_ref[...], xl) + both[N_EXPERTS:]
    aff = 1.0 / (1.0 + jnp.exp(-logits))
    sel = aff + rb_ref[...]
    s = [sel[e:e + 1, :] for e in range(N_EXPERTS)]
    a = [aff[e:e + 1, :] for e in range(N_EXPERTS)]

    def top2_sum(v):
        hi01, lo01 = jnp.maximum(v[0], v[1]), jnp.minimum(v[0], v[1])
        hi23, lo23 = jnp.maximum(v[2], v[3]), jnp.minimum(v[2], v[3])
        return jnp.maximum(hi01, hi23) + jnp.maximum(jnp.minimum(hi01, hi23), jnp.maximum(lo01, lo23))

    gscore = [top2_sum(s[EXPERTS_PER_GROUP * g:EXPERTS_PER_GROUP * (g + 1)]) for g in range(N_EXPERT_GROUPS)]
    best = jnp.zeros((1, tm), jnp.int32)
    best_score = gscore[0]
    for g in range(1, N_EXPERT_GROUPS):
        better = gscore[g] > best_score
        best = jnp.where(better, g, best)
        best_score = jnp.where(better, gscore[g], best_score)

    def pick(rows, j):
        out = rows[j]
        for g in range(1, N_EXPERT_GROUPS):
            out = jnp.where(best == g, rows[EXPERTS_PER_GROUP * g + j], out)
        return out

    t = [pick(s, j) for j in range(EXPERTS_PER_GROUP)]
    w = [pick(a, j) for j in range(EXPERTS_PER_GROUP)]

    def first_max(v):
        mx = jnp.maximum(jnp.maximum(v[0], v[1]), jnp.maximum(v[2], v[3]))
        idx = jnp.full((1, tm), EXPERTS_PER_GROUP - 1, jnp.int32)
        for j in range(EXPERTS_PER_GROUP - 2, -1, -1):
            idx = jnp.where(v[j] == mx, j, idx)
        return idx

    i1 = first_max(t)
    i2 = first_max([jnp.where(i1 == j, -jnp.inf, t[j]) for j in range(EXPERTS_PER_GROUP)])
    lo, hi = jnp.minimum(i1, i2), jnp.maximum(i1, i2)

    def take(rows, idx):
        out = rows[0]
        for j in range(1, EXPERTS_PER_GROUP):
            out = jnp.where(idx == j, rows[j], out)
        return out

    w_lo, w_hi = take(w, lo), take(w, hi)
    den = w_lo + w_hi
    pair = jnp.where(lo == 0, hi - 1, jnp.where(lo == 1, hi + 1, len(PAIRS) - 1))
    cls = best * len(PAIRS) + pair

    onehot = (lax.broadcasted_iota(jnp.int32, (CLASS_ROWS, tm), 0) == cls).astype(F32)
    before = _dot(onehot.astype(BF16), tri_ref[...]) + carry_ref[:, 0:1]
    rank = jnp.sum(onehot * before, axis=0, keepdims=True)
    carry_ref[...] = carry_ref[...] + jnp.sum(onehot, axis=1, keepdims=True)

    cls_ref[0, :, rows] = cls
    rank_ref[0, :, rows] = rank.astype(jnp.int32)
    cnt_ref[...] = carry_ref[...]
    gates = jnp.concatenate([w_lo / den, w_hi / den, jnp.zeros((LANES - 2, tm), F32)], axis=0)
    xa_ref[rows, :D_MODEL] = x
    xa_ref[rows, D_MODEL:] = gates.T


def _post_attn_kernel(*refs, merge):
    if merge:
        o_refs, l_refs, e_ref, refs = refs[:3], refs[3:6], refs[6], refs[7:]
    else:
        mix_ref, refs = refs[0], refs[1:]
    (qm_ref, mk_ref, mv_ref, w1_ref, w2_ref, x_ref, g_ref, b_ref, wh_ref, wl_ref, rb_ref, tri_ref,
     xa_ref, cls_ref, rank_ref, cnt_ref, carry_ref) = refs[:17]

    @pl.when(pl.program_id(0) == 0)
    def _():
        carry_ref[...] = jnp.zeros_like(carry_ref)

    first = _first_head((TOK_TILE, LANES))
    for sub in range(POST_SUB):
        rows = slice(sub * TOK_TILE, (sub + 1) * TOK_TILE)
        if merge:
            ot_ref, lt_ref = refs[17], refs[18]
            nch = B_WIDTH // LANES
            for g, d in enumerate(DILATIONS):
                n = TOK_TILE // d
                for r in range(d):
                    dst = pl.ds(r, n, stride=d) if d > 1 else slice(None)
                    og = o_refs[g][0, r, sub * n:(sub + 1) * n].astype(F32)
                    for ch in range(nch):
                        ot_ref[sub, g, ch, dst, :] = og[:, ch * LANES:(ch + 1) * LANES]
                    lt_ref[sub, g, dst, :] = l_refs[g][0, r, sub * n:(sub + 1) * n]
            lses = [lt_ref[sub, g] for g in range(len(DILATIONS))]
            m = functools.reduce(jnp.maximum, lses)
            es = [jnp.exp(l - m) for l in lses]
            inv = 1.0 / functools.reduce(jnp.add, es)
            expand = e_ref[...]

            def widen(w):
                hi, lo = _split_bf16(w)
                return _dot(hi, expand) + _dot(lo, expand)

            ws = [widen(e * inv) for e in es]
            mix = jnp.concatenate(
                [functools.reduce(jnp.add, [w[:, ch * LANES:(ch + 1) * LANES] * ot_ref[sub, g, ch]
                                            for g, w in enumerate(ws)]) for ch in range(nch)], axis=1).astype(BF16)
        else:
            mix = mix_ref[rows, :]

        memo = []
        for j in range(MEM_W // LANES):
            sl = slice(j * LANES, (j + 1) * LANES)
            memo.append(_pair_output(_pair_attention(qm_ref[rows, sl], mk_ref[:, sl], _with_ones(mv_ref[:, sl]), first),
                                     first))
        memo = jnp.concatenate(memo, axis=1).astype(BF16)

        attn = _dot(mix, w1_ref[...]) + _dot(memo, w2_ref[...])
        x1 = _layer_norm(ALPHA * x_ref[rows, :] + attn, g_ref[...], b_ref[...])
        _route(x1, rows, wh_ref, wl_ref, rb_ref, tri_ref, xa_ref, cls_ref, rank_ref, cnt_ref, carry_ref)


def _post_attn(mix, qm, mkv, w1, w2, x, g, b, router, expand=None):
    merge = expand is not None
    wh, wl, rb, tri = router
    blk = POST_SUB * TOK_TILE
    nblk = N_TOK // blk
    per_seq = SEQ // blk
    row = lambda width: pl.BlockSpec((blk, width), lambda i: (i, 0))
    const = lambda shape: pl.BlockSpec(shape, lambda i: (0,) * len(shape))
    row3 = pl.BlockSpec((1, 1, blk), lambda i: (i, 0, 0))
    scratch = [pltpu.VMEM((CLASS_ROWS, LANES), F32)]
    if merge:
        outs, lses = mix
        resid = lambda d, width: pl.BlockSpec((1, d, blk // d, width), lambda i: (i // per_seq, 0, i % per_seq, 0))
        lead_specs = ([resid(d, B_WIDTH) for d in DILATIONS] + [resid(d, LANES) for d in DILATIONS]
                      + [const(expand.shape)])
        lead = [*outs, *lses, expand]
        scratch += [pltpu.VMEM((POST_SUB, len(DILATIONS), B_WIDTH // LANES, TOK_TILE, LANES), F32),
                    pltpu.VMEM((POST_SUB, len(DILATIONS), TOK_TILE, LANES), F32)]
    else:
        lead_specs, lead = [row(mix.shape[1])], [mix]
    xa, cls, rank, cnt = pl.pallas_call(
        functools.partial(_post_attn_kernel, merge=merge),
        grid=(nblk,),
        in_specs=lead_specs + [row(MEM_W),
                               pl.BlockSpec((N_MEM, MEM_W), lambda i: (i // per_seq, 0)),
                               pl.BlockSpec((N_MEM, MEM_W), lambda i: (i // per_seq, 1)),
                               const(w1.shape), const(w2.shape), row(D_MODEL), const((1, D_MODEL)), const((1, D_MODEL)),
                               const(wh.shape), const(wl.shape), const(rb.shape), const(tri.shape)],
        out_specs=[row(XA_W), row3, row3, const((CLASS_ROWS, LANES))],
        out_shape=[jax.ShapeDtypeStruct((N_TOK, XA_W), F32), jax.ShapeDtypeStruct((nblk, 1, blk), jnp.int32),
                   jax.ShapeDtypeStruct((nblk, 1, blk), jnp.int32), jax.ShapeDtypeStruct((CLASS_ROWS, LANES), F32)],
        scratch_shapes=scratch,
        compiler_params=_params(("arbitrary",)),
        name="post_attn_merge" if merge else "post_attn",
    )(*lead, qm, mkv, mkv, w1, w2, x, g, b, wh, wl, rb, tri)
    return xa, cls.reshape(N_TOK), rank.reshape(N_TOK), cnt[:N_CLASSES, 0].astype(jnp.int32)


def _dispatch_kernel(pos_ref, ends_ref, xa_ref, xs_ref, zeros, sem, zsem, *, chunk):
    base = pl.program_id(0) * chunk

    @pl.when(pl.program_id(0) == 0)
    def _():
        zeros[...] = jnp.zeros_like(zeros)

        def tail(c):
            start = pl.multiple_of(ends_ref[c] - MOE_TM, MOE_TM)
            return pltpu.make_async_copy(zeros, xs_ref.at[pl.ds(start, MOE_TM)], zsem)

        def nonempty(c):
            return ends_ref[c] > (ends_ref[c - 1] if c else 0)

        def spare(t):
            return pltpu.make_async_copy(zeros, xs_ref.at[pl.ds(pl.multiple_of(t * MOE_TM, MOE_TM), MOE_TM)], zsem)

        first_spare = ends_ref[N_CLASSES - 1] // MOE_TM
        for c in range(N_CLASSES):
            pl.when(nonempty(c))(lambda c=c: tail(c).start())
        lax.fori_loop(first_spare, MOE_TILES, lambda t, carry: (spare(t).start(), carry)[1], 0)
        for c in range(N_CLASSES):
            pl.when(nonempty(c))(lambda c=c: tail(c).wait())
        lax.fori_loop(first_spare, MOE_TILES, lambda t, carry: (spare(t).wait(), carry)[1], 0)

    for r in range(chunk):
        pltpu.make_async_copy(xa_ref.at[pl.ds(r, 1)], xs_ref.at[pl.ds(pos_ref[base + r], 1)], sem).start(priority=r % 2)
    pltpu.make_async_copy(xa_ref, xs_ref.at[pl.ds(0, chunk)], sem).wait()


def _dispatch(pos, ends, xa, chunk=1024):
    return pl.pallas_call(
        functools.partial(_dispatch_kernel, chunk=chunk),
        grid_spec=pltpu.PrefetchScalarGridSpec(
            num_scalar_prefetch=2,
            grid=(N_TOK // chunk,),
            in_specs=[pl.BlockSpec((chunk, XA_W), lambda i, p, e: (i, 0))],
            out_specs=pl.BlockSpec(memory_space=pl.ANY),
            scratch_shapes=[pltpu.VMEM((MOE_TM, XA_W), F32), pltpu.SemaphoreType.DMA(()),
                            pltpu.SemaphoreType.DMA(())],
        ),
        out_shape=jax.ShapeDtypeStruct((MOE_ROWS, XA_W), F32),
        compiler_params=_params(("arbitrary",)),
        name="moe_dispatch",
    )(pos, ends, xa)


def _expert_kernel(slot1_ref, slot2_ref, used_ref, first_ref, grp_ref, next_ref, xs_ref, wg_hbm, wu_hbm, wd_hbm,
                   ys_ref, stage_g, stage_u, stage_d, wg_s, wu_s, wd_s, sem, *, layer):
    i = pl.program_id(0)
    streams = ((wg_hbm, stage_g, wg_s), (wu_hbm, stage_u, wu_s), (wd_hbm, stage_d, wd_s))

    def group_copy(group, k):
        first_expert = layer * N_EXPERTS + group * EXPERTS_PER_GROUP
        hbm, stage, _ = streams[k]
        return pltpu.make_async_copy(hbm.at[pl.ds(first_expert, EXPERTS_PER_GROUP)], stage, sem.at[k])

    @pl.when(i == 0)
    def _():
        for k in range(len(streams)):
            group_copy(grp_ref[0], k).start()

    @pl.when(first_ref[i] != 0)
    def _():
        for k, (_, stage, dst) in enumerate(streams):
            group_copy(grp_ref[i], k).wait()
            for e in range(EXPERTS_PER_GROUP):
                dst[e] = stage[e].astype(BF16)

        @pl.when(next_ref[i] >= 0)
        def _():
            for k in range(len(streams)):
                group_copy(next_ref[i], k).start()

    @pl.when(used_ref[i] != 0)
    def _():
        x = xs_ref[:, :D_MODEL].astype(BF16)
        gates = xs_ref[:, D_MODEL:]

        def expert(slot):
            gate = _dot(x, wg_s[slot])
            h = gate * (1.0 / (1.0 + jnp.exp(-gate))) * _dot(x, wu_s[slot])
            return _dot(h.astype(BF16), wd_s[slot])

        ys_ref[...] = gates[:, 0:1] * expert(slot1_ref[i]) + gates[:, 1:2] * expert(slot2_ref[i])

    @pl.when(used_ref[i] == 0)
    def _():
        ys_ref[...] = jnp.zeros_like(ys_ref)


def _experts(tile_meta, xs, layer, wg, wu, wd):
    nmeta = len(tile_meta)
    up_shape = (EXPERTS_PER_GROUP, D_MODEL, D_EXPERT)
    down_shape = (EXPERTS_PER_GROUP, D_EXPERT, D_MODEL)
    return pl.pallas_call(
        functools.partial(_expert_kernel, layer=layer),
        grid_spec=pltpu.PrefetchScalarGridSpec(
            num_scalar_prefetch=nmeta,
            grid=(MOE_TILES,),
            in_specs=[pl.BlockSpec((MOE_TM, XA_W), lambda i, *_: (i, 0)),
                      pl.BlockSpec(memory_space=pl.ANY), pl.BlockSpec(memory_space=pl.ANY),
                      pl.BlockSpec(memory_space=pl.ANY)],
            out_specs=pl.BlockSpec((MOE_TM, D_MODEL), lambda i, *_: (i, 0)),
            scratch_shapes=[pltpu.VMEM(up_shape, F32), pltpu.VMEM(up_shape, F32), pltpu.VMEM(down_shape, F32),
                            pltpu.VMEM(up_shape, BF16), pltpu.VMEM(up_shape, BF16), pltpu.VMEM(down_shape, BF16),
                            pltpu.SemaphoreType.DMA((3,))],
        ),
        out_shape=jax.ShapeDtypeStruct((MOE_ROWS, D_MODEL), F32),
        compiler_params=_params(("arbitrary",)),
        name="moe_experts",
    )(*tile_meta, xs, wg, wu, wd)


def _combine_kernel(pos_ref, ys_ref, x_ref, g_ref, b_ref, o_ref, buf, sem, *, tc):
    i = pl.program_id(0)
    slot = i % 2

    def gather(step, into):
        for r in range(tc):
            row = pos_ref[step * tc + r]
            pltpu.make_async_copy(ys_ref.at[pl.ds(row, 1)], buf.at[into, pl.ds(r, 1)], sem.at[into]).start(
                priority=r % 2)

    @pl.when(i == 0)
    def _():
        gather(0, 0)

    @pl.when(i + 1 < pl.num_programs(0))
    def _():
        gather(i + 1, 1 - slot)

    pltpu.make_async_copy(ys_ref.at[pl.ds(0, tc)], buf.at[slot], sem.at[slot]).wait()
    o_ref[...] = _layer_norm(ALPHA * x_ref[...] + buf[slot], g_ref[...], b_ref[...])


def _combine_ln(pos, ys, xa, g, b, tc=512):
    return pl.pallas_call(
        functools.partial(_combine_kernel, tc=tc),
        grid_spec=pltpu.PrefetchScalarGridSpec(
            num_scalar_prefetch=1,
            grid=(N_TOK // tc,),
            in_specs=[pl.BlockSpec(memory_space=pl.ANY),
                      pl.BlockSpec((tc, D_MODEL), lambda i, p: (i, 0)),
                      pl.BlockSpec((1, D_MODEL), lambda i, p: (0, 0)),
                      pl.BlockSpec((1, D_MODEL), lambda i, p: (0, 0))],
            out_specs=pl.BlockSpec((tc, D_MODEL), lambda i, p: (i, 0)),
            scratch_shapes=[pltpu.VMEM((2, tc, D_MODEL), F32), pltpu.SemaphoreType.DMA((2,))],
        ),
        out_shape=jax.ShapeDtypeStruct((N_TOK, D_MODEL), F32),
        compiler_params=_params(("arbitrary",)),
        name="moe_combine_ln",
    )(pos, ys, xa, g, b)


def _moe_ln(routed, layer, wg, wu, wd, g, b):
    xa, cls, rank, counts = routed
    padded = (counts + MOE_TM - 1) // MOE_TM * MOE_TM
    ends = jnp.cumsum(padded)
    starts = ends - padded
    pos = starts[cls] + rank
    tile_start = jnp.arange(MOE_TILES, dtype=jnp.int32) * MOE_TM
    tile_used = (tile_start < ends[-1]).astype(jnp.int32)
    last_cls = jnp.max(jnp.where(counts > 0, jnp.arange(N_CLASSES), 0))
    tile_cls = jnp.minimum(jnp.sum(tile_start[:, None] >= ends[None, :], axis=1), last_cls)
    tile_cls = tile_cls.astype(jnp.int32)
    tile_grp = tile_cls // len(PAIRS)
    prev_grp = jnp.concatenate([jnp.full((1,), -1, jnp.int32), tile_grp[:-1]])
    tile_first = tile_used * (tile_grp != prev_grp).astype(jnp.int32)
    has = jnp.sum(counts.reshape(N_EXPERT_GROUPS, len(PAIRS)), axis=1) > 0
    gid = jnp.arange(N_EXPERT_GROUPS, dtype=jnp.int32)
    later = jnp.where(jnp.logical_and(has[None, :], gid[None, :] > gid[:, None]), gid[None, :], N_EXPERT_GROUPS)
    next_grp = jnp.min(later, axis=1)
    next_grp = jnp.where(next_grp == N_EXPERT_GROUPS, -1, next_grp)
    pair_slots = jnp.asarray(np.array(PAIRS, np.int32))[tile_cls % len(PAIRS)]
    tile_meta = (pair_slots[:, 0], pair_slots[:, 1], tile_used, tile_first, tile_grp, next_grp[tile_grp])
    xs = _dispatch(pos, ends.astype(jnp.int32), xa)
    ys = _experts(tile_meta, xs, layer, wg, wu, wd)
    return _combine_ln(pos, ys, xa, g, b)


def _tables(c, sa, sb):
    return tuple(np.concatenate([t, t], axis=-1).astype(np.float32) for t in (c, sa, sb))


def _axial_tables():
    rows = SEQ // GRID_W
    row = np.repeat(np.arange(rows), GRID_W).astype(np.float64)
    col = np.tile(np.arange(GRID_W), rows).astype(np.float64)
    half = HEAD_DIM // 2
    inv = 1.0 / (AXIAL_THETA ** (np.arange(0, half, 2, dtype=np.float64) / half))
    ar, ac = row[:, None] * inv, col[:, None] * inv
    z = np.zeros_like(ar)
    c = np.concatenate([np.cos(ar), np.cos(ar), np.cos(ac), np.cos(ac)], -1)
    sa = np.concatenate([-np.sin(ar), z, -np.sin(ac), z], -1)
    sb = np.concatenate([z, np.sin(ar), z, np.sin(ac)], -1)
    return _tables(c, sa, sb)


def _partial_tables():
    n = PARTIAL_ROT_DIMS
    pos = np.arange(SEQ, dtype=np.float64)
    ang = pos[:, None] * (1.0 / (PARTIAL_THETA ** (np.arange(0, n, 2, dtype=np.float64) / n)))
    z = np.zeros_like(ang)
    rest = np.zeros((SEQ, HEAD_DIM - n))
    c = np.concatenate([np.cos(ang), np.cos(ang), rest + 1.0], -1)
    sa = np.concatenate([-np.sin(ang), z, rest], -1)
    sb = np.concatenate([z, np.sin(ang), rest], -1)
    return _tables(c, sa, sb)


def _residue_order(tab, d):
    return np.ascontiguousarray(tab.reshape(SEQ // d, d, LANES).transpose(1, 0, 2))


def kernel(x, mem, w_mem_kv, router_w, router_b, a_w_in, a_w_out, a_q_norm, a_k_norm, b_w_in, b_w_out,
           ln1_g, ln1_b, ln2_g, ln2_b, w_gate, w_up, w_down):
    x = x.reshape(N_TOK, D_MODEL)
    row = lambda v: v.reshape(1, -1)

    lane = np.arange(LANES)
    m128 = jnp.asarray((lane[:, None] // HEAD_DIM == lane[None, :] // HEAD_DIM), BF16)
    expand = jnp.asarray(lane[:, None] == np.arange(B_WIDTH)[None, :] // HEAD_DIM, BF16)
    tri = jnp.asarray(np.triu(np.ones((TOK_TILE, TOK_TILE), np.float32), 1), BF16)
    rw_hi = router_w.T.astype(BF16)
    rw_lo = (router_w.T - rw_hi.astype(F32)).astype(BF16)
    router = (rw_hi, rw_lo, router_b.reshape(N_EXPERTS, 1), tri)

    mkv = _proj(mem.reshape(BATCH * N_MEM, D_MODEL), w_mem_kv.astype(BF16))
    experts = tuple(w.reshape(DEPTH * N_EXPERTS, *w.shape[2:]) for w in (w_gate, w_up, w_down))

    kmix = A_Q_HEADS * HEAD_DIM
    qcols = (np.asarray(_Q_HEAD_ORDER)[:, None] * HEAD_DIM + np.arange(HEAD_DIM)[None, :]).reshape(-1)
    w_in = jnp.concatenate([a_w_in[0][:, qcols], a_w_in[0][:, kmix:]], axis=1).astype(BF16)
    qg = row(jnp.tile(a_q_norm[0], 2) * (SCALE * LOG2_E))
    kg = row(jnp.tile(a_k_norm[0], 2))
    q, k, v, qm = _a_in_proj(x, w_in, m128, qg, kg, _axial_tables())
    mix = _gqa_attention(q, k, v)
    w_out = a_w_out[0].astype(BF16)
    routed = _post_attn(mix, qm, mkv, w_out[qcols], w_out[kmix:], x, row(ln1_g[0]), row(ln1_b[0]), router)
    x = _moe_ln(routed, 0, *experts, row(ln2_g[0]), row(ln2_b[0]))

    ptabs = _partial_tables()
    tabs = [tuple(_residue_order(t, d) for t in ptabs) for d in DILATIONS]
    qkv, qm = _b_in_proj(x, b_w_in[0].astype(BF16), tabs)
    outs, lses = zip(*[_band_attention(*qkv[g], d) for g, d in enumerate(DILATIONS)])
    w_out = b_w_out[0].astype(BF16)
    routed = _post_attn((outs, lses), qm, mkv, w_out[:B_WIDTH], w_out[B_WIDTH:], x, row(ln1_g[1]), row(ln1_b[1]),
                        router, expand)
    x = _moe_ln(routed, 1, *experts, row(ln2_g[1]), row(ln2_b[1]))
    return x.reshape(BATCH, SEQ, D_MODEL)
```

```python
import functools
import math

import numpy as np
import jax
import jax.numpy as jnp
from jax import lax
from jax.experimental import pallas as pl
from jax.experimental.pallas import tpu as pltpu

F32 = jnp.float32
BF16 = jnp.bfloat16

D_MODEL = 1024
BATCH = 8
SEQ = 2048
N_TOK = BATCH * SEQ
HEAD_DIM = 64
GRID_W = 64
N_MEM = 256
MEM_HEADS = 4
A_Q_HEADS = 12
A_KV_HEADS = 4
AXIAL_THETA = 10000.0
B_HEADS = 8
B_WIDTH = B_HEADS * HEAD_DIM
DILATIONS = (1, 4, 16)
BAND_RADIUS = 64
PARTIAL_ROT_DIMS = HEAD_DIM // 4
PARTIAL_THETA = 500000.0
N_EXPERTS = 16
N_EXPERT_GROUPS = 4
EXPERTS_PER_GROUP = 4
D_EXPERT = 512
DEPTH = 2
ALPHA = (2 * DEPTH) ** 0.25
NORM_EPS = 1e-6
NEG_BIG = -1e30
SCALE = HEAD_DIM ** -0.5
LOG2_E = math.log2(math.e)
LN_2 = math.log(2.0)
Q_SCALE = SCALE * LOG2_E
_Q_HEAD_ORDER = tuple(6 * p + 3 * half + t for p in range(2) for t in range(3) for half in range(2))

LANES = 128
MEM_W = MEM_HEADS * HEAD_DIM
TOK_TILE = 512
TILES_PER_SEQ = SEQ // TOK_TILE
POST_SUB = 2
GQA_KC = 256
BAND_TQ = 128
BAND_TILES = 8

PAIRS = ((0, 1), (0, 2), (0, 3), (1, 2), (1, 3), (2, 3))
N_CLASSES = N_EXPERT_GROUPS * len(PAIRS)
CLASS_ROWS = 32
MOE_TM = 256
MOE_ROWS = N_TOK + N_CLASSES * MOE_TM
MOE_TILES = MOE_ROWS // MOE_TM
XA_W = D_MODEL + LANES
VMEM_LIMIT = 56 * 1024 * 1024


def _params(sem, vmem=VMEM_LIMIT):
    return pltpu.CompilerParams(dimension_semantics=sem, vmem_limit_bytes=vmem)


def _dot(a, b):
    return jnp.dot(a, b, preferred_element_type=F32)


def _dot_nt(a, b):
    return lax.dot_general(a, b, (((1,), (1,)), ((), ())), preferred_element_type=F32)


def _split_bf16(x):
    hi = x.astype(BF16)
    lo = (x - hi.astype(F32)).astype(BF16)
    return hi, lo


def _layer_norm(z, g, b):
    mu = jnp.mean(z, axis=-1, keepdims=True)
    zc = z - mu
    var = jnp.mean(zc * zc, axis=-1, keepdims=True)
    return zc * lax.rsqrt(var + NORM_EPS) * g + b


def _rope(z, c, sa, sb, shift):
    return z * c + pltpu.roll(z, LANES - shift, 1) * sa + pltpu.roll(z, shift, 1) * sb


def _first_head(shape):
    return lax.broadcasted_iota(jnp.int32, shape, 1) < HEAD_DIM


def _pair_attention(q2, k2, v3, first, valid=None):
    res = []
    for pick in (first, jnp.logical_not(first)):
        s = _dot_nt(jnp.where(pick, q2, jnp.zeros_like(q2)), k2)
        if valid is not None:
            s = jnp.where(valid, s, NEG_BIG)
        m = jnp.max(s, axis=1, keepdims=True)
        p = jnp.exp2(s - m)
        res.append((_dot(p.astype(BF16), v3), m))
    return res


def _pair_output(res, first):
    (ol_a, _), (ol_b, _) = res
    return jnp.where(first, ol_a[:, :LANES], ol_b[:, :LANES]) / jnp.where(first, ol_a[:, LANES:], ol_b[:, LANES:])


def _with_ones(v2):
    return jnp.concatenate([v2, jnp.ones_like(v2)], axis=1)


def _proj_kernel(x_ref, w_ref, o_ref):
    o_ref[...] = _dot(x_ref[...].astype(BF16), w_ref[...]).astype(o_ref.dtype)


def _proj(x, w, tm=TOK_TILE):
    n, k = x.shape
    width = w.shape[1]
    return pl.pallas_call(
        _proj_kernel,
        grid=(n // tm,),
        in_specs=[pl.BlockSpec((tm, k), lambda i: (i, 0)), pl.BlockSpec((k, width), lambda i: (0, 0))],
        out_specs=pl.BlockSpec((tm, width), lambda i: (i, 0)),
        out_shape=jax.ShapeDtypeStruct((n, width), BF16),
        compiler_params=_params(("parallel",)),
        name="proj",
    )(x, w)


def _a_in_kernel(x_ref, w_ref, m_ref, qg_ref, kg_ref, c_ref, sa_ref, sb_ref, q_ref, k_ref, v_ref, qm_ref):
    y = _dot(x_ref[...].astype(BF16), w_ref[...])
    m = m_ref[...]
    c, sa, sb = c_ref[...], sa_ref[...], sb_ref[...]

    def norm_rope(z, gain):
        hi, lo = _split_bf16(z * z)
        ms = (_dot(hi, m) + _dot(lo, m)) * (1.0 / HEAD_DIM)
        z = z * lax.rsqrt(ms + NORM_EPS) * gain
        return _rope(z, c, sa, sb, 16)

    nq = A_Q_HEADS * HEAD_DIM // LANES
    nk = A_KV_HEADS * HEAD_DIM // LANES
    for ch in range(nq):
        q_ref[:, ch * LANES:(ch + 1) * LANES] = norm_rope(y[:, ch * LANES:(ch + 1) * LANES], qg_ref[...]).astype(BF16)
    off = nq * LANES
    for ch in range(nk):
        z = y[:, off + ch * LANES: off + (ch + 1) * LANES]
        k_ref[:, ch * LANES:(ch + 1) * LANES] = norm_rope(z, kg_ref[...]).astype(BF16)
    off += nk * LANES
    ones = jnp.ones((y.shape[0], LANES), BF16)
    for ch in range(nk):
        v_ref[:, 2 * ch * LANES:(2 * ch + 1) * LANES] = y[:, off + ch * LANES: off + (ch + 1) * LANES].astype(BF16)
        v_ref[:, (2 * ch + 1) * LANES:(2 * ch + 2) * LANES] = ones
    off += nk * LANES
    qm_ref[...] = (y[:, off:off + MEM_W] * Q_SCALE).astype(BF16)


def _a_in_proj(x, w, m128, qg, kg, tabs, tm=2 * TOK_TILE):
    c, sa, sb = tabs
    nblk = SEQ // tm
    a_in = w.shape[1]
    tab_spec = pl.BlockSpec((tm, LANES), lambda i: (i % nblk, 0))
    row = lambda width: pl.BlockSpec((tm, width), lambda i: (i, 0))
    const = lambda shape: pl.BlockSpec(shape, lambda i: (0, 0))
    qw, kw = A_Q_HEADS * HEAD_DIM, A_KV_HEADS * HEAD_DIM
    return pl.pallas_call(
        _a_in_kernel,
        grid=(N_TOK // tm,),
        in_specs=[row(D_MODEL), const((D_MODEL, a_in)), const((LANES, LANES)), const((1, LANES)), const((1, LANES)),
                  tab_spec, tab_spec, tab_spec],
        out_specs=[row(qw), row(kw), row(2 * kw), row(MEM_W)],
        out_shape=[jax.ShapeDtypeStruct((N_TOK, qw), BF16), jax.ShapeDtypeStruct((N_TOK, kw), BF16),
                   jax.ShapeDtypeStruct((N_TOK, 2 * kw), BF16), jax.ShapeDtypeStruct((N_TOK, MEM_W), BF16)],
        compiler_params=_params(("parallel",)),
        name="a_in_proj",
    )(x, w, m128, qg, kg, c, sa, sb)


def _b_in_kernel(*refs):
    ng = len(DILATIONS)
    nx = D_MODEL // LANES
    x_refs, w_ref, refs = refs[:nx], refs[nx], refs[nx + 1:]
    tabs, outs, qm_ref = refs[:3 * ng], refs[3 * ng:6 * ng], refs[6 * ng]
    xb1 = None
    for g, d in enumerate(DILATIONS):
        n = TOK_TILE // d
        if d == 1:
            cols = [r[...] for r in x_refs]
        else:
            cols = [jnp.concatenate([r[pl.ds(k, n, stride=d), :] for k in range(d)], axis=0) for r in x_refs]
        xb = jnp.concatenate(cols, axis=1).astype(BF16)
        xb1 = xb if d == 1 else xb1
        part = pl.ds(pl.multiple_of((pl.program_id(0) % TILES_PER_SEQ) * n, n), n)
        c, sa, sb = (t[:, part, :].reshape(TOK_TILE, LANES) for t in tabs[3 * g:3 * g + 3])
        q_ref, k_ref, v_ref = outs[3 * g:3 * g + 3]
        yq = _dot(xb, w_ref[:, g * B_WIDTH:(g + 1) * B_WIDTH]) * Q_SCALE
        yk = _dot(xb, w_ref[:, (ng + g) * B_WIDTH:(ng + g + 1) * B_WIDTH])
        yv = _dot(xb, w_ref[:, (2 * ng + g) * B_WIDTH:(2 * ng + g + 1) * B_WIDTH])
        for ch in range(B_WIDTH // LANES):
            sl = slice(ch * LANES, (ch + 1) * LANES)
            q_ref[0, :, :, sl] = _rope(yq[:, sl], c, sa, sb, PARTIAL_ROT_DIMS // 2).reshape(d, n, LANES).astype(BF16)
            k_ref[0, :, :, sl] = _rope(yk[:, sl], c, sa, sb, PARTIAL_ROT_DIMS // 2).reshape(d, n, LANES).astype(BF16)
        v_ref[0] = yv.reshape(d, n, B_WIDTH).astype(BF16)
    qm = _dot(xb1, w_ref[:, 3 * ng * B_WIDTH:3 * ng * B_WIDTH + MEM_W])
    qm_ref[...] = (qm * Q_SCALE).astype(BF16)


def _b_in_proj(x, w, tabs):
    nx = D_MODEL // LANES
    in_specs = [pl.BlockSpec((TOK_TILE, LANES), functools.partial(lambda i, c: (i, c), c=c)) for c in range(nx)]
    resident = pl.Buffered(1)
    in_specs.append(pl.BlockSpec(w.shape, lambda i: (0, 0), pipeline_mode=resident))
    out_specs, out_shape = [], []
    for d in DILATIONS:
        n = TOK_TILE // d
        in_specs += [pl.BlockSpec((d, SEQ // d, LANES), lambda i: (0, 0, 0), pipeline_mode=resident)] * 3
        out_specs += [pl.BlockSpec((1, d, n, B_WIDTH), lambda i: (i // TILES_PER_SEQ, 0, i % TILES_PER_SEQ, 0))] * 3
        out_shape += [jax.ShapeDtypeStruct((BATCH, d, SEQ // d, B_WIDTH), BF16)] * 3
    out_specs.append(pl.BlockSpec((TOK_TILE, MEM_W), lambda i: (i, 0)))
    out_shape.append(jax.ShapeDtypeStruct((N_TOK, MEM_W), BF16))
    res = pl.pallas_call(
        _b_in_kernel,
        grid=(N_TOK // TOK_TILE,),
        in_specs=in_specs,
        out_specs=out_specs,
        out_shape=out_shape,
        compiler_params=_params(("parallel",)),
        name="b_in_proj",
    )(*([x] * nx), w, *[t for group in tabs for t in group])
    return [res[3 * g:3 * g + 3] for g in range(len(DILATIONS))], res[-1]


def _gqa_kernel(q_ref, k_ref, v_ref, o_ref):
    tq = q_ref.shape[0]
    ntile = q_ref.shape[1] // LANES
    q = jnp.concatenate([q_ref[:, t * LANES:(t + 1) * LANES] for t in range(ntile)], axis=0)
    first = _first_head(q.shape)
    res = []
    for pick in (first, jnp.logical_not(first)):
        qh = jnp.where(pick, q, jnp.zeros_like(q))
        m = jnp.full((q.shape[0], 1), -jnp.inf, F32)
        acc = jnp.zeros((q.shape[0], 2 * LANES), F32)
        for c in range(SEQ // GQA_KC):
            keys = slice(c * GQA_KC, (c + 1) * GQA_KC)
            s = _dot_nt(qh, k_ref[keys, :])
            m_new = jnp.maximum(m, jnp.max(s, axis=1, keepdims=True))
            p = jnp.exp2(s - m_new).astype(BF16)
            acc = acc * jnp.exp2(m - m_new) + _dot(p, v_ref[keys, :])
            m = m_new
        res.append((acc, m))
    o = _pair_output(res, first)
    for t in range(ntile):
        o_ref[:, t * LANES:(t + 1) * LANES] = o[t * tq:(t + 1) * tq].astype(BF16)


def _gqa_attention(q, k, v, tq=1024):
    nq = SEQ // tq
    qw = q.shape[1] // 2
    kw = k.shape[1] // 2
    return pl.pallas_call(
        _gqa_kernel,
        grid=(BATCH, 2, nq),
        in_specs=[pl.BlockSpec((tq, qw), lambda b, p, i: (b * nq + i, p)),
                  pl.BlockSpec((SEQ, kw), lambda b, p, i: (b, p)),
                  pl.BlockSpec((SEQ, 2 * kw), lambda b, p, i: (b, p))],
        out_specs=pl.BlockSpec((tq, qw), lambda b, p, i: (b * nq + i, p)),
        out_shape=jax.ShapeDtypeStruct(q.shape, BF16),
        compiler_params=_params(("parallel", "parallel", "parallel")),
        name="gqa_attention",
    )(q, k, v)


def _band_kernel(q_ref, k_ref, v_ref, o_ref, lse_ref, *, length, seg, win):
    tq = BAND_TQ
    first = _first_head((tq, LANES))
    lane = lax.broadcasted_iota(jnp.int32, (tq, LANES), 1)
    row = lax.broadcasted_iota(jnp.int32, (tq, win), 0)
    col = lax.broadcasted_iota(jnp.int32, (tq, win), 1)
    for u in range(BAND_TILES):
        tile = pl.program_id(1) * BAND_TILES + u
        start = pl.multiple_of(jnp.clip(tile * tq - BAND_RADIUS, 0, length - win), BAND_RADIUS)
        qpos, kpos = tile * tq + row, start + col
        valid = jnp.abs(qpos - kpos) <= BAND_RADIUS
        if seg < length:
            valid = jnp.logical_and(valid, qpos // seg == kpos // seg)
        rows = slice(u * tq, (u + 1) * tq)
        lse = jnp.zeros((tq, LANES), F32)
        for j in range(B_WIDTH // LANES):
            sl = slice(j * LANES, (j + 1) * LANES)
            res = _pair_attention(q_ref[0, rows, sl], k_ref[0, pl.ds(start, win), sl],
                                  _with_ones(v_ref[0, pl.ds(start, win), sl]), first, valid)
            o_ref[0, rows, sl] = _pair_output(res, first).astype(BF16)
            for half, (ol, m) in enumerate(res):
                lse = jnp.where(lane == 2 * j + half, m * LN_2 + jnp.log(ol[:, LANES:]), lse)
        lse_ref[0, rows, :] = lse


def _band_attention(q, k, v, d):
    seg = SEQ // d
    step_rows = BAND_TILES * BAND_TQ
    length = max(seg, step_rows)
    win = BAND_TQ + 2 * BAND_RADIUS
    nblk = N_TOK // length
    q, k, v = (t.reshape(nblk, length, B_WIDTH) for t in (q, k, v))
    seq_spec = pl.BlockSpec((1, length, B_WIDTH), lambda s, i: (s, 0, 0))
    o, lse = pl.pallas_call(
        functools.partial(_band_kernel, length=length, seg=seg, win=win),
        grid=(nblk, length // step_rows),
        in_specs=[pl.BlockSpec((1, step_rows, B_WIDTH), lambda s, i: (s, i, 0)), seq_spec, seq_spec],
        out_specs=[pl.BlockSpec((1, step_rows, B_WIDTH), lambda s, i: (s, i, 0)),
                   pl.BlockSpec((1, step_rows, LANES), lambda s, i: (s, i, 0))],
        out_shape=[jax.ShapeDtypeStruct((nblk, length, B_WIDTH), BF16),
                   jax.ShapeDtypeStruct((nblk, length, LANES), F32)],
        compiler_params=_params(("parallel", "parallel")),
        name=f"band_attention_d{d}",
    )(q, k, v)
    return o.reshape(BATCH, d, seg, B_WIDTH), lse.reshape(BATCH, d, seg, LANES)


def _route(x, rows, wh_ref, wl_ref, rb_ref, tri_ref, xa_ref, cls_ref, rank_ref, cnt_ref, carry_ref):
    tm = x.shape[0]
    xh, xl = _split_bf16(x)
    both = _dot_nt(jnp.concatenate([wh_ref[...], wl_ref[...]], axis=0), xh)
    logits = both[:N_EXPERTS] + _dot_nt(wh_ref[...], xl) + both[N_EXPERTS:]
    aff = 1.0 / (1.0 + jnp.exp(-logits))
    sel = aff + rb_ref[...]
    s = [sel[e:e + 1, :] for e in range(N_EXPERTS)]
    a = [aff[e:e + 1, :] for e in range(N_EXPERTS)]

    def top2_sum(v):
        hi01, lo01 = jnp.maximum(v[0], v[1]), jnp.minimum(v[0], v[1])
        hi23, lo23 = jnp.maximum(v[2], v[3]), jnp.minimum(v[2], v[3])
        return jnp.maximum(hi01, hi23) + jnp.maximum(jnp.minimum(hi01, hi23), jnp.maximum(lo01, lo23))

    gscore = [top2_sum(s[EXPERTS_PER_GROUP * g:EXPERTS_PER_GROUP * (g + 1)]) for g in range(N_EXPERT_GROUPS)]
    best = jnp.zeros((1, tm), jnp.int32)
    best_score = gscore[0]
    for g in range(1, N_EXPERT_GROUPS):
        better = gscore[g] > best_score
        best = jnp.where(better, g, best)
        best_score = jnp.where(better, gscore[g], best_score)

    def pick(rows, j):
        out = rows[j]
        for g in range(1, N_EXPERT_GROUPS):
            out = jnp.where(best == g, rows[EXPERTS_PER_GROUP * g + j], out)
        return out

    t = [pick(s, j) for j in range(EXPERTS_PER_GROUP)]
    w = [pick(a, j) for j in range(EXPERTS_PER_GROUP)]

    def first_max(v):
        mx = jnp.maximum(jnp.maximum(v[0], v[1]), jnp.maximum(v[2], v[3]))
        idx = jnp.full((1, tm), EXPERTS_PER_GROUP - 1, jnp.int32)
        for j in range(EXPERTS_PER_GROUP - 2, -1, -1):
            idx = jnp.where(v[j] == mx, j, idx)
        return idx

    i1 = first_max(t)
    i2 = first_max([jnp.where(i1 == j, -jnp.inf, t[j]) for j in range(EXPERTS_PER_GROUP)])
    lo, hi = jnp.minimum(i1, i2), jnp.maximum(i1, i2)

    def take(rows, idx):
        out = rows[0]
        for j in range(1, EXPERTS_PER_GROUP):
            out = jnp.where(idx == j, rows[j], out)
        return out

    w_lo, w_hi = take(w, lo), take(w, hi)
    den = w_lo + w_hi
    pair = jnp.where(lo == 0, hi - 1, jnp.where(lo == 1, hi + 1, len(PAIRS) - 1))
    cls = best * len(PAIRS) + pair

    onehot = (lax.broadcasted_iota(jnp.int32, (CLASS_ROWS, tm), 0) == cls).astype(F32)
    before = _dot(onehot.astype(BF16), tri_ref[...]) + carry_ref[:, 0:1]
    rank = jnp.sum(onehot * before, axis=0, keepdims=True)
    carry_ref[...] = carry_ref[...] + jnp.sum(onehot, axis=1, keepdims=True)

    cls_ref[0, :, rows] = cls
    rank_ref[0, :, rows] = rank.astype(jnp.int32)
    cnt_ref[...] = carry_ref[...]
    gates = jnp.concatenate([w_lo / den, w_hi / den, jnp.zeros((LANES - 2, tm), F32)], axis=0)
    xa_ref[rows, :D_MODEL] = x
    xa_ref[rows, D_MODEL:] = gates.T


def _post_attn_kernel(*refs, merge):
    if merge:
        o_refs, l_refs, e_ref, refs = refs[:3], refs[3:6], refs[6], refs[7:]
    else:
        mix_ref, refs = refs[0], refs[1:]
    (qm_ref, mk_ref, mv_ref, w1_ref, w2_ref, x_ref, g_ref, b_ref, wh_ref, wl_ref, rb_ref, tri_ref,
     xa_ref, cls_ref, rank_ref, cnt_ref, carry_ref) = refs[:17]

    @pl.when(pl.program_id(0) == 0)
    def _():
        carry_ref[...] = jnp.zeros_like(carry_ref)

    first = _first_head((TOK_TILE, LANES))
    for sub in range(POST_SUB):
        rows = slice(sub * TOK_TILE, (sub + 1) * TOK_TILE)
        if merge:
            ot_ref, lt_ref = refs[17], refs[18]
            nch = B_WIDTH // LANES
            for g, d in enumerate(DILATIONS):
                n = TOK_TILE // d
                for r in range(d):
                    dst = pl.ds(r, n, stride=d) if d > 1 else slice(None)
                    og = o_refs[g][0, r, sub * n:(sub + 1) * n].astype(F32)
                    for ch in range(nch):
                        ot_ref[sub, g, ch, dst, :] = og[:, ch * LANES:(ch + 1) * LANES]
                    lt_ref[sub, g, dst, :] = l_refs[g][0, r, sub * n:(sub + 1) * n]
            lses = [lt_ref[sub, g] for g in range(len(DILATIONS))]
            m = functools.reduce(jnp.maximum, lses)
            es = [jnp.exp(l - m) for l in lses]
            inv = 1.0 / functools.reduce(jnp.add, es)
            expand = e_ref[...]

            def widen(w):
                hi, lo = _split_bf16(w)
                return _dot(hi, expand) + _dot(lo, expand)

            ws = [widen(e * inv) for e in es]
            mix = jnp.concatenate(
                [functools.reduce(jnp.add, [w[:, ch * LANES:(ch + 1) * LANES] * ot_ref[sub, g, ch]
                                            for g, w in enumerate(ws)]) for ch in range(nch)], axis=1).astype(BF16)
        else:
            mix = mix_ref[rows, :]

        memo = []
        for j in range(MEM_W // LANES):
            sl = slice(j * LANES, (j + 1) * LANES)
            memo.append(_pair_output(_pair_attention(qm_ref[rows, sl], mk_ref[:, sl], _with_ones(mv_ref[:, sl]), first),
                                     first))
        memo = jnp.concatenate(memo, axis=1).astype(BF16)

        attn = _dot(mix, w1_ref[...]) + _dot(memo, w2_ref[...])
        x1 = _layer_norm(ALPHA * x_ref[rows, :] + attn, g_ref[...], b_ref[...])
        _route(x1, rows, wh_ref, wl_ref, rb_ref, tri_ref, xa_ref, cls_ref, rank_ref, cnt_ref, carry_ref)


def _post_attn(mix, qm, mkv, w1, w2, x, g, b, router, expand=None):
    merge = expand is not None
    wh, wl, rb, tri = router
    blk = POST_SUB * TOK_TILE
    nblk = N_TOK // blk
    per_seq = SEQ // blk
    row = lambda width: pl.BlockSpec((blk, width), lambda i: (i, 0))
    const = lambda shape: pl.BlockSpec(shape, lambda i: (0,) * len(shape))
    row3 = pl.BlockSpec((1, 1, blk), lambda i: (i, 0, 0))
    scratch = [pltpu.VMEM((CLASS_ROWS, LANES), F32)]
    if merge:
        outs, lses = mix
        resid = lambda d, width: pl.BlockSpec((1, d, blk // d, width), lambda i: (i // per_seq, 0, i % per_seq, 0))
        lead_specs = ([resid(d, B_WIDTH) for d in DILATIONS] + [resid(d, LANES) for d in DILATIONS]
                      + [const(expand.shape)])
        lead = [*outs, *lses, expand]
        scratch += [pltpu.VMEM((POST_SUB, len(DILATIONS), B_WIDTH // LANES, TOK_TILE, LANES), F32),
                    pltpu.VMEM((POST_SUB, len(DILATIONS), TOK_TILE, LANES), F32)]
    else:
        lead_specs, lead = [row(mix.shape[1])], [mix]
    xa, cls, rank, cnt = pl.pallas_call(
        functools.partial(_post_attn_kernel, merge=merge),
        grid=(nblk,),
        in_specs=lead_specs + [row(MEM_W),
                               pl.BlockSpec((N_MEM, MEM_W), lambda i: (i // per_seq, 0)),
                               pl.BlockSpec((N_MEM, MEM_W), lambda i: (i // per_seq, 1)),
                               const(w1.shape), const(w2.shape), row(D_MODEL), const((1, D_MODEL)), const((1, D_MODEL)),
                               const(wh.shape), const(wl.shape), const(rb.shape), const(tri.shape)],
        out_specs=[row(XA_W), row3, row3, const((CLASS_ROWS, LANES))],
        out_shape=[jax.ShapeDtypeStruct((N_TOK, XA_W), F32), jax.ShapeDtypeStruct((nblk, 1, blk), jnp.int32),
                   jax.ShapeDtypeStruct((nblk, 1, blk), jnp.int32), jax.ShapeDtypeStruct((CLASS_ROWS, LANES), F32)],
        scratch_shapes=scratch,
        compiler_params=_params(("arbitrary",)),
        name="post_attn_merge" if merge else "post_attn",
    )(*lead, qm, mkv, mkv, w1, w2, x, g, b, wh, wl, rb, tri)
    return xa, cls.reshape(N_TOK), rank.reshape(N_TOK), cnt[:N_CLASSES, 0].astype(jnp.int32)


def _dispatch_kernel(pos_ref, ends_ref, xa_ref, xs_ref, zeros, sem, zsem, *, chunk):
    base = pl.program_id(0) * chunk

    @pl.when(pl.program_id(0) == 0)
    def _():
        zeros[...] = jnp.zeros_like(zeros)

        def tail(c):
            start = pl.multiple_of(ends_ref[c] - MOE_TM, MOE_TM)
            return pltpu.make_async_copy(zeros, xs_ref.at[pl.ds(start, MOE_TM)], zsem)

        def nonempty(c):
            return ends_ref[c] > (ends_ref[c - 1] if c else 0)

        def spare(t):
            return pltpu.make_async_copy(zeros, xs_ref.at[pl.ds(pl.multiple_of(t * MOE_TM, MOE_TM), MOE_TM)], zsem)

        first_spare = ends_ref[N_CLASSES - 1] // MOE_TM
        for c in range(N_CLASSES):
            pl.when(nonempty(c))(lambda c=c: tail(c).start())
        lax.fori_loop(first_spare, MOE_TILES, lambda t, carry: (spare(t).start(), carry)[1], 0)
        for c in range(N_CLASSES):
            pl.when(nonempty(c))(lambda c=c: tail(c).wait())
        lax.fori_loop(first_spare, MOE_TILES, lambda t, carry: (spare(t).wait(), carry)[1], 0)

    for r in range(chunk):
        pltpu.make_async_copy(xa_ref.at[pl.ds(r, 1)], xs_ref.at[pl.ds(pos_ref[base + r], 1)], sem).start(priority=r % 2)
    pltpu.make_async_copy(xa_ref, xs_ref.at[pl.ds(0, chunk)], sem).wait()


def _dispatch(pos, ends, xa, chunk=1024):
    return pl.pallas_call(
        functools.partial(_dispatch_kernel, chunk=chunk),
        grid_spec=pltpu.PrefetchScalarGridSpec(
            num_scalar_prefetch=2,
            grid=(N_TOK // chunk,),
            in_specs=[pl.BlockSpec((chunk, XA_W), lambda i, p, e: (i, 0))],
            out_specs=pl.BlockSpec(memory_space=pl.ANY),
            scratch_shapes=[pltpu.VMEM((MOE_TM, XA_W), F32), pltpu.SemaphoreType.DMA(()),
                            pltpu.SemaphoreType.DMA(())],
        ),
        out_shape=jax.ShapeDtypeStruct((MOE_ROWS, XA_W), F32),
        compiler_params=_params(("arbitrary",)),
        name="moe_dispatch",
    )(pos, ends, xa)


def _expert_kernel(slot1_ref, slot2_ref, used_ref, first_ref, grp_ref, next_ref, xs_ref, wg_hbm, wu_hbm, wd_hbm,
                   ys_ref, stage_g, stage_u, stage_d, wg_s, wu_s, wd_s, sem, *, layer):
    i = pl.program_id(0)
    streams = ((wg_hbm, stage_g, wg_s), (wu_hbm, stage_u, wu_s), (wd_hbm, stage_d, wd_s))

    def group_copy(group, k):
        first_expert = layer * N_EXPERTS + group * EXPERTS_PER_GROUP
        hbm, stage, _ = streams[k]
        return pltpu.make_async_copy(hbm.at[pl.ds(first_expert, EXPERTS_PER_GROUP)], stage, sem.at[k])

    @pl.when(i == 0)
    def _():
        for k in range(len(streams)):
            group_copy(grp_ref[0], k).start()

    @pl.when(first_ref[i] != 0)
    def _():
        for k, (_, stage, dst) in enumerate(streams):
            group_copy(grp_ref[i], k).wait()
            for e in range(EXPERTS_PER_GROUP):
                dst[e] = stage[e].astype(BF16)

        @pl.when(next_ref[i] >= 0)
        def _():
            for k in range(len(streams)):
                group_copy(next_ref[i], k).start()

    @pl.when(used_ref[i] != 0)
    def _():
        x = xs_ref[:, :D_MODEL].astype(BF16)
        gates = xs_ref[:, D_MODEL:]

        def expert(slot):
            gate = _dot(x, wg_s[slot])
            h = gate * (1.0 / (1.0 + jnp.exp(-gate))) * _dot(x, wu_s[slot])
            return _dot(h.astype(BF16), wd_s[slot])

        ys_ref[...] = gates[:, 0:1] * expert(slot1_ref[i]) + gates[:, 1:2] * expert(slot2_ref[i])

    @pl.when(used_ref[i] == 0)
    def _():
        ys_ref[...] = jnp.zeros_like(ys_ref)


def _experts(tile_meta, xs, layer, wg, wu, wd):
    nmeta = len(tile_meta)
    up_shape = (EXPERTS_PER_GROUP, D_MODEL, D_EXPERT)
    down_shape = (EXPERTS_PER_GROUP, D_EXPERT, D_MODEL)
    return pl.pallas_call(
        functools.partial(_expert_kernel, layer=layer),
        grid_spec=pltpu.PrefetchScalarGridSpec(
            num_scalar_prefetch=nmeta,
            grid=(MOE_TILES,),
            in_specs=[pl.BlockSpec((MOE_TM, XA_W), lambda i, *_: (i, 0)),
                      pl.BlockSpec(memory_space=pl.ANY), pl.BlockSpec(memory_space=pl.ANY),
                      pl.BlockSpec(memory_space=pl.ANY)],
            out_specs=pl.BlockSpec((MOE_TM, D_MODEL), lambda i, *_: (i, 0)),
            scratch_shapes=[pltpu.VMEM(up_shape, F32), pltpu.VMEM(up_shape, F32), pltpu.VMEM(down_shape, F32),
                            pltpu.VMEM(up_shape, BF16), pltpu.VMEM(up_shape, BF16), pltpu.VMEM(down_shape, BF16),
                            pltpu.SemaphoreType.DMA((3,))],
        ),
        out_shape=jax.ShapeDtypeStruct((MOE_ROWS, D_MODEL), F32),
        compiler_params=_params(("arbitrary",)),
        name="moe_experts",
    )(*tile_meta, xs, wg, wu, wd)


def _combine_kernel(pos_ref, ys_ref, x_ref, g_ref, b_ref, o_ref, buf, sem, *, tc):
    i = pl.program_id(0)
    slot = i % 2

    def gather(step, into):
        for r in range(tc):
            row = pos_ref[step * tc + r]
            pltpu.make_async_copy(ys_ref.at[pl.ds(row, 1)], buf.at[into, pl.ds(r, 1)], sem.at[into]).start(
                priority=r % 2)

    @pl.when(i == 0)
    def _():
        gather(0, 0)

    @pl.when(i + 1 < pl.num_programs(0))
    def _():
        gather(i + 1, 1 - slot)

    pltpu.make_async_copy(ys_ref.at[pl.ds(0, tc)], buf.at[slot], sem.at[slot]).wait()
    o_ref[...] = _layer_norm(ALPHA * x_ref[...] + buf[slot], g_ref[...], b_ref[...])


def _combine_ln(pos, ys, xa, g, b, tc=512):
    return pl.pallas_call(
        functools.partial(_combine_kernel, tc=tc),
        grid_spec=pltpu.PrefetchScalarGridSpec(
            num_scalar_prefetch=1,
            grid=(N_TOK // tc,),
            in_specs=[pl.BlockSpec(memory_space=pl.ANY),
                      pl.BlockSpec((tc, D_MODEL), lambda i, p: (i, 0)),
                      pl.BlockSpec((1, D_MODEL), lambda i, p: (0, 0)),
                      pl.BlockSpec((1, D_MODEL), lambda i, p: (0, 0))],
            out_specs=pl.BlockSpec((tc, D_MODEL), lambda i, p: (i, 0)),
            scratch_shapes=[pltpu.VMEM((2, tc, D_MODEL), F32), pltpu.SemaphoreType.DMA((2,))],
        ),
        out_shape=jax.ShapeDtypeStruct((N_TOK, D_MODEL), F32),
        compiler_params=_params(("arbitrary",)),
        name="moe_combine_ln",
    )(pos, ys, xa, g, b)


def _moe_ln(routed, layer, wg, wu, wd, g, b):
    xa, cls, rank, counts = routed
    padded = (counts + MOE_TM - 1) // MOE_TM * MOE_TM
    ends = jnp.cumsum(padded)
    starts = ends - padded
    pos = starts[cls] + rank
    tile_start = jnp.arange(MOE_TILES, dtype=jnp.int32) * MOE_TM
    tile_used = (tile_start < ends[-1]).astype(jnp.int32)
    last_cls = jnp.max(jnp.where(counts > 0, jnp.arange(N_CLASSES), 0))
    tile_cls = jnp.minimum(jnp.sum(tile_start[:, None] >= ends[None, :], axis=1), last_cls)
    tile_cls = tile_cls.astype(jnp.int32)
    tile_grp = tile_cls // len(PAIRS)
    prev_grp = jnp.concatenate([jnp.full((1,), -1, jnp.int32), tile_grp[:-1]])
    tile_first = tile_used * (tile_grp != prev_grp).astype(jnp.int32)
    has = jnp.sum(counts.reshape(N_EXPERT_GROUPS, len(PAIRS)), axis=1) > 0
    gid = jnp.arange(N_EXPERT_GROUPS, dtype=jnp.int32)
    later = jnp.where(jnp.logical_and(has[None, :], gid[None, :] > gid[:, None]), gid[None, :], N_EXPERT_GROUPS)
    next_grp = jnp.min(later, axis=1)
    next_grp = jnp.where(next_grp == N_EXPERT_GROUPS, -1, next_grp)
    pair_slots = jnp.asarray(np.array(PAIRS, np.int32))[tile_cls % len(PAIRS)]
    tile_meta = (pair_slots[:, 0], pair_slots[:, 1], tile_used, tile_first, tile_grp, next_grp[tile_grp])
    xs = _dispatch(pos, ends.astype(jnp.int32), xa)
    ys = _experts(tile_meta, xs, layer, wg, wu, wd)
    return _combine_ln(pos, ys, xa, g, b)


def _tables(c, sa, sb):
    return tuple(np.concatenate([t, t], axis=-1).astype(np.float32) for t in (c, sa, sb))


def _axial_tables():
    rows = SEQ // GRID_W
    row = np.repeat(np.arange(rows), GRID_W).astype(np.float64)
    col = np.tile(np.arange(GRID_W), rows).astype(np.float64)
    half = HEAD_DIM // 2
    inv = 1.0 / (AXIAL_THETA ** (np.arange(0, half, 2, dtype=np.float64) / half))
    ar, ac = row[:, None] * inv, col[:, None] * inv
    z = np.zeros_like(ar)
    c = np.concatenate([np.cos(ar), np.cos(ar), np.cos(ac), np.cos(ac)], -1)
    sa = np.concatenate([-np.sin(ar), z, -np.sin(ac), z], -1)
    sb = np.concatenate([z, np.sin(ar), z, np.sin(ac)], -1)
    return _tables(c, sa, sb)


def _partial_tables():
    n = PARTIAL_ROT_DIMS
    pos = np.arange(SEQ, dtype=np.float64)
    ang = pos[:, None] * (1.0 / (PARTIAL_THETA ** (np.arange(0, n, 2, dtype=np.float64) / n)))
    z = np.zeros_like(ang)
    rest = np.zeros((SEQ, HEAD_DIM - n))
    c = np.concatenate([np.cos(ang), np.cos(ang), rest + 1.0], -1)
    sa = np.concatenate([-np.sin(ang), z, rest], -1)
    sb = np.concatenate([z, np.sin(ang), rest], -1)
    return _tables(c, sa, sb)


def _residue_order(tab, d):
    return np.ascontiguousarray(tab.reshape(SEQ // d, d, LANES).transpose(1, 0, 2))


def kernel(x, mem, w_mem_kv, router_w, router_b, a_w_in, a_w_out, a_q_norm, a_k_norm, b_w_in, b_w_out,
           ln1_g, ln1_b, ln2_g, ln2_b, w_gate, w_up, w_down):
    x = x.reshape(N_TOK, D_MODEL)
    row = lambda v: v.reshape(1, -1)

    lane = np.arange(LANES)
    m128 = jnp.asarray((lane[:, None] // HEAD_DIM == lane[None, :] // HEAD_DIM), BF16)
    expand = jnp.asarray(lane[:, None] == np.arange(B_WIDTH)[None, :] // HEAD_DIM, BF16)
    tri = jnp.asarray(np.triu(np.ones((TOK_TILE, TOK_TILE), np.float32), 1), BF16)
    rw_hi = router_w.T.astype(BF16)
    rw_lo = (router_w.T - rw_hi.astype(F32)).astype(BF16)
    router = (rw_hi, rw_lo, router_b.reshape(N_EXPERTS, 1), tri)

    mkv = _proj(mem.reshape(BATCH * N_MEM, D_MODEL), w_mem_kv.astype(BF16))
    experts = tuple(w.reshape(DEPTH * N_EXPERTS, *w.shape[2:]) for w in (w_gate, w_up, w_down))

    kmix = A_Q_HEADS * HEAD_DIM
    qcols = (np.asarray(_Q_HEAD_ORDER)[:, None] * HEAD_DIM + np.arange(HEAD_DIM)[None, :]).reshape(-1)
    w_in = jnp.concatenate([a_w_in[0][:, qcols], a_w_in[0][:, kmix:]], axis=1).astype(BF16)
    qg = row(jnp.tile(a_q_norm[0], 2) * Q_SCALE)
    kg = row(jnp.tile(a_k_norm[0], 2))
    q, k, v, qm = _a_in_proj(x, w_in, m128, qg, kg, _axial_tables())
    mix = _gqa_attention(q, k, v)
    w_out = a_w_out[0].astype(BF16)
    routed = _post_attn(mix, qm, mkv, w_out[qcols], w_out[kmix:], x, row(ln1_g[0]), row(ln1_b[0]), router)
    x = _moe_ln(routed, 0, *experts, row(ln2_g[0]), row(ln2_b[0]))

    ptabs = _partial_tables()
    tabs = [tuple(_residue_order(t, d) for t in ptabs) for d in DILATIONS]
    qkv, qm = _b_in_proj(x, b_w_in[0].astype(BF16), tabs)
    outs, lses = zip(*[_band_attention(*qkv[g], d) for g, d in enumerate(DILATIONS)])
    w_out = b_w_out[0].astype(BF16)
    routed = _post_attn((outs, lses), qm, mkv, w_out[:B_WIDTH], w_out[B_WIDTH:], x, row(ln1_g[1]), row(ln1_b[1]),
                        router, expand)
    x = _moe_ln(routed, 1, *experts, row(ln2_g[1]), row(ln2_b[1]))
    return x.reshape(BATCH, SEQ, D_MODEL)
```
